```python
import jax, jax.numpy as jnp
from jax import lax
import numpy as np

D_MODEL = 1024
BATCH = 8
SEQ = 4096
DEPTH = 2
DEC_BATCH = 1
DEC_SEQ = 16384
PAST_LEN = 128

GRID_W = 64
D_MIX = D_MODEL
HEAD_DIM = 64
NA_HEADS = 4
NA_ROWS = 8
NA_COLS = 16
NA_WIDTH = NA_HEADS * HEAD_DIM
FN_GROUPS = 4
FN_GROUP_DIM = 64
FN_WIDTH = FN_GROUPS * FN_GROUP_DIM
MLA_HEADS = 4
MLA_Q_LORA = 256
MLA_KV_LORA = 128
MLA_NOPE = 64
MLA_ROPE = 32
MLA_V = 64
MLA_WIDTH = MLA_HEADS * MLA_V
MLA_Q_BLOCK = 128
ROPE_THETA = 10000.0
SGU_HEADS = 4
SGU_HEAD_DIM = 64
SGU_WIDTH = SGU_HEADS * SGU_HEAD_DIM
SGU_CHUNK = 128
D_IN = 3 * NA_WIDTH + FN_WIDTH + MLA_Q_LORA + MLA_KV_LORA + MLA_ROPE + 2 * SGU_WIDTH
D_FF = 2816
CONV_WIDTH = 3
ALPHA = (2 * DEPTH) ** 0.25
BETA = (8 * DEPTH) ** -0.25
LN_EPS = 1e-5
RMS_EPS = 1e-6

kernel_name = 'hymba_natten_fnet_mla_gmlp_encoder'


def _layer_norm(x, g, b):
    xf = x.astype(jnp.float32)
    mu = jnp.mean(xf, axis=-1, keepdims=True)
    var = jnp.mean(jnp.square(xf - mu), axis=-1, keepdims=True)
    return ((xf - mu) * lax.rsqrt(var + LN_EPS) * g + b).astype(x.dtype)


def _rms_norm(x, g):
    xf = x.astype(jnp.float32)
    ms = jnp.mean(jnp.square(xf), axis=-1, keepdims=True)
    return (xf * lax.rsqrt(ms + RMS_EPS) * g).astype(x.dtype)


def _rope_tables(T):
    inv_freq = ROPE_THETA ** (-jnp.arange(0, MLA_ROPE, 2, dtype=jnp.float32) / MLA_ROPE)
    ang = jnp.arange(T, dtype=jnp.float32)[:, None] * inv_freq[None, :]
    return jnp.cos(ang), jnp.sin(ang)


def _rope(x, cos, sin):
    x1, x2 = jnp.split(x, 2, axis=-1)
    return jnp.concatenate([x1 * cos - x2 * sin, x1 * sin + x2 * cos], axis=-1).astype(x.dtype)


def _natten(q, k, v, rpb):
    B, T, H, Dh = q.shape
    rows = T // GRID_W
    kh = min(NA_ROWS, rows)
    qg = q.reshape(B, rows, GRID_W, H, Dh)
    kg = k.reshape(B, rows, GRID_W, H, Dh)
    vg = v.reshape(B, rows, GRID_W, H, Dh)
    col = jnp.arange(GRID_W)
    col_start = jnp.clip(col - NA_COLS // 2, 0, GRID_W - NA_COLS)
    col_idx = col_start[:, None] + jnp.arange(NA_COLS)[None, :]
    col_off = col_idx - col[:, None] + (NA_COLS - 1)
    scale = Dh ** -0.5

    def row_block(r):
        r_start = jnp.clip(r - kh // 2, 0, rows - kh)
        q_r = lax.dynamic_index_in_dim(qg, r, axis=1, keepdims=False)
        k_win = lax.dynamic_slice_in_dim(kg, r_start, kh, axis=1)[:, :, col_idx]
        v_win = lax.dynamic_slice_in_dim(vg, r_start, kh, axis=1)[:, :, col_idx]
        row_off = r_start + jnp.arange(kh) - r + (NA_ROWS - 1)
        bias = rpb[:, row_off][:, :, col_off]
        s = (jnp.einsum('bqhd,biqjhd->bhqij', q_r, k_win).astype(jnp.float32) * scale
             + jnp.transpose(bias, (0, 2, 1, 3)).astype(jnp.float32))
        p = jax.nn.softmax(s.reshape(B, H, GRID_W, kh * NA_COLS), axis=-1).reshape(s.shape).astype(v.dtype)
        return jnp.einsum('bhqij,biqjhd->bqhd', p, v_win)

    o = lax.map(row_block, jnp.arange(rows))
    return jnp.moveaxis(o, 0, 1).reshape(B, T, H * Dh)


def _fourier_mix(f):
    B, T, _ = f.shape
    fg = f.reshape(B, T, FN_GROUPS, FN_GROUP_DIM).astype(jnp.float32)
    out = jnp.real(jnp.fft.fft2(fg, axes=(1, 3), norm='ortho'))
    return out.reshape(B, T, FN_WIDTH).astype(f.dtype)


def _mla(cq, ckv, kr, q_g, w_uq, kv_g, w_ukv, cos, sin):
    B, T, _ = cq.shape
    q = jnp.einsum('btr,rhe->bthe', _rms_norm(cq, q_g), w_uq)
    q_nope = q[..., :MLA_NOPE]
    q_rope = _rope(q[..., MLA_NOPE:], cos[:, None, :], sin[:, None, :])
    kv = jnp.einsum('btr,rhe->bthe', _rms_norm(ckv, kv_g), w_ukv)
    k_nope = kv[..., :MLA_NOPE]
    v = kv[..., MLA_NOPE:]
    k_rope = _rope(kr, cos, sin)
    nb = T // MLA_Q_BLOCK
    scale = (MLA_NOPE + MLA_ROPE) ** -0.5

    def to_blocks(a):
        return jnp.moveaxis(a.reshape((B, nb, MLA_Q_BLOCK) + a.shape[2:]), 1, 0)

    def block(args):
        qn, qr = args
        s = jnp.einsum('bqhe,bkhe->bhqk', qn, k_nope) + jnp.einsum('bqhe,bke->bhqk', qr, k_rope)
        p = jax.nn.softmax(s.astype(jnp.float32) * scale, axis=-1).astype(v.dtype)
        return jnp.einsum('bhqk,bkhe->bqhe', p, v)

    o = lax.map(block, (to_blocks(q_nope), to_blocks(q_rope)))
    return jnp.moveaxis(o, 0, 1).reshape(B, T, MLA_WIDTH)


def _spatial_gating(z, ln_g, ln_b, w_s, b_s):
    B, T, _ = z.shape
    u, v = jnp.split(jax.nn.gelu(z), 2, axis=-1)
    v = v.reshape(B, T // SGU_CHUNK, SGU_CHUNK, SGU_HEADS, SGU_HEAD_DIM)
    v = _layer_norm(v, ln_g.reshape(SGU_HEADS, SGU_HEAD_DIM), ln_b.reshape(SGU_HEADS, SGU_HEAD_DIM))
    v = jnp.einsum('gpq,bcqgd->bcpgd', w_s, v) + b_s.T[:, :, None]
    return u * v.reshape(B, T, SGU_WIDTH)


def _mixer(h, cos, sin, w_in, na_rpb, mla_q_g, w_uq, mla_kv_g, w_ukv,
           sgu_ln_g, sgu_ln_b, sgu_w, sgu_b, w_out):
    B, T, _ = h.shape
    z = h @ w_in
    sizes = [NA_WIDTH, NA_WIDTH, NA_WIDTH, FN_WIDTH, MLA_Q_LORA, MLA_KV_LORA, MLA_ROPE, 2 * SGU_WIDTH]
    idx = [int(i) for i in np.cumsum(sizes)[:-1]]
    qa, ka, va, fb, cq, ckv, kr, sg = jnp.split(z, idx, axis=-1)
    hs = (B, T, NA_HEADS, HEAD_DIM)
    o_a = _natten(qa.reshape(hs), ka.reshape(hs), va.reshape(hs), na_rpb)
    o_b = _fourier_mix(fb)
    o_c = _mla(cq, ckv, kr, mla_q_g, w_uq, mla_kv_g, w_ukv, cos, sin)
    o_d = _spatial_gating(sg, sgu_ln_g, sgu_ln_b, sgu_w, sgu_b)
    mix = jnp.concatenate([o_a, o_b, o_c, o_d], axis=-1)
    return mix @ w_out


def _conv_ffn(x, w_up, conv_w, conv_b, w_down):
    h = x @ w_up
    hp = jnp.pad(h, ((0, 0), (CONV_WIDTH // 2, CONV_WIDTH // 2), (0, 0)))
    h = hp[:, :-2] * conv_w[0] + hp[:, 1:-1] * conv_w[1] + hp[:, 2:] * conv_w[2] + conv_b
    gate, val = jnp.split(h, 2, axis=-1)
    return (jax.nn.gelu(gate) * val) @ w_down


def _trunk(x, emb_ln_g, emb_ln_b, w_in, na_rpb, mla_q_g, w_uq, mla_kv_g, w_ukv,
           sgu_ln_g, sgu_ln_b, sgu_w, sgu_b, w_out, ln1_g, ln1_b,
           w_up, conv_w, conv_b, w_down, ln2_g, ln2_b):
    T = x.shape[1]
    cos, sin = _rope_tables(T)
    x = _layer_norm(x, emb_ln_g, emb_ln_b)
    for l in range(DEPTH):
        mix = _mixer(x, cos, sin, w_in[l], na_rpb[l], mla_q_g[l], w_uq[l], mla_kv_g[l], w_ukv[l],
                     sgu_ln_g[l], sgu_ln_b[l], sgu_w[l], sgu_b[l], w_out[l])
        x = _layer_norm(ALPHA * x + mix, ln1_g[l], ln1_b[l])
        x = _layer_norm(ALPHA * x + _conv_ffn(x, w_up[l], conv_w[l], conv_b[l], w_down[l]), ln2_g[l], ln2_b[l])
    return x


def setup_inputs(seed: int = 0) -> dict:
    key = jax.random.key(seed)
    ks = jax.random.split(key, 23)

    def nrm(k, shape, scale):
        return jax.random.normal(k, shape, jnp.float32) * scale

    L = DEPTH
    return {
        'x_prompt': nrm(ks[0], (BATCH, SEQ, D_MODEL), 1.0),
        'x_sample': nrm(ks[1], (DEC_BATCH, DEC_SEQ, D_MODEL), 1.0),
        'emb_ln_g': 1.0 + nrm(ks[2], (D_MODEL,), 0.05),
        'emb_ln_b': nrm(ks[3], (D_MODEL,), 0.02),
        'w_in': nrm(ks[4], (L, D_MODEL, D_IN), D_MODEL ** -0.5),
        'na_rpb': nrm(ks[5], (L, NA_HEADS, 2 * NA_ROWS - 1, 2 * NA_COLS - 1), 0.1),
        'mla_q_g': 1.0 + nrm(ks[6], (L, MLA_Q_LORA), 0.05),
        'w_uq': nrm(ks[7], (L, MLA_Q_LORA, MLA_HEADS, MLA_NOPE + MLA_ROPE), MLA_Q_LORA ** -0.5),
        'mla_kv_g': 1.0 + nrm(ks[8], (L, MLA_KV_LORA), 0.05),
        'w_ukv': nrm(ks[9], (L, MLA_KV_LORA, MLA_HEADS, MLA_NOPE + MLA_V), MLA_KV_LORA ** -0.5),
        'sgu_ln_g': 1.0 + nrm(ks[10], (L, SGU_WIDTH), 0.05),
        'sgu_ln_b': nrm(ks[11], (L, SGU_WIDTH), 0.02),
        'sgu_w': nrm(ks[12], (L, SGU_HEADS, SGU_CHUNK, SGU_CHUNK), SGU_CHUNK ** -0.5),
        'sgu_b': 1.0 + nrm(ks[13], (L, SGU_HEADS, SGU_CHUNK), 0.05),
        'w_out': nrm(ks[14], (L, D_MIX, D_MODEL), BETA * D_MIX ** -0.5),
        'ln1_g': 1.0 + nrm(ks[15], (L, D_MODEL), 0.05),
        'ln1_b': nrm(ks[16], (L, D_MODEL), 0.02),
        'w_up': nrm(ks[17], (L, D_MODEL, 2 * D_FF), D_MODEL ** -0.5),
        'conv_w': nrm(ks[18], (L, CONV_WIDTH, 2 * D_FF), 0.5),
        'conv_b': nrm(ks[19], (L, 2 * D_FF), 0.02),
        'w_down': nrm(ks[20], (L, D_FF, D_MODEL), BETA * D_FF ** -0.5),
        'ln2_g': 1.0 + nrm(ks[21], (L, D_MODEL), 0.05),
        'ln2_b': nrm(ks[22], (L, D_MODEL), 0.02),
    }


def reference(x_prompt, x_sample, emb_ln_g, emb_ln_b, w_in, na_rpb, mla_q_g, w_uq, mla_kv_g, w_ukv,
              sgu_ln_g, sgu_ln_b, sgu_w, sgu_b, w_out, ln1_g, ln1_b,
              w_up, conv_w, conv_b, w_down, ln2_g, ln2_b):
    y_prompt = _trunk(x_prompt, emb_ln_g, emb_ln_b, w_in, na_rpb, mla_q_g, w_uq, mla_kv_g, w_ukv,
                      sgu_ln_g, sgu_ln_b, sgu_w, sgu_b, w_out, ln1_g, ln1_b,
                      w_up, conv_w, conv_b, w_down, ln2_g, ln2_b)
    y_sample = _trunk(x_sample, emb_ln_g, emb_ln_b, w_in, na_rpb, mla_q_g, w_uq, mla_kv_g, w_ukv,
                      sgu_ln_g, sgu_ln_b, sgu_w, sgu_b, w_out, ln1_g, ln1_b,
                      w_up, conv_w, conv_b, w_down, ln2_g, ln2_b)
    return (y_prompt, y_sample)
```

```python
import functools
import math

import numpy as np
import jax
import jax.numpy as jnp
from jax import lax
from jax.experimental import pallas as pl
from jax.experimental.pallas import tpu as pltpu

F32 = jnp.float32
BF16 = jnp.bfloat16

D_MODEL = 1024
DEPTH = 2
GRID_W = 64
HEAD_DIM = 64
NA_HEADS = 4
NA_ROWS = 8
NA_COLS = 16
NA_WIDTH = NA_HEADS * HEAD_DIM
FN_GROUPS = 4
FN_GROUP_DIM = 64
FN_WIDTH = FN_GROUPS * FN_GROUP_DIM
MLA_HEADS = 4
MLA_Q_LORA = 256
MLA_KV_LORA = 128
MLA_NOPE = 64
MLA_ROPE = 32
MLA_V = 64
MLA_WIDTH = MLA_HEADS * MLA_V
ROPE_THETA = 10000.0
SGU_HEADS = 4
SGU_HEAD_DIM = 64
SGU_WIDTH = SGU_HEADS * SGU_HEAD_DIM
SGU_CHUNK = 128
D_FF = 2816
ALPHA = (2 * DEPTH) ** 0.25
LN_EPS = 1e-5
RMS_EPS = 1e-6

LANES = 128
SUBLANES = 8
VMEM_LIMIT = 56 * 1024 * 1024

TM_IN = 512
TM_OUT = 512
TM_FFN = 512
FF_CHUNK = 256
NA_QROWS = 4
NA_QBLK = NA_QROWS * GRID_W
NA_KBLKS = 3
MLA_PAD = 128
MLA_TQ = 512
MLA_TK = 512
HALO = SUBLANES

HI = lax.Precision.HIGHEST
NT_DIMS = (((1,), (1,)), ((), ()))


def _dot(a, b):
    return jnp.dot(a, b, preferred_element_type=F32)


def _dot_nt(a, b):
    return lax.dot_general(a, b, NT_DIMS, preferred_element_type=F32)


def _dot_hi(a, b):
    return jnp.dot(a, b, precision=HI, preferred_element_type=F32)


def _layer_norm_rows(x, g, b):
    mu = jnp.mean(x, axis=-1, keepdims=True)
    xc = x - mu
    var = jnp.mean(xc * xc, axis=-1, keepdims=True)
    return xc * lax.rsqrt(var + LN_EPS) * g + b


def _gelu_tanh(x):
    return 0.5 * x * (1.0 + jnp.tanh(0.7978845608028654 * (x + 0.044715 * (x * x * x))))


def _split_dot(v, m):
    hi = v.astype(BF16)
    lo = (v - hi.astype(F32)).astype(BF16)
    return _dot(hi, m) + _dot(lo, m)


def _params(*sem):
    return pltpu.CompilerParams(dimension_semantics=sem, vmem_limit_bytes=VMEM_LIMIT)


def _full(shape):
    nd = len(shape)
    return pl.BlockSpec(shape, lambda *_: (0,) * nd)


def _inproj_kernel(apply_ln, x_ref, eg_ref, eb_ref, wnat_ref, wfb_ref, wcq_ref, wckv_ref, wkr_ref,
                   wsg_ref, qg_ref, wa_ref, wb_ref, kvg_ref, wkn_ref, wvt_ref, e_ref, ca_ref, cb_ref,
                   tk_ref, gmat_ref, slg_ref, slb_ref, ws_ref, sb_ref, *out_refs):
    if apply_ln:
        xn_ref, nat_ref, fb_ref, q_ref, k_ref, vt_ref, od_ref = out_refs
    else:
        nat_ref, fb_ref, q_ref, k_ref, vt_ref, od_ref = out_refs
    x = x_ref[...]
    if apply_ln:
        x = _layer_norm_rows(x, eg_ref[...], eb_ref[...])
        xn_ref[...] = x
    xb = x.astype(BF16)
    tm = x.shape[0]

    nat = _dot(xb, wnat_ref[...])
    nat_ref[:, :NA_WIDTH] = (nat[:, :NA_WIDTH] * (HEAD_DIM ** -0.5)).astype(BF16)
    nat_ref[:, NA_WIDTH:] = nat[:, NA_WIDTH:].astype(BF16)

    fb_ref[...] = _dot(xb, wfb_ref[...])

    cq = _dot(xb, wcq_ref[...])
    cqn = (cq * lax.rsqrt(jnp.mean(cq * cq, axis=-1, keepdims=True) + RMS_EPS) * qg_ref[...]).astype(BF16)
    qa = _dot(cqn, wa_ref[...])
    qb = _dot(cqn, wb_ref[...])
    ca = ca_ref[...]
    cb = cb_ref[...]
    for h in range(MLA_HEADS):
        sl = slice(h * MLA_PAD, (h + 1) * MLA_PAD)
        q_ref[:, sl] = (qa[:, sl] * ca + qb[:, sl] * cb).astype(BF16)

    ckv = _dot(xb, wckv_ref[...])
    ckvn = (ckv * lax.rsqrt(jnp.mean(ckv * ckv, axis=-1, keepdims=True) + RMS_EPS) * kvg_ref[...]).astype(BF16)
    kr = _dot(xb, wkr_ref[...]) * tk_ref[...]
    kr = kr + pltpu.roll(kr, LANES - MLA_ROPE, axis=1)
    k_ref[...] = (_dot(ckvn, wkn_ref[...]) + _dot(kr.astype(BF16), e_ref[...])).astype(BF16)
    vt = _dot_nt(wvt_ref[...], ckvn)
    tkc = vt_ref.shape[2]
    for c in range(tm // tkc):
        vt_ref[c] = vt[:, c * tkc:(c + 1) * tkc].astype(BF16)

    sg = _gelu_tanh(_dot(xb, wsg_ref[...]))
    u = sg[:, :SGU_WIDTH]
    v = sg[:, SGU_WIDTH:]
    gmat = gmat_ref[...]
    mu = _split_dot(v, gmat)
    vc = v - mu
    var = _split_dot(vc * vc, gmat)
    vn = (vc * lax.rsqrt(var + LN_EPS) * slg_ref[...] + slb_ref[...]).astype(BF16)
    head = lax.broadcasted_iota(jnp.int32, (1, SGU_WIDTH), 1) // SGU_HEAD_DIM
    zero = jnp.zeros((), BF16)
    for c in range(tm // SGU_CHUNK):
        rows = slice(c * SGU_CHUNK, (c + 1) * SGU_CHUNK)
        vch = vn[rows]
        mixed = sb_ref[...]
        for g in range(SGU_HEADS):
            mixed = mixed + _dot(ws_ref[g], jnp.where(head == g, vch, zero))
        od_ref[rows, :] = (u[rows] * mixed).astype(BF16)


def _inproj(x, lp, tabs, apply_ln, T):
    n = x.shape[0]
    tm = TM_IN
    nt = T // tm
    row = lambda i: (i, 0)
    tab = lambda i: (i % nt, 0)
    weights = [lp['eg'], lp['eb'], lp['w_nat'], lp['w_fb'], lp['w_cq'], lp['w_ckv'], lp['w_kr'], lp['w_sg'],
               lp['q_g'], lp['w_a'], lp['w_b'], lp['kv_g'], lp['w_kn'], lp['w_vt'], lp['e_place']]
    tables = [tabs['ca'], tabs['cb'], tabs['tk']]
    tail = [lp['gmat'], lp['sl_g'], lp['sl_b'], lp['w_s'], lp['s_b']]
    in_specs = ([pl.BlockSpec((tm, D_MODEL), row)] + [_full(w.shape) for w in weights]
                + [pl.BlockSpec((tm, LANES), tab) for _ in tables] + [_full(w.shape) for w in tail])
    out_shape = [jax.ShapeDtypeStruct((n, 3 * NA_WIDTH), BF16),
                 jax.ShapeDtypeStruct((n, FN_WIDTH), F32),
                 jax.ShapeDtypeStruct((n, MLA_HEADS * MLA_PAD), BF16),
                 jax.ShapeDtypeStruct((n, MLA_HEADS * MLA_PAD), BF16),
                 jax.ShapeDtypeStruct((n // MLA_TK, MLA_WIDTH, MLA_TK), BF16),
                 jax.ShapeDtypeStruct((n, SGU_WIDTH), BF16)]
    out_specs = [pl.BlockSpec((tm, 3 * NA_WIDTH), row),
                 pl.BlockSpec((tm, FN_WIDTH), row),
                 pl.BlockSpec((tm, MLA_HEADS * MLA_PAD), row),
                 pl.BlockSpec((tm, MLA_HEADS * MLA_PAD), row),
                 pl.BlockSpec((tm // MLA_TK, MLA_WIDTH, MLA_TK), lambda i: (i, 0, 0)),
                 pl.BlockSpec((tm, SGU_WIDTH), row)]
    if apply_ln:
        out_shape = [jax.ShapeDtypeStruct((n, D_MODEL), F32)] + out_shape
        out_specs = [pl.BlockSpec((tm, D_MODEL), row)] + out_specs
    return pl.pallas_call(
        functools.partial(_inproj_kernel, apply_ln),
        grid=(n // tm,),
        in_specs=in_specs,
        out_specs=out_specs,
        out_shape=out_shape,
        compiler_params=_params("parallel"),
        name="inproj_ln" if apply_ln else "inproj",
    )(x, *weights, *tables, *tail)


def _natten_kernel(q_ref, k0_ref, k1_ref, k2_ref, v0_ref, v1_ref, v2_ref, bias_ref, o_ref):
    q = q_ref[...]
    k = jnp.concatenate([k0_ref[...], k1_ref[...], k2_ref[...]], axis=0)
    v = jnp.concatenate([v0_ref[...], v1_ref[...], v2_ref[...]], axis=0)
    head = lax.broadcasted_iota(jnp.int32, (1, NA_WIDTH), 1) // HEAD_DIM
    zero = jnp.zeros((), BF16)
    acc = jnp.zeros((NA_QBLK, NA_WIDTH), F32)
    for h in range(NA_HEADS):
        sel = head == h
        s = _dot_nt(jnp.where(sel, q, zero), k) + bias_ref[h]
        m = jnp.max(s, axis=-1, keepdims=True)
        p = jnp.exp(s - m)
        l = jnp.sum(p, axis=-1, keepdims=True)
        acc = acc + _dot(p.astype(BF16), jnp.where(sel, v, zero)) * (1.0 / l)
    o_ref[...] = acc.astype(BF16)


def _natten(nat, bias, B, T):
    n = nat.shape[0]
    nblk = T // NA_QBLK

    def kv_map(d, col):
        def f(b, j):
            base = jnp.clip(j - 1, 0, nblk - NA_KBLKS)
            return (b * nblk + base + d, col)
        return f

    def bias_map(b, j):
        typ = jnp.where(j == 0, 0, jnp.where(j == nblk - 1, 2, 1))
        return (typ, 0, 0, 0)

    blk = (NA_QBLK, NA_WIDTH)
    in_specs = ([pl.BlockSpec(blk, lambda b, j: (b * nblk + j, 0))]
                + [pl.BlockSpec(blk, kv_map(d, 1)) for d in range(NA_KBLKS)]
                + [pl.BlockSpec(blk, kv_map(d, 2)) for d in range(NA_KBLKS)]
                + [pl.BlockSpec((None, NA_HEADS, NA_QBLK, NA_KBLKS * NA_QBLK), bias_map)])
    return pl.pallas_call(
        _natten_kernel,
        grid=(B, nblk),
        in_specs=in_specs,
        out_specs=pl.BlockSpec(blk, lambda b, j: (b * nblk + j, 0)),
        out_shape=jax.ShapeDtypeStruct((n, NA_WIDTH), BF16),
        compiler_params=_params("parallel", "arbitrary"),
        name="natten",
    )(nat, nat, nat, nat, nat, nat, nat, bias)


def _fft1_kernel(x_ref, m_ref, a_ref):
    kk = m_ref.shape[0]
    n1 = x_ref.shape[0]
    for i in range(kk):
        res = _dot_hi(m_ref[i], x_ref[:, i * FN_WIDTH:(i + 1) * FN_WIDTH])
        a_ref[i, :, :FN_WIDTH] = res[:n1]
        a_ref[i, :, FN_WIDTH:] = res[n1:]


def _fft2_kernel(a_ref, cs_ref, bc_ref, bs_ref, y_ref):
    n2 = a_ref.shape[0]
    kk = a_ref.shape[1] // (2 * FN_WIDTH)
    r = _dot_hi(cs_ref[...], a_ref[...])
    bc = bc_ref[...]
    bs = bs_ref[...]
    for i in range(kk):
        re = slice(i * 2 * FN_WIDTH, i * 2 * FN_WIDTH + FN_WIDTH)
        im = slice(i * 2 * FN_WIDTH + FN_WIDTH, (i + 1) * 2 * FN_WIDTH)
        gr = r[:n2, re] + r[n2:, im]
        gi = r[:n2, im] - r[n2:, re]
        y_ref[:, i * FN_WIDTH:(i + 1) * FN_WIDTH] = _dot_hi(gr, bc) + _dot_hi(gi, bs)


def _fourier(fb, ft, B, T):
    n1, n2 = ft['n1'], ft['n2']
    k1 = 8
    x3 = fb.reshape(B, n1, n2 * FN_WIDTH)
    a = pl.pallas_call(
        _fft1_kernel,
        grid=(B, n2 // k1),
        in_specs=[pl.BlockSpec((None, n1, k1 * FN_WIDTH), lambda b, j: (b, 0, j)),
                  pl.BlockSpec((k1, 2 * n1, n1), lambda b, j: (j, 0, 0))],
        out_specs=pl.BlockSpec((None, k1, n1, 2 * FN_WIDTH), lambda b, j: (b, j, 0, 0)),
        out_shape=jax.ShapeDtypeStruct((B, n2, n1, 2 * FN_WIDTH), F32),
        compiler_params=_params("parallel", "arbitrary"),
        name="fft_stage1",
    )(x3, ft['m1'])
    k2 = 8
    a3 = a.reshape(B, n2, n1 * 2 * FN_WIDTH)
    y = pl.pallas_call(
        _fft2_kernel,
        grid=(B, n1 // k2),
        in_specs=[pl.BlockSpec((None, n2, k2 * 2 * FN_WIDTH), lambda b, j: (b, 0, j)),
                  _full(ft['cs2'].shape), _full(ft['bc'].shape), _full(ft['bs'].shape)],
        out_specs=pl.BlockSpec((None, n2, k2 * FN_WIDTH), lambda b, j: (b, 0, j)),
        out_shape=jax.ShapeDtypeStruct((B, n2, n1 * FN_WIDTH), F32),
        compiler_params=_params("parallel", "arbitrary"),
        name="fft_stage2",
    )(a3, ft['cs2'], ft['bc'], ft['bs'])
    return y.reshape(B * T, FN_WIDTH)


def _mla_kernel(q_ref, k_ref, vt_ref, o_ref):
    nk, _, tk = vt_ref.shape
    tq = q_ref.shape[0]
    q = q_ref[...]

    def body(j, carry):
        m, l, acc = carry
        kj = k_ref[pl.ds(pl.multiple_of(j * tk, tk), tk), :]
        st = _dot_nt(kj, q)
        m_new = jnp.maximum(m, jnp.max(st, axis=0, keepdims=True))
        alpha = jnp.exp2(m - m_new)
        p = jnp.exp2(st - m_new)
        l = alpha * l + jnp.sum(p, axis=0, keepdims=True)
        acc = alpha * acc + _dot(vt_ref[j], p.astype(BF16))
        return m_new, l, acc

    init = (jnp.full((1, tq), -1e30, F32), jnp.zeros((1, tq), F32), jnp.zeros((MLA_V, tq), F32))
    _, l, acc = lax.fori_loop(0, nk, body, init)
    o_ref[...] = acc * (1.0 / l)


def _mla(q, k, vt, B, T):
    n = q.shape[0]
    nq = T // MLA_TQ
    nk = T // MLA_TK
    return pl.pallas_call(
        _mla_kernel,
        grid=(B, MLA_HEADS, nq),
        in_specs=[pl.BlockSpec((MLA_TQ, MLA_PAD), lambda b, h, i: (b * nq + i, h)),
                  pl.BlockSpec((T, MLA_PAD), lambda b, h, i: (b, h)),
                  pl.BlockSpec((nk, MLA_V, MLA_TK), lambda b, h, i: (b, h, 0))],
        out_specs=pl.BlockSpec((MLA_V, MLA_TQ), lambda b, h, i: (h, b * nq + i)),
        out_shape=jax.ShapeDtypeStruct((MLA_WIDTH, n), F32),
        compiler_params=_params("parallel", "parallel", "arbitrary"),
        name="mla",
    )(q, k, vt)


def _outproj_kernel(oa_ref, ob_ref, oct_ref, od_ref, x_ref, w_ref, g_ref, b_ref, o_ref):
    w = NA_WIDTH
    oc = oct_ref[...].T.astype(BF16)
    y = (_dot(oa_ref[...], w_ref[0:w]) + _dot(ob_ref[...].astype(BF16), w_ref[w:2 * w])
         + _dot(oc, w_ref[2 * w:3 * w]) + _dot(od_ref[...], w_ref[3 * w:4 * w]))
    o_ref[...] = _layer_norm_rows(ALPHA * x_ref[...] + y, g_ref[...], b_ref[...])


def _outproj(oa, ob, oct, od, x, lp):
    n = x.shape[0]
    tm = TM_OUT
    row = lambda i: (i, 0)
    return pl.pallas_call(
        _outproj_kernel,
        grid=(n // tm,),
        in_specs=[pl.BlockSpec((tm, NA_WIDTH), row), pl.BlockSpec((tm, FN_WIDTH), row),
                  pl.BlockSpec((MLA_WIDTH, tm), lambda i: (0, i)), pl.BlockSpec((tm, SGU_WIDTH), row),
                  pl.BlockSpec((tm, D_MODEL), row), _full(lp['w_out'].shape),
                  _full(lp['ln1_g'].shape), _full(lp['ln1_b'].shape)],
        out_specs=pl.BlockSpec((tm, D_MODEL), row),
        out_shape=jax.ShapeDtypeStruct((n, D_MODEL), F32),
        compiler_params=_params("parallel"),
        name="outproj",
    )(oa, ob, oct, od, x, lp['w_out'], lp['ln1_g'], lp['ln1_b'])


def _ffn_kernel(nt, x_ref, xp_ref, xn_ref, wg_ref, wv_ref, cwg_ref, cwv_ref, cbg_ref, cbv_ref, wd_ref,
                g_ref, b_ref, o_ref, xe_ref, hg_ref, hv_ref, acc_ref):
    i = pl.program_id(0)
    j = pl.program_id(1)
    tm = x_ref.shape[0]

    @pl.when(j == 0)
    def _():
        first = (i % nt) == 0
        last = (i % nt) == nt - 1
        xe_ref[0:HALO] = jnp.where(first, 0.0, xp_ref[...]).astype(BF16)
        xe_ref[HALO:HALO + tm] = x_ref[...].astype(BF16)
        xe_ref[HALO + tm:] = jnp.where(last, 0.0, xn_ref[...]).astype(BF16)
        acc_ref[...] = jnp.zeros_like(acc_ref)

    xe = xe_ref[...]
    hg_ref[...] = _dot(xe, wg_ref[...])
    hv_ref[...] = _dot(xe, wv_ref[...])

    def conv(h_ref, cw_ref, cb_ref):
        return (h_ref[pl.ds(HALO - 1, tm), :] * cw_ref[0:1, :] + h_ref[pl.ds(HALO, tm), :] * cw_ref[1:2, :]
                + h_ref[pl.ds(HALO + 1, tm), :] * cw_ref[2:3, :] + cb_ref[...])

    act = _gelu_tanh(conv(hg_ref, cwg_ref, cbg_ref)) * conv(hv_ref, cwv_ref, cbv_ref)
    acc_ref[...] += _dot(act.astype(BF16), wd_ref[...])

    @pl.when(j == pl.num_programs(1) - 1)
    def _():
        o_ref[...] = _layer_norm_rows(ALPHA * x_ref[...] + acc_ref[...], g_ref[...], b_ref[...])


def _ffn(x, lp, T):
    n = x.shape[0]
    tm = TM_FFN
    nt = T // tm
    cf = FF_CHUNK
    nch = D_FF // cf
    hb = tm // HALO
    nhb = n // HALO
    row = lambda i, j: (i, 0)
    return pl.pallas_call(
        functools.partial(_ffn_kernel, nt),
        grid=(n // tm, nch),
        in_specs=[pl.BlockSpec((tm, D_MODEL), row),
                  pl.BlockSpec((HALO, D_MODEL), lambda i, j: (jnp.maximum(i * hb - 1, 0), 0)),
                  pl.BlockSpec((HALO, D_MODEL), lambda i, j: (jnp.minimum((i + 1) * hb, nhb - 1), 0)),
                  pl.BlockSpec((D_MODEL, cf), lambda i, j: (0, j)),
                  pl.BlockSpec((D_MODEL, cf), lambda i, j: (0, j + nch)),
                  pl.BlockSpec((3, cf), lambda i, j: (0, j)),
                  pl.BlockSpec((3, cf), lambda i, j: (0, j + nch)),
                  pl.BlockSpec((1, cf), lambda i, j: (0, j)),
                  pl.BlockSpec((1, cf), lambda i, j: (0, j + nch)),
                  pl.BlockSpec((cf, D_MODEL), lambda i, j: (j, 0)),
                  pl.BlockSpec((1, D_MODEL), lambda i, j: (0, 0)),
                  pl.BlockSpec((1, D_MODEL), lambda i, j: (0, 0))],
        out_specs=pl.BlockSpec((tm, D_MODEL), row),
        out_shape=jax.ShapeDtypeStruct((n, D_MODEL), F32),
        scratch_shapes=[pltpu.VMEM((tm + 2 * HALO, D_MODEL), BF16),
                        pltpu.VMEM((tm + 2 * HALO, cf), F32),
                        pltpu.VMEM((tm + 2 * HALO, cf), F32),
                        pltpu.VMEM((tm, D_MODEL), F32)],
        compiler_params=_params("parallel", "arbitrary"),
        name="conv_ffn",
    )(x, x, x, lp['w_up'], lp['w_up'], lp['conv_w'], lp['conv_w'], lp['conv_b'], lp['conv_b'],
      lp['w_down'], lp['ln2_g'], lp['ln2_b'])


def _natten_bias(rpb):
    qi = np.arange(NA_QBLK)
    ki = np.arange(NA_KBLKS * NA_QBLK)
    qa, qc = qi // GRID_W, qi % GRID_W
    kr, kc = ki // GRID_W, ki % GRID_W
    cs = np.clip(qc - NA_COLS // 2, 0, GRID_W - NA_COLS)
    kinds = []
    for kind in range(3):
        q_row = (0, NA_QROWS, 2 * NA_QROWS)[kind] + qa
        r_start = (np.zeros_like(qa), qa, np.full_like(qa, NA_QROWS))[kind]
        dr = kr[None, :] - r_start[:, None]
        dc = kc[None, :] - cs[:, None]
        valid = (dr >= 0) & (dr < NA_ROWS) & (dc >= 0) & (dc < NA_COLS)
        ri = np.clip(kr[None, :] - q_row[:, None] + NA_ROWS - 1, 0, 2 * NA_ROWS - 2)
        ci = np.clip(kc[None, :] - qc[:, None] + NA_COLS - 1, 0, 2 * NA_COLS - 2)
        kinds.append(jnp.where(valid[None], rpb[:, ri, ci], -1e30))
    return jnp.stack(kinds).astype(F32)


def _prep_layer(l, emb_ln_g, emb_ln_b, w_in, na_rpb, mla_q_g, w_uq, mla_kv_g, w_ukv, sgu_ln_g, sgu_ln_b,
                sgu_w, sgu_b, w_out, ln1_g, ln1_b, w_up, conv_w, conv_b, w_down, ln2_g, ln2_b):
    wi = w_in[l]
    o_fb = 3 * NA_WIDTH
    o_cq = o_fb + FN_WIDTH
    o_ckv = o_cq + MLA_Q_LORA
    o_kr = o_ckv + MLA_KV_LORA
    o_sg = o_kr + MLA_ROPE
    half = MLA_ROPE // 2
    swap = np.concatenate([np.arange(half, MLA_ROPE), np.arange(half)])
    w_kr = wi[:, o_kr:o_sg]
    w_kr2 = jnp.concatenate([w_kr, w_kr[:, swap], jnp.zeros((D_MODEL, LANES - 2 * MLA_ROPE), F32)], axis=1)

    uq = w_uq[l]
    pad = jnp.zeros((MLA_Q_LORA, MLA_HEADS, MLA_PAD - MLA_NOPE - MLA_ROPE), F32)
    w_a = jnp.concatenate([uq, pad], axis=2).reshape(MLA_Q_LORA, MLA_HEADS * MLA_PAD)
    w_b = jnp.concatenate([jnp.zeros((MLA_Q_LORA, MLA_HEADS, MLA_NOPE), F32),
                           uq[:, :, MLA_NOPE:][:, :, swap], pad], axis=2).reshape(MLA_Q_LORA, MLA_HEADS * MLA_PAD)
    ukv = w_ukv[l]
    w_kn = jnp.concatenate([ukv[:, :, :MLA_NOPE],
                            jnp.zeros((MLA_KV_LORA, MLA_HEADS, MLA_PAD - MLA_NOPE), F32)],
                           axis=2).reshape(MLA_KV_LORA, MLA_HEADS * MLA_PAD)
    w_vt = ukv[:, :, MLA_NOPE:].reshape(MLA_KV_LORA, MLA_WIDTH).T
    e_place = np.zeros((LANES, MLA_HEADS * MLA_PAD), np.float32)
    for h in range(MLA_HEADS):
        e_place[np.arange(MLA_ROPE), h * MLA_PAD + MLA_NOPE + np.arange(MLA_ROPE)] = 1.0
    gmat = np.kron(np.eye(SGU_HEADS), np.full((SGU_HEAD_DIM, SGU_HEAD_DIM), 1.0 / SGU_HEAD_DIM)).astype(np.float32)
    s_b = jnp.repeat(sgu_b[l].T, SGU_HEAD_DIM, axis=1)
    r1 = lambda a: a.reshape(1, -1).astype(F32)
    return dict(
        eg=r1(emb_ln_g), eb=r1(emb_ln_b),
        w_nat=wi[:, :o_fb].astype(BF16), w_fb=wi[:, o_fb:o_cq].astype(BF16),
        w_cq=wi[:, o_cq:o_ckv].astype(BF16), w_ckv=wi[:, o_ckv:o_kr].astype(BF16),
        w_kr=w_kr2.astype(BF16), w_sg=wi[:, o_sg:].astype(BF16),
        q_g=r1(mla_q_g[l]), w_a=w_a.astype(BF16), w_b=w_b.astype(BF16),
        kv_g=r1(mla_kv_g[l]), w_kn=w_kn.astype(BF16), w_vt=w_vt.astype(BF16),
        e_place=jnp.asarray(e_place, BF16), gmat=jnp.asarray(gmat, BF16),
        sl_g=r1(sgu_ln_g[l]), sl_b=r1(sgu_ln_b[l]), w_s=sgu_w[l].astype(BF16), s_b=s_b.astype(F32),
        na_bias=_natten_bias(na_rpb[l]),
        w_out=w_out[l].astype(BF16), ln1_g=r1(ln1_g[l]), ln1_b=r1(ln1_b[l]),
        w_up=w_up[l].astype(BF16), conv_w=conv_w[l].astype(F32), conv_b=r1(conv_b[l]),
        w_down=w_down[l].astype(BF16), ln2_g=r1(ln2_g[l]), ln2_b=r1(ln2_b[l]),
    )


def _rope_tables(T):
    inv_freq = ROPE_THETA ** (-jnp.arange(0, MLA_ROPE, 2, dtype=F32) / MLA_ROPE)
    ang = jnp.arange(T, dtype=F32)[:, None] * inv_freq[None, :]
    cos, sin = jnp.cos(ang), jnp.sin(ang)
    cos2 = jnp.concatenate([cos, cos], axis=1)
    sin2 = jnp.concatenate([-sin, sin], axis=1)
    scale = (MLA_NOPE + MLA_ROPE) ** -0.5 * math.log2(math.e)
    zq = jnp.zeros((T, MLA_PAD - MLA_NOPE - MLA_ROPE), F32)
    ca = jnp.concatenate([jnp.full((T, MLA_NOPE), scale, F32), scale * cos2, zq], axis=1)
    cb = jnp.concatenate([jnp.zeros((T, MLA_NOPE), F32), scale * sin2, zq], axis=1)
    tk = jnp.concatenate([cos2, sin2, jnp.zeros((T, LANES - 2 * MLA_ROPE), F32)], axis=1)
    return dict(ca=ca, cb=cb, tk=tk)


def _dft_tables(T):
    n1 = 1 << (int(math.log2(T)) // 2)
    n2 = T // n1
    two_pi = 2.0 * math.pi

    def cs(num, den):
        ang = (num % den).astype(F32) * (two_pi / den)
        return jnp.cos(ang), jnp.sin(ang)

    t2 = jnp.arange(n2, dtype=jnp.int32)[:, None, None]
    k1 = jnp.arange(n1, dtype=jnp.int32)[None, :, None]
    t1 = jnp.arange(n1, dtype=jnp.int32)[None, None, :]
    c, s = cs(k1 * (t1 * n2 + t2), T)
    m1 = jnp.concatenate([c, -s], axis=1)
    k2 = jnp.arange(n2, dtype=jnp.int32)
    c2, s2 = cs(k2[:, None] * k2[None, :], n2)
    cs2 = jnp.concatenate([c2, s2], axis=0)
    cg = jnp.arange(FN_GROUP_DIM, dtype=jnp.int32)
    cc, sc = cs(cg[:, None] * cg[None, :], FN_GROUP_DIM)
    norm = 1.0 / math.sqrt(T * FN_GROUP_DIM)
    eye = jnp.eye(FN_GROUPS, dtype=F32)
    return dict(n1=n1, n2=n2, m1=m1, cs2=cs2, bc=jnp.kron(eye, cc) * norm, bs=jnp.kron(eye, sc) * norm)


def _trunk(x, layers):
    B, T, _ = x.shape
    tabs = _rope_tables(T)
    ft = _dft_tables(T)
    h = x.reshape(B * T, D_MODEL)
    for l, lp in enumerate(layers):
        outs = _inproj(h, lp, tabs, l == 0, T)
        if l == 0:
            h = outs[0]
            outs = outs[1:]
        nat, fb, q, k, vt, od = outs
        oa = _natten(nat, lp['na_bias'], B, T)
        ob = _fourier(fb, ft, B, T)
        oct = _mla(q, k, vt, B, T)
        h1 = _outproj(oa, ob, oct, od, h, lp)
        h = _ffn(h1, lp, T)
    return h.reshape(B, T, D_MODEL)


def kernel(x_prompt, x_sample, emb_ln_g, emb_ln_b, w_in, na_rpb, mla_q_g, w_uq, mla_kv_g, w_ukv, sgu_ln_g,
           sgu_ln_b, sgu_w, sgu_b, w_out, ln1_g, ln1_b, w_up, conv_w, conv_b, w_down, ln2_g, ln2_b):
    layers = [_prep_layer(l, emb_ln_g, emb_ln_b, w_in, na_rpb, mla_q_g, w_uq, mla_kv_g, w_ukv, sgu_ln_g,
                          sgu_ln_b, sgu_w, sgu_b, w_out, ln1_g, ln1_b, w_up, conv_w, conv_b, w_down,
                          ln2_g, ln2_b) for l in range(DEPTH)]
    return (_trunk(x_prompt, layers), _trunk(x_sample, layers))
```

```python
import functools
import math

import numpy as np
import jax
import jax.numpy as jnp
from jax import lax
from jax.experimental import pallas as pl
from jax.experimental.pallas import tpu as pltpu

F32 = jnp.float32
BF16 = jnp.bfloat16

D_MODEL = 1024
DEPTH = 2
GRID_W = 64
HEAD_DIM = 64
NA_HEADS = 4
NA_ROWS = 8
NA_COLS = 16
NA_WIDTH = NA_HEADS * HEAD_DIM
FN_GROUPS = 4
FN_GROUP_DIM = 64
FN_WIDTH = FN_GROUPS * FN_GROUP_DIM
MLA_HEADS = 4
MLA_Q_LORA = 256
MLA_KV_LORA = 128
MLA_NOPE = 64
MLA_ROPE = 32
MLA_V = 64
MLA_WIDTH = MLA_HEADS * MLA_V
ROPE_THETA = 10000.0
SGU_HEADS = 4
SGU_HEAD_DIM = 64
SGU_WIDTH = SGU_HEADS * SGU_HEAD_DIM
SGU_CHUNK = 128
D_FF = 2816
ALPHA = (2 * DEPTH) ** 0.25
LN_EPS = 1e-5
RMS_EPS = 1e-6

LANES = 128
SUBLANES = 8
VMEM_LIMIT = 56 * 1024 * 1024

TM_IN = 512
TM_OUT = 512
TM_FFN = 512
FF_CHUNK = 256
NA_QROWS = 4
NA_QBLK = NA_QROWS * GRID_W
NA_KBLKS = 3
MLA_PAD = 128
MLA_TQ = 512
MLA_TK = 512
MLA_ONES = 16
HALO = SUBLANES

HI = lax.Precision.HIGHEST
NT_DIMS = (((1,), (1,)), ((), ()))


def _dot(a, b):
    return jnp.dot(a, b, preferred_element_type=F32)


def _dot_nt(a, b):
    return lax.dot_general(a, b, NT_DIMS, preferred_element_type=F32)


def _dot_hi(a, b):
    return jnp.dot(a, b, precision=HI, preferred_element_type=F32)


def _layer_norm_rows(x, g, b):
    mu = jnp.mean(x, axis=-1, keepdims=True)
    xc = x - mu
    var = jnp.mean(xc * xc, axis=-1, keepdims=True)
    return xc * lax.rsqrt(var + LN_EPS) * g + b


def _gelu_tanh(x):
    return 0.5 * x * (1.0 + jnp.tanh(0.7978845608028654 * (x + 0.044715 * (x * x * x))))


def _split_dot(v, m):
    hi = v.astype(BF16)
    lo = (v - hi.astype(F32)).astype(BF16)
    return _dot(hi, m) + _dot(lo, m)


def _params(*sem):
    return pltpu.CompilerParams(dimension_semantics=sem, vmem_limit_bytes=VMEM_LIMIT)


def _full(shape):
    nd = len(shape)
    return pl.BlockSpec(shape, lambda *_: (0,) * nd)


def _inproj_kernel(apply_ln, x_ref, eg_ref, eb_ref, wnat_ref, wfb_ref, wcq_ref, wckv_ref, wkr_ref,
                   wsg_ref, qg_ref, wa_ref, wb_ref, kvg_ref, wkn_ref, wvt_ref, e_ref, ca_ref, cb_ref,
                   tk_ref, gmat_ref, slg_ref, slb_ref, ws_ref, sb_ref, *out_refs):
    if apply_ln:
        xn_ref, nat_ref, fb_ref, q_ref, k_ref, vt_ref, od_ref = out_refs
    else:
        nat_ref, fb_ref, q_ref, k_ref, vt_ref, od_ref = out_refs
    x = x_ref[...]
    if apply_ln:
        x = _layer_norm_rows(x, eg_ref[...], eb_ref[...])
        xn_ref[...] = x
    xb = x.astype(BF16)
    tm = x.shape[0]

    nat = _dot(xb, wnat_ref[...])
    nat_ref[:, :NA_WIDTH] = (nat[:, :NA_WIDTH] * (HEAD_DIM ** -0.5)).astype(BF16)
    nat_ref[:, NA_WIDTH:] = nat[:, NA_WIDTH:].astype(BF16)

    fb_ref[...] = _dot(xb, wfb_ref[...])

    cq = _dot(xb, wcq_ref[...])
    cqn = (cq * lax.rsqrt(jnp.mean(cq * cq, axis=-1, keepdims=True) + RMS_EPS) * qg_ref[...]).astype(BF16)
    qa = _dot(cqn, wa_ref[...])
    qb = _dot(cqn, wb_ref[...])
    ca = ca_ref[...]
    cb = cb_ref[...]
    for h in range(MLA_HEADS):
        sl = slice(h * MLA_PAD, (h + 1) * MLA_PAD)
        q_ref[:, sl] = (qa[:, sl] * ca + qb[:, sl] * cb).astype(BF16)

    ckv = _dot(xb, wckv_ref[...])
    ckvn = (ckv * lax.rsqrt(jnp.mean(ckv * ckv, axis=-1, keepdims=True) + RMS_EPS) * kvg_ref[...]).astype(BF16)
    kr = _dot(xb, wkr_ref[...]) * tk_ref[...]
    kr = kr + pltpu.roll(kr, LANES - MLA_ROPE, axis=1)
    k_ref[...] = (_dot(ckvn, wkn_ref[...]) + _dot(kr.astype(BF16), e_ref[...])).astype(BF16)
    vt = _dot_nt(wvt_ref[...], ckvn)
    tkc = vt_ref.shape[2]
    for c in range(tm // tkc):
        vt_ref[c] = vt[:, c * tkc:(c + 1) * tkc].astype(BF16)

    sg = _gelu_tanh(_dot(xb, wsg_ref[...]))
    u = sg[:, :SGU_WIDTH]
    v = sg[:, SGU_WIDTH:]
    gmat = gmat_ref[...]
    mu = _split_dot(v, gmat)
    vc = v - mu
    var = _split_dot(vc * vc, gmat)
    vn = (vc * lax.rsqrt(var + LN_EPS) * slg_ref[...] + slb_ref[...]).astype(BF16)
    head = lax.broadcasted_iota(jnp.int32, (1, SGU_WIDTH), 1) // SGU_HEAD_DIM
    zero = jnp.zeros((), BF16)
    for c in range(tm // SGU_CHUNK):
        rows = slice(c * SGU_CHUNK, (c + 1) * SGU_CHUNK)
        vch = vn[rows]
        mixed = sb_ref[...]
        for g in range(SGU_HEADS):
            mixed = mixed + _dot(ws_ref[g], jnp.where(head == g, vch, zero))
        od_ref[rows, :] = (u[rows] * mixed).astype(BF16)


def _inproj(x, lp, tabs, apply_ln, T):
    n = x.shape[0]
    tm = TM_IN
    nt = T // tm
    row = lambda i: (i, 0)
    tab = lambda i: (i % nt, 0)
    weights = [lp['eg'], lp['eb'], lp['w_nat'], lp['w_fb'], lp['w_cq'], lp['w_ckv'], lp['w_kr'], lp['w_sg'],
               lp['q_g'], lp['w_a'], lp['w_b'], lp['kv_g'], lp['w_kn'], lp['w_vt'], lp['e_place']]
    tables = [tabs['ca'], tabs['cb'], tabs['tk']]
    tail = [lp['gmat'], lp['sl_g'], lp['sl_b'], lp['w_s'], lp['s_b']]
    in_specs = ([pl.BlockSpec((tm, D_MODEL), row)] + [_full(w.shape) for w in weights]
                + [pl.BlockSpec((tm, LANES), tab) for _ in tables] + [_full(w.shape) for w in tail])
    out_shape = [jax.ShapeDtypeStruct((n, 3 * NA_WIDTH), BF16),
                 jax.ShapeDtypeStruct((n, FN_WIDTH), F32),
                 jax.ShapeDtypeStruct((n, MLA_HEADS * MLA_PAD), BF16),
                 jax.ShapeDtypeStruct((n, MLA_HEADS * MLA_PAD), BF16),
                 jax.ShapeDtypeStruct((n // MLA_TK, MLA_WIDTH, MLA_TK), BF16),
                 jax.ShapeDtypeStruct((n, SGU_WIDTH), BF16)]
    out_specs = [pl.BlockSpec((tm, 3 * NA_WIDTH), row),
                 pl.BlockSpec((tm, FN_WIDTH), row),
                 pl.BlockSpec((tm, MLA_HEADS * MLA_PAD), row),
                 pl.BlockSpec((tm, MLA_HEADS * MLA_PAD), row),
                 pl.BlockSpec((tm // MLA_TK, MLA_WIDTH, MLA_TK), lambda i: (i, 0, 0)),
                 pl.BlockSpec((tm, SGU_WIDTH), row)]
    if apply_ln:
        out_shape = [jax.ShapeDtypeStruct((n, D_MODEL), F32)] + out_shape
        out_specs = [pl.BlockSpec((tm, D_MODEL), row)] + out_specs
    return pl.pallas_call(
        functools.partial(_inproj_kernel, apply_ln),
        grid=(n // tm,),
        in_specs=in_specs,
        out_specs=out_specs,
        out_shape=out_shape,
        compiler_params=_params("parallel"),
        name="inproj_ln" if apply_ln else "inproj",
    )(x, *weights, *tables, *tail)


def _natten_kernel(q_ref, k0_ref, k1_ref, k2_ref, v0_ref, v1_ref, v2_ref, bias_ref, o_ref):
    q = q_ref[...]
    k = jnp.concatenate([k0_ref[...], k1_ref[...], k2_ref[...]], axis=0)
    v = jnp.concatenate([v0_ref[...], v1_ref[...], v2_ref[...]], axis=0)
    head = lax.broadcasted_iota(jnp.int32, (1, NA_WIDTH), 1) // HEAD_DIM
    zero = jnp.zeros((), BF16)
    acc = jnp.zeros((NA_QBLK, NA_WIDTH), F32)
    for h in range(NA_HEADS):
        sel = head == h
        s = _dot_nt(jnp.where(sel, q, zero), k) + bias_ref[h]
        m = jnp.max(s, axis=-1, keepdims=True)
        p = jnp.exp(s - m)
        l = jnp.sum(p, axis=-1, keepdims=True)
        acc = acc + _dot(p.astype(BF16), jnp.where(sel, v, zero)) * (1.0 / l)
    o_ref[...] = acc.astype(BF16)


def _natten(nat, bias, B, T):
    n = nat.shape[0]
    nblk = T // NA_QBLK

    def kv_map(d, col):
        def f(b, j):
            base = jnp.clip(j - 1, 0, nblk - NA_KBLKS)
            return (b * nblk + base + d, col)
        return f

    def bias_map(b, j):
        typ = jnp.where(j == 0, 0, jnp.where(j == nblk - 1, 2, 1))
        return (typ, 0, 0, 0)

    blk = (NA_QBLK, NA_WIDTH)
    in_specs = ([pl.BlockSpec(blk, lambda b, j: (b * nblk + j, 0))]
                + [pl.BlockSpec(blk, kv_map(d, 1)) for d in range(NA_KBLKS)]
                + [pl.BlockSpec(blk, kv_map(d, 2)) for d in range(NA_KBLKS)]
                + [pl.BlockSpec((None, NA_HEADS, NA_QBLK, NA_KBLKS * NA_QBLK), bias_map)])
    return pl.pallas_call(
        _natten_kernel,
        grid=(B, nblk),
        in_specs=in_specs,
        out_specs=pl.BlockSpec(blk, lambda b, j: (b * nblk + j, 0)),
        out_shape=jax.ShapeDtypeStruct((n, NA_WIDTH), BF16),
        compiler_params=_params("parallel", "arbitrary"),
        name="natten",
    )(nat, nat, nat, nat, nat, nat, nat, bias)


def _fft1_kernel(x_ref, m_ref, a_ref):
    kk = m_ref.shape[0]
    n1 = x_ref.shape[0]
    for i in range(kk):
        res = _dot_hi(m_ref[i], x_ref[:, i * FN_WIDTH:(i + 1) * FN_WIDTH])
        a_ref[i, :, :FN_WIDTH] = res[:n1]
        a_ref[i, :, FN_WIDTH:] = res[n1:]


def _fft2_kernel(a_ref, cs_ref, bc_ref, bs_ref, y_ref):
    n2 = a_ref.shape[0]
    kk = a_ref.shape[1] // (2 * FN_WIDTH)
    r = _dot_hi(cs_ref[...], a_ref[...])
    bc = bc_ref[...]
    bs = bs_ref[...]
    for i in range(kk):
        re = slice(i * 2 * FN_WIDTH, i * 2 * FN_WIDTH + FN_WIDTH)
        im = slice(i * 2 * FN_WIDTH + FN_WIDTH, (i + 1) * 2 * FN_WIDTH)
        gr = r[:n2, re] + r[n2:, im]
        gi = r[:n2, im] - r[n2:, re]
        y_ref[:, i * FN_WIDTH:(i + 1) * FN_WIDTH] = _dot_hi(gr, bc) + _dot_hi(gi, bs)


def _fourier(fb, ft, B, T):
    n1, n2 = ft['n1'], ft['n2']
    k1 = 8
    x3 = fb.reshape(B, n1, n2 * FN_WIDTH)
    a = pl.pallas_call(
        _fft1_kernel,
        grid=(B, n2 // k1),
        in_specs=[pl.BlockSpec((None, n1, k1 * FN_WIDTH), lambda b, j: (b, 0, j)),
                  pl.BlockSpec((k1, 2 * n1, n1), lambda b, j: (j, 0, 0))],
        out_specs=pl.BlockSpec((None, k1, n1, 2 * FN_WIDTH), lambda b, j: (b, j, 0, 0)),
        out_shape=jax.ShapeDtypeStruct((B, n2, n1, 2 * FN_WIDTH), F32),
        compiler_params=_params("parallel", "arbitrary"),
        name="fft_stage1",
    )(x3, ft['m1'])
    k2 = 8
    a3 = a.reshape(B, n2, n1 * 2 * FN_WIDTH)
    y = pl.pallas_call(
        _fft2_kernel,
        grid=(B, n1 // k2),
        in_specs=[pl.BlockSpec((None, n2, k2 * 2 * FN_WIDTH), lambda b, j: (b, 0, j)),
                  _full(ft['cs2'].shape), _full(ft['bc'].shape), _full(ft['bs'].shape)],
        out_specs=pl.BlockSpec((None, n2, k2 * FN_WIDTH), lambda b, j: (b, 0, j)),
        out_shape=jax.ShapeDtypeStruct((B, n2, n1 * FN_WIDTH), F32),
        compiler_params=_params("parallel", "arbitrary"),
        name="fft_stage2",
    )(a3, ft['cs2'], ft['bc'], ft['bs'])
    return y.reshape(B * T, FN_WIDTH)


def _mla_kernel(q_ref, k_ref, vt_ref, o_ref, s_ref):
    nk, _, tk = vt_ref.shape
    tq = q_ref.shape[0]
    q = q_ref[...]
    ones = jnp.ones((MLA_ONES, tk), BF16)

    def scores(j):
        kj = k_ref[pl.ds(pl.multiple_of(j * tk, tk), tk), :]
        return _dot_nt(kj, q)

    def update(carry, j, slot):
        m, acc = carry
        st = s_ref[slot]
        m_new = jnp.maximum(m, jnp.max(st, axis=0, keepdims=True))
        alpha = jnp.exp2(m - m_new)
        p = jnp.exp2(st - m_new).astype(BF16)
        vte = jnp.concatenate([vt_ref[j], ones], axis=0)
        return m_new, alpha * acc + _dot(vte, p)

    s_ref[0] = scores(0)

    def body(jj, carry):
        j0 = 2 * jj
        s_ref[1] = scores(j0 + 1)
        carry = update(carry, j0, 0)
        s_ref[0] = scores(jnp.minimum(j0 + 2, nk - 1))
        return update(carry, j0 + 1, 1)

    init = (jnp.full((1, tq), -1e30, F32), jnp.zeros((MLA_V + MLA_ONES, tq), F32))
    _, acc = lax.fori_loop(0, nk // 2, body, init)
    o_ref[...] = acc[:MLA_V] * (1.0 / acc[MLA_V:MLA_V + 1])


def _mla(q, k, vt, B, T):
    n = q.shape[0]
    nq = T // MLA_TQ
    nk = T // MLA_TK
    assert nk % 2 == 0
    return pl.pallas_call(
        _mla_kernel,
        grid=(B, MLA_HEADS, nq),
        in_specs=[pl.BlockSpec((MLA_TQ, MLA_PAD), lambda b, h, i: (b * nq + i, h)),
                  pl.BlockSpec((T, MLA_PAD), lambda b, h, i: (b, h)),
                  pl.BlockSpec((nk, MLA_V, MLA_TK), lambda b, h, i: (b, h, 0))],
        out_specs=pl.BlockSpec((MLA_V, MLA_TQ), lambda b, h, i: (h, b * nq + i)),
        out_shape=jax.ShapeDtypeStruct((MLA_WIDTH, n), F32),
        scratch_shapes=[pltpu.VMEM((2, MLA_TK, MLA_TQ), F32)],
        compiler_params=_params("parallel", "parallel", "arbitrary"),
        name="mla",
    )(q, k, vt)


def _outproj_kernel(oa_ref, ob_ref, oct_ref, od_ref, x_ref, w_ref, g_ref, b_ref, o_ref):
    w = NA_WIDTH
    oc = oct_ref[...].T.astype(BF16)
    y = (_dot(oa_ref[...], w_ref[0:w]) + _dot(ob_ref[...].astype(BF16), w_ref[w:2 * w])
         + _dot(oc, w_ref[2 * w:3 * w]) + _dot(od_ref[...], w_ref[3 * w:4 * w]))
    o_ref[...] = _layer_norm_rows(ALPHA * x_ref[...] + y, g_ref[...], b_ref[...])


def _outproj(oa, ob, oct, od, x, lp):
    n = x.shape[0]
    tm = TM_OUT
    row = lambda i: (i, 0)
    return pl.pallas_call(
        _outproj_kernel,
        grid=(n // tm,),
        in_specs=[pl.BlockSpec((tm, NA_WIDTH), row), pl.BlockSpec((tm, FN_WIDTH), row),
                  pl.BlockSpec((MLA_WIDTH, tm), lambda i: (0, i)), pl.BlockSpec((tm, SGU_WIDTH), row),
                  pl.BlockSpec((tm, D_MODEL), row), _full(lp['w_out'].shape),
                  _full(lp['ln1_g'].shape), _full(lp['ln1_b'].shape)],
        out_specs=pl.BlockSpec((tm, D_MODEL), row),
        out_shape=jax.ShapeDtypeStruct((n, D_MODEL), F32),
        compiler_params=_params("parallel"),
        name="outproj",
    )(oa, ob, oct, od, x, lp['w_out'], lp['ln1_g'], lp['ln1_b'])


def _ffn_kernel(nt, x_ref, xp_ref, xn_ref, wup_ref, cw_ref, cb_ref, wd_ref, g_ref, b_ref, o_ref,
                xe_ref, hg_ref, hv_ref, act_ref):
    i = pl.program_id(0)
    tm = x_ref.shape[0]
    cf = hg_ref.shape[2]
    first = (i % nt) == 0
    last = (i % nt) == nt - 1
    xe_ref[0:HALO] = jnp.where(first, 0.0, xp_ref[...]).astype(BF16)
    xe_ref[HALO:HALO + tm] = x_ref[...].astype(BF16)
    xe_ref[HALO + tm:] = jnp.where(last, 0.0, xn_ref[...]).astype(BF16)
    xe = xe_ref[...]

    def conv(h_ref, cols):
        return (h_ref[pl.ds(HALO - 1, tm), :] * cw_ref[0:1, cols] + h_ref[pl.ds(HALO, tm), :] * cw_ref[1:2, cols]
                + h_ref[pl.ds(HALO + 1, tm), :] * cw_ref[2:3, cols] + cb_ref[:, cols])

    for c in range(D_FF // cf):
        slot = c % 2
        gate = slice(c * cf, (c + 1) * cf)
        val = slice(D_FF + c * cf, D_FF + (c + 1) * cf)
        hg_ref[slot] = _dot(xe, wup_ref[:, gate])
        hv_ref[slot] = _dot(xe, wup_ref[:, val])
        act = _gelu_tanh(conv(hg_ref.at[slot], gate)) * conv(hv_ref.at[slot], val)
        act_ref[:, gate] = act.astype(BF16)

    y = _dot(act_ref[...], wd_ref[...])
    o_ref[...] = _layer_norm_rows(ALPHA * x_ref[...] + y, g_ref[...], b_ref[...])


def _resident(shape):
    nd = len(shape)
    return pl.BlockSpec(shape, lambda *_: (0,) * nd, pipeline_mode=pl.Buffered(1))


def _ffn(x, lp, T):
    n = x.shape[0]
    tm = TM_FFN
    nt = T // tm
    cf = FF_CHUNK
    hb = tm // HALO
    nhb = n // HALO
    row = lambda i: (i, 0)
    return pl.pallas_call(
        functools.partial(_ffn_kernel, nt),
        grid=(n // tm,),
        in_specs=[pl.BlockSpec((tm, D_MODEL), row),
                  pl.BlockSpec((HALO, D_MODEL), lambda i: (jnp.maximum(i * hb - 1, 0), 0)),
                  pl.BlockSpec((HALO, D_MODEL), lambda i: (jnp.minimum((i + 1) * hb, nhb - 1), 0)),
                  _resident(lp['w_up'].shape), _resident(lp['conv_w'].shape), _resident(lp['conv_b'].shape),
                  _resident(lp['w_down'].shape), _resident(lp['ln2_g'].shape), _resident(lp['ln2_b'].shape)],
        out_specs=pl.BlockSpec((tm, D_MODEL), row),
        out_shape=jax.ShapeDtypeStruct((n, D_MODEL), F32),
        scratch_shapes=[pltpu.VMEM((tm + 2 * HALO, D_MODEL), BF16),
                        pltpu.VMEM((2, tm + 2 * HALO, cf), F32),
                        pltpu.VMEM((2, tm + 2 * HALO, cf), F32),
                        pltpu.VMEM((tm, D_FF), BF16)],
        compiler_params=_params("parallel"),
        name="conv_ffn",
    )(x, x, x, lp['w_up'], lp['conv_w'], lp['conv_b'], lp['w_down'], lp['ln2_g'], lp['ln2_b'])


def _natten_bias(rpb):
    nr, nc = 2 * NA_ROWS - 1, 2 * NA_COLS - 1
    krows = NA_KBLKS * NA_QROWS
    col = np.arange(GRID_W)
    ci = np.clip(col[None, :] - col[:, None] + NA_COLS - 1, 0, nc - 1)
    onehot_c = (ci.reshape(-1)[None, :] == np.arange(nc)[:, None]).astype(np.float32)
    cs = np.clip(col - NA_COLS // 2, 0, GRID_W - NA_COLS)
    dc = col[None, :] - cs[:, None]
    valid_c = (dc >= 0) & (dc < NA_COLS)
    qa = np.arange(NA_QROWS)
    kr = np.arange(krows)
    onehot_r, valid_r = [], []
    for kind in range(3):
        q_row = (0, NA_QROWS, 2 * NA_QROWS)[kind] + qa
        r_start = (np.zeros_like(qa), qa, np.full_like(qa, NA_QROWS))[kind]
        dr = kr[None, :] - r_start[:, None]
        valid_r.append((dr >= 0) & (dr < NA_ROWS))
        ri = np.clip(kr[None, :] - q_row[:, None] + NA_ROWS - 1, 0, nr - 1)
        onehot_r.append((ri.reshape(-1)[:, None] == np.arange(nr)[None, :]).astype(np.float32))
    onehot_r = np.stack(onehot_r)
    valid = np.stack(valid_r)[:, :, None, :, None] & valid_c[None, None, :, None, :]
    by_col = jnp.einsum('hrc,cx->hrx', rpb, onehot_c, precision=HI)
    full = jnp.einsum('kpr,hrx->khpx', onehot_r, by_col, precision=HI)
    full = full.reshape(3, NA_HEADS, NA_QROWS, krows, GRID_W, GRID_W).transpose(0, 1, 2, 4, 3, 5)
    full = jnp.where(valid[:, None], full, -1e30)
    return full.reshape(3, NA_HEADS, NA_QBLK, krows * GRID_W).astype(F32)


def _prep_layer(l, emb_ln_g, emb_ln_b, w_in, na_rpb, mla_q_g, w_uq, mla_kv_g, w_ukv, sgu_ln_g, sgu_ln_b,
                sgu_w, sgu_b, w_out, ln1_g, ln1_b, w_up, conv_w, conv_b, w_down, ln2_g, ln2_b):
    wi = w_in[l]
    o_fb = 3 * NA_WIDTH
    o_cq = o_fb + FN_WIDTH
    o_ckv = o_cq + MLA_Q_LORA
    o_kr = o_ckv + MLA_KV_LORA
    o_sg = o_kr + MLA_ROPE
    half = MLA_ROPE // 2
    swap = np.concatenate([np.arange(half, MLA_ROPE), np.arange(half)])
    w_kr = wi[:, o_kr:o_sg]
    w_kr2 = jnp.concatenate([w_kr, w_kr[:, swap], jnp.zeros((D_MODEL, LANES - 2 * MLA_ROPE), F32)], axis=1)

    uq = w_uq[l]
    pad = jnp.zeros((MLA_Q_LORA, MLA_HEADS, MLA_PAD - MLA_NOPE - MLA_ROPE), F32)
    w_a = jnp.concatenate([uq, pad], axis=2).reshape(MLA_Q_LORA, MLA_HEADS * MLA_PAD)
    w_b = jnp.concatenate([jnp.zeros((MLA_Q_LORA, MLA_HEADS, MLA_NOPE), F32),
                           uq[:, :, MLA_NOPE:][:, :, swap], pad], axis=2).reshape(MLA_Q_LORA, MLA_HEADS * MLA_PAD)
    ukv = w_ukv[l]
    w_kn = jnp.concatenate([ukv[:, :, :MLA_NOPE],
                            jnp.zeros((MLA_KV_LORA, MLA_HEADS, MLA_PAD - MLA_NOPE), F32)],
                           axis=2).reshape(MLA_KV_LORA, MLA_HEADS * MLA_PAD)
    w_vt = ukv[:, :, MLA_NOPE:].reshape(MLA_KV_LORA, MLA_WIDTH).T
    e_place = np.zeros((LANES, MLA_HEADS * MLA_PAD), np.float32)
    for h in range(MLA_HEADS):
        e_place[np.arange(MLA_ROPE), h * MLA_PAD + MLA_NOPE + np.arange(MLA_ROPE)] = 1.0
    gmat = np.kron(np.eye(SGU_HEADS), np.full((SGU_HEAD_DIM, SGU_HEAD_DIM), 1.0 / SGU_HEAD_DIM)).astype(np.float32)
    s_b = jnp.repeat(sgu_b[l].T, SGU_HEAD_DIM, axis=1)
    r1 = lambda a: a.reshape(1, -1).astype(F32)
    return dict(
        eg=r1(emb_ln_g), eb=r1(emb_ln_b),
        w_nat=wi[:, :o_fb].astype(BF16), w_fb=wi[:, o_fb:o_cq].astype(BF16),
        w_cq=wi[:, o_cq:o_ckv].astype(BF16), w_ckv=wi[:, o_ckv:o_kr].astype(BF16),
        w_kr=w_kr2.astype(BF16), w_sg=wi[:, o_sg:].astype(BF16),
        q_g=r1(mla_q_g[l]), w_a=w_a.astype(BF16), w_b=w_b.astype(BF16),
        kv_g=r1(mla_kv_g[l]), w_kn=w_kn.astype(BF16), w_vt=w_vt.astype(BF16),
        e_place=jnp.asarray(e_place, BF16), gmat=jnp.asarray(gmat, BF16),
        sl_g=r1(sgu_ln_g[l]), sl_b=r1(sgu_ln_b[l]), w_s=sgu_w[l].astype(BF16), s_b=s_b.astype(F32),
        na_bias=_natten_bias(na_rpb[l]),
        w_out=w_out[l].astype(BF16), ln1_g=r1(ln1_g[l]), ln1_b=r1(ln1_b[l]),
        w_up=w_up[l].astype(BF16), conv_w=conv_w[l].astype(F32), conv_b=r1(conv_b[l]),
        w_down=w_down[l].astype(BF16), ln2_g=r1(ln2_g[l]), ln2_b=r1(ln2_b[l]),
    )


def _rope_tables(T):
    inv_freq = ROPE_THETA ** (-jnp.arange(0, MLA_ROPE, 2, dtype=F32) / MLA_ROPE)
    ang = jnp.arange(T, dtype=F32)[:, None] * inv_freq[None, :]
    cos, sin = jnp.cos(ang), jnp.sin(ang)
    cos2 = jnp.concatenate([cos, cos], axis=1)
    sin2 = jnp.concatenate([-sin, sin], axis=1)
    scale = (MLA_NOPE + MLA_ROPE) ** -0.5 * math.log2(math.e)
    zq = jnp.zeros((T, MLA_PAD - MLA_NOPE - MLA_ROPE), F32)
    ca = jnp.concatenate([jnp.full((T, MLA_NOPE), scale, F32), scale * cos2, zq], axis=1)
    cb = jnp.concatenate([jnp.zeros((T, MLA_NOPE), F32), scale * sin2, zq], axis=1)
    tk = jnp.concatenate([cos2, sin2, jnp.zeros((T, LANES - 2 * MLA_ROPE), F32)], axis=1)
    return dict(ca=ca, cb=cb, tk=tk)


def _dft_tables(T):
    n1 = 1 << (int(math.log2(T)) // 2)
    n2 = T // n1
    two_pi = 2.0 * math.pi

    def cs(num, den):
        ang = (num % den).astype(F32) * (two_pi / den)
        return jnp.cos(ang), jnp.sin(ang)

    t2 = jnp.arange(n2, dtype=jnp.int32)[:, None, None]
    k1 = jnp.arange(n1, dtype=jnp.int32)[None, :, None]
    t1 = jnp.arange(n1, dtype=jnp.int32)[None, None, :]
    c, s = cs(k1 * (t1 * n2 + t2), T)
    m1 = jnp.concatenate([c, -s], axis=1)
    k2 = jnp.arange(n2, dtype=jnp.int32)
    c2, s2 = cs(k2[:, None] * k2[None, :], n2)
    cs2 = jnp.concatenate([c2, s2], axis=0)
    cg = jnp.arange(FN_GROUP_DIM, dtype=jnp.int32)
    cc, sc = cs(cg[:, None] * cg[None, :], FN_GROUP_DIM)
    norm = 1.0 / math.sqrt(T * FN_GROUP_DIM)
    eye = jnp.eye(FN_GROUPS, dtype=F32)
    return dict(n1=n1, n2=n2, m1=m1, cs2=cs2, bc=jnp.kron(eye, cc) * norm, bs=jnp.kron(eye, sc) * norm)


def _trunk(x, layers):
    B, T, _ = x.shape
    tabs = _rope_tables(T)
    ft = _dft_tables(T)
    h = x.reshape(B * T, D_MODEL)
    for l, lp in enumerate(layers):
        outs = _inproj(h, lp, tabs, l == 0, T)
        if l == 0:
            h = outs[0]
            outs = outs[1:]
        nat, fb, q, k, vt, od = outs
        oa = _natten(nat, lp['na_bias'], B, T)
        ob = _fourier(fb, ft, B, T)
        oct = _mla(q, k, vt, B, T)
        h1 = _outproj(oa, ob, oct, od, h, lp)
        h = _ffn(h1, lp, T)
    return h.reshape(B, T, D_MODEL)


def kernel(x_prompt, x_sample, emb_ln_g, emb_ln_b, w_in, na_rpb, mla_q_g, w_uq, mla_kv_g, w_ukv, sgu_ln_g,
           sgu_ln_b, sgu_w, sgu_b, w_out, ln1_g, ln1_b, w_up, conv_w, conv_b, w_down, ln2_g, ln2_b):
    layers = [_prep_layer(l, emb_ln_g, emb_ln_b, w_in, na_rpb, mla_q_g, w_uq, mla_kv_g, w_ukv, sgu_ln_g,
                          sgu_ln_b, sgu_w, sgu_b, w_out, ln1_g, ln1_b, w_up, conv_w, conv_b, w_down,
                          ln2_g, ln2_b) for l in range(DEPTH)]
    return (_trunk(x_prompt, layers), _trunk(x_sample, layers))
```

```python
import functools
import math

import numpy as np
import jax
import jax.numpy as jnp
from jax import lax
from jax.experimental import pallas as pl
from jax.experimental.pallas import tpu as pltpu

F32 = jnp.float32
BF16 = jnp.bfloat16

D_MODEL = 1024
DEPTH = 2
GRID_W = 64
HEAD_DIM = 64
NA_HEADS = 4
NA_ROWS = 8
NA_COLS = 16
NA_WIDTH = NA_HEADS * HEAD_DIM
FN_GROUPS = 4
FN_GROUP_DIM = 64
FN_WIDTH = FN_GROUPS * FN_GROUP_DIM
MLA_HEADS = 4
MLA_Q_LORA = 256
MLA_KV_LORA = 128
MLA_NOPE = 64
MLA_ROPE = 32
MLA_V = 64
MLA_WIDTH = MLA_HEADS * MLA_V
ROPE_THETA = 10000.0
SGU_HEADS = 4
SGU_HEAD_DIM = 64
SGU_WIDTH = SGU_HEADS * SGU_HEAD_DIM
SGU_CHUNK = 128
D_FF = 2816
ALPHA = (2 * DEPTH) ** 0.25
LN_EPS = 1e-5
RMS_EPS = 1e-6

LANES = 128
SUBLANES = 8
VMEM_LIMIT = 56 * 1024 * 1024

TM_IN = 512
TM_OUT = 512
TM_FFN = 512
FF_CHUNK = 256
NA_QROWS = 4
NA_QBLK = NA_QROWS * GRID_W
NA_KBLKS = 3
MLA_PAD = 128
MLA_TQ = 512
MLA_TK = 512
MLA_ONES = 16
HALO = SUBLANES

HI = lax.Precision.HIGHEST
NT_DIMS = (((1,), (1,)), ((), ()))


def _dot(a, b):
    return jnp.dot(a, b, preferred_element_type=F32)


def _dot_nt(a, b):
    return lax.dot_general(a, b, NT_DIMS, preferred_element_type=F32)


def _dot_hi(a, b):
    return jnp.dot(a, b, precision=HI, preferred_element_type=F32)


def _layer_norm_rows(x, g, b):
    mu = jnp.mean(x, axis=-1, keepdims=True)
    xc = x - mu
    var = jnp.mean(xc * xc, axis=-1, keepdims=True)
    return xc * lax.rsqrt(var + LN_EPS) * g + b


def _gelu_tanh(x):
    return 0.5 * x * (1.0 + jnp.tanh(0.7978845608028654 * (x + 0.044715 * (x * x * x))))


def _split_dot(v, m):
    hi = v.astype(BF16)
    lo = (v - hi.astype(F32)).astype(BF16)
    return _dot(hi, m) + _dot(lo, m)


def _params(*sem):
    return pltpu.CompilerParams(dimension_semantics=sem, vmem_limit_bytes=VMEM_LIMIT)


def _full(shape):
    nd = len(shape)
    return pl.BlockSpec(shape, lambda *_: (0,) * nd)


def _inproj_kernel(apply_ln, x_ref, eg_ref, eb_ref, wnat_ref, wfb_ref, wcq_ref, wckv_ref, wkr_ref,
                   wsg_ref, qg_ref, wa_ref, wb_ref, kvg_ref, wkn_ref, wvt_ref, e_ref, ca_ref, cb_ref,
                   tk_ref, gmat_ref, slg_ref, slb_ref, ws_ref, sb_ref, *out_refs):
    if apply_ln:
        xn_ref, nat_ref, fb_ref, q_ref, k_ref, vt_ref, od_ref = out_refs
    else:
        nat_ref, fb_ref, q_ref, k_ref, vt_ref, od_ref = out_refs
    x = x_ref[...]
    if apply_ln:
        x = _layer_norm_rows(x, eg_ref[...], eb_ref[...])
        xn_ref[...] = x
    xb = x.astype(BF16)
    tm = x.shape[0]

    nat = _dot(xb, wnat_ref[...])
    nat_ref[:, :NA_WIDTH] = (nat[:, :NA_WIDTH] * (HEAD_DIM ** -0.5)).astype(BF16)
    nat_ref[:, NA_WIDTH:] = nat[:, NA_WIDTH:].astype(BF16)

    fb_ref[...] = _dot(xb, wfb_ref[...])

    cq = _dot(xb, wcq_ref[...])
    cqn = (cq * lax.rsqrt(jnp.mean(cq * cq, axis=-1, keepdims=True) + RMS_EPS) * qg_ref[...]).astype(BF16)
    qa = _dot_nt(wa_ref[...], cqn)
    qb = _dot_nt(wb_ref[...], cqn)
    ca = ca_ref[...]
    cb = cb_ref[...]
    for h in range(MLA_HEADS):
        sl = slice(h * MLA_PAD, (h + 1) * MLA_PAD)
        q_ref[sl, :] = (qa[sl] * ca + qb[sl] * cb).astype(BF16)

    ckv = _dot(xb, wckv_ref[...])
    ckvn = (ckv * lax.rsqrt(jnp.mean(ckv * ckv, axis=-1, keepdims=True) + RMS_EPS) * kvg_ref[...]).astype(BF16)
    kr = _dot(xb, wkr_ref[...]) * tk_ref[...]
    kr = kr + pltpu.roll(kr, LANES - MLA_ROPE, axis=1)
    k_ref[...] = (_dot(ckvn, wkn_ref[...]) + _dot(kr.astype(BF16), e_ref[...])).astype(BF16)
    vt = _dot_nt(wvt_ref[...], ckvn)
    tkc = vt_ref.shape[2]
    for c in range(tm // tkc):
        vt_ref[c] = vt[:, c * tkc:(c + 1) * tkc].astype(BF16)

    sg = _gelu_tanh(_dot(xb, wsg_ref[...]))
    u = sg[:, :SGU_WIDTH]
    v = sg[:, SGU_WIDTH:]
    gmat = gmat_ref[...]
    mu = _split_dot(v, gmat)
    vc = v - mu
    var = _split_dot(vc * vc, gmat)
    vn = (vc * lax.rsqrt(var + LN_EPS) * slg_ref[...] + slb_ref[...]).astype(BF16)
    head = lax.broadcasted_iota(jnp.int32, (1, SGU_WIDTH), 1) // SGU_HEAD_DIM
    zero = jnp.zeros((), BF16)
    for c in range(tm // SGU_CHUNK):
        rows = slice(c * SGU_CHUNK, (c + 1) * SGU_CHUNK)
        vch = vn[rows]
        mixed = sb_ref[...]
        for g in range(SGU_HEADS):
            mixed = mixed + _dot(ws_ref[g], jnp.where(head == g, vch, zero))
        od_ref[rows, :] = (u[rows] * mixed).astype(BF16)


def _inproj(x, lp, tabs, apply_ln, T):
    n = x.shape[0]
    tm = TM_IN
    nt = T // tm
    row = lambda i: (i, 0)
    tab = lambda i: (i % nt, 0)
    weights = [lp['eg'], lp['eb'], lp['w_nat'], lp['w_fb'], lp['w_cq'], lp['w_ckv'], lp['w_kr'], lp['w_sg'],
               lp['q_g'], lp['w_a'], lp['w_b'], lp['kv_g'], lp['w_kn'], lp['w_vt'], lp['e_place']]
    tables = [tabs['ca'], tabs['cb'], tabs['tk']]
    tail = [lp['gmat'], lp['sl_g'], lp['sl_b'], lp['w_s'], lp['s_b']]
    tab_t = lambda i: (0, i % nt)
    in_specs = ([pl.BlockSpec((tm, D_MODEL), row)] + [_full(w.shape) for w in weights]
                + [pl.BlockSpec((MLA_PAD, tm), tab_t), pl.BlockSpec((MLA_PAD, tm), tab_t),
                   pl.BlockSpec((tm, LANES), tab)] + [_full(w.shape) for w in tail])
    out_shape = [jax.ShapeDtypeStruct((n, 3 * NA_WIDTH), BF16),
                 jax.ShapeDtypeStruct((n, FN_WIDTH), F32),
                 jax.ShapeDtypeStruct((MLA_HEADS * MLA_PAD, n), BF16),
                 jax.ShapeDtypeStruct((n, MLA_HEADS * MLA_PAD), BF16),
                 jax.ShapeDtypeStruct((n // MLA_TK, MLA_WIDTH, MLA_TK), BF16),
                 jax.ShapeDtypeStruct((n, SGU_WIDTH), BF16)]
    out_specs = [pl.BlockSpec((tm, 3 * NA_WIDTH), row),
                 pl.BlockSpec((tm, FN_WIDTH), row),
                 pl.BlockSpec((MLA_HEADS * MLA_PAD, tm), lambda i: (0, i)),
                 pl.BlockSpec((tm, MLA_HEADS * MLA_PAD), row),
                 pl.BlockSpec((tm // MLA_TK, MLA_WIDTH, MLA_TK), lambda i: (i, 0, 0)),
                 pl.BlockSpec((tm, SGU_WIDTH), row)]
    if apply_ln:
        out_shape = [jax.ShapeDtypeStruct((n, D_MODEL), F32)] + out_shape
        out_specs = [pl.BlockSpec((tm, D_MODEL), row)] + out_specs
    return pl.pallas_call(
        functools.partial(_inproj_kernel, apply_ln),
        grid=(n // tm,),
        in_specs=in_specs,
        out_specs=out_specs,
        out_shape=out_shape,
        compiler_params=_params("parallel"),
        name="inproj_ln" if apply_ln else "inproj",
    )(x, *weights, *tables, *tail)


def _natten_kernel(q_ref, k0_ref, k1_ref, k2_ref, v0_ref, v1_ref, v2_ref, bias_ref, o_ref):
    q = q_ref[...]
    k = jnp.concatenate([k0_ref[...], k1_ref[...], k2_ref[...]], axis=0)
    v = jnp.concatenate([v0_ref[...], v1_ref[...], v2_ref[...]], axis=0)
    head = lax.broadcasted_iota(jnp.int32, (1, NA_WIDTH), 1) // HEAD_DIM
    zero = jnp.zeros((), BF16)
    acc = jnp.zeros((NA_QBLK, NA_WIDTH), F32)
    for h in range(NA_HEADS):
        sel = head == h
        s = _dot_nt(jnp.where(sel, q, zero), k) + bias_ref[h]
        m = jnp.max(s, axis=-1, keepdims=True)
        p = jnp.exp(s - m)
        l = jnp.sum(p, axis=-1, keepdims=True)
        acc = acc + _dot(p.astype(BF16), jnp.where(sel, v, zero)) * (1.0 / l)
    o_ref[...] = acc.astype(BF16)


def _natten(nat, bias, B, T):
    n = nat.shape[0]
    nblk = T // NA_QBLK

    def kv_map(d, col):
        def f(b, j):
            base = jnp.clip(j - 1, 0, nblk - NA_KBLKS)
            return (b * nblk + base + d, col)
        return f

    def bias_map(b, j):
        typ = jnp.where(j == 0, 0, jnp.where(j == nblk - 1, 2, 1))
        return (typ, 0, 0, 0)

    blk = (NA_QBLK, NA_WIDTH)
    in_specs = ([pl.BlockSpec(blk, lambda b, j: (b * nblk + j, 0))]
                + [pl.BlockSpec(blk, kv_map(d, 1)) for d in range(NA_KBLKS)]
                + [pl.BlockSpec(blk, kv_map(d, 2)) for d in range(NA_KBLKS)]
                + [pl.BlockSpec((None, NA_HEADS, NA_QBLK, NA_KBLKS * NA_QBLK), bias_map)])
    return pl.pallas_call(
        _natten_kernel,
        grid=(B, nblk),
        in_specs=in_specs,
        out_specs=pl.BlockSpec(blk, lambda b, j: (b * nblk + j, 0)),
        out_shape=jax.ShapeDtypeStruct((n, NA_WIDTH), BF16),
        compiler_params=_params("parallel", "arbitrary"),
        name="natten",
    )(nat, nat, nat, nat, nat, nat, nat, bias)


def _split_bf16(x):
    hi = x.astype(BF16)
    return hi, (x - hi.astype(F32)).astype(BF16)


def _fft1_kernel(x_ref, m_ref, a_ref):
    kk = m_ref.shape[0]
    n1 = x_ref.shape[0]
    for i in range(kk):
        x_hi, x_lo = _split_bf16(x_ref[:, i, :])
        m = m_ref[i]
        r = _dot(m, x_hi)
        res = r[:2 * n1] + r[2 * n1:] + _dot(m[:2 * n1], x_lo)
        a_ref[i, :, :FN_WIDTH] = res[:n1]
        a_ref[i, :, FN_WIDTH:] = res[n1:]


def _fft2_kernel(a_ref, cs_ref, w_ref, y_ref):
    n2, kk, _ = a_ref.shape
    cs = cs_ref[...]
    g = []
    for i in range(kk):
        a_hi, a_lo = _split_bf16(a_ref[:, i, :])
        r = _dot(cs, a_hi)
        r = r[:2 * n2] + r[2 * n2:] + _dot(cs[:2 * n2], a_lo)
        g.append(jnp.concatenate([r[:n2, :FN_WIDTH] + r[n2:, FN_WIDTH:],
                                  r[:n2, FN_WIDTH:] - r[n2:, :FN_WIDTH]], axis=1))
    g_hi, g_lo = _split_bf16(jnp.concatenate(g, axis=0))
    y = _dot(g_hi, w_ref[0]) + _dot(g_lo, w_ref[0]) + _dot(g_hi, w_ref[1])
    for i in range(kk):
        y_ref[:, i, :] = y[i * n2:(i + 1) * n2]


def _fourier(fb, ft, B, T):
    n1, n2 = ft['n1'], ft['n2']
    kk = SUBLANES
    a = pl.pallas_call(
        _fft1_kernel,
        grid=(B, n2 // kk),
        in_specs=[pl.BlockSpec((None, n1, kk, FN_WIDTH), lambda b, j: (b, 0, j, 0)),
                  pl.BlockSpec((kk, 4 * n1, n1), lambda b, j: (j, 0, 0))],
        out_specs=pl.BlockSpec((None, kk, n1, 2 * FN_WIDTH), lambda b, j: (b, j, 0, 0)),
        out_shape=jax.ShapeDtypeStruct((B, n2, n1, 2 * FN_WIDTH), F32),
        compiler_params=_params("parallel", "arbitrary"),
        name="fft_stage1",
    )(fb.reshape(B, n1, n2, FN_WIDTH), ft['m1'])
    y = pl.pallas_call(
        _fft2_kernel,
        grid=(B, n1 // kk),
        in_specs=[pl.BlockSpec((None, n2, kk, 2 * FN_WIDTH), lambda b, j: (b, 0, j, 0)),
                  _full(ft['cs2'].shape), _full(ft['w'].shape)],
        out_specs=pl.BlockSpec((None, n2, kk, FN_WIDTH), lambda b, j: (b, 0, j, 0)),
        out_shape=jax.ShapeDtypeStruct((B, n2, n1, FN_WIDTH), F32),
        compiler_params=_params("parallel", "arbitrary"),
        name="fft_stage2",
    )(a, ft['cs2'], ft['w'])
    return y.reshape(B * T, FN_WIDTH)


def _mla_kernel(q_ref, k_ref, vt_ref, o_ref, s_ref):
    nk, _, tk = vt_ref.shape
    tq = q_ref.shape[1]
    q = q_ref[...]
    ones = jnp.ones((MLA_ONES, tk), BF16)

    def scores(j):
        kj = k_ref[pl.ds(pl.multiple_of(j * tk, tk), tk), :]
        return _dot(kj, q)

    def update(carry, j, slot):
        m, acc = carry
        st = s_ref[slot]
        m_new = jnp.maximum(m, jnp.max(st, axis=0, keepdims=True))
        alpha = jnp.exp2(m - m_new)
        p = jnp.exp2(st - m_new).astype(BF16)
        vte = jnp.concatenate([vt_ref[j], ones], axis=0)
        return m_new, alpha * acc + _dot(vte, p)

    s_ref[0] = scores(0)

    def body(jj, carry):
        j0 = 2 * jj
        s_ref[1] = scores(j0 + 1)
        carry = update(carry, j0, 0)
        s_ref[0] = scores(jnp.minimum(j0 + 2, nk - 1))
        return update(carry, j0 + 1, 1)

    init = (jnp.full((1, tq), -1e30, F32), jnp.zeros((MLA_V + MLA_ONES, tq), F32))
    _, acc = lax.fori_loop(0, nk // 2, body, init)
    o_ref[...] = acc[:MLA_V] * (1.0 / acc[MLA_V:MLA_V + 1])


def _mla(q, k, vt, B, T):
    n = k.shape[0]
    nq = T // MLA_TQ
    nk = T // MLA_TK
    assert nk % 2 == 0
    return pl.pallas_call(
        _mla_kernel,
        grid=(B, MLA_HEADS, nq),
        in_specs=[pl.BlockSpec((MLA_PAD, MLA_TQ), lambda b, h, i: (h, b * nq + i)),
                  pl.BlockSpec((T, MLA_PAD), lambda b, h, i: (b, h)),
                  pl.BlockSpec((nk, MLA_V, MLA_TK), lambda b, h, i: (b, h, 0))],
        out_specs=pl.BlockSpec((MLA_V, MLA_TQ), lambda b, h, i: (h, b * nq + i)),
        out_shape=jax.ShapeDtypeStruct((MLA_WIDTH, n), F32),
        scratch_shapes=[pltpu.VMEM((2, MLA_TK, MLA_TQ), F32)],
        compiler_params=_params("parallel", "parallel", "arbitrary"),
        name="mla",
    )(q, k, vt)


def _outproj_kernel(oa_ref, ob_ref, oct_ref, od_ref, x_ref, w_ref, g_ref, b_ref, o_ref):
    w = NA_WIDTH
    oc = oct_ref[...].T.astype(BF16)
    y = (_dot(oa_ref[...], w_ref[0:w]) + _dot(ob_ref[...].astype(BF16), w_ref[w:2 * w])
         + _dot(oc, w_ref[2 * w:3 * w]) + _dot(od_ref[...], w_ref[3 * w:4 * w]))
    o_ref[...] = _layer_norm_rows(ALPHA * x_ref[...] + y, g_ref[...], b_ref[...])


def _outproj(oa, ob, oct, od, x, lp):
    n = x.shape[0]
    tm = TM_OUT
    row = lambda i: (i, 0)
    return pl.pallas_call(
        _outproj_kernel,
        grid=(n // tm,),
        in_specs=[pl.BlockSpec((tm, NA_WIDTH), row), pl.BlockSpec((tm, FN_WIDTH), row),
                  pl.BlockSpec((MLA_WIDTH, tm), lambda i: (0, i)), pl.BlockSpec((tm, SGU_WIDTH), row),
                  pl.BlockSpec((tm, D_MODEL), row), _full(lp['w_out'].shape),
                  _full(lp['ln1_g'].shape), _full(lp['ln1_b'].shape)],
        out_specs=pl.BlockSpec((tm, D_MODEL), row),
        out_shape=jax.ShapeDtypeStruct((n, D_MODEL), F32),
        compiler_params=_params("parallel"),
        name="outproj",
    )(oa, ob, oct, od, x, lp['w_out'], lp['ln1_g'], lp['ln1_b'])


def _ffn_kernel(nt, x_ref, xp_ref, xn_ref, wup_ref, cw_ref, cb_ref, wd_ref, g_ref, b_ref, o_ref,
                xe_ref, hg_ref, hv_ref, act_ref):
    i = pl.program_id(0)
    tm = x_ref.shape[0]
    cf = hg_ref.shape[2]
    first = (i % nt) == 0
    last = (i % nt) == nt - 1
    xe_ref[0:HALO] = jnp.where(first, 0.0, xp_ref[...]).astype(BF16)
    xe_ref[HALO:HALO + tm] = x_ref[...].astype(BF16)
    xe_ref[HALO + tm:] = jnp.where(last, 0.0, xn_ref[...]).astype(BF16)
    xe = xe_ref[...]

    def conv(h_ref, cols):
        return (h_ref[pl.ds(HALO - 1, tm), :] * cw_ref[0:1, cols] + h_ref[pl.ds(HALO, tm), :] * cw_ref[1:2, cols]
                + h_ref[pl.ds(HALO + 1, tm), :] * cw_ref[2:3, cols] + cb_ref[:, cols])

    for c in range(D_FF // cf):
        slot = c % 2
        gate = slice(c * cf, (c + 1) * cf)
        val = slice(D_FF + c * cf, D_FF + (c + 1) * cf)
        hg_ref[slot] = _dot(xe, wup_ref[:, gate])
        hv_ref[slot] = _dot(xe, wup_ref[:, val])
        act = _gelu_tanh(conv(hg_ref.at[slot], gate)) * conv(hv_ref.at[slot], val)
        act_ref[:, gate] = act.astype(BF16)

    y = _dot(act_ref[...], wd_ref[...])
    o_ref[...] = _layer_norm_rows(ALPHA * x_ref[...] + y, g_ref[...], b_ref[...])


def _resident(shape):
    nd = len(shape)
    return pl.BlockSpec(shape, lambda *_: (0,) * nd, pipeline_mode=pl.Buffered(1))


def _ffn(x, lp, T):
    n = x.shape[0]
    tm = TM_FFN
    nt = T // tm
    cf = FF_CHUNK
    hb = tm // HALO
    nhb = n // HALO
    row = lambda i: (i, 0)
    return pl.pallas_call(
        functools.partial(_ffn_kernel, nt),
        grid=(n // tm,),
        in_specs=[pl.BlockSpec((tm, D_MODEL), row),
                  pl.BlockSpec((HALO, D_MODEL), lambda i: (jnp.maximum(i * hb - 1, 0), 0)),
                  pl.BlockSpec((HALO, D_MODEL), lambda i: (jnp.minimum((i + 1) * hb, nhb - 1), 0)),
                  _resident(lp['w_up'].shape), _resident(lp['conv_w'].shape), _resident(lp['conv_b'].shape),
                  _resident(lp['w_down'].shape), _resident(lp['ln2_g'].shape), _resident(lp['ln2_b'].shape)],
        out_specs=pl.BlockSpec((tm, D_MODEL), row),
        out_shape=jax.ShapeDtypeStruct((n, D_MODEL), F32),
        scratch_shapes=[pltpu.VMEM((tm + 2 * HALO, D_MODEL), BF16),
                        pltpu.VMEM((2, tm + 2 * HALO, cf), F32),
                        pltpu.VMEM((2, tm + 2 * HALO, cf), F32),
                        pltpu.VMEM((tm, D_FF), BF16)],
        compiler_params=_params("parallel"),
        name="conv_ffn",
    )(x, x, x, lp['w_up'], lp['conv_w'], lp['conv_b'], lp['w_down'], lp['ln2_g'], lp['ln2_b'])


def _natten_bias(rpb):
    nr, nc = 2 * NA_ROWS - 1, 2 * NA_COLS - 1
    krows = NA_KBLKS * NA_QROWS
    col = np.arange(GRID_W)
    ci = np.clip(col[None, :] - col[:, None] + NA_COLS - 1, 0, nc - 1)
    onehot_c = (ci.reshape(-1)[None, :] == np.arange(nc)[:, None]).astype(np.float32)
    cs = np.clip(col - NA_COLS // 2, 0, GRID_W - NA_COLS)
    dc = col[None, :] - cs[:, None]
    valid_c = (dc >= 0) & (dc < NA_COLS)
    qa = np.arange(NA_QROWS)
    kr = np.arange(krows)
    onehot_r, valid_r = [], []
    for kind in range(3):
        q_row = (0, NA_QROWS, 2 * NA_QROWS)[kind] + qa
        r_start = (np.zeros_like(qa), qa, np.full_like(qa, NA_QROWS))[kind]
        dr = kr[None, :] - r_start[:, None]
        valid_r.append((dr >= 0) & (dr < NA_ROWS))
        ri = np.clip(kr[None, :] - q_row[:, None] + NA_ROWS - 1, 0, nr - 1)
        onehot_r.append((ri.reshape(-1)[:, None] == np.arange(nr)[None, :]).astype(np.float32))
    onehot_r = np.stack(onehot_r)
    valid = np.stack(valid_r)[:, :, None, :, None] & valid_c[None, None, :, None, :]
    by_col = jnp.einsum('hrc,cx->hrx', rpb, onehot_c, precision=HI)
    full = jnp.einsum('kpr,hrx->khpx', onehot_r, by_col, precision=HI)
    full = full.reshape(3, NA_HEADS, NA_QROWS, krows, GRID_W, GRID_W).transpose(0, 1, 2, 4, 3, 5)
    full = jnp.where(valid[:, None], full, -1e30)
    return full.reshape(3, NA_HEADS, NA_QBLK, krows * GRID_W).astype(F32)


def _prep_layer(l, emb_ln_g, emb_ln_b, w_in, na_rpb, mla_q_g, w_uq, mla_kv_g, w_ukv, sgu_ln_g, sgu_ln_b,
                sgu_w, sgu_b, w_out, ln1_g, ln1_b, w_up, conv_w, conv_b, w_down, ln2_g, ln2_b):
    wi = w_in[l]
    o_fb = 3 * NA_WIDTH
    o_cq = o_fb + FN_WIDTH
    o_ckv = o_cq + MLA_Q_LORA
    o_kr = o_ckv + MLA_KV_LORA
    o_sg = o_kr + MLA_ROPE
    half = MLA_ROPE // 2
    swap = np.concatenate([np.arange(half, MLA_ROPE), np.arange(half)])
    w_kr = wi[:, o_kr:o_sg]
    w_kr2 = jnp.concatenate([w_kr, w_kr[:, swap], jnp.zeros((D_MODEL, LANES - 2 * MLA_ROPE), F32)], axis=1)

    uq = w_uq[l]
    pad = jnp.zeros((MLA_Q_LORA, MLA_HEADS, MLA_PAD - MLA_NOPE - MLA_ROPE), F32)
    w_a = jnp.concatenate([uq, pad], axis=2).reshape(MLA_Q_LORA, MLA_HEADS * MLA_PAD)
    w_b = jnp.concatenate([jnp.zeros((MLA_Q_LORA, MLA_HEADS, MLA_NOPE), F32),
                           uq[:, :, MLA_NOPE:][:, :, swap], pad], axis=2).reshape(MLA_Q_LORA, MLA_HEADS * MLA_PAD)
    ukv = w_ukv[l]
    w_kn = jnp.concatenate([ukv[:, :, :MLA_NOPE],
                            jnp.zeros((MLA_KV_LORA, MLA_HEADS, MLA_PAD - MLA_NOPE), F32)],
                           axis=2).reshape(MLA_KV_LORA, MLA_HEADS * MLA_PAD)
    w_vt = ukv[:, :, MLA_NOPE:].reshape(MLA_KV_LORA, MLA_WIDTH).T
    e_place = np.zeros((LANES, MLA_HEADS * MLA_PAD), np.float32)
    for h in range(MLA_HEADS):
        e_place[np.arange(MLA_ROPE), h * MLA_PAD + MLA_NOPE + np.arange(MLA_ROPE)] = 1.0
    gmat = np.kron(np.eye(SGU_HEADS), np.full((SGU_HEAD_DIM, SGU_HEAD_DIM), 1.0 / SGU_HEAD_DIM)).astype(np.float32)
    s_b = jnp.repeat(sgu_b[l].T, SGU_HEAD_DIM, axis=1)
    r1 = lambda a: a.reshape(1, -1).astype(F32)
    return dict(
        eg=r1(emb_ln_g), eb=r1(emb_ln_b),
        w_nat=wi[:, :o_fb].astype(BF16), w_fb=wi[:, o_fb:o_cq].astype(BF16),
        w_cq=wi[:, o_cq:o_ckv].astype(BF16), w_ckv=wi[:, o_ckv:o_kr].astype(BF16),
        w_kr=w_kr2.astype(BF16), w_sg=wi[:, o_sg:].astype(BF16),
        q_g=r1(mla_q_g[l]), w_a=w_a.T.astype(BF16), w_b=w_b.T.astype(BF16),
        kv_g=r1(mla_kv_g[l]), w_kn=w_kn.astype(BF16), w_vt=w_vt.astype(BF16),
        e_place=jnp.asarray(e_place, BF16), gmat=jnp.asarray(gmat, BF16),
        sl_g=r1(sgu_ln_g[l]), sl_b=r1(sgu_ln_b[l]), w_s=sgu_w[l].astype(BF16), s_b=s_b.astype(F32),
        na_bias=_natten_bias(na_rpb[l]),
        w_out=w_out[l].astype(BF16), ln1_g=r1(ln1_g[l]), ln1_b=r1(ln1_b[l]),
        w_up=w_up[l].astype(BF16), conv_w=conv_w[l].astype(F32), conv_b=r1(conv_b[l]),
        w_down=w_down[l].astype(BF16), ln2_g=r1(ln2_g[l]), ln2_b=r1(ln2_b[l]),
    )


def _rope_tables(T):
    inv_freq = ROPE_THETA ** (-jnp.arange(0, MLA_ROPE, 2, dtype=F32) / MLA_ROPE)
    ang = jnp.arange(T, dtype=F32)[:, None] * inv_freq[None, :]
    cos, sin = jnp.cos(ang), jnp.sin(ang)
    cos2 = jnp.concatenate([cos, cos], axis=1)
    sin2 = jnp.concatenate([-sin, sin], axis=1)
    scale = (MLA_NOPE + MLA_ROPE) ** -0.5 * math.log2(math.e)
    zq = jnp.zeros((T, MLA_PAD - MLA_NOPE - MLA_ROPE), F32)
    ca = jnp.concatenate([jnp.full((T, MLA_NOPE), scale, F32), scale * cos2, zq], axis=1)
    cb = jnp.concatenate([jnp.zeros((T, MLA_NOPE), F32), scale * sin2, zq], axis=1)
    tk = jnp.concatenate([cos2, sin2, jnp.zeros((T, LANES - 2 * MLA_ROPE), F32)], axis=1)
    return dict(ca=ca.T, cb=cb.T, tk=tk)


def _dft_tables(T):
    n1 = 1 << (int(math.log2(T)) // 2)
    n2 = T // n1
    two_pi = 2.0 * math.pi

    def cs(num, den):
        ang = (num % den).astype(F32) * (two_pi / den)
        return jnp.cos(ang), jnp.sin(ang)

    t2 = jnp.arange(n2, dtype=jnp.int32)[:, None, None]
    k1 = jnp.arange(n1, dtype=jnp.int32)[None, :, None]
    t1 = jnp.arange(n1, dtype=jnp.int32)[None, None, :]
    c, s = cs(k1 * (t1 * n2 + t2), T)
    def split(x, axis):
        hi = x.astype(BF16)
        lo = (x - hi.astype(F32)).astype(BF16)
        return jnp.concatenate([hi, lo], axis=axis) if axis is not None else jnp.stack([hi, lo])

    m1 = split(jnp.concatenate([c, -s], axis=1), 1)
    k2 = jnp.arange(n2, dtype=jnp.int32)
    c2, s2 = cs(k2[:, None] * k2[None, :], n2)
    cs2 = split(jnp.concatenate([c2, s2], axis=0), 0)
    cg = jnp.arange(FN_GROUP_DIM, dtype=jnp.int32)
    cc, sc = cs(cg[:, None] * cg[None, :], FN_GROUP_DIM)
    norm = 1.0 / math.sqrt(T * FN_GROUP_DIM)
    eye = jnp.eye(FN_GROUPS, dtype=F32)
    w = split(jnp.concatenate([jnp.kron(eye, cc), jnp.kron(eye, sc)], axis=0) * norm, None)
    return dict(n1=n1, n2=n2, m1=m1, cs2=cs2, w=w)


def _trunk(x, layers):
    B, T, _ = x.shape
    tabs = _rope_tables(T)
    ft = _dft_tables(T)
    h = x.reshape(B * T, D_MODEL)
    for l, lp in enumerate(layers):
        outs = _inproj(h, lp, tabs, l == 0, T)
        if l == 0:
            h = outs[0]
            outs = outs[1:]
        nat, fb, q, k, vt, od = outs
        oa = _natten(nat, lp['na_bias'], B, T)
        ob = _fourier(fb, ft, B, T)
        oct = _mla(q, k, vt, B, T)
        h1 = _outproj(oa, ob, oct, od, h, lp)
        h = _ffn(h1, lp, T)
    return h.reshape(B, T, D_MODEL)


def kernel(x_prompt, x_sample, emb_ln_g, emb_ln_b, w_in, na_rpb, mla_q_g, w_uq, mla_kv_g, w_ukv, sgu_ln_g,
           sgu_ln_b, sgu_w, sgu_b, w_out, ln1_g, ln1_b, w_up, conv_w, conv_b, w_down, ln2_g, ln2_b):
    layers = [_prep_layer(l, emb_ln_g, emb_ln_b, w_in, na_rpb, mla_q_g, w_uq, mla_kv_g, w_ukv, sgu_ln_g,
                          sgu_ln_b, sgu_w, sgu_b, w_out, ln1_g, ln1_b, w_up, conv_w, conv_b, w_down,
                          ln2_g, ln2_b) for l in range(DEPTH)]
    return (_trunk(x_prompt, layers), _trunk(x_sample, layers))
```

```python
import functools
import math

import numpy as np
import jax
import jax.numpy as jnp
from jax import lax
from jax.experimental import pallas as pl
from jax.experimental.pallas import tpu as pltpu

F32 = jnp.float32
BF16 = jnp.bfloat16

D_MODEL = 1024
DEPTH = 2
GRID_W = 64
HEAD_DIM = 64
NA_HEADS = 4
NA_ROWS = 8
NA_COLS = 16
NA_WIDTH = NA_HEADS * HEAD_DIM
FN_GROUPS = 4
FN_GROUP_DIM = 64
FN_WIDTH = FN_GROUPS * FN_GROUP_DIM
MLA_HEADS = 4
MLA_Q_LORA = 256
MLA_KV_LORA = 128
MLA_NOPE = 64
MLA_ROPE = 32
MLA_V = 64
MLA_WIDTH = MLA_HEADS * MLA_V
ROPE_THETA = 10000.0
SGU_HEADS = 4
SGU_HEAD_DIM = 64
SGU_WIDTH = SGU_HEADS * SGU_HEAD_DIM
SGU_CHUNK = 128
D_FF = 2816
ALPHA = (2 * DEPTH) ** 0.25
LN_EPS = 1e-5
RMS_EPS = 1e-6

LANES = 128
SUBLANES = 8
VMEM_LIMIT = 56 * 1024 * 1024

TM_IN = 512
TM_OUT = 512
TM_FFN = 512
FF_CHUNK = 256
NA_QROWS = 4
NA_QBLK = NA_QROWS * GRID_W
NA_KBLKS = 3
MLA_PAD = 128
MLA_TQ = 512
MLA_TK = 512
MLA_ONES = 16
MLA_UNROLL = 8
HALO = SUBLANES

HI = lax.Precision.HIGHEST
NT_DIMS = (((1,), (1,)), ((), ()))


def _dot(a, b):
    return jnp.dot(a, b, preferred_element_type=F32)


def _dot_nt(a, b):
    return lax.dot_general(a, b, NT_DIMS, preferred_element_type=F32)


def _dot_hi(a, b):
    return jnp.dot(a, b, precision=HI, preferred_element_type=F32)


def _layer_norm_rows(x, g, b):
    mu = jnp.mean(x, axis=-1, keepdims=True)
    xc = x - mu
    var = jnp.mean(xc * xc, axis=-1, keepdims=True)
    return xc * lax.rsqrt(var + LN_EPS) * g + b


def _gelu_tanh(x):
    return 0.5 * x * (1.0 + jnp.tanh(0.7978845608028654 * (x + 0.044715 * (x * x * x))))


def _split_dot(v, m):
    hi = v.astype(BF16)
    lo = (v - hi.astype(F32)).astype(BF16)
    return _dot(hi, m) + _dot(lo, m)


def _params(*sem):
    return pltpu.CompilerParams(dimension_semantics=sem, vmem_limit_bytes=VMEM_LIMIT)


def _full(shape):
    nd = len(shape)
    return pl.BlockSpec(shape, lambda *_: (0,) * nd)


def _inproj_kernel(apply_ln, x_ref, eg_ref, eb_ref, wnat_ref, wfb_ref, wcq_ref, wckv_ref, wkr_ref,
                   wsg_ref, qg_ref, wa_ref, wb_ref, kvg_ref, wkn_ref, wvt_ref, e_ref, ca_ref, cb_ref,
                   tk_ref, gmat_ref, slg_ref, slb_ref, ws_ref, sb_ref, *out_refs):
    if apply_ln:
        xn_ref, nat_ref, fb_ref, q_ref, k_ref, vt_ref, od_ref = out_refs
    else:
        nat_ref, fb_ref, q_ref, k_ref, vt_ref, od_ref = out_refs
    x = x_ref[...]
    if apply_ln:
        x = _layer_norm_rows(x, eg_ref[...], eb_ref[...])
        xn_ref[...] = x
    xb = x.astype(BF16)
    tm = x.shape[0]

    nat = _dot(xb, wnat_ref[...])
    nat_ref[:, :NA_WIDTH] = (nat[:, :NA_WIDTH] * (HEAD_DIM ** -0.5)).astype(BF16)
    nat_ref[:, NA_WIDTH:] = nat[:, NA_WIDTH:].astype(BF16)

    fb_ref[...] = _dot(xb, wfb_ref[...])

    cq = _dot(xb, wcq_ref[...])
    cqn = (cq * lax.rsqrt(jnp.mean(cq * cq, axis=-1, keepdims=True) + RMS_EPS) * qg_ref[...]).astype(BF16)
    qa = _dot_nt(wa_ref[...], cqn)
    qb = _dot_nt(wb_ref[...], cqn)
    ca = ca_ref[...]
    cb = cb_ref[...]
    for h in range(MLA_HEADS):
        sl = slice(h * MLA_PAD, (h + 1) * MLA_PAD)
        q_ref[sl, :] = (qa[sl] * ca + qb[sl] * cb).astype(BF16)

    ckv = _dot(xb, wckv_ref[...])
    ckvn = (ckv * lax.rsqrt(jnp.mean(ckv * ckv, axis=-1, keepdims=True) + RMS_EPS) * kvg_ref[...]).astype(BF16)
    kr = _dot(xb, wkr_ref[...]) * tk_ref[...]
    kr = kr + pltpu.roll(kr, LANES - MLA_ROPE, axis=1)
    k_ref[...] = (_dot(ckvn, wkn_ref[...]) + _dot(kr.astype(BF16), e_ref[...])).astype(BF16)
    vt = _dot_nt(wvt_ref[...], ckvn)
    tkc = vt_ref.shape[2]
    for c in range(tm // tkc):
        vt_ref[c] = vt[:, c * tkc:(c + 1) * tkc].astype(BF16)

    sg = _gelu_tanh(_dot(xb, wsg_ref[...]))
    u = sg[:, :SGU_WIDTH]
    v = sg[:, SGU_WIDTH:]
    gmat = gmat_ref[...]
    mu = _split_dot(v, gmat)
    vc = v - mu
    var = _split_dot(vc * vc, gmat)
    vn = (vc * lax.rsqrt(var + LN_EPS) * slg_ref[...] + slb_ref[...]).astype(BF16)
    head = lax.broadcasted_iota(jnp.int32, (1, SGU_WIDTH), 1) // SGU_HEAD_DIM
    zero = jnp.zeros((), BF16)
    for c in range(tm // SGU_CHUNK):
        rows = slice(c * SGU_CHUNK, (c + 1) * SGU_CHUNK)
        vch = vn[rows]
        mixed = sb_ref[...]
        for g in range(SGU_HEADS):
            mixed = mixed + _dot(ws_ref[g], jnp.where(head == g, vch, zero))
        od_ref[rows, :] = (u[rows] * mixed).astype(BF16)


def _inproj(x, lp, tabs, apply_ln, T):
    n = x.shape[0]
    tm = TM_IN
    nt = T // tm
    row = lambda i: (i, 0)
    tab = lambda i: (i % nt, 0)
    weights = [lp['eg'], lp['eb'], lp['w_nat'], lp['w_fb'], lp['w_cq'], lp['w_ckv'], lp['w_kr'], lp['w_sg'],
               lp['q_g'], lp['w_a'], lp['w_b'], lp['kv_g'], lp['w_kn'], lp['w_vt'], lp['e_place']]
    tables = [tabs['ca'], tabs['cb'], tabs['tk']]
    tail = [lp['gmat'], lp['sl_g'], lp['sl_b'], lp['w_s'], lp['s_b']]
    tab_t = lambda i: (0, i % nt)
    in_specs = ([pl.BlockSpec((tm, D_MODEL), row)] + [_full(w.shape) for w in weights]
                + [pl.BlockSpec((MLA_PAD, tm), tab_t), pl.BlockSpec((MLA_PAD, tm), tab_t),
                   pl.BlockSpec((tm, LANES), tab)] + [_full(w.shape) for w in tail])
    out_shape = [jax.ShapeDtypeStruct((n, 3 * NA_WIDTH), BF16),
                 jax.ShapeDtypeStruct((n, FN_WIDTH), F32),
                 jax.ShapeDtypeStruct((MLA_HEADS * MLA_PAD, n), BF16),
                 jax.ShapeDtypeStruct((n, MLA_HEADS * MLA_PAD), BF16),
                 jax.ShapeDtypeStruct((n // MLA_TK, MLA_WIDTH, MLA_TK), BF16),
                 jax.ShapeDtypeStruct((n, SGU_WIDTH), BF16)]
    out_specs = [pl.BlockSpec((tm, 3 * NA_WIDTH), row),
                 pl.BlockSpec((tm, FN_WIDTH), row),
                 pl.BlockSpec((MLA_HEADS * MLA_PAD, tm), lambda i: (0, i)),
                 pl.BlockSpec((tm, MLA_HEADS * MLA_PAD), row),
                 pl.BlockSpec((tm // MLA_TK, MLA_WIDTH, MLA_TK), lambda i: (i, 0, 0)),
                 pl.BlockSpec((tm, SGU_WIDTH), row)]
    if apply_ln:
        out_shape = [jax.ShapeDtypeStruct((n, D_MODEL), F32)] + out_shape
        out_specs = [pl.BlockSpec((tm, D_MODEL), row)] + out_specs
    return pl.pallas_call(
        functools.partial(_inproj_kernel, apply_ln),
        grid=(n // tm,),
        in_specs=in_specs,
        out_specs=out_specs,
        out_shape=out_shape,
        compiler_params=_params("parallel"),
        name="inproj_ln" if apply_ln else "inproj",
    )(x, *weights, *tables, *tail)


def _natten_kernel(q_ref, k0_ref, k1_ref, k2_ref, v0_ref, v1_ref, v2_ref, bias_ref, o_ref):
    q = q_ref[...]
    k = jnp.concatenate([k0_ref[...], k1_ref[...], k2_ref[...]], axis=0)
    v = jnp.concatenate([v0_ref[...], v1_ref[...], v2_ref[...]], axis=0)
    head = lax.broadcasted_iota(jnp.int32, (1, NA_WIDTH), 1) // HEAD_DIM
    zero = jnp.zeros((), BF16)
    acc = jnp.zeros((NA_QBLK, NA_WIDTH), F32)
    for h in range(NA_HEADS):
        sel = head == h
        s = _dot_nt(jnp.where(sel, q, zero), k) + bias_ref[h]
        m = jnp.max(s, axis=-1, keepdims=True)
        p = jnp.exp(s - m)
        l = jnp.sum(p, axis=-1, keepdims=True)
        acc = acc + _dot(p.astype(BF16), jnp.where(sel, v, zero)) * (1.0 / l)
    o_ref[...] = acc.astype(BF16)


def _natten(nat, bias, B, T):
    n = nat.shape[0]
    nblk = T // NA_QBLK

    def kv_map(d, col):
        def f(b, j):
            base = jnp.clip(j - 1, 0, nblk - NA_KBLKS)
            return (b * nblk + base + d, col)
        return f

    def bias_map(b, j):
        typ = jnp.where(j == 0, 0, jnp.where(j == nblk - 1, 2, 1))
        return (typ, 0, 0, 0)

    blk = (NA_QBLK, NA_WIDTH)
    in_specs = ([pl.BlockSpec(blk, lambda b, j: (b * nblk + j, 0))]
                + [pl.BlockSpec(blk, kv_map(d, 1)) for d in range(NA_KBLKS)]
                + [pl.BlockSpec(blk, kv_map(d, 2)) for d in range(NA_KBLKS)]
                + [pl.BlockSpec((None, NA_HEADS, NA_QBLK, NA_KBLKS * NA_QBLK), bias_map)])
    return pl.pallas_call(
        _natten_kernel,
        grid=(B, nblk),
        in_specs=in_specs,
        out_specs=pl.BlockSpec(blk, lambda b, j: (b * nblk + j, 0)),
        out_shape=jax.ShapeDtypeStruct((n, NA_WIDTH), BF16),
        compiler_params=_params("parallel", "arbitrary"),
        name="natten",
    )(nat, nat, nat, nat, nat, nat, nat, bias)


def _split_bf16(x):
    hi = x.astype(BF16)
    return hi, (x - hi.astype(F32)).astype(BF16)


def _fft1_kernel(x_ref, m_ref, a_ref):
    kk = m_ref.shape[0]
    n1 = x_ref.shape[0]
    xt = pltpu.einshape("abc->bac", x_ref[...])
    for i in range(kk):
        x_hi, x_lo = _split_bf16(xt[i])
        m = m_ref[i]
        r = _dot(m, x_hi)
        res = r[:2 * n1] + r[2 * n1:] + _dot(m[:2 * n1], x_lo)
        a_ref[i, :, :FN_WIDTH] = res[:n1]
        a_ref[i, :, FN_WIDTH:] = res[n1:]


def _fft2_kernel(a_ref, cs_ref, w_ref, y_ref):
    n2, kk, _ = a_ref.shape
    cs = cs_ref[...]
    at = pltpu.einshape("abc->bac", a_ref[...])
    g = []
    for i in range(kk):
        a_hi, a_lo = _split_bf16(at[i])
        r = _dot(cs, a_hi)
        r = r[:2 * n2] + r[2 * n2:] + _dot(cs[:2 * n2], a_lo)
        g.append(jnp.concatenate([r[:n2, :FN_WIDTH] + r[n2:, FN_WIDTH:],
                                  r[:n2, FN_WIDTH:] - r[n2:, :FN_WIDTH]], axis=1))
    g_hi, g_lo = _split_bf16(jnp.concatenate(g, axis=0))
    y = _dot(g_hi, w_ref[0]) + _dot(g_lo, w_ref[0]) + _dot(g_hi, w_ref[1])
    y_ref[...] = pltpu.einshape("abc->bac", y.reshape(kk, n2, FN_WIDTH))


def _fourier(fb, ft, B, T):
    n1, n2 = ft['n1'], ft['n2']
    kk = SUBLANES
    a = pl.pallas_call(
        _fft1_kernel,
        grid=(B, n2 // kk),
        in_specs=[pl.BlockSpec((None, n1, kk, FN_WIDTH), lambda b, j: (b, 0, j, 0)),
                  pl.BlockSpec((kk, 4 * n1, n1), lambda b, j: (j, 0, 0))],
        out_specs=pl.BlockSpec((None, kk, n1, 2 * FN_WIDTH), lambda b, j: (b, j, 0, 0)),
        out_shape=jax.ShapeDtypeStruct((B, n2, n1, 2 * FN_WIDTH), F32),
        compiler_params=_params("parallel", "arbitrary"),
        name="fft_stage1",
    )(fb.reshape(B, n1, n2, FN_WIDTH), ft['m1'])
    y = pl.pallas_call(
        _fft2_kernel,
        grid=(B, n1 // kk),
        in_specs=[pl.BlockSpec((None, n2, kk, 2 * FN_WIDTH), lambda b, j: (b, 0, j, 0)),
                  _full(ft['cs2'].shape), _full(ft['w'].shape)],
        out_specs=pl.BlockSpec((None, n2, kk, FN_WIDTH), lambda b, j: (b, 0, j, 0)),
        out_shape=jax.ShapeDtypeStruct((B, n2, n1, FN_WIDTH), F32),
        compiler_params=_params("parallel", "arbitrary"),
        name="fft_stage2",
    )(a, ft['cs2'], ft['w'])
    return y.reshape(B * T, FN_WIDTH)


def _mla_unroll(nk):
    return max(2, min(MLA_UNROLL, nk // 2))


def _mla_kernel(q_ref, k_ref, vt_ref, o_ref, s_ref):
    nk, _, tk = vt_ref.shape
    tq = q_ref.shape[1]
    q = q_ref[...]
    ones = jnp.ones((MLA_ONES, tk), BF16)

    def scores(j):
        start = j * tk if isinstance(j, int) else pl.multiple_of(j * tk, tk)
        return _dot(k_ref[pl.ds(start, tk), :], q)

    def update(carry, j, slot):
        m, acc = carry
        st = s_ref[slot]
        m_new = jnp.maximum(m, jnp.max(st, axis=0, keepdims=True))
        alpha = jnp.exp2(m - m_new)
        p = jnp.exp2(st - m_new).astype(BF16)
        vte = jnp.concatenate([vt_ref[j], ones], axis=0)
        return m_new, alpha * acc + _dot(vte, p)

    unroll = _mla_unroll(nk)
    trips = nk // unroll
    s_ref[0] = scores(0)

    def body(t, carry):
        for c in range(unroll):
            j = t * unroll + c
            if trips > 1:
                s_ref[(c + 1) % 2] = scores(jnp.minimum(j + 1, nk - 1))
            elif c + 1 < unroll:
                s_ref[(c + 1) % 2] = scores(j + 1)
            carry = update(carry, j, c % 2)
        return carry

    init = (jnp.full((1, tq), -1e30, F32), jnp.zeros((MLA_V + MLA_ONES, tq), F32))
    _, acc = body(0, init) if trips == 1 else lax.fori_loop(0, trips, body, init)
    o_ref[...] = acc[:MLA_V] * (1.0 / acc[MLA_V:MLA_V + 1])


def _mla(q, k, vt, B, T):
    n = k.shape[0]
    nq = T // MLA_TQ
    nk = T // MLA_TK
    assert nk % _mla_unroll(nk) == 0
    return pl.pallas_call(
        _mla_kernel,
        grid=(B, MLA_HEADS, nq),
        in_specs=[pl.BlockSpec((MLA_PAD, MLA_TQ), lambda b, h, i: (h, b * nq + i)),
                  pl.BlockSpec((T, MLA_PAD), lambda b, h, i: (b, h)),
                  pl.BlockSpec((nk, MLA_V, MLA_TK), lambda b, h, i: (b, h, 0))],
        out_specs=pl.BlockSpec((MLA_V, MLA_TQ), lambda b, h, i: (h, b * nq + i)),
        out_shape=jax.ShapeDtypeStruct((MLA_WIDTH, n), F32),
        scratch_shapes=[pltpu.VMEM((2, MLA_TK, MLA_TQ), F32)],
        compiler_params=_params("parallel", "parallel", "arbitrary"),
        name="mla",
    )(q, k, vt)


def _outproj_kernel(oa_ref, ob_ref, oct_ref, od_ref, x_ref, w_ref, g_ref, b_ref, o_ref):
    w = NA_WIDTH
    oc = oct_ref[...].T.astype(BF16)
    y = (_dot(oa_ref[...], w_ref[0:w]) + _dot(ob_ref[...].astype(BF16), w_ref[w:2 * w])
         + _dot(oc, w_ref[2 * w:3 * w]) + _dot(od_ref[...], w_ref[3 * w:4 * w]))
    o_ref[...] = _layer_norm_rows(ALPHA * x_ref[...] + y, g_ref[...], b_ref[...])


def _outproj(oa, ob, oct, od, x, lp):
    n = x.shape[0]
    tm = TM_OUT
    row = lambda i: (i, 0)
    return pl.pallas_call(
        _outproj_kernel,
        grid=(n // tm,),
        in_specs=[pl.BlockSpec((tm, NA_WIDTH), row), pl.BlockSpec((tm, FN_WIDTH), row),
                  pl.BlockSpec((MLA_WIDTH, tm), lambda i: (0, i)), pl.BlockSpec((tm, SGU_WIDTH), row),
                  pl.BlockSpec((tm, D_MODEL), row), _full(lp['w_out'].shape),
                  _full(lp['ln1_g'].shape), _full(lp['ln1_b'].shape)],
        out_specs=pl.BlockSpec((tm, D_MODEL), row),
        out_shape=jax.ShapeDtypeStruct((n, D_MODEL), F32),
        compiler_params=_params("parallel"),
        name="outproj",
    )(oa, ob, oct, od, x, lp['w_out'], lp['ln1_g'], lp['ln1_b'])


def _ffn_kernel(nt, x_ref, xp_ref, xn_ref, wup_ref, cw_ref, cb_ref, wd_ref, g_ref, b_ref, o_ref,
                xe_ref, hg_ref, hv_ref, act_ref):
    i = pl.program_id(0)
    tm = x_ref.shape[0]
    cf = hg_ref.shape[2]
    first = (i % nt) == 0
    last = (i % nt) == nt - 1
    xe_ref[0:HALO] = jnp.where(first, 0.0, xp_ref[...]).astype(BF16)
    xe_ref[HALO:HALO + tm] = x_ref[...].astype(BF16)
    xe_ref[HALO + tm:] = jnp.where(last, 0.0, xn_ref[...]).astype(BF16)
    xe = xe_ref[...]

    def conv(h_ref, cols):
        return (h_ref[pl.ds(HALO - 1, tm), :] * cw_ref[0:1, cols] + h_ref[pl.ds(HALO, tm), :] * cw_ref[1:2, cols]
                + h_ref[pl.ds(HALO + 1, tm), :] * cw_ref[2:3, cols] + cb_ref[:, cols])

    for c in range(D_FF // cf):
        slot = c % 2
        gate = slice(c * cf, (c + 1) * cf)
        val = slice(D_FF + c * cf, D_FF + (c + 1) * cf)
        hg_ref[slot] = _dot(xe, wup_ref[:, gate])
        hv_ref[slot] = _dot(xe, wup_ref[:, val])
        act = _gelu_tanh(conv(hg_ref.at[slot], gate)) * conv(hv_ref.at[slot], val)
        act_ref[:, gate] = act.astype(BF16)

    y = _dot(act_ref[...], wd_ref[...])
    o_ref[...] = _layer_norm_rows(ALPHA * x_ref[...] + y, g_ref[...], b_ref[...])


def _resident(shape):
    nd = len(shape)
    return pl.BlockSpec(shape, lambda *_: (0,) * nd, pipeline_mode=pl.Buffered(1))


def _ffn(x, lp, T):
    n = x.shape[0]
    tm = TM_FFN
    nt = T // tm
    cf = FF_CHUNK
    hb = tm // HALO
    nhb = n // HALO
    row = lambda i: (i, 0)
    return pl.pallas_call(
        functools.partial(_ffn_kernel, nt),
        grid=(n // tm,),
        in_specs=[pl.BlockSpec((tm, D_MODEL), row),
                  pl.BlockSpec((HALO, D_MODEL), lambda i: (jnp.maximum(i * hb - 1, 0), 0)),
                  pl.BlockSpec((HALO, D_MODEL), lambda i: (jnp.minimum((i + 1) * hb, nhb - 1), 0)),
                  _resident(lp['w_up'].shape), _resident(lp['conv_w'].shape), _resident(lp['conv_b'].shape),
                  _resident(lp['w_down'].shape), _resident(lp['ln2_g'].shape), _resident(lp['ln2_b'].shape)],
        out_specs=pl.BlockSpec((tm, D_MODEL), row),
        out_shape=jax.ShapeDtypeStruct((n, D_MODEL), F32),
        scratch_shapes=[pltpu.VMEM((tm + 2 * HALO, D_MODEL), BF16),
                        pltpu.VMEM((2, tm + 2 * HALO, cf), F32),
                        pltpu.VMEM((2, tm + 2 * HALO, cf), F32),
                        pltpu.VMEM((tm, D_FF), BF16)],
        compiler_params=_params("parallel"),
        name="conv_ffn",
    )(x, x, x, lp['w_up'], lp['conv_w'], lp['conv_b'], lp['w_down'], lp['ln2_g'], lp['ln2_b'])


def _natten_bias(rpb):
    nr, nc = 2 * NA_ROWS - 1, 2 * NA_COLS - 1
    krows = NA_KBLKS * NA_QROWS
    col = np.arange(GRID_W)
    ci = np.clip(col[None, :] - col[:, None] + NA_COLS - 1, 0, nc - 1)
    onehot_c = (ci.reshape(-1)[None, :] == np.arange(nc)[:, None]).astype(np.float32)
    cs = np.clip(col - NA_COLS // 2, 0, GRID_W - NA_COLS)
    dc = col[None, :] - cs[:, None]
    valid_c = (dc >= 0) & (dc < NA_COLS)
    qa = np.arange(NA_QROWS)
    kr = np.arange(krows)
    onehot_r, valid_r = [], []
    for kind in range(3):
        q_row = (0, NA_QROWS, 2 * NA_QROWS)[kind] + qa
        r_start = (np.zeros_like(qa), qa, np.full_like(qa, NA_QROWS))[kind]
        dr = kr[None, :] - r_start[:, None]
        valid_r.append((dr >= 0) & (dr < NA_ROWS))
        ri = np.clip(kr[None, :] - q_row[:, None] + NA_ROWS - 1, 0, nr - 1)
        onehot_r.append((ri.reshape(-1)[:, None] == np.arange(nr)[None, :]).astype(np.float32))
    onehot_r = np.stack(onehot_r)
    valid = np.stack(valid_r)[:, :, None, :, None] & valid_c[None, None, :, None, :]
    by_col = jnp.einsum('hrc,cx->hrx', rpb, onehot_c, precision=HI)
    full = jnp.einsum('kpr,hrx->khpx', onehot_r, by_col, precision=HI)
    full = full.reshape(3, NA_HEADS, NA_QROWS, krows, GRID_W, GRID_W).transpose(0, 1, 2, 4, 3, 5)
    full = jnp.where(valid[:, None], full, -1e30)
    return full.reshape(3, NA_HEADS, NA_QBLK, krows * GRID_W).astype(F32)


def _prep_layer(l, emb_ln_g, emb_ln_b, w_in, na_rpb, mla_q_g, w_uq, mla_kv_g, w_ukv, sgu_ln_g, sgu_ln_b,
                sgu_w, sgu_b, w_out, ln1_g, ln1_b, w_up, conv_w, conv_b, w_down, ln2_g, ln2_b):
    wi = w_in[l]
    o_fb = 3 * NA_WIDTH
    o_cq = o_fb + FN_WIDTH
    o_ckv = o_cq + MLA_Q_LORA
    o_kr = o_ckv + MLA_KV_LORA
    o_sg = o_kr + MLA_ROPE
    half = MLA_ROPE // 2
    swap = np.concatenate([np.arange(half, MLA_ROPE), np.arange(half)])
    w_kr = wi[:, o_kr:o_sg]
    w_kr2 = jnp.concatenate([w_kr, w_kr[:, swap], jnp.zeros((D_MODEL, LANES - 2 * MLA_ROPE), F32)], axis=1)

    uq = w_uq[l]
    pad = jnp.zeros((MLA_Q_LORA, MLA_HEADS, MLA_PAD - MLA_NOPE - MLA_ROPE), F32)
    w_a = jnp.concatenate([uq, pad], axis=2).reshape(MLA_Q_LORA, MLA_HEADS * MLA_PAD)
    w_b = jnp.concatenate([jnp.zeros((MLA_Q_LORA, MLA_HEADS, MLA_NOPE), F32),
                           uq[:, :, MLA_NOPE:][:, :, swap], pad], axis=2).reshape(MLA_Q_LORA, MLA_HEADS * MLA_PAD)
    ukv = w_ukv[l]
    w_kn = jnp.concatenate([ukv[:, :, :MLA_NOPE],
                            jnp.zeros((MLA_KV_LORA, MLA_HEADS, MLA_PAD - MLA_NOPE), F32)],
                           axis=2).reshape(MLA_KV_LORA, MLA_HEADS * MLA_PAD)
    w_vt = ukv[:, :, MLA_NOPE:].reshape(MLA_KV_LORA, MLA_WIDTH).T
    e_place = np.zeros((LANES, MLA_HEADS * MLA_PAD), np.float32)
    for h in range(MLA_HEADS):
        e_place[np.arange(MLA_ROPE), h * MLA_PAD + MLA_NOPE + np.arange(MLA_ROPE)] = 1.0
    gmat = np.kron(np.eye(SGU_HEADS), np.full((SGU_HEAD_DIM, SGU_HEAD_DIM), 1.0 / SGU_HEAD_DIM)).astype(np.float32)
    s_b = jnp.repeat(sgu_b[l].T, SGU_HEAD_DIM, axis=1)
    r1 = lambda a: a.reshape(1, -1).astype(F32)
    return dict(
        eg=r1(emb_ln_g), eb=r1(emb_ln_b),
        w_nat=wi[:, :o_fb].astype(BF16), w_fb=wi[:, o_fb:o_cq].astype(BF16),
        w_cq=wi[:, o_cq:o_ckv].astype(BF16), w_ckv=wi[:, o_ckv:o_kr].astype(BF16),
        w_kr=w_kr2.astype(BF16), w_sg=wi[:, o_sg:].astype(BF16),
        q_g=r1(mla_q_g[l]), w_a=w_a.T.astype(BF16), w_b=w_b.T.astype(BF16),
        kv_g=r1(mla_kv_g[l]), w_kn=w_kn.astype(BF16), w_vt=w_vt.astype(BF16),
        e_place=jnp.asarray(e_place, BF16), gmat=jnp.asarray(gmat, BF16),
        sl_g=r1(sgu_ln_g[l]), sl_b=r1(sgu_ln_b[l]), w_s=sgu_w[l].astype(BF16), s_b=s_b.astype(F32),
        na_bias=_natten_bias(na_rpb[l]),
        w_out=w_out[l].astype(BF16), ln1_g=r1(ln1_g[l]), ln1_b=r1(ln1_b[l]),
        w_up=w_up[l].astype(BF16), conv_w=conv_w[l].astype(F32), conv_b=r1(conv_b[l]),
        w_down=w_down[l].astype(BF16), ln2_g=r1(ln2_g[l]), ln2_b=r1(ln2_b[l]),
    )


def _rope_tables(T):
    inv_freq = ROPE_THETA ** (-jnp.arange(0, MLA_ROPE, 2, dtype=F32) / MLA_ROPE)
    ang = jnp.arange(T, dtype=F32)[:, None] * inv_freq[None, :]
    cos, sin = jnp.cos(ang), jnp.sin(ang)
    cos2 = jnp.concatenate([cos, cos], axis=1)
    sin2 = jnp.concatenate([-sin, sin], axis=1)
    scale = (MLA_NOPE + MLA_ROPE) ** -0.5 * math.log2(math.e)
    zq = jnp.zeros((T, MLA_PAD - MLA_NOPE - MLA_ROPE), F32)
    ca = jnp.concatenate([jnp.full((T, MLA_NOPE), scale, F32), scale * cos2, zq], axis=1)
    cb = jnp.concatenate([jnp.zeros((T, MLA_NOPE), F32), scale * sin2, zq], axis=1)
    tk = jnp.concatenate([cos2, sin2, jnp.zeros((T, LANES - 2 * MLA_ROPE), F32)], axis=1)
    return dict(ca=ca.T, cb=cb.T, tk=tk)


def _dft_tables(T):
    n1 = 1 << (int(math.log2(T)) // 2)
    n2 = T // n1
    two_pi = 2.0 * math.pi

    def cs(num, den):
        ang = (num % den).astype(F32) * (two_pi / den)
        return jnp.cos(ang), jnp.sin(ang)

    t2 = jnp.arange(n2, dtype=jnp.int32)[:, None, None]
    k1 = jnp.arange(n1, dtype=jnp.int32)[None, :, None]
    t1 = jnp.arange(n1, dtype=jnp.int32)[None, None, :]
    c, s = cs(k1 * (t1 * n2 + t2), T)
    def split(x, axis):
        hi = x.astype(BF16)
        lo = (x - hi.astype(F32)).astype(BF16)
        return jnp.concatenate([hi, lo], axis=axis) if axis is not None else jnp.stack([hi, lo])

    m1 = split(jnp.concatenate([c, -s], axis=1), 1)
    k2 = jnp.arange(n2, dtype=jnp.int32)
    c2, s2 = cs(k2[:, None] * k2[None, :], n2)
    cs2 = split(jnp.concatenate([c2, s2], axis=0), 0)
    cg = jnp.arange(FN_GROUP_DIM, dtype=jnp.int32)
    cc, sc = cs(cg[:, None] * cg[None, :], FN_GROUP_DIM)
    norm = 1.0 / math.sqrt(T * FN_GROUP_DIM)
    eye = jnp.eye(FN_GROUPS, dtype=F32)
    w = split(jnp.concatenate([jnp.kron(eye, cc), jnp.kron(eye, sc)], axis=0) * norm, None)
    return dict(n1=n1, n2=n2, m1=m1, cs2=cs2, w=w)


def _trunk(x, layers):
    B, T, _ = x.shape
    tabs = _rope_tables(T)
    ft = _dft_tables(T)
    h = x.reshape(B * T, D_MODEL)
    for l, lp in enumerate(layers):
        outs = _inproj(h, lp, tabs, l == 0, T)
        if l == 0:
            h = outs[0]
            outs = outs[1:]
        nat, fb, q, k, vt, od = outs
        oa = _natten(nat, lp['na_bias'], B, T)
        ob = _fourier(fb, ft, B, T)
        oct = _mla(q, k, vt, B, T)
        h1 = _outproj(oa, ob, oct, od, h, lp)
        h = _ffn(h1, lp, T)
    return h.reshape(B, T, D_MODEL)


def kernel(x_prompt, x_sample, emb_ln_g, emb_ln_b, w_in, na_rpb, mla_q_g, w_uq, mla_kv_g, w_ukv, sgu_ln_g,
           sgu_ln_b, sgu_w, sgu_b, w_out, ln1_g, ln1_b, w_up, conv_w, conv_b, w_down, ln2_g, ln2_b):
    layers = [_prep_layer(l, emb_ln_g, emb_ln_b, w_in, na_rpb, mla_q_g, w_uq, mla_kv_g, w_ukv, sgu_ln_g,
                          sgu_ln_b, sgu_w, sgu_b, w_out, ln1_g, ln1_b, w_up, conv_w, conv_b, w_down,
                          ln2_g, ln2_b) for l in range(DEPTH)]
    return (_trunk(x_prompt, layers), _trunk(x_sample, layers))
```

```python
import functools
import math

import numpy as np
import jax
import jax.numpy as jnp
from jax import lax
from jax.experimental import pallas as pl
from jax.experimental.pallas import tpu as pltpu

F32 = jnp.float32
BF16 = jnp.bfloat16

D_MODEL = 1024
DEPTH = 2
GRID_W = 64
HEAD_DIM = 64
NA_HEADS = 4
NA_ROWS = 8
NA_COLS = 16
NA_WIDTH = NA_HEADS * HEAD_DIM
FN_GROUPS = 4
FN_GROUP_DIM = 64
FN_WIDTH = FN_GROUPS * FN_GROUP_DIM
MLA_HEADS = 4
MLA_Q_LORA = 256
MLA_KV_LORA = 128
MLA_NOPE = 64
MLA_ROPE = 32
MLA_V = 64
MLA_WIDTH = MLA_HEADS * MLA_V
ROPE_THETA = 10000.0
SGU_HEADS = 4
SGU_HEAD_DIM = 64
SGU_WIDTH = SGU_HEADS * SGU_HEAD_DIM
SGU_CHUNK = 128
D_FF = 2816
ALPHA = (2 * DEPTH) ** 0.25
LN_EPS = 1e-5
RMS_EPS = 1e-6

LANES = 128
SUBLANES = 8
VMEM_LIMIT = 56 * 1024 * 1024

TM_IN = 512
TM_OUT = 512
TM_FFN = 512
FF_CHUNK = 256
NA_QROWS = 4
NA_QBLK = NA_QROWS * GRID_W
NA_KBLKS = 3
MLA_PAD = 128
MLA_TQ = 512
MLA_TK = 512
MLA_ONES = 16
MLA_UNROLL = 8
HALO = SUBLANES

HI = lax.Precision.HIGHEST
NT_DIMS = (((1,), (1,)), ((), ()))


def _dot(a, b):
    return jnp.dot(a, b, preferred_element_type=F32)


def _dot_nt(a, b):
    return lax.dot_general(a, b, NT_DIMS, preferred_element_type=F32)


def _dot_hi(a, b):
    return jnp.dot(a, b, precision=HI, preferred_element_type=F32)


def _layer_norm_rows(x, g, b):
    mu = jnp.mean(x, axis=-1, keepdims=True)
    xc = x - mu
    var = jnp.mean(xc * xc, axis=-1, keepdims=True)
    return xc * lax.rsqrt(var + LN_EPS) * g + b


GELU_C = 0.7978845608028654
GELU_K = 0.044715


def _gelu_tanh_x2(x):
    t = jnp.tanh(x * (GELU_C + (GELU_C * GELU_K) * (x * x)))
    return x + x * t


def _gelu_tanh(x):
    return 0.5 * _gelu_tanh_x2(x)


def _split_dot(v, m):
    hi = v.astype(BF16)
    lo = (v - hi.astype(F32)).astype(BF16)
    return _dot(hi, m) + _dot(lo, m)


def _params(*sem):
    return pltpu.CompilerParams(dimension_semantics=sem, vmem_limit_bytes=VMEM_LIMIT)


def _full(shape):
    nd = len(shape)
    return pl.BlockSpec(shape, lambda *_: (0,) * nd)


def _inproj_kernel(apply_ln, x_ref, eg_ref, eb_ref, wnat_ref, wfb_ref, wcq_ref, wckv_ref, wkr_ref,
                   wsg_ref, qg_ref, wa_ref, wb_ref, kvg_ref, wkn_ref, wvt_ref, e_ref, ca_ref, cb_ref,
                   tk_ref, gmat_ref, slg_ref, slb_ref, ws_ref, sb_ref, *out_refs):
    if apply_ln:
        xn_ref, nat_ref, fb_ref, q_ref, k_ref, vt_ref, od_ref = out_refs
    else:
        nat_ref, fb_ref, q_ref, k_ref, vt_ref, od_ref = out_refs
    x = x_ref[...]
    if apply_ln:
        x = _layer_norm_rows(x, eg_ref[...], eb_ref[...])
        xn_ref[...] = x
    xb = x.astype(BF16)
    tm = x.shape[0]

    nat = _dot(xb, wnat_ref[...])
    nat_ref[:, :NA_WIDTH] = (nat[:, :NA_WIDTH] * (HEAD_DIM ** -0.5)).astype(BF16)
    nat_ref[:, NA_WIDTH:] = nat[:, NA_WIDTH:].astype(BF16)

    fb_ref[...] = _dot(xb, wfb_ref[...])

    cq = _dot(xb, wcq_ref[...])
    cqn = (cq * lax.rsqrt(jnp.mean(cq * cq, axis=-1, keepdims=True) + RMS_EPS) * qg_ref[...]).astype(BF16)
    qa = _dot_nt(wa_ref[...], cqn)
    qb = _dot_nt(wb_ref[...], cqn)
    ca = ca_ref[...]
    cb = cb_ref[...]
    for h in range(MLA_HEADS):
        sl = slice(h * MLA_PAD, (h + 1) * MLA_PAD)
        qh = (qa[sl] * ca + qb[sl] * cb).astype(BF16)
        for c in range(tm // MLA_TQ):
            q_ref[c, sl, :] = qh[:, c * MLA_TQ:(c + 1) * MLA_TQ]

    ckv = _dot(xb, wckv_ref[...])
    ckvn = (ckv * lax.rsqrt(jnp.mean(ckv * ckv, axis=-1, keepdims=True) + RMS_EPS) * kvg_ref[...]).astype(BF16)
    kr = _dot(xb, wkr_ref[...]) * tk_ref[...]
    kr = kr + pltpu.roll(kr, LANES - MLA_ROPE, axis=1)
    k_ref[...] = (_dot(ckvn, wkn_ref[...]) + _dot(kr.astype(BF16), e_ref[...])).astype(BF16)
    vt = _dot_nt(wvt_ref[...], ckvn)
    tkc = vt_ref.shape[2]
    for c in range(tm // tkc):
        vt_ref[c] = vt[:, c * tkc:(c + 1) * tkc].astype(BF16)

    sg = _gelu_tanh(_dot(xb, wsg_ref[...]))
    u = sg[:, :SGU_WIDTH]
    v = sg[:, SGU_WIDTH:]
    gmat = gmat_ref[...]
    mu = _split_dot(v, gmat)
    vc = v - mu
    var = _split_dot(vc * vc, gmat)
    vn = (vc * lax.rsqrt(var + LN_EPS) * slg_ref[...] + slb_ref[...]).astype(BF16)
    head = lax.broadcasted_iota(jnp.int32, (1, SGU_WIDTH), 1) // SGU_HEAD_DIM
    zero = jnp.zeros((), BF16)
    for c in range(tm // SGU_CHUNK):
        rows = slice(c * SGU_CHUNK, (c + 1) * SGU_CHUNK)
        vch = vn[rows]
        mixed = sb_ref[...]
        for g in range(SGU_HEADS):
            mixed = mixed + _dot(ws_ref[g], jnp.where(head == g, vch, zero))
        od_ref[rows, :] = (u[rows] * mixed).astype(BF16)


def _inproj(x, lp, tabs, apply_ln, T):
    n = x.shape[0]
    tm = TM_IN
    nt = T // tm
    row = lambda i: (i, 0)
    tab = lambda i: (i % nt, 0)
    weights = [lp['eg'], lp['eb'], lp['w_nat'], lp['w_fb'], lp['w_cq'], lp['w_ckv'], lp['w_kr'], lp['w_sg'],
               lp['q_g'], lp['w_a'], lp['w_b'], lp['kv_g'], lp['w_kn'], lp['w_vt'], lp['e_place']]
    tables = [tabs['ca'], tabs['cb'], tabs['tk']]
    tail = [lp['gmat'], lp['sl_g'], lp['sl_b'], lp['w_s'], lp['s_b']]
    tab_t = lambda i: (0, i % nt)
    in_specs = ([pl.BlockSpec((tm, D_MODEL), row)] + [_full(w.shape) for w in weights]
                + [pl.BlockSpec((MLA_PAD, tm), tab_t), pl.BlockSpec((MLA_PAD, tm), tab_t),
                   pl.BlockSpec((tm, LANES), tab)] + [_full(w.shape) for w in tail])
    out_shape = [jax.ShapeDtypeStruct((n, 3 * NA_WIDTH), BF16),
                 jax.ShapeDtypeStruct((n, FN_WIDTH), F32),
                 jax.ShapeDtypeStruct((n // MLA_TQ, MLA_HEADS * MLA_PAD, MLA_TQ), BF16),
                 jax.ShapeDtypeStruct((n, MLA_HEADS * MLA_PAD), BF16),
                 jax.ShapeDtypeStruct((n // MLA_TK, MLA_WIDTH, MLA_TK), BF16),
                 jax.ShapeDtypeStruct((n, SGU_WIDTH), BF16)]
    out_specs = [pl.BlockSpec((tm, 3 * NA_WIDTH), row),
                 pl.BlockSpec((tm, FN_WIDTH), row),
                 pl.BlockSpec((tm // MLA_TQ, MLA_HEADS * MLA_PAD, MLA_TQ), lambda i: (i, 0, 0)),
                 pl.BlockSpec((tm, MLA_HEADS * MLA_PAD), row),
                 pl.BlockSpec((tm // MLA_TK, MLA_WIDTH, MLA_TK), lambda i: (i, 0, 0)),
                 pl.BlockSpec((tm, SGU_WIDTH), row)]
    if apply_ln:
        out_shape = [jax.ShapeDtypeStruct((n, D_MODEL), F32)] + out_shape
        out_specs = [pl.BlockSpec((tm, D_MODEL), row)] + out_specs
    return pl.pallas_call(
        functools.partial(_inproj_kernel, apply_ln),
        grid=(n // tm,),
        in_specs=in_specs,
        out_specs=out_specs,
        out_shape=out_shape,
        compiler_params=_params("parallel"),
        name="inproj_ln" if apply_ln else "inproj",
    )(x, *weights, *tables, *tail)


def _natten_kernel(q_ref, k0_ref, k1_ref, k2_ref, v0_ref, v1_ref, v2_ref, bias_ref, o_ref, s_ref):
    q = q_ref[...]
    k = jnp.concatenate([k0_ref[...], k1_ref[...], k2_ref[...]], axis=0)
    v = jnp.concatenate([v0_ref[...], v1_ref[...], v2_ref[...]], axis=0)
    head = lax.broadcasted_iota(jnp.int32, (1, NA_WIDTH), 1) // HEAD_DIM
    zero = jnp.zeros((), BF16)

    def scores(h):
        return _dot_nt(jnp.where(head == h, q, zero), k) + bias_ref[h]

    s_ref[0] = scores(0)
    acc = jnp.zeros((NA_QBLK, NA_WIDTH), F32)
    for h in range(NA_HEADS):
        if h + 1 < NA_HEADS:
            s_ref[(h + 1) % 2] = scores(h + 1)
        s = s_ref[h % 2]
        m = jnp.max(s, axis=-1, keepdims=True)
        p = jnp.exp(s - m)
        l = jnp.sum(p, axis=-1, keepdims=True)
        acc = acc + _dot(p.astype(BF16), jnp.where(head == h, v, zero)) * (1.0 / l)
    o_ref[...] = acc.astype(BF16)


def _natten(nat, bias, B, T):
    n = nat.shape[0]
    nblk = T // NA_QBLK

    def kv_map(d, col):
        def f(b, j):
            base = jnp.clip(j - 1, 0, nblk - NA_KBLKS)
            return (b * nblk + base + d, col)
        return f

    def bias_map(b, j):
        typ = jnp.where(j == 0, 0, jnp.where(j == nblk - 1, 2, 1))
        return (typ, 0, 0, 0)

    blk = (NA_QBLK, NA_WIDTH)
    in_specs = ([pl.BlockSpec(blk, lambda b, j: (b * nblk + j, 0))]
                + [pl.BlockSpec(blk, kv_map(d, 1)) for d in range(NA_KBLKS)]
                + [pl.BlockSpec(blk, kv_map(d, 2)) for d in range(NA_KBLKS)]
                + [pl.BlockSpec((None, NA_HEADS, NA_QBLK, NA_KBLKS * NA_QBLK), bias_map)])
    return pl.pallas_call(
        _natten_kernel,
        grid=(B, nblk),
        in_specs=in_specs,
        out_specs=pl.BlockSpec(blk, lambda b, j: (b * nblk + j, 0)),
        out_shape=jax.ShapeDtypeStruct((n, NA_WIDTH), BF16),
        scratch_shapes=[pltpu.VMEM((2, NA_QBLK, NA_KBLKS * NA_QBLK), F32)],
        compiler_params=_params("parallel", "arbitrary"),
        name="natten",
    )(nat, nat, nat, nat, nat, nat, nat, bias)


def _split_bf16(x):
    hi = x.astype(BF16)
    return hi, (x - hi.astype(F32)).astype(BF16)


def _fft1_kernel(x_ref, m_ref, a_ref):
    kk = m_ref.shape[0]
    n1 = x_ref.shape[0]
    xt = pltpu.einshape("abc->bac", x_ref[...])
    for i in range(kk):
        x_hi, x_lo = _split_bf16(xt[i])
        m = m_ref[i]
        r = _dot(m, x_hi)
        res = r[:2 * n1] + r[2 * n1:] + _dot(m[:2 * n1], x_lo)
        a_ref[i, :, :FN_WIDTH] = res[:n1]
        a_ref[i, :, FN_WIDTH:] = res[n1:]


def _fft2_kernel(a_ref, cs_ref, w_ref, y_ref):
    n2, kk, _ = a_ref.shape
    cs = cs_ref[...]
    at = pltpu.einshape("abc->bac", a_ref[...])
    g = []
    for i in range(kk):
        a_hi, a_lo = _split_bf16(at[i])
        r = _dot(cs, a_hi)
        r = r[:2 * n2] + r[2 * n2:] + _dot(cs[:2 * n2], a_lo)
        g.append(jnp.concatenate([r[:n2, :FN_WIDTH] + r[n2:, FN_WIDTH:],
                                  r[:n2, FN_WIDTH:] - r[n2:, :FN_WIDTH]], axis=1))
    g_hi, g_lo = _split_bf16(jnp.concatenate(g, axis=0))
    y = _dot(g_hi, w_ref[0]) + _dot(g_lo, w_ref[0]) + _dot(g_hi, w_ref[1])
    y_ref[...] = pltpu.einshape("abc->bac", y.reshape(kk, n2, FN_WIDTH))


def _fourier(fb, ft, B, T):
    n1, n2 = ft['n1'], ft['n2']
    kk = SUBLANES
    a = pl.pallas_call(
        _fft1_kernel,
        grid=(B, n2 // kk),
        in_specs=[pl.BlockSpec((None, n1, kk, FN_WIDTH), lambda b, j: (b, 0, j, 0)),
                  pl.BlockSpec((kk, 4 * n1, n1), lambda b, j: (j, 0, 0))],
        out_specs=pl.BlockSpec((None, kk, n1, 2 * FN_WIDTH), lambda b, j: (b, j, 0, 0)),
        out_shape=jax.ShapeDtypeStruct((B, n2, n1, 2 * FN_WIDTH), F32),
        compiler_params=_params("parallel", "arbitrary"),
        name="fft_stage1",
    )(fb.reshape(B, n1, n2, FN_WIDTH), ft['m1'])
    y = pl.pallas_call(
        _fft2_kernel,
        grid=(B, n1 // kk),
        in_specs=[pl.BlockSpec((None, n2, kk, 2 * FN_WIDTH), lambda b, j: (b, 0, j, 0)),
                  _full(ft['cs2'].shape), _full(ft['w'].shape)],
        out_specs=pl.BlockSpec((None, n2, kk, FN_WIDTH), lambda b, j: (b, 0, j, 0)),
        out_shape=jax.ShapeDtypeStruct((B, n2, n1, FN_WIDTH), F32),
        compiler_params=_params("parallel", "arbitrary"),
        name="fft_stage2",
    )(a, ft['cs2'], ft['w'])
    return y.reshape(B * T, FN_WIDTH)


def _mla_kernel(q_ref, k_ref, vt_ref, o_ref, s_ref):
    nk, _, tk = vt_ref.shape
    nq, _, tq = q_ref.shape
    unroll = min(MLA_UNROLL, nk)
    ones = jnp.ones((MLA_ONES, tk), BF16)

    def scores(i, j):
        return _dot(k_ref[pl.ds(pl.multiple_of(j * tk, tk), tk), :], q_ref[i])

    def update(carry, j, slot):
        m, acc = carry
        st = s_ref[slot]
        m_new = jnp.maximum(m, jnp.max(st, axis=0, keepdims=True))
        alpha = jnp.exp2(m - m_new)
        p = jnp.exp2(st - m_new).astype(BF16)
        vte = jnp.concatenate([vt_ref[j], ones], axis=0)
        return m_new, alpha * acc + _dot(vte, p)

    s_ref[0] = scores(0, 0)
    m0 = jnp.full((1, tq), -1e30, F32)
    acc0 = jnp.zeros((MLA_V + MLA_ONES, tq), F32)

    def body(t, carry):
        i = (t * unroll) // nk
        j0 = (t * unroll) % nk
        fresh = j0 == 0
        carry = (jnp.where(fresh, m0, carry[0]), jnp.where(fresh, acc0, carry[1]))
        for c in range(unroll):
            if c + 1 < unroll:
                s_ref[(c + 1) % 2] = scores(i, j0 + c + 1)
            else:
                nxt = jnp.minimum(t * unroll + unroll, nq * nk - 1)
                s_ref[(c + 1) % 2] = scores(nxt // nk, nxt % nk)
            carry = update(carry, j0 + c, c % 2)

        @pl.when(j0 + unroll == nk)
        def _():
            acc = carry[1]
            o_ref[i] = acc[:MLA_V] * (1.0 / acc[MLA_V:MLA_V + 1])

        return carry

    lax.fori_loop(0, nq * nk // unroll, body, (m0, acc0))


def _mla(q, k, vt, B, T):
    nq = T // MLA_TQ
    nk = T // MLA_TK
    assert nk % min(MLA_UNROLL, nk) == 0 and min(MLA_UNROLL, nk) % 2 == 0
    return pl.pallas_call(
        _mla_kernel,
        grid=(B, MLA_HEADS),
        in_specs=[pl.BlockSpec((nq, MLA_PAD, MLA_TQ), lambda b, h: (b, h, 0)),
                  pl.BlockSpec((T, MLA_PAD), lambda b, h: (b, h)),
                  pl.BlockSpec((nk, MLA_V, MLA_TK), lambda b, h: (b, h, 0))],
        out_specs=pl.BlockSpec((nq, MLA_V, MLA_TQ), lambda b, h: (b, h, 0)),
        out_shape=jax.ShapeDtypeStruct((B * nq, MLA_WIDTH, MLA_TQ), F32),
        scratch_shapes=[pltpu.VMEM((2, MLA_TK, MLA_TQ), F32)],
        compiler_params=_params("parallel", "arbitrary"),
        name="mla",
    )(q, k, vt)


def _outproj_kernel(oa_ref, ob_ref, oct_ref, od_ref, x_ref, w_ref, g_ref, b_ref, o_ref):
    w = NA_WIDTH
    oc = oct_ref[...].T.astype(BF16)
    y = (_dot(oa_ref[...], w_ref[0:w]) + _dot(ob_ref[...].astype(BF16), w_ref[w:2 * w])
         + _dot(oc, w_ref[2 * w:3 * w]) + _dot(od_ref[...], w_ref[3 * w:4 * w]))
    o_ref[...] = _layer_norm_rows(ALPHA * x_ref[...] + y, g_ref[...], b_ref[...])


def _outproj(oa, ob, oct, od, x, lp):
    n = x.shape[0]
    tm = TM_OUT
    row = lambda i: (i, 0)
    return pl.pallas_call(
        _outproj_kernel,
        grid=(n // tm,),
        in_specs=[pl.BlockSpec((tm, NA_WIDTH), row), pl.BlockSpec((tm, FN_WIDTH), row),
                  pl.BlockSpec((None, MLA_WIDTH, tm), lambda i: (i, 0, 0)), pl.BlockSpec((tm, SGU_WIDTH), row),
                  pl.BlockSpec((tm, D_MODEL), row), _full(lp['w_out'].shape),
                  _full(lp['ln1_g'].shape), _full(lp['ln1_b'].shape)],
        out_specs=pl.BlockSpec((tm, D_MODEL), row),
        out_shape=jax.ShapeDtypeStruct((n, D_MODEL), F32),
        compiler_params=_params("parallel"),
        name="outproj",
    )(oa, ob, oct, od, x, lp['w_out'], lp['ln1_g'], lp['ln1_b'])


def _ffn_kernel(nt, x_ref, xp_ref, xn_ref, wup_ref, cw_ref, cb_ref, wd_ref, g_ref, b_ref, o_ref,
                xe_ref, hg_ref, hv_ref, act_ref):
    i = pl.program_id(0)
    tm = x_ref.shape[0]
    cf = hg_ref.shape[2]
    first = (i % nt) == 0
    last = (i % nt) == nt - 1
    xe_ref[0:HALO] = jnp.where(first, 0.0, xp_ref[...]).astype(BF16)
    xe_ref[HALO:HALO + tm] = x_ref[...].astype(BF16)
    xe_ref[HALO + tm:] = jnp.where(last, 0.0, xn_ref[...]).astype(BF16)
    xe = xe_ref[...]

    def conv(h_ref, cols):
        return (h_ref[pl.ds(HALO - 1, tm), :] * cw_ref[0:1, cols] + h_ref[pl.ds(HALO, tm), :] * cw_ref[1:2, cols]
                + h_ref[pl.ds(HALO + 1, tm), :] * cw_ref[2:3, cols] + cb_ref[:, cols])

    for c in range(D_FF // cf):
        slot = c % 2
        gate = slice(c * cf, (c + 1) * cf)
        val = slice(D_FF + c * cf, D_FF + (c + 1) * cf)
        hg_ref[slot] = _dot(xe, wup_ref[:, gate])
        hv_ref[slot] = _dot(xe, wup_ref[:, val])
        act = _gelu_tanh_x2(conv(hg_ref.at[slot], gate)) * conv(hv_ref.at[slot], val)
        act_ref[:, gate] = act.astype(BF16)

    y = _dot(act_ref[...], wd_ref[...])
    o_ref[...] = _layer_norm_rows(ALPHA * x_ref[...] + y, g_ref[...], b_ref[...])


def _resident(shape):
    nd = len(shape)
    return pl.BlockSpec(shape, lambda *_: (0,) * nd, pipeline_mode=pl.Buffered(1))


def _ffn(x, lp, T):
    n = x.shape[0]
    tm = TM_FFN
    nt = T // tm
    cf = FF_CHUNK
    hb = tm // HALO
    nhb = n // HALO
    row = lambda i: (i, 0)
    return pl.pallas_call(
        functools.partial(_ffn_kernel, nt),
        grid=(n // tm,),
        in_specs=[pl.BlockSpec((tm, D_MODEL), row),
                  pl.BlockSpec((HALO, D_MODEL), lambda i: (jnp.maximum(i * hb - 1, 0), 0)),
                  pl.BlockSpec((HALO, D_MODEL), lambda i: (jnp.minimum((i + 1) * hb, nhb - 1), 0)),
                  _resident(lp['w_up'].shape), _resident(lp['conv_w'].shape), _resident(lp['conv_b'].shape),
                  _resident(lp['w_down'].shape), _resident(lp['ln2_g'].shape), _resident(lp['ln2_b'].shape)],
        out_specs=pl.BlockSpec((tm, D_MODEL), row),
        out_shape=jax.ShapeDtypeStruct((n, D_MODEL), F32),
        scratch_shapes=[pltpu.VMEM((tm + 2 * HALO, D_MODEL), BF16),
                        pltpu.VMEM((2, tm + 2 * HALO, cf), F32),
                        pltpu.VMEM((2, tm + 2 * HALO, cf), F32),
                        pltpu.VMEM((tm, D_FF), BF16)],
        compiler_params=_params("parallel"),
        name="conv_ffn",
    )(x, x, x, lp['w_up'], lp['conv_w'], lp['conv_b'], lp['w_down'], lp['ln2_g'], lp['ln2_b'])


def _natten_bias(rpb):
    nr, nc = 2 * NA_ROWS - 1, 2 * NA_COLS - 1
    krows = NA_KBLKS * NA_QROWS
    col = np.arange(GRID_W)
    ci = np.clip(col[None, :] - col[:, None] + NA_COLS - 1, 0, nc - 1)
    onehot_c = (ci.reshape(-1)[None, :] == np.arange(nc)[:, None]).astype(np.float32)
    cs = np.clip(col - NA_COLS // 2, 0, GRID_W - NA_COLS)
    dc = col[None, :] - cs[:, None]
    valid_c = (dc >= 0) & (dc < NA_COLS)
    qa = np.arange(NA_QROWS)
    kr = np.arange(krows)
    onehot_r, valid_r = [], []
    for kind in range(3):
        q_row = (0, NA_QROWS, 2 * NA_QROWS)[kind] + qa
        r_start = (np.zeros_like(qa), qa, np.full_like(qa, NA_QROWS))[kind]
        dr = kr[None, :] - r_start[:, None]
        valid_r.append((dr >= 0) & (dr < NA_ROWS))
        ri = np.clip(kr[None, :] - q_row[:, None] + NA_ROWS - 1, 0, nr - 1)
        onehot_r.append((ri.reshape(-1)[:, None] == np.arange(nr)[None, :]).astype(np.float32))
    onehot_r = np.stack(onehot_r)
    valid = np.stack(valid_r)[:, :, None, :, None] & valid_c[None, None, :, None, :]
    by_col = jnp.einsum('hrc,cx->hrx', rpb, onehot_c, precision=HI)
    full = jnp.einsum('kpr,hrx->khpx', onehot_r, by_col, precision=HI)
    full = full.reshape(3, NA_HEADS, NA_QROWS, krows, GRID_W, GRID_W).transpose(0, 1, 2, 4, 3, 5)
    full = jnp.where(valid[:, None], full, -1e30)
    return full.reshape(3, NA_HEADS, NA_QBLK, krows * GRID_W).astype(F32)


def _prep_layer(l, emb_ln_g, emb_ln_b, w_in, na_rpb, mla_q_g, w_uq, mla_kv_g, w_ukv, sgu_ln_g, sgu_ln_b,
                sgu_w, sgu_b, w_out, ln1_g, ln1_b, w_up, conv_w, conv_b, w_down, ln2_g, ln2_b):
    wi = w_in[l]
    o_fb = 3 * NA_WIDTH
    o_cq = o_fb + FN_WIDTH
    o_ckv = o_cq + MLA_Q_LORA
    o_kr = o_ckv + MLA_KV_LORA
    o_sg = o_kr + MLA_ROPE
    half = MLA_ROPE // 2
    swap = np.concatenate([np.arange(half, MLA_ROPE), np.arange(half)])
    w_kr = wi[:, o_kr:o_sg]
    w_kr2 = jnp.concatenate([w_kr, w_kr[:, swap], jnp.zeros((D_MODEL, LANES - 2 * MLA_ROPE), F32)], axis=1)

    uq = w_uq[l]
    pad = jnp.zeros((MLA_Q_LORA, MLA_HEADS, MLA_PAD - MLA_NOPE - MLA_ROPE), F32)
    w_a = jnp.concatenate([uq, pad], axis=2).reshape(MLA_Q_LORA, MLA_HEADS * MLA_PAD)
    w_b = jnp.concatenate([jnp.zeros((MLA_Q_LORA, MLA_HEADS, MLA_NOPE), F32),
                           uq[:, :, MLA_NOPE:][:, :, swap], pad], axis=2).reshape(MLA_Q_LORA, MLA_HEADS * MLA_PAD)
    ukv = w_ukv[l]
    w_kn = jnp.concatenate([ukv[:, :, :MLA_NOPE],
                            jnp.zeros((MLA_KV_LORA, MLA_HEADS, MLA_PAD - MLA_NOPE), F32)],
                           axis=2).reshape(MLA_KV_LORA, MLA_HEADS * MLA_PAD)
    w_vt = ukv[:, :, MLA_NOPE:].reshape(MLA_KV_LORA, MLA_WIDTH).T
    e_place = np.zeros((LANES, MLA_HEADS * MLA_PAD), np.float32)
    for h in range(MLA_HEADS):
        e_place[np.arange(MLA_ROPE), h * MLA_PAD + MLA_NOPE + np.arange(MLA_ROPE)] = 1.0
    gmat = np.kron(np.eye(SGU_HEADS), np.full((SGU_HEAD_DIM, SGU_HEAD_DIM), 1.0 / SGU_HEAD_DIM)).astype(np.float32)
    s_b = jnp.repeat(sgu_b[l].T, SGU_HEAD_DIM, axis=1)
    r1 = lambda a: a.reshape(1, -1).astype(F32)
    half_val = jnp.concatenate([jnp.ones((1, D_FF), F32), jnp.full((1, D_FF), 0.5, F32)], axis=1)
    return dict(
        eg=r1(emb_ln_g), eb=r1(emb_ln_b),
        w_nat=wi[:, :o_fb].astype(BF16), w_fb=wi[:, o_fb:o_cq].astype(BF16),
        w_cq=wi[:, o_cq:o_ckv].astype(BF16), w_ckv=wi[:, o_ckv:o_kr].astype(BF16),
        w_kr=w_kr2.astype(BF16), w_sg=wi[:, o_sg:].astype(BF16),
        q_g=r1(mla_q_g[l]), w_a=w_a.T.astype(BF16), w_b=w_b.T.astype(BF16),
        kv_g=r1(mla_kv_g[l]), w_kn=w_kn.astype(BF16), w_vt=w_vt.astype(BF16),
        e_place=jnp.asarray(e_place, BF16), gmat=jnp.asarray(gmat, BF16),
        sl_g=r1(sgu_ln_g[l]), sl_b=r1(sgu_ln_b[l]), w_s=sgu_w[l].astype(BF16), s_b=s_b.astype(F32),
        na_bias=_natten_bias(na_rpb[l]),
        w_out=w_out[l].astype(BF16), ln1_g=r1(ln1_g[l]), ln1_b=r1(ln1_b[l]),
        w_up=w_up[l].astype(BF16), conv_w=conv_w[l].astype(F32) * half_val, conv_b=r1(conv_b[l]) * half_val,
        w_down=w_down[l].astype(BF16), ln2_g=r1(ln2_g[l]), ln2_b=r1(ln2_b[l]),
    )


def _rope_tables(T):
    inv_freq = ROPE_THETA ** (-jnp.arange(0, MLA_ROPE, 2, dtype=F32) / MLA_ROPE)
    ang = jnp.arange(T, dtype=F32)[:, None] * inv_freq[None, :]
    cos, sin = jnp.cos(ang), jnp.sin(ang)
    cos2 = jnp.concatenate([cos, cos], axis=1)
    sin2 = jnp.concatenate([-sin, sin], axis=1)
    scale = (MLA_NOPE + MLA_ROPE) ** -0.5 * math.log2(math.e)
    zq = jnp.zeros((T, MLA_PAD - MLA_NOPE - MLA_ROPE), F32)
    ca = jnp.concatenate([jnp.full((T, MLA_NOPE), scale, F32), scale * cos2, zq], axis=1)
    cb = jnp.concatenate([jnp.zeros((T, MLA_NOPE), F32), scale * sin2, zq], axis=1)
    tk = jnp.concatenate([cos2, sin2, jnp.zeros((T, LANES - 2 * MLA_ROPE), F32)], axis=1)
    return dict(ca=ca.T, cb=cb.T, tk=tk)


def _dft_tables(T):
    n1 = 1 << (int(math.log2(T)) // 2)
    n2 = T // n1
    two_pi = 2.0 * math.pi

    def cs(num, den):
        ang = (num % den).astype(F32) * (two_pi / den)
        return jnp.cos(ang), jnp.sin(ang)

    t2 = jnp.arange(n2, dtype=jnp.int32)[:, None, None]
    k1 = jnp.arange(n1, dtype=jnp.int32)[None, :, None]
    t1 = jnp.arange(n1, dtype=jnp.int32)[None, None, :]
    c, s = cs(k1 * (t1 * n2 + t2), T)
    def split(x, axis):
        hi = x.astype(BF16)
        lo = (x - hi.astype(F32)).astype(BF16)
        return jnp.concatenate([hi, lo], axis=axis) if axis is not None else jnp.stack([hi, lo])

    m1 = split(jnp.concatenate([c, -s], axis=1), 1)
    k2 = jnp.arange(n2, dtype=jnp.int32)
    c2, s2 = cs(k2[:, None] * k2[None, :], n2)
    cs2 = split(jnp.concatenate([c2, s2], axis=0), 0)
    cg = jnp.arange(FN_GROUP_DIM, dtype=jnp.int32)
    cc, sc = cs(cg[:, None] * cg[None, :], FN_GROUP_DIM)
    norm = 1.0 / math.sqrt(T * FN_GROUP_DIM)
    eye = jnp.eye(FN_GROUPS, dtype=F32)
    w = split(jnp.concatenate([jnp.kron(eye, cc), jnp.kron(eye, sc)], axis=0) * norm, None)
    return dict(n1=n1, n2=n2, m1=m1, cs2=cs2, w=w)


def _trunk(x, layers):
    B, T, _ = x.shape
    tabs = _rope_tables(T)
    ft = _dft_tables(T)
    h = x.reshape(B * T, D_MODEL)
    for l, lp in enumerate(layers):
        outs = _inproj(h, lp, tabs, l == 0, T)
        if l == 0:
            h = outs[0]
            outs = outs[1:]
        nat, fb, q, k, vt, od = outs
        oa = _natten(nat, lp['na_bias'], B, T)
        ob = _fourier(fb, ft, B, T)
        oct = _mla(q, k, vt, B, T)
        h1 = _outproj(oa, ob, oct, od, h, lp)
        h = _ffn(h1, lp, T)
    return h.reshape(B, T, D_MODEL)


def kernel(x_prompt, x_sample, emb_ln_g, emb_ln_b, w_in, na_rpb, mla_q_g, w_uq, mla_kv_g, w_ukv, sgu_ln_g,
           sgu_ln_b, sgu_w, sgu_b, w_out, ln1_g, ln1_b, w_up, conv_w, conv_b, w_down, ln2_g, ln2_b):
    layers = [_prep_layer(l, emb_ln_g, emb_ln_b, w_in, na_rpb, mla_q_g, w_uq, mla_kv_g, w_ukv, sgu_ln_g,
                          sgu_ln_b, sgu_w, sgu_b, w_out, ln1_g, ln1_b, w_up, conv_w, conv_b, w_down,
                          ln2_g, ln2_b) for l in range(DEPTH)]
    return (_trunk(x_prompt, layers), _trunk(x_sample, layers))
```

```python
import functools
import math

import numpy as np
import jax
import jax.numpy as jnp
from jax import lax
from jax.experimental import pallas as pl
from jax.experimental.pallas import tpu as pltpu

F32 = jnp.float32
BF16 = jnp.bfloat16

D_MODEL = 1024
DEPTH = 2
GRID_W = 64
HEAD_DIM = 64
NA_HEADS = 4
NA_ROWS = 8
NA_COLS = 16
NA_WIDTH = NA_HEADS * HEAD_DIM
FN_GROUPS = 4
FN_GROUP_DIM = 64
FN_WIDTH = FN_GROUPS * FN_GROUP_DIM
MLA_HEADS = 4
MLA_Q_LORA = 256
MLA_KV_LORA = 128
MLA_NOPE = 64
MLA_ROPE = 32
MLA_V = 64
MLA_WIDTH = MLA_HEADS * MLA_V
ROPE_THETA = 10000.0
SGU_HEADS = 4
SGU_HEAD_DIM = 64
SGU_WIDTH = SGU_HEADS * SGU_HEAD_DIM
SGU_CHUNK = 128
D_FF = 2816
ALPHA = (2 * DEPTH) ** 0.25
LN_EPS = 1e-5
RMS_EPS = 1e-6

LANES = 128
SUBLANES = 8
VMEM_LIMIT = 56 * 1024 * 1024

TM_IN = 512
TM_OUT = 512
TM_FFN = 512
FF_CHUNK = 256
NA_QROWS = 4
NA_QBLK = NA_QROWS * GRID_W
NA_KBLKS = 3
MLA_PAD = 128
MLA_TQ = 512
MLA_TK = 512
MLA_ONES = 16
MLA_UNROLL = 8
HALO = SUBLANES

HI = lax.Precision.HIGHEST
NT_DIMS = (((1,), (1,)), ((), ()))


def _dot(a, b):
    return jnp.dot(a, b, preferred_element_type=F32)


def _dot_nt(a, b):
    return lax.dot_general(a, b, NT_DIMS, preferred_element_type=F32)


def _dot_hi(a, b):
    return jnp.dot(a, b, precision=HI, preferred_element_type=F32)


def _layer_norm_rows(x, g, b):
    mu = jnp.mean(x, axis=-1, keepdims=True)
    xc = x - mu
    var = jnp.mean(xc * xc, axis=-1, keepdims=True)
    return xc * lax.rsqrt(var + LN_EPS) * g + b


GELU_C = 0.7978845608028654
GELU_K = 0.044715


def _gelu_tanh_x2(x):
    t = jnp.tanh(x * (GELU_C + (GELU_C * GELU_K) * (x * x)))
    return x + x * t


def _gelu_tanh(x):
    return 0.5 * _gelu_tanh_x2(x)


def _split_dot(v, m):
    hi = v.astype(BF16)
    lo = (v - hi.astype(F32)).astype(BF16)
    return _dot(hi, m) + _dot(lo, m)


def _params(*sem):
    return pltpu.CompilerParams(dimension_semantics=sem, vmem_limit_bytes=VMEM_LIMIT)


def _full(shape):
    nd = len(shape)
    return pl.BlockSpec(shape, lambda *_: (0,) * nd)


def _inproj_kernel(apply_ln, x_ref, eg_ref, eb_ref, wnat_ref, wfb_ref, wcq_ref, wckv_ref,
                   wsg_ref, qg_ref, wa_ref, kvg_ref, wkn_ref, wvt_ref, ca_ref, cb_ref,
                   tk_ref, gmat_ref, slg_ref, slb_ref, ws_ref, sb_ref, *out_refs):
    if apply_ln:
        xn_ref, nat_ref, fb_ref, q_ref, k_ref, vt_ref, od_ref = out_refs
    else:
        nat_ref, fb_ref, q_ref, k_ref, vt_ref, od_ref = out_refs
    x = x_ref[...]
    if apply_ln:
        x = _layer_norm_rows(x, eg_ref[...], eb_ref[...])
        xn_ref[...] = x
    xb = x.astype(BF16)
    tm = x.shape[0]

    cq = _dot(xb, wcq_ref[...])
    ckv_kr = _dot(xb, wckv_ref[...])
    sg = _dot(xb, wsg_ref[...])

    nat = _dot(xb, wnat_ref[...])
    nat_ref[:, :NA_WIDTH] = (nat[:, :NA_WIDTH] * (HEAD_DIM ** -0.5)).astype(BF16)
    nat_ref[:, NA_WIDTH:] = nat[:, NA_WIDTH:].astype(BF16)

    fb_ref[...] = _dot(xb, wfb_ref[...])

    cqn = (cq * lax.rsqrt(jnp.mean(cq * cq, axis=-1, keepdims=True) + RMS_EPS) * qg_ref[...]).astype(BF16)
    qa = _dot_nt(wa_ref[...], cqn)
    ca = ca_ref[...]
    cb = cb_ref[MLA_NOPE:MLA_NOPE + MLA_ROPE]
    half = MLA_ROPE // 2
    for h in range(MLA_HEADS):
        base = h * MLA_PAD
        qh = qa[base:base + MLA_PAD] * ca
        lo = qa[base + MLA_NOPE:base + MLA_NOPE + half]
        hi = qa[base + MLA_NOPE + half:base + MLA_NOPE + MLA_ROPE]
        rope = qh[MLA_NOPE:MLA_NOPE + MLA_ROPE] + jnp.concatenate([hi, lo], axis=0) * cb
        qh = jnp.concatenate([qh[:MLA_NOPE], rope, qh[MLA_NOPE + MLA_ROPE:]], axis=0).astype(BF16)
        for c in range(tm // MLA_TQ):
            q_ref[c, base:base + MLA_PAD, :] = qh[:, c * MLA_TQ:(c + 1) * MLA_TQ]

    ckv = ckv_kr[:, :MLA_KV_LORA]
    ckvn = (ckv * lax.rsqrt(jnp.mean(ckv * ckv, axis=-1, keepdims=True) + RMS_EPS) * kvg_ref[...]).astype(BF16)
    kr = ckv_kr[:, MLA_KV_LORA:] * tk_ref[...]
    kr = pltpu.roll(kr, MLA_NOPE, axis=1) + pltpu.roll(kr, MLA_NOPE - MLA_ROPE, axis=1)
    lane = lax.broadcasted_iota(jnp.int32, (1, MLA_PAD), 1)
    kr = jnp.where((lane >= MLA_NOPE) & (lane < MLA_NOPE + MLA_ROPE), kr, 0.0)
    kn = _dot(ckvn, wkn_ref[...])
    for h in range(MLA_HEADS):
        sl = slice(h * MLA_PAD, (h + 1) * MLA_PAD)
        k_ref[:, sl] = (kn[:, sl] + kr).astype(BF16)
    vt = _dot_nt(wvt_ref[...], ckvn)
    tkc = vt_ref.shape[2]
    for c in range(tm // tkc):
        vt_ref[c] = vt[:, c * tkc:(c + 1) * tkc].astype(BF16)

    sg = _gelu_tanh(sg)
    u = sg[:, :SGU_WIDTH]
    v = sg[:, SGU_WIDTH:]
    gmat = gmat_ref[...]
    mu = _split_dot(v, gmat)
    vc = v - mu
    var = _split_dot(vc * vc, gmat)
    vn = (vc * lax.rsqrt(var + LN_EPS) * slg_ref[...] + slb_ref[...]).astype(BF16)
    head = lax.broadcasted_iota(jnp.int32, (1, SGU_WIDTH), 1) // SGU_HEAD_DIM
    zero = jnp.zeros((), BF16)
    for c in range(tm // SGU_CHUNK):
        rows = slice(c * SGU_CHUNK, (c + 1) * SGU_CHUNK)
        vch = vn[rows]
        stacked = jnp.concatenate([jnp.where(head == g, vch, zero) for g in range(SGU_HEADS)], axis=0)
        mixed = sb_ref[...] + _dot(ws_ref[...], stacked)
        od_ref[rows, :] = (u[rows] * mixed).astype(BF16)


def _inproj(x, lp, tabs, apply_ln, T):
    n = x.shape[0]
    tm = TM_IN
    nt = T // tm
    row = lambda i: (i, 0)
    tab = lambda i: (i % nt, 0)
    weights = [lp['eg'], lp['eb'], lp['w_nat'], lp['w_fb'], lp['w_cq'], lp['w_ckv'], lp['w_sg'],
               lp['q_g'], lp['w_a'], lp['kv_g'], lp['w_kn'], lp['w_vt']]
    tables = [tabs['ca'], tabs['cb'], tabs['tk']]
    tail = [lp['gmat'], lp['sl_g'], lp['sl_b'], lp['w_s'], lp['s_b']]
    tab_t = lambda i: (0, i % nt)
    in_specs = ([pl.BlockSpec((tm, D_MODEL), row)] + [_full(w.shape) for w in weights]
                + [pl.BlockSpec((MLA_PAD, tm), tab_t), pl.BlockSpec((MLA_PAD, tm), tab_t),
                   pl.BlockSpec((tm, LANES), tab)] + [_full(w.shape) for w in tail])
    out_shape = [jax.ShapeDtypeStruct((n, 3 * NA_WIDTH), BF16),
                 jax.ShapeDtypeStruct((n, FN_WIDTH), F32),
                 jax.ShapeDtypeStruct((n // MLA_TQ, MLA_HEADS * MLA_PAD, MLA_TQ), BF16),
                 jax.ShapeDtypeStruct((n, MLA_HEADS * MLA_PAD), BF16),
                 jax.ShapeDtypeStruct((n // MLA_TK, MLA_WIDTH, MLA_TK), BF16),
                 jax.ShapeDtypeStruct((n, SGU_WIDTH), BF16)]
    out_specs = [pl.BlockSpec((tm, 3 * NA_WIDTH), row),
                 pl.BlockSpec((tm, FN_WIDTH), row),
                 pl.BlockSpec((tm // MLA_TQ, MLA_HEADS * MLA_PAD, MLA_TQ), lambda i: (i, 0, 0)),
                 pl.BlockSpec((tm, MLA_HEADS * MLA_PAD), row),
                 pl.BlockSpec((tm // MLA_TK, MLA_WIDTH, MLA_TK), lambda i: (i, 0, 0)),
                 pl.BlockSpec((tm, SGU_WIDTH), row)]
    if apply_ln:
        out_shape = [jax.ShapeDtypeStruct((n, D_MODEL), F32)] + out_shape
        out_specs = [pl.BlockSpec((tm, D_MODEL), row)] + out_specs
    return pl.pallas_call(
        functools.partial(_inproj_kernel, apply_ln),
        grid=(n // tm,),
        in_specs=in_specs,
        out_specs=out_specs,
        out_shape=out_shape,
        compiler_params=_params("parallel"),
        name="inproj_ln" if apply_ln else "inproj",
    )(x, *weights, *tables, *tail)


def _natten_kernel(q_ref, k0_ref, k1_ref, k2_ref, v0_ref, v1_ref, v2_ref, bias_ref, o_ref, s_ref):
    q = q_ref[...]
    k = jnp.concatenate([k0_ref[...], k1_ref[...], k2_ref[...]], axis=0)
    v = jnp.concatenate([v0_ref[...], v1_ref[...], v2_ref[...]], axis=0)
    head = lax.broadcasted_iota(jnp.int32, (1, NA_WIDTH), 1) // HEAD_DIM
    zero = jnp.zeros((), BF16)

    def scores(h):
        return _dot_nt(jnp.where(head == h, q, zero), k) + bias_ref[h]

    s_ref[0] = scores(0)
    acc = jnp.zeros((NA_QBLK, NA_WIDTH), F32)
    for h in range(NA_HEADS):
        if h + 1 < NA_HEADS:
            s_ref[(h + 1) % 2] = scores(h + 1)
        s = s_ref[h % 2]
        m = jnp.max(s, axis=-1, keepdims=True)
        p = jnp.exp(s - m)
        l = jnp.sum(p, axis=-1, keepdims=True)
        acc = acc + _dot(p.astype(BF16), jnp.where(head == h, v, zero)) * (1.0 / l)
    o_ref[...] = acc.astype(BF16)


def _natten(nat, bias, B, T):
    n = nat.shape[0]
    nblk = T // NA_QBLK

    def kv_map(d, col):
        def f(b, j):
            base = jnp.clip(j - 1, 0, nblk - NA_KBLKS)
            return (b * nblk + base + d, col)
        return f

    def bias_map(b, j):
        typ = jnp.where(j == 0, 0, jnp.where(j == nblk - 1, 2, 1))
        return (typ, 0, 0, 0)

    blk = (NA_QBLK, NA_WIDTH)
    in_specs = ([pl.BlockSpec(blk, lambda b, j: (b * nblk + j, 0))]
                + [pl.BlockSpec(blk, kv_map(d, 1)) for d in range(NA_KBLKS)]
                + [pl.BlockSpec(blk, kv_map(d, 2)) for d in range(NA_KBLKS)]
                + [pl.BlockSpec((None, NA_HEADS, NA_QBLK, NA_KBLKS * NA_QBLK), bias_map)])
    return pl.pallas_call(
        _natten_kernel,
        grid=(B, nblk),
        in_specs=in_specs,
        out_specs=pl.BlockSpec(blk, lambda b, j: (b * nblk + j, 0)),
        out_shape=jax.ShapeDtypeStruct((n, NA_WIDTH), BF16),
        scratch_shapes=[pltpu.VMEM((2, NA_QBLK, NA_KBLKS * NA_QBLK), F32)],
        compiler_params=_params("parallel", "arbitrary"),
        name="natten",
    )(nat, nat, nat, nat, nat, nat, nat, bias)


def _split_bf16(x):
    hi = x.astype(BF16)
    return hi, (x - hi.astype(F32)).astype(BF16)


def _fft1_kernel(x_ref, m_ref, a_ref):
    kk = m_ref.shape[0]
    n1 = x_ref.shape[0]
    xt = pltpu.einshape("abc->bac", x_ref[...])
    for i in range(kk):
        x_hi, x_lo = _split_bf16(xt[i])
        m = m_ref[i]
        r = _dot(m, x_hi)
        res = r[:2 * n1] + r[2 * n1:] + _dot(m[:2 * n1], x_lo)
        a_ref[i, :, :FN_WIDTH] = res[:n1]
        a_ref[i, :, FN_WIDTH:] = res[n1:]


def _fft2_kernel(a_ref, cs_ref, w_ref, y_ref):
    n2, kk, _ = a_ref.shape
    cs = cs_ref[...]
    at = pltpu.einshape("abc->bac", a_ref[...])
    g = []
    for i in range(kk):
        a_hi, a_lo = _split_bf16(at[i])
        r = _dot(cs, a_hi)
        r = r[:2 * n2] + r[2 * n2:] + _dot(cs[:2 * n2], a_lo)
        g.append(jnp.concatenate([r[:n2, :FN_WIDTH] + r[n2:, FN_WIDTH:],
                                  r[:n2, FN_WIDTH:] - r[n2:, :FN_WIDTH]], axis=1))
    g_hi, g_lo = _split_bf16(jnp.concatenate(g, axis=0))
    y = _dot(g_hi, w_ref[0]) + _dot(g_lo, w_ref[0]) + _dot(g_hi, w_ref[1])
    y_ref[...] = pltpu.einshape("abc->bac", y.reshape(kk, n2, FN_WIDTH))


def _fourier(fb, ft, B, T):
    n1, n2 = ft['n1'], ft['n2']
    kk = SUBLANES
    a = pl.pallas_call(
        _fft1_kernel,
        grid=(B, n2 // kk),
        in_specs=[pl.BlockSpec((None, n1, kk, FN_WIDTH), lambda b, j: (b, 0, j, 0)),
                  pl.BlockSpec((kk, 4 * n1, n1), lambda b, j: (j, 0, 0))],
        out_specs=pl.BlockSpec((None, kk, n1, 2 * FN_WIDTH), lambda b, j: (b, j, 0, 0)),
        out_shape=jax.ShapeDtypeStruct((B, n2, n1, 2 * FN_WIDTH), F32),
        compiler_params=_params("parallel", "arbitrary"),
        name="fft_stage1",
    )(fb.reshape(B, n1, n2, FN_WIDTH), ft['m1'])
    y = pl.pallas_call(
        _fft2_kernel,
        grid=(B, n1 // kk),
        in_specs=[pl.BlockSpec((None, n2, kk, 2 * FN_WIDTH), lambda b, j: (b, 0, j, 0)),
                  _full(ft['cs2'].shape), _full(ft['w'].shape)],
        out_specs=pl.BlockSpec((None, n2, kk, FN_WIDTH), lambda b, j: (b, 0, j, 0)),
        out_shape=jax.ShapeDtypeStruct((B, n2, n1, FN_WIDTH), F32),
        compiler_params=_params("parallel", "arbitrary"),
        name="fft_stage2",
    )(a, ft['cs2'], ft['w'])
    return y.reshape(B * T, FN_WIDTH)


def _mla_kernel(q_ref, k_ref, vt_ref, o_ref, s_ref):
    nk, _, tk = vt_ref.shape
    nq, _, tq = q_ref.shape
    unroll = min(MLA_UNROLL, nk)
    ones = jnp.ones((MLA_ONES, tk), BF16)

    def scores(i, j):
        return _dot(k_ref[pl.ds(pl.multiple_of(j * tk, tk), tk), :], q_ref[i])

    def update(carry, j, slot):
        m, acc = carry
        st = s_ref[slot]
        m_new = jnp.maximum(m, jnp.max(st, axis=0, keepdims=True))
        alpha = jnp.exp2(m - m_new)
        p = jnp.exp2(st - m_new).astype(BF16)
        vte = jnp.concatenate([vt_ref[j], ones], axis=0)
        return m_new, alpha * acc + _dot(vte, p)

    s_ref[0] = scores(0, 0)
    m0 = jnp.full((1, tq), -1e30, F32)
    acc0 = jnp.zeros((MLA_V + MLA_ONES, tq), F32)

    def body(t, carry):
        i = (t * unroll) // nk
        j0 = (t * unroll) % nk
        fresh = j0 == 0
        carry = (jnp.where(fresh, m0, carry[0]), jnp.where(fresh, acc0, carry[1]))
        for c in range(unroll):
            if c + 1 < unroll:
                s_ref[(c + 1) % 2] = scores(i, j0 + c + 1)
            else:
                nxt = jnp.minimum(t * unroll + unroll, nq * nk - 1)
                s_ref[(c + 1) % 2] = scores(nxt // nk, nxt % nk)
            carry = update(carry, j0 + c, c % 2)

        @pl.when(j0 + unroll == nk)
        def _():
            acc = carry[1]
            o_ref[i] = acc[:MLA_V] * (1.0 / acc[MLA_V:MLA_V + 1])

        return carry

    lax.fori_loop(0, nq * nk // unroll, body, (m0, acc0))


def _mla(q, k, vt, B, T):
    nq = T // MLA_TQ
    nk = T // MLA_TK
    assert nk % min(MLA_UNROLL, nk) == 0 and min(MLA_UNROLL, nk) % 2 == 0
    return pl.pallas_call(
        _mla_kernel,
        grid=(B, MLA_HEADS),
        in_specs=[pl.BlockSpec((nq, MLA_PAD, MLA_TQ), lambda b, h: (b, h, 0)),
                  pl.BlockSpec((T, MLA_PAD), lambda b, h: (b, h)),
                  pl.BlockSpec((nk, MLA_V, MLA_TK), lambda b, h: (b, h, 0))],
        out_specs=pl.BlockSpec((nq, MLA_V, MLA_TQ), lambda b, h: (b, h, 0)),
        out_shape=jax.ShapeDtypeStruct((B * nq, MLA_WIDTH, MLA_TQ), F32),
        scratch_shapes=[pltpu.VMEM((2, MLA_TK, MLA_TQ), F32)],
        compiler_params=_params("parallel", "arbitrary"),
        name="mla",
    )(q, k, vt)


def _outproj_kernel(oa_ref, ob_ref, oct_ref, od_ref, x_ref, w_ref, g_ref, b_ref, o_ref):
    w = NA_WIDTH
    oc = oct_ref[...].T.astype(BF16)
    y = (_dot(oa_ref[...], w_ref[0:w]) + _dot(ob_ref[...].astype(BF16), w_ref[w:2 * w])
         + _dot(oc, w_ref[2 * w:3 * w]) + _dot(od_ref[...], w_ref[3 * w:4 * w]))
    o_ref[...] = _layer_norm_rows(ALPHA * x_ref[...] + y, g_ref[...], b_ref[...])


def _outproj(oa, ob, oct, od, x, lp):
    n = x.shape[0]
    tm = TM_OUT
    row = lambda i: (i, 0)
    return pl.pallas_call(
        _outproj_kernel,
        grid=(n // tm,),
        in_specs=[pl.BlockSpec((tm, NA_WIDTH), row), pl.BlockSpec((tm, FN_WIDTH), row),
                  pl.BlockSpec((None, MLA_WIDTH, tm), lambda i: (i, 0, 0)), pl.BlockSpec((tm, SGU_WIDTH), row),
                  pl.BlockSpec((tm, D_MODEL), row), _full(lp['w_out'].shape),
                  _full(lp['ln1_g'].shape), _full(lp['ln1_b'].shape)],
        out_specs=pl.BlockSpec((tm, D_MODEL), row),
        out_shape=jax.ShapeDtypeStruct((n, D_MODEL), F32),
        compiler_params=_params("parallel"),
        name="outproj",
    )(oa, ob, oct, od, x, lp['w_out'], lp['ln1_g'], lp['ln1_b'])


def _ffn_kernel(nt, x_ref, xp_ref, xn_ref, wup_ref, cw_ref, cb_ref, wd_ref, g_ref, b_ref, o_ref,
                xe_ref, hg_ref, hv_ref, act_ref):
    i = pl.program_id(0)
    tm = x_ref.shape[0]
    cf = hg_ref.shape[2]
    first = (i % nt) == 0
    last = (i % nt) == nt - 1
    xe_ref[0:HALO] = jnp.where(first, 0.0, xp_ref[...]).astype(BF16)
    xe_ref[HALO:HALO + tm] = x_ref[...].astype(BF16)
    xe_ref[HALO + tm:] = jnp.where(last, 0.0, xn_ref[...]).astype(BF16)
    xe = xe_ref[...]

    def conv(h_ref, cols):
        return (h_ref[pl.ds(HALO - 1, tm), :] * cw_ref[0:1, cols] + h_ref[pl.ds(HALO, tm), :] * cw_ref[1:2, cols]
                + h_ref[pl.ds(HALO + 1, tm), :] * cw_ref[2:3, cols] + cb_ref[:, cols])

    for c in range(D_FF // cf):
        slot = c % 2
        gate = slice(c * cf, (c + 1) * cf)
        val = slice(D_FF + c * cf, D_FF + (c + 1) * cf)
        hg_ref[slot] = _dot(xe, wup_ref[:, gate])
        hv_ref[slot] = _dot(xe, wup_ref[:, val])
        act = _gelu_tanh_x2(conv(hg_ref.at[slot], gate)) * conv(hv_ref.at[slot], val)
        act_ref[:, gate] = act.astype(BF16)

    y = _dot(act_ref[...], wd_ref[...])
    o_ref[...] = _layer_norm_rows(ALPHA * x_ref[...] + y, g_ref[...], b_ref[...])


def _resident(shape):
    nd = len(shape)
    return pl.BlockSpec(shape, lambda *_: (0,) * nd, pipeline_mode=pl.Buffered(1))


def _ffn(x, lp, T):
    n = x.shape[0]
    tm = TM_FFN
    nt = T // tm
    cf = FF_CHUNK
    hb = tm // HALO
    nhb = n // HALO
    row = lambda i: (i, 0)
    return pl.pallas_call(
        functools.partial(_ffn_kernel, nt),
        grid=(n // tm,),
        in_specs=[pl.BlockSpec((tm, D_MODEL), row),
                  pl.BlockSpec((HALO, D_MODEL), lambda i: (jnp.maximum(i * hb - 1, 0), 0)),
                  pl.BlockSpec((HALO, D_MODEL), lambda i: (jnp.minimum((i + 1) * hb, nhb - 1), 0)),
                  _resident(lp['w_up'].shape), _resident(lp['conv_w'].shape), _resident(lp['conv_b'].shape),
                  _resident(lp['w_down'].shape), _resident(lp['ln2_g'].shape), _resident(lp['ln2_b'].shape)],
        out_specs=pl.BlockSpec((tm, D_MODEL), row),
        out_shape=jax.ShapeDtypeStruct((n, D_MODEL), F32),
        scratch_shapes=[pltpu.VMEM((tm + 2 * HALO, D_MODEL), BF16),
                        pltpu.VMEM((2, tm + 2 * HALO, cf), F32),
                        pltpu.VMEM((2, tm + 2 * HALO, cf), F32),
                        pltpu.VMEM((tm, D_FF), BF16)],
        compiler_params=_params("parallel"),
        name="conv_ffn",
    )(x, x, x, lp['w_up'], lp['conv_w'], lp['conv_b'], lp['w_down'], lp['ln2_g'], lp['ln2_b'])


def _natten_bias(rpb):
    nr, nc = 2 * NA_ROWS - 1, 2 * NA_COLS - 1
    krows = NA_KBLKS * NA_QROWS
    col = np.arange(GRID_W)
    ci = np.clip(col[None, :] - col[:, None] + NA_COLS - 1, 0, nc - 1)
    onehot_c = (ci.reshape(-1)[None, :] == np.arange(nc)[:, None]).astype(np.float32)
    cs = np.clip(col - NA_COLS // 2, 0, GRID_W - NA_COLS)
    dc = col[None, :] - cs[:, None]
    valid_c = (dc >= 0) & (dc < NA_COLS)
    qa = np.arange(NA_QROWS)
    kr = np.arange(krows)
    onehot_r, valid_r = [], []
    for kind in range(3):
        q_row = (0, NA_QROWS, 2 * NA_QROWS)[kind] + qa
        r_start = (np.zeros_like(qa), qa, np.full_like(qa, NA_QROWS))[kind]
        dr = kr[None, :] - r_start[:, None]
        valid_r.append((dr >= 0) & (dr < NA_ROWS))
        ri = np.clip(kr[None, :] - q_row[:, None] + NA_ROWS - 1, 0, nr - 1)
        onehot_r.append((ri.reshape(-1)[:, None] == np.arange(nr)[None, :]).astype(np.float32))
    onehot_r = np.stack(onehot_r)
    valid = np.stack(valid_r)[:, :, None, :, None] & valid_c[None, None, :, None, :]
    by_col = jnp.einsum('hrc,cx->hrx', rpb, onehot_c, precision=HI)
    full = jnp.einsum('kpr,hrx->khpx', onehot_r, by_col, precision=HI)
    full = full.reshape(3, NA_HEADS, NA_QROWS, krows, GRID_W, GRID_W).transpose(0, 1, 2, 4, 3, 5)
    full = jnp.where(valid[:, None], full, -1e30)
    return full.reshape(3, NA_HEADS, NA_QBLK, krows * GRID_W).astype(F32)


def _prep_layer(l, emb_ln_g, emb_ln_b, w_in, na_rpb, mla_q_g, w_uq, mla_kv_g, w_ukv, sgu_ln_g, sgu_ln_b,
                sgu_w, sgu_b, w_out, ln1_g, ln1_b, w_up, conv_w, conv_b, w_down, ln2_g, ln2_b):
    wi = w_in[l]
    o_fb = 3 * NA_WIDTH
    o_cq = o_fb + FN_WIDTH
    o_ckv = o_cq + MLA_Q_LORA
    o_kr = o_ckv + MLA_KV_LORA
    o_sg = o_kr + MLA_ROPE
    half = MLA_ROPE // 2
    swap = np.concatenate([np.arange(half, MLA_ROPE), np.arange(half)])
    w_kr = wi[:, o_kr:o_sg]
    w_ckv = jnp.concatenate([wi[:, o_ckv:o_kr], w_kr, w_kr[:, swap],
                             jnp.zeros((D_MODEL, LANES - 2 * MLA_ROPE), F32)], axis=1)

    uq = w_uq[l]
    pad = jnp.zeros((MLA_Q_LORA, MLA_HEADS, MLA_PAD - MLA_NOPE - MLA_ROPE), F32)
    w_a = jnp.concatenate([uq, pad], axis=2).reshape(MLA_Q_LORA, MLA_HEADS * MLA_PAD)
    ukv = w_ukv[l]
    w_kn = jnp.concatenate([ukv[:, :, :MLA_NOPE],
                            jnp.zeros((MLA_KV_LORA, MLA_HEADS, MLA_PAD - MLA_NOPE), F32)],
                           axis=2).reshape(MLA_KV_LORA, MLA_HEADS * MLA_PAD)
    w_vt = ukv[:, :, MLA_NOPE:].reshape(MLA_KV_LORA, MLA_WIDTH).T
    w_s = jnp.concatenate([sgu_w[l][g] for g in range(SGU_HEADS)], axis=1)
    gmat =np.kron(np.eye(SGU_HEADS), np.full((SGU_HEAD_DIM, SGU_HEAD_DIM), 1.0 / SGU_HEAD_DIM)).astype(np.float32)
    s_b = jnp.repeat(sgu_b[l].T, SGU_HEAD_DIM, axis=1)
    r1 = lambda a: a.reshape(1, -1).astype(F32)
    half_val = jnp.concatenate([jnp.ones((1, D_FF), F32), jnp.full((1, D_FF), 0.5, F32)], axis=1)
    return dict(
        eg=r1(emb_ln_g), eb=r1(emb_ln_b),
        w_nat=wi[:, :o_fb].astype(BF16), w_fb=wi[:, o_fb:o_cq].astype(BF16),
        w_cq=wi[:, o_cq:o_ckv].astype(BF16), w_ckv=w_ckv.astype(BF16), w_sg=wi[:, o_sg:].astype(BF16),
        q_g=r1(mla_q_g[l]), w_a=w_a.T.astype(BF16),
        kv_g=r1(mla_kv_g[l]), w_kn=w_kn.astype(BF16), w_vt=w_vt.astype(BF16),
        gmat=jnp.asarray(gmat, BF16),
        sl_g=r1(sgu_ln_g[l]), sl_b=r1(sgu_ln_b[l]), w_s=w_s.astype(BF16), s_b=s_b.astype(F32),
        na_bias=_natten_bias(na_rpb[l]),
        w_out=w_out[l].astype(BF16), ln1_g=r1(ln1_g[l]), ln1_b=r1(ln1_b[l]),
        w_up=w_up[l].astype(BF16), conv_w=conv_w[l].astype(F32) * half_val, conv_b=r1(conv_b[l]) * half_val,
        w_down=w_down[l].astype(BF16), ln2_g=r1(ln2_g[l]), ln2_b=r1(ln2_b[l]),
    )


def _rope_tables(T):
    inv_freq = ROPE_THETA ** (-np.arange(0, MLA_ROPE, 2, dtype=np.float64) / MLA_ROPE)
    ang = np.arange(T, dtype=np.float64)[:, None] * inv_freq[None, :]
    cos, sin = np.cos(ang), np.sin(ang)
    cos2 = np.concatenate([cos, cos], axis=1)
    sin2 = np.concatenate([-sin, sin], axis=1)
    scale = (MLA_NOPE + MLA_ROPE) ** -0.5 * math.log2(math.e)
    zq = np.zeros((T, MLA_PAD - MLA_NOPE - MLA_ROPE))
    ca = np.concatenate([np.full((T, MLA_NOPE), scale), scale * cos2, zq], axis=1)
    cb = np.concatenate([np.zeros((T, MLA_NOPE)), scale * sin2, zq], axis=1)
    tk = np.concatenate([cos2, sin2, np.zeros((T, LANES - 2 * MLA_ROPE))], axis=1)
    f32 = lambda a: jnp.asarray(np.ascontiguousarray(a), F32)
    return dict(ca=f32(ca.T), cb=f32(cb.T), tk=f32(tk))


def _dft_tables(T):
    n1 = 1 << (int(math.log2(T)) // 2)
    n2 = T // n1

    def cs(num, den):
        ang = (num % den).astype(np.float64) * (2.0 * np.pi / den)
        return np.cos(ang), np.sin(ang)

    def split(x, axis):
        x = x.astype(np.float32)
        hi = x.astype(BF16)
        lo = (x - hi.astype(np.float32)).astype(BF16)
        return np.concatenate([hi, lo], axis=axis) if axis is not None else np.stack([hi, lo])

    t2 = np.arange(n2, dtype=np.int64)[:, None, None]
    k1 = np.arange(n1, dtype=np.int64)[None, :, None]
    t1 = np.arange(n1, dtype=np.int64)[None, None, :]
    c, s = cs(k1 * (t1 * n2 + t2), T)
    m1 = split(np.concatenate([c, -s], axis=1), 1)
    k2 = np.arange(n2, dtype=np.int64)
    c2, s2 = cs(k2[:, None] * k2[None, :], n2)
    cs2 = split(np.concatenate([c2, s2], axis=0), 0)
    cg = np.arange(FN_GROUP_DIM, dtype=np.int64)
    cc, sc = cs(cg[:, None] * cg[None, :], FN_GROUP_DIM)
    norm = 1.0 / math.sqrt(T * FN_GROUP_DIM)
    eye = np.eye(FN_GROUPS)
    w = split(np.concatenate([np.kron(eye, cc), np.kron(eye, sc)], axis=0) * norm, None)
    return dict(n1=n1, n2=n2, m1=jnp.asarray(m1), cs2=jnp.asarray(cs2), w=jnp.asarray(w))


def _trunk(x, layers):
    B, T, _ = x.shape
    tabs = _rope_tables(T)
    ft = _dft_tables(T)
    h = x.reshape(B * T, D_MODEL)
    for l, lp in enumerate(layers):
        outs = _inproj(h, lp, tabs, l == 0, T)
        if l == 0:
            h = outs[0]
            outs = outs[1:]
        nat, fb, q, k, vt, od = outs
        oa = _natten(nat, lp['na_bias'], B, T)
        ob = _fourier(fb, ft, B, T)
        oct = _mla(q, k, vt, B, T)
        h1 = _outproj(oa, ob, oct, od, h, lp)
        h = _ffn(h1, lp, T)
    return h.reshape(B, T, D_MODEL)


def kernel(x_prompt, x_sample, emb_ln_g, emb_ln_b, w_in, na_rpb, mla_q_g, w_uq, mla_kv_g, w_ukv, sgu_ln_g,
           sgu_ln_b, sgu_w, sgu_b, w_out, ln1_g, ln1_b, w_up, conv_w, conv_b, w_down, ln2_g, ln2_b):
    layers = [_prep_layer(l, emb_ln_g, emb_ln_b, w_in, na_rpb, mla_q_g, w_uq, mla_kv_g, w_ukv, sgu_ln_g,
                          sgu_ln_b, sgu_w, sgu_b, w_out, ln1_g, ln1_b, w_up, conv_w, conv_b, w_down,
                          ln2_g, ln2_b) for l in range(DEPTH)]
    return (_trunk(x_prompt, layers), _trunk(x_sample, layers))
```

```python
import functools
import math

import numpy as np
import jax
import jax.numpy as jnp
from jax import lax
from jax.experimental import pallas as pl
from jax.experimental.pallas import tpu as pltpu

F32 = jnp.float32
BF16 = jnp.bfloat16

D_MODEL = 1024
DEPTH = 2
GRID_W = 64
HEAD_DIM = 64
NA_HEADS = 4
NA_ROWS = 8
NA_COLS = 16
NA_WIDTH = NA_HEADS * HEAD_DIM
FN_GROUPS = 4
FN_GROUP_DIM = 64
FN_WIDTH = FN_GROUPS * FN_GROUP_DIM
MLA_HEADS = 4
MLA_Q_LORA = 256
MLA_KV_LORA = 128
MLA_NOPE = 64
MLA_ROPE = 32
MLA_V = 64
MLA_WIDTH = MLA_HEADS * MLA_V
ROPE_THETA = 10000.0
SGU_HEADS = 4
SGU_HEAD_DIM = 64
SGU_WIDTH = SGU_HEADS * SGU_HEAD_DIM
SGU_CHUNK = 128
D_FF = 2816
ALPHA = (2 * DEPTH) ** 0.25
LN_EPS = 1e-5
RMS_EPS = 1e-6

LANES = 128
SUBLANES = 8
VMEM_LIMIT = 56 * 1024 * 1024

TM_IN = 512
TM_OUT = 512
TM_FFN = 512
FF_CHUNK = 256
NA_QROWS = 4
NA_QBLK = NA_QROWS * GRID_W
NA_KBLKS = 3
MLA_PAD = 128
MLA_TQ = 512
MLA_TK = 512
MLA_ONES = 16
MLA_UNROLL = 8
HALO = SUBLANES

LOG2_E = math.log2(math.e)
HI = lax.Precision.HIGHEST
NT_DIMS = (((1,), (1,)), ((), ()))


def _dot(a, b):
    return jnp.dot(a, b, preferred_element_type=F32)


def _dot_nt(a, b):
    return lax.dot_general(a, b, NT_DIMS, preferred_element_type=F32)


def _dot_hi(a, b):
    return jnp.dot(a, b, precision=HI, preferred_element_type=F32)


def _layer_norm_rows(x, g, b):
    mu = jnp.mean(x, axis=-1, keepdims=True)
    xc = x - mu
    var = jnp.mean(xc * xc, axis=-1, keepdims=True)
    return xc * lax.rsqrt(var + LN_EPS) * g + b


GELU_C = 0.7978845608028654
GELU_K = 0.044715


def _gelu_tanh_x2(x):
    t = jnp.tanh(x * (GELU_C + (GELU_C * GELU_K) * (x * x)))
    return x + x * t


def _gelu_tanh(x):
    return 0.5 * _gelu_tanh_x2(x)


def _split_dot(v, m):
    hi = v.astype(BF16)
    lo = (v - hi.astype(F32)).astype(BF16)
    return _dot(hi, m) + _dot(lo, m)


def _params(*sem):
    return pltpu.CompilerParams(dimension_semantics=sem, vmem_limit_bytes=VMEM_LIMIT)


def _full(shape):
    nd = len(shape)
    return pl.BlockSpec(shape, lambda *_: (0,) * nd)


def _inproj_kernel(apply_ln, x_ref, eg_ref, eb_ref, wnat_ref, wfb_ref, wcq_ref, wckv_ref,
                   wsg_ref, qg_ref, wa_ref, kvg_ref, wkn_ref, wvt_ref, ca_ref, cb_ref,
                   tk_ref, gmat_ref, slg_ref, slb_ref, ws_ref, sb_ref, *out_refs):
    if apply_ln:
        xn_ref, nat_ref, fb_ref, q_ref, k_ref, vt_ref, od_ref = out_refs
    else:
        nat_ref, fb_ref, q_ref, k_ref, vt_ref, od_ref = out_refs
    x = x_ref[...]
    if apply_ln:
        x = _layer_norm_rows(x, eg_ref[...], eb_ref[...])
        xn_ref[...] = x
    xb = x.astype(BF16)
    tm = x.shape[0]

    cq = _dot(xb, wcq_ref[...])
    ckv_kr = _dot(xb, wckv_ref[...])
    sg = _dot(xb, wsg_ref[...])

    nat = _dot(xb, wnat_ref[...])
    nat_ref[:, :NA_WIDTH] = (nat[:, :NA_WIDTH] * (HEAD_DIM ** -0.5 * LOG2_E)).astype(BF16)
    nat_ref[:, NA_WIDTH:] = nat[:, NA_WIDTH:].astype(BF16)

    fb_ref[...] = _dot(xb, wfb_ref[...])

    cqn = (cq * lax.rsqrt(jnp.mean(cq * cq, axis=-1, keepdims=True) + RMS_EPS) * qg_ref[...]).astype(BF16)
    qa = _dot_nt(wa_ref[...], cqn)
    ca = ca_ref[...]
    cb = cb_ref[MLA_NOPE:MLA_NOPE + MLA_ROPE]
    half = MLA_ROPE // 2
    for h in range(MLA_HEADS):
        base = h * MLA_PAD
        qh = qa[base:base + MLA_PAD] * ca
        lo = qa[base + MLA_NOPE:base + MLA_NOPE + half]
        hi = qa[base + MLA_NOPE + half:base + MLA_NOPE + MLA_ROPE]
        rope = qh[MLA_NOPE:MLA_NOPE + MLA_ROPE] + jnp.concatenate([hi, lo], axis=0) * cb
        qh = jnp.concatenate([qh[:MLA_NOPE], rope, qh[MLA_NOPE + MLA_ROPE:]], axis=0).astype(BF16)
        for c in range(tm // MLA_TQ):
            q_ref[c, base:base + MLA_PAD, :] = qh[:, c * MLA_TQ:(c + 1) * MLA_TQ]

    ckv = ckv_kr[:, :MLA_KV_LORA]
    ckvn = (ckv * lax.rsqrt(jnp.mean(ckv * ckv, axis=-1, keepdims=True) + RMS_EPS) * kvg_ref[...]).astype(BF16)
    kr = ckv_kr[:, MLA_KV_LORA:] * tk_ref[...]
    kr = pltpu.roll(kr, MLA_NOPE, axis=1) + pltpu.roll(kr, MLA_NOPE - MLA_ROPE, axis=1)
    lane = lax.broadcasted_iota(jnp.int32, (1, MLA_PAD), 1)
    kr = jnp.where((lane >= MLA_NOPE) & (lane < MLA_NOPE + MLA_ROPE), kr, 0.0)
    kn = _dot(ckvn, wkn_ref[...])
    for h in range(MLA_HEADS):
        sl = slice(h * MLA_PAD, (h + 1) * MLA_PAD)
        k_ref[:, sl] = (kn[:, sl] + kr).astype(BF16)
    vt = _dot_nt(wvt_ref[...], ckvn)
    tkc = vt_ref.shape[2]
    for c in range(tm // tkc):
        vt_ref[c] = vt[:, c * tkc:(c + 1) * tkc].astype(BF16)

    sg = _gelu_tanh(sg)
    u = sg[:, :SGU_WIDTH]
    v = sg[:, SGU_WIDTH:]
    gmat = gmat_ref[...]
    mu = _split_dot(v, gmat)
    vc = v - mu
    var = _split_dot(vc * vc, gmat)
    vn = (vc * lax.rsqrt(var + LN_EPS) * slg_ref[...] + slb_ref[...]).astype(BF16)
    head = lax.broadcasted_iota(jnp.int32, (1, SGU_WIDTH), 1) // SGU_HEAD_DIM
    zero = jnp.zeros((), BF16)
    for c in range(tm // SGU_CHUNK):
        rows = slice(c * SGU_CHUNK, (c + 1) * SGU_CHUNK)
        vch = vn[rows]
        stacked = jnp.concatenate([jnp.where(head == g, vch, zero) for g in range(SGU_HEADS)], axis=0)
        mixed = sb_ref[...] + _dot(ws_ref[...], stacked)
        od_ref[rows, :] = (u[rows] * mixed).astype(BF16)


def _inproj(x, lp, tabs, apply_ln, T):
    n = x.shape[0]
    tm = TM_IN
    nt = T // tm
    row = lambda i: (i, 0)
    tab = lambda i: (i % nt, 0)
    weights = [lp['eg'], lp['eb'], lp['w_nat'], lp['w_fb'], lp['w_cq'], lp['w_ckv'], lp['w_sg'],
               lp['q_g'], lp['w_a'], lp['kv_g'], lp['w_kn'], lp['w_vt']]
    tables = [tabs['ca'], tabs['cb'], tabs['tk']]
    tail = [lp['gmat'], lp['sl_g'], lp['sl_b'], lp['w_s'], lp['s_b']]
    tab_t = lambda i: (0, i % nt)
    in_specs = ([pl.BlockSpec((tm, D_MODEL), row)] + [_full(w.shape) for w in weights]
                + [pl.BlockSpec((MLA_PAD, tm), tab_t), pl.BlockSpec((MLA_PAD, tm), tab_t),
                   pl.BlockSpec((tm, LANES), tab)] + [_full(w.shape) for w in tail])
    out_shape = [jax.ShapeDtypeStruct((n, 3 * NA_WIDTH), BF16),
                 jax.ShapeDtypeStruct((n, FN_WIDTH), F32),
                 jax.ShapeDtypeStruct((n // MLA_TQ, MLA_HEADS * MLA_PAD, MLA_TQ), BF16),
                 jax.ShapeDtypeStruct((n, MLA_HEADS * MLA_PAD), BF16),
                 jax.ShapeDtypeStruct((n // MLA_TK, MLA_WIDTH, MLA_TK), BF16),
                 jax.ShapeDtypeStruct((n, SGU_WIDTH), BF16)]
    out_specs = [pl.BlockSpec((tm, 3 * NA_WIDTH), row),
                 pl.BlockSpec((tm, FN_WIDTH), row),
                 pl.BlockSpec((tm // MLA_TQ, MLA_HEADS * MLA_PAD, MLA_TQ), lambda i: (i, 0, 0)),
                 pl.BlockSpec((tm, MLA_HEADS * MLA_PAD), row),
                 pl.BlockSpec((tm // MLA_TK, MLA_WIDTH, MLA_TK), lambda i: (i, 0, 0)),
                 pl.BlockSpec((tm, SGU_WIDTH), row)]
    if apply_ln:
        out_shape = [jax.ShapeDtypeStruct((n, D_MODEL), F32)] + out_shape
        out_specs = [pl.BlockSpec((tm, D_MODEL), row)] + out_specs
    return pl.pallas_call(
        functools.partial(_inproj_kernel, apply_ln),
        grid=(n // tm,),
        in_specs=in_specs,
        out_specs=out_specs,
        out_shape=out_shape,
        compiler_params=_params("parallel"),
        name="inproj_ln" if apply_ln else "inproj",
    )(x, *weights, *tables, *tail)


def _natten_kernel(q_ref, k0_ref, k1_ref, k2_ref, v0_ref, v1_ref, v2_ref, bias_ref, o_ref, s_ref):
    q = q_ref[...]
    k = jnp.concatenate([k0_ref[...], k1_ref[...], k2_ref[...]], axis=0)
    v = jnp.concatenate([v0_ref[...], v1_ref[...], v2_ref[...]], axis=0)
    head = lax.broadcasted_iota(jnp.int32, (1, NA_WIDTH), 1) // HEAD_DIM
    zero = jnp.zeros((), BF16)

    def scores(h):
        return _dot_nt(jnp.where(head == h, q, zero), k) + bias_ref[h]

    s_ref[0] = scores(0)
    acc = jnp.zeros((NA_QBLK, NA_WIDTH), F32)
    for h in range(NA_HEADS):
        if h + 1 < NA_HEADS:
            s_ref[(h + 1) % 2] = scores(h + 1)
        s = s_ref[h % 2]
        m = jnp.max(s, axis=-1, keepdims=True)
        p = jnp.exp2(s - m)
        l = jnp.sum(p, axis=-1, keepdims=True)
        acc = acc + _dot(p.astype(BF16), jnp.where(head == h, v, zero)) * (1.0 / l)
    o_ref[...] = acc.astype(BF16)


def _natten(nat, bias, B, T):
    n = nat.shape[0]
    nblk = T // NA_QBLK

    def kv_map(d, col):
        def f(b, j):
            base = jnp.clip(j - 1, 0, nblk - NA_KBLKS)
            return (b * nblk + base + d, col)
        return f

    def bias_map(b, j):
        typ = jnp.where(j == 0, 0, jnp.where(j == nblk - 1, 2, 1))
        return (typ, 0, 0, 0)

    blk = (NA_QBLK, NA_WIDTH)
    in_specs = ([pl.BlockSpec(blk, lambda b, j: (b * nblk + j, 0))]
                + [pl.BlockSpec(blk, kv_map(d, 1)) for d in range(NA_KBLKS)]
                + [pl.BlockSpec(blk, kv_map(d, 2)) for d in range(NA_KBLKS)]
                + [pl.BlockSpec((None, NA_HEADS, NA_QBLK, NA_KBLKS * NA_QBLK), bias_map)])
    return pl.pallas_call(
        _natten_kernel,
        grid=(B, nblk),
        in_specs=in_specs,
        out_specs=pl.BlockSpec(blk, lambda b, j: (b * nblk + j, 0)),
        out_shape=jax.ShapeDtypeStruct((n, NA_WIDTH), BF16),
        scratch_shapes=[pltpu.VMEM((2, NA_QBLK, NA_KBLKS * NA_QBLK), F32)],
        compiler_params=_params("parallel", "arbitrary"),
        name="natten",
    )(nat, nat, nat, nat, nat, nat, nat, bias)


def _split_bf16(x):
    hi = x.astype(BF16)
    return hi, (x - hi.astype(F32)).astype(BF16)


def _fft1_kernel(x_ref, m_ref, a_ref):
    kk = m_ref.shape[0]
    n1 = x_ref.shape[0]
    xt = pltpu.einshape("abc->bac", x_ref[...])
    for i in range(kk):
        x_hi, x_lo = _split_bf16(xt[i])
        m = m_ref[i]
        r = _dot(m, x_hi)
        res = r[:2 * n1] + r[2 * n1:] + _dot(m[:2 * n1], x_lo)
        a_ref[i, :, :FN_WIDTH] = res[:n1]
        a_ref[i, :, FN_WIDTH:] = res[n1:]


def _fft2_kernel(a_ref, cs_ref, w_ref, y_ref):
    n2, kk, _ = a_ref.shape
    cs = cs_ref[...]
    at = pltpu.einshape("abc->bac", a_ref[...])
    g = []
    for i in range(kk):
        a_hi, a_lo = _split_bf16(at[i])
        r = _dot(cs, a_hi)
        r = r[:2 * n2] + r[2 * n2:] + _dot(cs[:2 * n2], a_lo)
        g.append(jnp.concatenate([r[:n2, :FN_WIDTH] + r[n2:, FN_WIDTH:],
                                  r[:n2, FN_WIDTH:] - r[n2:, :FN_WIDTH]], axis=1))
    g_hi, g_lo = _split_bf16(jnp.concatenate(g, axis=0))
    y = _dot(g_hi, w_ref[0]) + _dot(g_lo, w_ref[0]) + _dot(g_hi, w_ref[1])
    y_ref[...] = pltpu.einshape("abc->bac", y.reshape(kk, n2, FN_WIDTH))


def _fourier(fb, ft, B, T):
    n1, n2 = ft['n1'], ft['n2']
    kk = SUBLANES
    a = pl.pallas_call(
        _fft1_kernel,
        grid=(B, n2 // kk),
        in_specs=[pl.BlockSpec((None, n1, kk, FN_WIDTH), lambda b, j: (b, 0, j, 0)),
                  pl.BlockSpec((kk, 4 * n1, n1), lambda b, j: (j, 0, 0))],
        out_specs=pl.BlockSpec((None, kk, n1, 2 * FN_WIDTH), lambda b, j: (b, j, 0, 0)),
        out_shape=jax.ShapeDtypeStruct((B, n2, n1, 2 * FN_WIDTH), F32),
        compiler_params=_params("parallel", "arbitrary"),
        name="fft_stage1",
    )(fb.reshape(B, n1, n2, FN_WIDTH), ft['m1'])
    y = pl.pallas_call(
        _fft2_kernel,
        grid=(B, n1 // kk),
        in_specs=[pl.BlockSpec((None, n2, kk, 2 * FN_WIDTH), lambda b, j: (b, 0, j, 0)),
                  _full(ft['cs2'].shape), _full(ft['w'].shape)],
        out_specs=pl.BlockSpec((None, n2, kk, FN_WIDTH), lambda b, j: (b, 0, j, 0)),
        out_shape=jax.ShapeDtypeStruct((B, n2, n1, FN_WIDTH), F32),
        compiler_params=_params("parallel", "arbitrary"),
        name="fft_stage2",
    )(a, ft['cs2'], ft['w'])
    return y.reshape(B * T, FN_WIDTH)


def _mla_kernel(q_ref, k_ref, vt_ref, o_ref, s_ref):
    nk, _, tk = vt_ref.shape
    nq, _, tq = q_ref.shape
    unroll = min(MLA_UNROLL, nk)
    ones = jnp.ones((MLA_ONES, tk), BF16)

    def scores(i, j):
        return _dot(k_ref[pl.ds(pl.multiple_of(j * tk, tk), tk), :], q_ref[i])

    def update(carry, j, slot):
        m, acc = carry
        st = s_ref[slot]
        m_new = jnp.maximum(m, jnp.max(st, axis=0, keepdims=True))
        alpha = jnp.exp2(m - m_new)
        p = jnp.exp2(st - m_new).astype(BF16)
        vte = jnp.concatenate([vt_ref[j], ones], axis=0)
        return m_new, alpha * acc + _dot(vte, p)

    s_ref[0] = scores(0, 0)
    m0 = jnp.full((1, tq), -1e30, F32)
    acc0 = jnp.zeros((MLA_V + MLA_ONES, tq), F32)

    def body(t, carry):
        i = (t * unroll) // nk
        j0 = (t * unroll) % nk
        fresh = j0 == 0
        carry = (jnp.where(fresh, m0, carry[0]), jnp.where(fresh, acc0, carry[1]))
        for c in range(unroll):
            if c + 1 < unroll:
                s_ref[(c + 1) % 2] = scores(i, j0 + c + 1)
            else:
                nxt = jnp.minimum(t * unroll + unroll, nq * nk - 1)
                s_ref[(c + 1) % 2] = scores(nxt // nk, nxt % nk)
            carry = update(carry, j0 + c, c % 2)

        @pl.when(j0 + unroll == nk)
        def _():
            acc = carry[1]
            o_ref[i] = acc[:MLA_V] * (1.0 / acc[MLA_V:MLA_V + 1])

        return carry

    lax.fori_loop(0, nq * nk // unroll, body, (m0, acc0))


def _mla(q, k, vt, B, T):
    nq = T // MLA_TQ
    nk = T // MLA_TK
    assert nk % min(MLA_UNROLL, nk) == 0 and min(MLA_UNROLL, nk) % 2 == 0
    return pl.pallas_call(
        _mla_kernel,
        grid=(B, MLA_HEADS),
        in_specs=[pl.BlockSpec((nq, MLA_PAD, MLA_TQ), lambda b, h: (b, h, 0)),
                  pl.BlockSpec((T, MLA_PAD), lambda b, h: (b, h)),
                  pl.BlockSpec((nk, MLA_V, MLA_TK), lambda b, h: (b, h, 0))],
        out_specs=pl.BlockSpec((nq, MLA_V, MLA_TQ), lambda b, h: (b, h, 0)),
        out_shape=jax.ShapeDtypeStruct((B * nq, MLA_WIDTH, MLA_TQ), F32),
        scratch_shapes=[pltpu.VMEM((2, MLA_TK, MLA_TQ), F32)],
        compiler_params=_params("parallel", "arbitrary"),
        name="mla",
    )(q, k, vt)


def _outproj_kernel(oa_ref, ob_ref, oct_ref, od_ref, x_ref, w_ref, g_ref, b_ref, o_ref):
    w = NA_WIDTH
    oc = oct_ref[...].T.astype(BF16)
    y = (_dot(oa_ref[...], w_ref[0:w]) + _dot(ob_ref[...].astype(BF16), w_ref[w:2 * w])
         + _dot(oc, w_ref[2 * w:3 * w]) + _dot(od_ref[...], w_ref[3 * w:4 * w]))
    o_ref[...] = _layer_norm_rows(ALPHA * x_ref[...] + y, g_ref[...], b_ref[...])


def _outproj(oa, ob, oct, od, x, lp):
    n = x.shape[0]
    tm = TM_OUT
    row = lambda i: (i, 0)
    return pl.pallas_call(
        _outproj_kernel,
        grid=(n // tm,),
        in_specs=[pl.BlockSpec((tm, NA_WIDTH), row), pl.BlockSpec((tm, FN_WIDTH), row),
                  pl.BlockSpec((None, MLA_WIDTH, tm), lambda i: (i, 0, 0)), pl.BlockSpec((tm, SGU_WIDTH), row),
                  pl.BlockSpec((tm, D_MODEL), row), _full(lp['w_out'].shape),
                  _full(lp['ln1_g'].shape), _full(lp['ln1_b'].shape)],
        out_specs=pl.BlockSpec((tm, D_MODEL), row),
        out_shape=jax.ShapeDtypeStruct((n, D_MODEL), F32),
        compiler_params=_params("parallel"),
        name="outproj",
    )(oa, ob, oct, od, x, lp['w_out'], lp['ln1_g'], lp['ln1_b'])


def _ffn_kernel(nt, x_ref, xp_ref, xn_ref, wup_ref, cw_ref, cb_ref, wd_ref, g_ref, b_ref, o_ref,
                xe_ref, act_ref):
    i = pl.program_id(0)
    tm = x_ref.shape[0]
    cf = FF_CHUNK
    first = (i % nt) == 0
    last = (i % nt) == nt - 1
    xe_ref[0:HALO] = jnp.where(first, 0.0, xp_ref[...]).astype(BF16)
    xe_ref[HALO:HALO + tm] = x_ref[...].astype(BF16)
    xe_ref[HALO + tm:] = jnp.where(last, 0.0, xn_ref[...]).astype(BF16)
    xe = xe_ref[...]

    rows = slice(HALO, HALO + tm)

    def conv(h, cols):
        prev = pltpu.roll(h, 1, axis=0)
        nxt = pltpu.roll(h, h.shape[0] - 1, axis=0)
        return (prev[rows] * cw_ref[0:1, cols] + h[rows] * cw_ref[1:2, cols]
                + nxt[rows] * cw_ref[2:3, cols] + cb_ref[:, cols])

    for c in range(D_FF // cf):
        gate = slice(c * cf, (c + 1) * cf)
        val = slice(D_FF + c * cf, D_FF + (c + 1) * cf)
        act = _gelu_tanh_x2(conv(_dot(xe, wup_ref[:, gate]), gate)) * conv(_dot(xe, wup_ref[:, val]), val)
        act_ref[:, gate] = act.astype(BF16)

    y = _dot(act_ref[...], wd_ref[...])
    o_ref[...] = _layer_norm_rows(ALPHA * x_ref[...] + y, g_ref[...], b_ref[...])


def _resident(shape):
    nd = len(shape)
    return pl.BlockSpec(shape, lambda *_: (0,) * nd, pipeline_mode=pl.Buffered(1))


def _ffn(x, lp, T):
    n = x.shape[0]
    tm = TM_FFN
    nt = T // tm
    cf = FF_CHUNK
    hb = tm // HALO
    nhb = n // HALO
    row = lambda i: (i, 0)
    return pl.pallas_call(
        functools.partial(_ffn_kernel, nt),
        grid=(n // tm,),
        in_specs=[pl.BlockSpec((tm, D_MODEL), row),
                  pl.BlockSpec((HALO, D_MODEL), lambda i: (jnp.maximum(i * hb - 1, 0), 0)),
                  pl.BlockSpec((HALO, D_MODEL), lambda i: (jnp.minimum((i + 1) * hb, nhb - 1), 0)),
                  _resident(lp['w_up'].shape), _resident(lp['conv_w'].shape), _resident(lp['conv_b'].shape),
                  _resident(lp['w_down'].shape), _resident(lp['ln2_g'].shape), _resident(lp['ln2_b'].shape)],
        out_specs=pl.BlockSpec((tm, D_MODEL), row),
        out_shape=jax.ShapeDtypeStruct((n, D_MODEL), F32),
        scratch_shapes=[pltpu.VMEM((tm + 2 * HALO, D_MODEL), BF16),
                        pltpu.VMEM((tm, D_FF), BF16)],
        compiler_params=_params("parallel"),
        name="conv_ffn",
    )(x, x, x, lp['w_up'], lp['conv_w'], lp['conv_b'], lp['w_down'], lp['ln2_g'], lp['ln2_b'])


def _natten_bias(rpb):
    nr, nc = 2 * NA_ROWS - 1, 2 * NA_COLS - 1
    krows = NA_KBLKS * NA_QROWS
    col = np.arange(GRID_W)
    ci = np.clip(col[None, :] - col[:, None] + NA_COLS - 1, 0, nc - 1)
    onehot_c = (ci.reshape(-1)[None, :] == np.arange(nc)[:, None]).astype(np.float32)
    cs = np.clip(col - NA_COLS // 2, 0, GRID_W - NA_COLS)
    dc = col[None, :] - cs[:, None]
    valid_c = (dc >= 0) & (dc < NA_COLS)
    qa = np.arange(NA_QROWS)
    kr = np.arange(krows)
    onehot_r, valid_r = [], []
    for kind in range(3):
        q_row = (0, NA_QROWS, 2 * NA_QROWS)[kind] + qa
        r_start = (np.zeros_like(qa), qa, np.full_like(qa, NA_QROWS))[kind]
        dr = kr[None, :] - r_start[:, None]
        valid_r.append((dr >= 0) & (dr < NA_ROWS))
        ri = np.clip(kr[None, :] - q_row[:, None] + NA_ROWS - 1, 0, nr - 1)
        onehot_r.append((ri.reshape(-1)[:, None] == np.arange(nr)[None, :]).astype(np.float32))
    onehot_r = np.stack(onehot_r)
    valid = np.stack(valid_r)[:, :, None, :, None] & valid_c[None, None, :, None, :]
    by_col = jnp.einsum('hrc,cx->hrx', rpb, onehot_c, precision=HI)
    full = jnp.einsum('kpr,hrx->khpx', onehot_r, by_col, precision=HI)
    full = full.reshape(3, NA_HEADS, NA_QROWS, krows, GRID_W, GRID_W).transpose(0, 1, 2, 4, 3, 5)
    full = jnp.where(valid[:, None], full * LOG2_E, -1e30)
    return full.reshape(3, NA_HEADS, NA_QBLK, krows * GRID_W).astype(F32)


def _prep_layer(l, emb_ln_g, emb_ln_b, w_in, na_rpb, mla_q_g, w_uq, mla_kv_g, w_ukv, sgu_ln_g, sgu_ln_b,
                sgu_w, sgu_b, w_out, ln1_g, ln1_b, w_up, conv_w, conv_b, w_down, ln2_g, ln2_b):
    wi = w_in[l]
    o_fb = 3 * NA_WIDTH
    o_cq = o_fb + FN_WIDTH
    o_ckv = o_cq + MLA_Q_LORA
    o_kr = o_ckv + MLA_KV_LORA
    o_sg = o_kr + MLA_ROPE
    half = MLA_ROPE // 2
    swap = np.concatenate([np.arange(half, MLA_ROPE), np.arange(half)])
    w_kr = wi[:, o_kr:o_sg]
    w_ckv = jnp.concatenate([wi[:, o_ckv:o_kr], w_kr, w_kr[:, swap],
                             jnp.zeros((D_MODEL, LANES - 2 * MLA_ROPE), F32)], axis=1)

    uq = w_uq[l]
    pad = jnp.zeros((MLA_Q_LORA, MLA_HEADS, MLA_PAD - MLA_NOPE - MLA_ROPE), F32)
    w_a = jnp.concatenate([uq, pad], axis=2).reshape(MLA_Q_LORA, MLA_HEADS * MLA_PAD)
    ukv = w_ukv[l]
    w_kn = jnp.concatenate([ukv[:, :, :MLA_NOPE],
                            jnp.zeros((MLA_KV_LORA, MLA_HEADS, MLA_PAD - MLA_NOPE), F32)],
                           axis=2).reshape(MLA_KV_LORA, MLA_HEADS * MLA_PAD)
    w_vt = ukv[:, :, MLA_NOPE:].reshape(MLA_KV_LORA, MLA_WIDTH).T
    w_s = jnp.concatenate([sgu_w[l][g] for g in range(SGU_HEADS)], axis=1)
    gmat =np.kron(np.eye(SGU_HEADS), np.full((SGU_HEAD_DIM, SGU_HEAD_DIM), 1.0 / SGU_HEAD_DIM)).astype(np.float32)
    s_b = jnp.repeat(sgu_b[l].T, SGU_HEAD_DIM, axis=1)
    r1 = lambda a: a.reshape(1, -1).astype(F32)
    half_val = jnp.concatenate([jnp.ones((1, D_FF), F32), jnp.full((1, D_FF), 0.5, F32)], axis=1)
    return dict(
        eg=r1(emb_ln_g), eb=r1(emb_ln_b),
        w_nat=wi[:, :o_fb].astype(BF16), w_fb=wi[:, o_fb:o_cq].astype(BF16),
        w_cq=wi[:, o_cq:o_ckv].astype(BF16), w_ckv=w_ckv.astype(BF16), w_sg=wi[:, o_sg:].astype(BF16),
        q_g=r1(mla_q_g[l]), w_a=w_a.T.astype(BF16),
        kv_g=r1(mla_kv_g[l]), w_kn=w_kn.astype(BF16), w_vt=w_vt.astype(BF16),
        gmat=jnp.asarray(gmat, BF16),
        sl_g=r1(sgu_ln_g[l]), sl_b=r1(sgu_ln_b[l]), w_s=w_s.astype(BF16), s_b=s_b.astype(F32),
        na_bias=_natten_bias(na_rpb[l]),
        w_out=w_out[l].astype(BF16), ln1_g=r1(ln1_g[l]), ln1_b=r1(ln1_b[l]),
        w_up=w_up[l].astype(BF16), conv_w=conv_w[l].astype(F32) * half_val, conv_b=r1(conv_b[l]) * half_val,
        w_down=w_down[l].astype(BF16), ln2_g=r1(ln2_g[l]), ln2_b=r1(ln2_b[l]),
    )


def _rope_tables(T):
    inv_freq = ROPE_THETA ** (-np.arange(0, MLA_ROPE, 2, dtype=np.float64) / MLA_ROPE)
    ang = np.arange(T, dtype=np.float64)[:, None] * inv_freq[None, :]
    cos, sin = np.cos(ang), np.sin(ang)
    cos2 = np.concatenate([cos, cos], axis=1)
    sin2 = np.concatenate([-sin, sin], axis=1)
    scale = (MLA_NOPE + MLA_ROPE) ** -0.5 * math.log2(math.e)
    zq = np.zeros((T, MLA_PAD - MLA_NOPE - MLA_ROPE))
    ca = np.concatenate([np.full((T, MLA_NOPE), scale), scale * cos2, zq], axis=1)
    cb = np.concatenate([np.zeros((T, MLA_NOPE)), scale * sin2, zq], axis=1)
    tk = np.concatenate([cos2, sin2, np.zeros((T, LANES - 2 * MLA_ROPE))], axis=1)
    f32 = lambda a: jnp.asarray(np.ascontiguousarray(a), F32)
    return dict(ca=f32(ca.T), cb=f32(cb.T), tk=f32(tk))


def _dft_tables(T):
    n1 = 1 << (int(math.log2(T)) // 2)
    n2 = T // n1

    def cs(num, den):
        ang = (num % den).astype(np.float64) * (2.0 * np.pi / den)
        return np.cos(ang), np.sin(ang)

    def split(x, axis):
        x = x.astype(np.float32)
        hi = x.astype(BF16)
        lo = (x - hi.astype(np.float32)).astype(BF16)
        return np.concatenate([hi, lo], axis=axis) if axis is not None else np.stack([hi, lo])

    t2 = np.arange(n2, dtype=np.int64)[:, None, None]
    k1 = np.arange(n1, dtype=np.int64)[None, :, None]
    t1 = np.arange(n1, dtype=np.int64)[None, None, :]
    c, s = cs(k1 * (t1 * n2 + t2), T)
    m1 = split(np.concatenate([c, -s], axis=1), 1)
    k2 = np.arange(n2, dtype=np.int64)
    c2, s2 = cs(k2[:, None] * k2[None, :], n2)
    cs2 = split(np.concatenate([c2, s2], axis=0), 0)
    cg = np.arange(FN_GROUP_DIM, dtype=np.int64)
    cc, sc = cs(cg[:, None] * cg[None, :], FN_GROUP_DIM)
    norm = 1.0 / math.sqrt(T * FN_GROUP_DIM)
    eye = np.eye(FN_GROUPS)
    w = split(np.concatenate([np.kron(eye, cc), np.kron(eye, sc)], axis=0) * norm, None)
    return dict(n1=n1, n2=n2, m1=jnp.asarray(m1), cs2=jnp.asarray(cs2), w=jnp.asarray(w))


def _trunk(x, layers):
    B, T, _ = x.shape
    tabs = _rope_tables(T)
    ft = _dft_tables(T)
    h = x.reshape(B * T, D_MODEL)
    for l, lp in enumerate(layers):
        outs = _inproj(h, lp, tabs, l == 0, T)
        if l == 0:
            h = outs[0]
            outs = outs[1:]
        nat, fb, q, k, vt, od = outs
        oa = _natten(nat, lp['na_bias'], B, T)
        ob = _fourier(fb, ft, B, T)
        oct = _mla(q, k, vt, B, T)
        h1 = _outproj(oa, ob, oct, od, h, lp)
        h = _ffn(h1, lp, T)
    return h.reshape(B, T, D_MODEL)


def kernel(x_prompt, x_sample, emb_ln_g, emb_ln_b, w_in, na_rpb, mla_q_g, w_uq, mla_kv_g, w_ukv, sgu_ln_g,
           sgu_ln_b, sgu_w, sgu_b, w_out, ln1_g, ln1_b, w_up, conv_w, conv_b, w_down, ln2_g, ln2_b):
    layers = [_prep_layer(l, emb_ln_g, emb_ln_b, w_in, na_rpb, mla_q_g, w_uq, mla_kv_g, w_ukv, sgu_ln_g,
                          sgu_ln_b, sgu_w, sgu_b, w_out, ln1_g, ln1_b, w_up, conv_w, conv_b, w_down,
                          ln2_g, ln2_b) for l in range(DEPTH)]
    return (_trunk(x_prompt, layers), _trunk(x_sample, layers))
```

```python
import functools
import math

import numpy as np
import jax
import jax.numpy as jnp
from jax import lax
from jax.experimental import pallas as pl
from jax.experimental.pallas import tpu as pltpu

F32 = jnp.float32
BF16 = jnp.bfloat16

D_MODEL = 1024
DEPTH = 2
GRID_W = 64
HEAD_DIM = 64
NA_HEADS = 4
NA_ROWS = 8
NA_COLS = 16
NA_WIDTH = NA_HEADS * HEAD_DIM
FN_GROUPS = 4
FN_GROUP_DIM = 64
FN_WIDTH = FN_GROUPS * FN_GROUP_DIM
MLA_HEADS = 4
MLA_Q_LORA = 256
MLA_KV_LORA = 128
MLA_NOPE = 64
MLA_ROPE = 32
MLA_V = 64
MLA_WIDTH = MLA_HEADS * MLA_V
ROPE_THETA = 10000.0
SGU_HEADS = 4
SGU_HEAD_DIM = 64
SGU_WIDTH = SGU_HEADS * SGU_HEAD_DIM
SGU_CHUNK = 128
D_FF = 2816
ALPHA = (2 * DEPTH) ** 0.25
LN_EPS = 1e-5
RMS_EPS = 1e-6

LANES = 128
SUBLANES = 8
BF16_ROWS = 16
VMEM_LIMIT = 56 * 1024 * 1024

TM_IN = 512
TM_OUT = 512
TM_FFN = 512
FF_CHUNK = 256
NA_QROWS = 4
NA_QBLK = NA_QROWS * GRID_W
NA_KBLKS = 3
MLA_PAD = 128
MLA_TQ = 512
MLA_TK = 512
MLA_ONES = 16
MLA_UNROLL = 8
MLA_SCORE_BUFS = 4
HALO = SUBLANES

LOG2_E = math.log2(math.e)
HI = lax.Precision.HIGHEST
NT_DIMS = (((1,), (1,)), ((), ()))


def _dot(a, b):
    return jnp.dot(a, b, preferred_element_type=F32)


def _dot_nt(a, b):
    return lax.dot_general(a, b, NT_DIMS, preferred_element_type=F32)


def _layer_norm_rows(x, g, b):
    mu = jnp.mean(x, axis=-1, keepdims=True)
    xc = x - mu
    var = jnp.mean(xc * xc, axis=-1, keepdims=True)
    return xc * lax.rsqrt(var + LN_EPS) * g + b


GELU_C = 0.7978845608028654
GELU_K = 0.044715


def _gelu_tanh_x2(x):
    t = jnp.tanh(x * (GELU_C + (GELU_C * GELU_K) * (x * x)))
    return x + x * t


def _gelu_tanh(x):
    return 0.5 * _gelu_tanh_x2(x)


def _split_dot(v, m):
    hi = v.astype(BF16)
    lo = (v - hi.astype(F32)).astype(BF16)
    return _dot(hi, m) + _dot(lo, m)


def _params(*sem):
    return pltpu.CompilerParams(dimension_semantics=sem, vmem_limit_bytes=VMEM_LIMIT)


def _full(shape):
    nd = len(shape)
    return pl.BlockSpec(shape, lambda *_: (0,) * nd)


def _inproj_kernel(apply_ln, x_ref, eg_ref, eb_ref, wnat_ref, wfb_ref, wcq_ref, wckv_ref,
                   wsg_ref, qg_ref, wa_ref, kvg_ref, wkn_ref, wvt_ref, ca_ref, cb_ref,
                   tk_ref, gmat_ref, slg_ref, slb_ref, ws_ref, sb_ref, *out_refs):
    if apply_ln:
        xn_ref, nat_ref, fb_ref, q_ref, k_ref, vt_ref, od_ref = out_refs
    else:
        nat_ref, fb_ref, q_ref, k_ref, vt_ref, od_ref = out_refs
    x = x_ref[...]
    if apply_ln:
        x = _layer_norm_rows(x, eg_ref[...], eb_ref[...])
        xn_ref[...] = x
    xb = x.astype(BF16)
    tm = x.shape[0]

    cq = _dot(xb, wcq_ref[...])
    ckv_kr = _dot(xb, wckv_ref[...])
    sg = _dot(xb, wsg_ref[...])

    nat = _dot(xb, wnat_ref[...])
    nat_ref[:, :NA_WIDTH] = (nat[:, :NA_WIDTH] * (HEAD_DIM ** -0.5 * LOG2_E)).astype(BF16)
    nat_ref[:, NA_WIDTH:] = nat[:, NA_WIDTH:].astype(BF16)

    fb_ref[...] = _dot(xb, wfb_ref[...]).astype(BF16)

    cqn = (cq * lax.rsqrt(jnp.mean(cq * cq, axis=-1, keepdims=True) + RMS_EPS) * qg_ref[...]).astype(BF16)
    qa = _dot_nt(wa_ref[...], cqn)
    ca = ca_ref[...]
    cb = cb_ref[MLA_NOPE:MLA_NOPE + MLA_ROPE]
    half = MLA_ROPE // 2
    for h in range(MLA_HEADS):
        base = h * MLA_PAD
        qh = qa[base:base + MLA_PAD] * ca
        lo = qa[base + MLA_NOPE:base + MLA_NOPE + half]
        hi = qa[base + MLA_NOPE + half:base + MLA_NOPE + MLA_ROPE]
        rope = qh[MLA_NOPE:MLA_NOPE + MLA_ROPE] + jnp.concatenate([hi, lo], axis=0) * cb
        qh = jnp.concatenate([qh[:MLA_NOPE], rope, qh[MLA_NOPE + MLA_ROPE:]], axis=0).astype(BF16)
        for c in range(tm // MLA_TQ):
            q_ref[c, base:base + MLA_PAD, :] = qh[:, c * MLA_TQ:(c + 1) * MLA_TQ]

    ckv = ckv_kr[:, :MLA_KV_LORA]
    ckvn = (ckv * lax.rsqrt(jnp.mean(ckv * ckv, axis=-1, keepdims=True) + RMS_EPS) * kvg_ref[...]).astype(BF16)
    kr = ckv_kr[:, MLA_KV_LORA:] * tk_ref[...]
    kr = pltpu.roll(kr, MLA_NOPE, axis=1) + pltpu.roll(kr, MLA_NOPE - MLA_ROPE, axis=1)
    lane = lax.broadcasted_iota(jnp.int32, (1, MLA_PAD), 1)
    kr = jnp.where((lane >= MLA_NOPE) & (lane < MLA_NOPE + MLA_ROPE), kr, 0.0)
    kn = _dot(ckvn, wkn_ref[...])
    for h in range(MLA_HEADS):
        sl = slice(h * MLA_PAD, (h + 1) * MLA_PAD)
        k_ref[:, sl] = (kn[:, sl] + kr).astype(BF16)
    vt = _dot_nt(wvt_ref[...], ckvn)
    tkc = vt_ref.shape[2]
    for c in range(tm // tkc):
        vt_ref[c] = vt[:, c * tkc:(c + 1) * tkc].astype(BF16)

    sg = _gelu_tanh(sg)
    u = sg[:, :SGU_WIDTH]
    v = sg[:, SGU_WIDTH:]
    gmat = gmat_ref[...]
    mu = _split_dot(v, gmat)
    vc = v - mu
    var = _split_dot(vc * vc, gmat)
    vn = (vc * lax.rsqrt(var + LN_EPS) * slg_ref[...] + slb_ref[...]).astype(BF16)
    head = lax.broadcasted_iota(jnp.int32, (1, SGU_WIDTH), 1) // SGU_HEAD_DIM
    zero = jnp.zeros((), BF16)
    for c in range(tm // SGU_CHUNK):
        rows = slice(c * SGU_CHUNK, (c + 1) * SGU_CHUNK)
        vch = vn[rows]
        stacked = jnp.concatenate([jnp.where(head == g, vch, zero) for g in range(SGU_HEADS)], axis=0)
        mixed = sb_ref[...] + _dot(ws_ref[...], stacked)
        od_ref[rows, :] = (u[rows] * mixed).astype(BF16)


def _inproj(x, lp, tabs, apply_ln, T):
    n = x.shape[0]
    tm = TM_IN
    assert T % tm == 0 and tm % MLA_TQ == 0 and tm % MLA_TK == 0 and tm % SGU_CHUNK == 0
    nt = T // tm
    row = lambda i: (i, 0)
    tab = lambda i: (i % nt, 0)
    weights = [lp['eg'], lp['eb'], lp['w_nat'], lp['w_fb'], lp['w_cq'], lp['w_ckv'], lp['w_sg'],
               lp['q_g'], lp['w_a'], lp['kv_g'], lp['w_kn'], lp['w_vt']]
    tables = [tabs['ca'], tabs['cb'], tabs['tk']]
    tail = [lp['gmat'], lp['sl_g'], lp['sl_b'], lp['w_s'], lp['s_b']]
    tab_t = lambda i: (0, i % nt)
    in_specs = ([pl.BlockSpec((tm, D_MODEL), row)] + [_full(w.shape) for w in weights]
                + [pl.BlockSpec((MLA_PAD, tm), tab_t), pl.BlockSpec((MLA_PAD, tm), tab_t),
                   pl.BlockSpec((tm, LANES), tab)] + [_full(w.shape) for w in tail])
    out_shape = [jax.ShapeDtypeStruct((n, 3 * NA_WIDTH), BF16),
                 jax.ShapeDtypeStruct((n, FN_WIDTH), BF16),
                 jax.ShapeDtypeStruct((n // MLA_TQ, MLA_HEADS * MLA_PAD, MLA_TQ), BF16),
                 jax.ShapeDtypeStruct((n, MLA_HEADS * MLA_PAD), BF16),
                 jax.ShapeDtypeStruct((n // MLA_TK, MLA_WIDTH, MLA_TK), BF16),
                 jax.ShapeDtypeStruct((n, SGU_WIDTH), BF16)]
    out_specs = [pl.BlockSpec((tm, 3 * NA_WIDTH), row),
                 pl.BlockSpec((tm, FN_WIDTH), row),
                 pl.BlockSpec((tm // MLA_TQ, MLA_HEADS * MLA_PAD, MLA_TQ), lambda i: (i, 0, 0)),
                 pl.BlockSpec((tm, MLA_HEADS * MLA_PAD), row),
                 pl.BlockSpec((tm // MLA_TK, MLA_WIDTH, MLA_TK), lambda i: (i, 0, 0)),
                 pl.BlockSpec((tm, SGU_WIDTH), row)]
    if apply_ln:
        out_shape = [jax.ShapeDtypeStruct((n, D_MODEL), F32)] + out_shape
        out_specs = [pl.BlockSpec((tm, D_MODEL), row)] + out_specs
    return pl.pallas_call(
        functools.partial(_inproj_kernel, apply_ln),
        grid=(n // tm,),
        in_specs=in_specs,
        out_specs=out_specs,
        out_shape=out_shape,
        compiler_params=_params("parallel"),
        name="inproj_ln" if apply_ln else "inproj",
    )(x, *weights, *tables, *tail)


def _natten_kernel(q_ref, k0_ref, k1_ref, k2_ref, v0_ref, v1_ref, v2_ref, bias_ref, o_ref, s_ref, p_ref):
    q = q_ref[...]
    k_refs = (k0_ref, k1_ref, k2_ref)
    v_refs = (v0_ref, v1_ref, v2_ref)
    head = lax.broadcasted_iota(jnp.int32, (1, NA_WIDTH), 1) // HEAD_DIM
    zero = jnp.zeros((), BF16)

    def scores(h, slot):
        qh = jnp.where(head == h, q, zero)
        for d in range(NA_KBLKS):
            cols = slice(d * NA_QBLK, (d + 1) * NA_QBLK)
            s_ref[slot, :, cols] = _dot_nt(qh, k_refs[d][...]) + bias_ref[h, :, cols]

    scores(0, 0)
    acc = jnp.zeros((NA_QBLK, NA_WIDTH), F32)
    for h in range(NA_HEADS):
        if h + 1 < NA_HEADS:
            scores(h + 1, (h + 1) % 2)
        s = s_ref[h % 2]
        p = jnp.exp2(s - jnp.max(s, axis=-1, keepdims=True))
        inv = 1.0 / jnp.sum(p, axis=-1, keepdims=True)
        p_ref[...] = p.astype(BF16)
        pv = _dot(p_ref[:, 0:NA_QBLK], jnp.where(head == h, v_refs[0][...], zero))
        for d in range(1, NA_KBLKS):
            pv = pv + _dot(p_ref[:, d * NA_QBLK:(d + 1) * NA_QBLK], jnp.where(head == h, v_refs[d][...], zero))
        acc = acc + pv * inv
    o_ref[...] = acc.astype(BF16)


def _natten(nat, bias, B, T):
    n = nat.shape[0]
    nblk = T // NA_QBLK

    def kv_map(d, col):
        def f(b, j):
            base = jnp.clip(j - 1, 0, nblk - NA_KBLKS)
            return (b * nblk + base + d, col)
        return f

    def bias_map(b, j):
        typ = jnp.where(j == 0, 0, jnp.where(j == nblk - 1, 2, 1))
        return (typ, 0, 0, 0)

    blk = (NA_QBLK, NA_WIDTH)
    in_specs = ([pl.BlockSpec(blk, lambda b, j: (b * nblk + j, 0))]
                + [pl.BlockSpec(blk, kv_map(d, 1)) for d in range(NA_KBLKS)]
                + [pl.BlockSpec(blk, kv_map(d, 2)) for d in range(NA_KBLKS)]
                + [pl.BlockSpec((None, NA_HEADS, NA_QBLK, NA_KBLKS * NA_QBLK), bias_map)])
    return pl.pallas_call(
        _natten_kernel,
        grid=(B, nblk),
        in_specs=in_specs,
        out_specs=pl.BlockSpec(blk, lambda b, j: (b * nblk + j, 0)),
        out_shape=jax.ShapeDtypeStruct((n, NA_WIDTH), BF16),
        scratch_shapes=[pltpu.VMEM((2, NA_QBLK, NA_KBLKS * NA_QBLK), F32),
                        pltpu.VMEM((NA_QBLK, NA_KBLKS * NA_QBLK), BF16)],
        compiler_params=_params("parallel", "arbitrary"),
        name="natten",
    )(nat, nat, nat, nat, nat, nat, nat, bias)


def _fft1_kernel(x_ref, m_ref, a_ref):
    kk = m_ref.shape[0]
    n1 = x_ref.shape[0]
    xt = jnp.swapaxes(x_ref[...], 0, 1)
    for i in range(kk):
        res = _dot(m_ref[i], xt[i])
        res = jnp.concatenate([res[:n1], res[n1:]], axis=1).astype(BF16)
        a_ref[:, i] = res.reshape(n1 // kk, kk, 2 * FN_WIDTH)


def _fft2_kernel(a_ref, cs_ref, w_ref, y_ref):
    n2, kk, _ = a_ref.shape
    cs = cs_ref[...]
    at = jnp.swapaxes(a_ref[...], 0, 1)
    g = []
    for i in range(kk):
        r = _dot(cs, at[i])
        g.append(jnp.concatenate([r[:n2, :FN_WIDTH] + r[n2:, FN_WIDTH:],
                                  r[:n2, FN_WIDTH:] - r[n2:, :FN_WIDTH]], axis=1).astype(BF16))
    g = jnp.concatenate(g, axis=0)
    y = _dot(g, w_ref[...]).astype(BF16)
    y_ref[...] = jnp.swapaxes(y.reshape(kk, n2, FN_WIDTH), 0, 1)


def _fourier(fb, ft, B, T):
    n1, n2 = ft['n1'], ft['n2']
    kk = BF16_ROWS
    a = pl.pallas_call(
        _fft1_kernel,
        grid=(B, n2 // kk),
        in_specs=[pl.BlockSpec((None, n1, kk, FN_WIDTH), lambda b, j: (b, 0, j, 0)),
                  pl.BlockSpec((kk, 2 * n1, n1), lambda b, j: (j, 0, 0))],
        out_specs=pl.BlockSpec((None, n1 // kk, kk, kk, 2 * FN_WIDTH), lambda b, j: (b, 0, j, 0, 0)),
        out_shape=jax.ShapeDtypeStruct((B, n1 // kk, n2, kk, 2 * FN_WIDTH), BF16),
        compiler_params=_params("parallel", "arbitrary"),
        name="fft_stage1",
    )(fb.reshape(B, n1, n2, FN_WIDTH), ft['m1'])
    y = pl.pallas_call(
        _fft2_kernel,
        grid=(B, n1 // kk),
        in_specs=[pl.BlockSpec((None, None, n2, kk, 2 * FN_WIDTH), lambda b, j: (b, j, 0, 0, 0)),
                  _full(ft['cs2'].shape), _full(ft['w'].shape)],
        out_specs=pl.BlockSpec((None, n2, kk, FN_WIDTH), lambda b, j: (b, 0, j, 0)),
        out_shape=jax.ShapeDtypeStruct((B, n2, n1, FN_WIDTH), BF16),
        compiler_params=_params("parallel", "arbitrary"),
        name="fft_stage2",
    )(a, ft['cs2'], ft['w'])
    return y.reshape(B * T, FN_WIDTH)


def _mla_kernel(q_ref, k_ref, vt_ref, o_ref, s_ref):
    nk, _, tk = vt_ref.shape
    nq, _, tq = q_ref.shape
    unroll = min(MLA_UNROLL, nk)
    ones = jnp.ones((MLA_ONES, tk), BF16)

    def scores(i, j):
        start = j * tk if isinstance(j, int) else pl.multiple_of(j * tk, tk)
        return _dot(k_ref[pl.ds(start, tk), :], q_ref[i])

    def update(carry, j, slot):
        m, acc = carry
        st = s_ref[slot]
        m_new = jnp.maximum(m, jnp.max(st, axis=0, keepdims=True))
        alpha = jnp.exp2(m - m_new)
        p = jnp.exp2(st - m_new).astype(BF16)
        vte = jnp.concatenate([vt_ref[j], ones], axis=0)
        return m_new, alpha * acc + _dot(vte, p)

    nbuf = s_ref.shape[0]
    ahead = nbuf // 2
    last = nq * nk - 1
    for f in range(ahead):
        s_ref[f] = scores(f // nk, f % nk)
    m0 = jnp.full((1, tq), -1e30, F32)
    acc0 = jnp.zeros((MLA_V + MLA_ONES, tq), F32)

    def body(t, carry):
        i = (t * unroll) // nk
        j0 = (t * unroll) % nk
        fresh = j0 == 0
        carry = (jnp.where(fresh, m0, carry[0]), jnp.where(fresh, acc0, carry[1]))
        for c in range(unroll):
            if c + ahead < unroll:
                s_ref[(c + ahead) % nbuf] = scores(i, j0 + c + ahead)
            else:
                nxt = jnp.minimum(t * unroll + c + ahead, last)
                s_ref[(c + ahead) % nbuf] = scores(nxt // nk, nxt % nk)
            carry = update(carry, j0 + c, c % nbuf)

        @pl.when(j0 + unroll == nk)
        def _():
            acc = carry[1]
            o_ref[i] = acc[:MLA_V] * (1.0 / acc[MLA_V:MLA_V + 1])

        return carry

    lax.fori_loop(0, nq * nk // unroll, body, (m0, acc0))


def _mla(q, k, vt, B, T):
    nq = T // MLA_TQ
    nk = T // MLA_TK
    unroll = min(MLA_UNROLL, nk)
    nbuf = MLA_SCORE_BUFS if unroll % MLA_SCORE_BUFS == 0 else 2
    assert nk % unroll == 0 and unroll % nbuf == 0
    return pl.pallas_call(
        _mla_kernel,
        grid=(B, MLA_HEADS),
        in_specs=[pl.BlockSpec((nq, MLA_PAD, MLA_TQ), lambda b, h: (b, h, 0)),
                  pl.BlockSpec((T, MLA_PAD), lambda b, h: (b, h)),
                  pl.BlockSpec((nk, MLA_V, MLA_TK), lambda b, h: (b, h, 0))],
        out_specs=pl.BlockSpec((nq, MLA_V, MLA_TQ), lambda b, h: (b, h, 0)),
        out_shape=jax.ShapeDtypeStruct((B * nq, MLA_WIDTH, MLA_TQ), F32),
        scratch_shapes=[pltpu.VMEM((nbuf, MLA_TK, MLA_TQ), F32)],
        compiler_params=_params("parallel", "arbitrary"),
        name="mla",
    )(q, k, vt)


def _outproj_kernel(oa_ref, ob_ref, oct_ref, od_ref, x_ref, w_ref, g_ref, b_ref, o_ref):
    w = NA_WIDTH
    oc = oct_ref[...].T.astype(BF16)
    y = (_dot(oa_ref[...], w_ref[0:w]) + _dot(ob_ref[...], w_ref[w:2 * w])
         + _dot(oc, w_ref[2 * w:3 * w]) + _dot(od_ref[...], w_ref[3 * w:4 * w]))
    o_ref[...] = _layer_norm_rows(ALPHA * x_ref[...] + y, g_ref[...], b_ref[...])


def _outproj(oa, ob, oct, od, x, lp):
    n = x.shape[0]
    tm = TM_OUT
    assert tm == MLA_TQ
    row = lambda i: (i, 0)
    return pl.pallas_call(
        _outproj_kernel,
        grid=(n // tm,),
        in_specs=[pl.BlockSpec((tm, NA_WIDTH), row), pl.BlockSpec((tm, FN_WIDTH), row),
                  pl.BlockSpec((None, MLA_WIDTH, tm), lambda i: (i, 0, 0)), pl.BlockSpec((tm, SGU_WIDTH), row),
                  pl.BlockSpec((tm, D_MODEL), row), _full(lp['w_out'].shape),
                  _full(lp['ln1_g'].shape), _full(lp['ln1_b'].shape)],
        out_specs=pl.BlockSpec((tm, D_MODEL), row),
        out_shape=jax.ShapeDtypeStruct((n, D_MODEL), F32),
        compiler_params=_params("parallel"),
        name="outproj",
    )(oa, ob, oct, od, x, lp['w_out'], lp['ln1_g'], lp['ln1_b'])


def _ffn_kernel(nt, x_ref, xp_ref, xn_ref, wup_ref, cw_ref, cb_ref, wd_ref, g_ref, b_ref, o_ref,
                xe_ref, act_ref):
    i = pl.program_id(0)
    tm = x_ref.shape[0]
    cf = FF_CHUNK
    first = (i % nt) == 0
    last = (i % nt) == nt - 1
    xe_ref[0:HALO] = jnp.where(first, 0.0, xp_ref[...]).astype(BF16)
    xe_ref[HALO:HALO + tm] = x_ref[...].astype(BF16)
    xe_ref[HALO + tm:] = jnp.where(last, 0.0, xn_ref[...]).astype(BF16)
    xe = xe_ref[...]

    rows = slice(HALO, HALO + tm)

    def conv(h, cols):
        prev = pltpu.roll(h, 1, axis=0)
        nxt = pltpu.roll(h, h.shape[0] - 1, axis=0)
        return (prev[rows] * cw_ref[0:1, cols] + h[rows] * cw_ref[1:2, cols]
                + nxt[rows] * cw_ref[2:3, cols] + cb_ref[:, cols])

    for c in range(D_FF // cf):
        gate = slice(c * cf, (c + 1) * cf)
        val = slice(D_FF + c * cf, D_FF + (c + 1) * cf)
        act = _gelu_tanh_x2(conv(_dot(xe, wup_ref[:, gate]), gate)) * conv(_dot(xe, wup_ref[:, val]), val)
        act_ref[:, gate] = act.astype(BF16)

    y = _dot(act_ref[...], wd_ref[...])
    o_ref[...] = _layer_norm_rows(ALPHA * x_ref[...] + y, g_ref[...], b_ref[...])


def _resident(shape):
    nd = len(shape)
    return pl.BlockSpec(shape, lambda *_: (0,) * nd, pipeline_mode=pl.Buffered(1))


def _ffn(x, lp, T):
    n = x.shape[0]
    tm = TM_FFN
    nt = T // tm
    assert T % tm == 0 and D_FF % FF_CHUNK == 0
    hb = tm // HALO
    nhb = n // HALO
    row = lambda i: (i, 0)
    return pl.pallas_call(
        functools.partial(_ffn_kernel, nt),
        grid=(n // tm,),
        in_specs=[pl.BlockSpec((tm, D_MODEL), row),
                  pl.BlockSpec((HALO, D_MODEL), lambda i: (jnp.maximum(i * hb - 1, 0), 0)),
                  pl.BlockSpec((HALO, D_MODEL), lambda i: (jnp.minimum((i + 1) * hb, nhb - 1), 0)),
                  _resident(lp['w_up'].shape), _resident(lp['conv_w'].shape), _resident(lp['conv_b'].shape),
                  _resident(lp['w_down'].shape), _resident(lp['ln2_g'].shape), _resident(lp['ln2_b'].shape)],
        out_specs=pl.BlockSpec((tm, D_MODEL), row),
        out_shape=jax.ShapeDtypeStruct((n, D_MODEL), F32),
        scratch_shapes=[pltpu.VMEM((tm + 2 * HALO, D_MODEL), BF16),
                        pltpu.VMEM((tm, D_FF), BF16)],
        compiler_params=_params("parallel"),
        name="conv_ffn",
    )(x, x, x, lp['w_up'], lp['conv_w'], lp['conv_b'], lp['w_down'], lp['ln2_g'], lp['ln2_b'])


def _natten_bias(rpb):
    nr, nc = 2 * NA_ROWS - 1, 2 * NA_COLS - 1
    krows = NA_KBLKS * NA_QROWS
    col = np.arange(GRID_W)
    ci = np.clip(col[None, :] - col[:, None] + NA_COLS - 1, 0, nc - 1)
    onehot_c = (ci.reshape(-1)[None, :] == np.arange(nc)[:, None]).astype(np.float32)
    cs = np.clip(col - NA_COLS // 2, 0, GRID_W - NA_COLS)
    dc = col[None, :] - cs[:, None]
    valid_c = (dc >= 0) & (dc < NA_COLS)
    qa = np.arange(NA_QROWS)
    kr = np.arange(krows)
    onehot_r, valid_r = [], []
    for kind in range(3):
        q_row = (0, NA_QROWS, 2 * NA_QROWS)[kind] + qa
        r_start = (np.zeros_like(qa), qa, np.full_like(qa, NA_QROWS))[kind]
        dr = kr[None, :] - r_start[:, None]
        valid_r.append((dr >= 0) & (dr < NA_ROWS))
        ri = np.clip(kr[None, :] - q_row[:, None] + NA_ROWS - 1, 0, nr - 1)
        onehot_r.append((ri.reshape(-1)[:, None] == np.arange(nr)[None, :]).astype(np.float32))
    onehot_r = np.stack(onehot_r)
    valid = np.stack(valid_r)[:, :, None, :, None] & valid_c[None, None, :, None, :]
    by_col = jnp.einsum('hrc,cx->hrx', rpb, onehot_c, precision=HI)
    full = jnp.einsum('kpr,hrx->khpx', onehot_r, by_col, precision=HI)
    full = full.reshape(3, NA_HEADS, NA_QROWS, krows, GRID_W, GRID_W).transpose(0, 1, 2, 4, 3, 5)
    full = jnp.where(valid[:, None], full * LOG2_E, -1e30)
    return full.reshape(3, NA_HEADS, NA_QBLK, krows * GRID_W).astype(F32)


def _prep_layer(l, emb_ln_g, emb_ln_b, w_in, na_rpb, mla_q_g, w_uq, mla_kv_g, w_ukv, sgu_ln_g, sgu_ln_b,
                sgu_w, sgu_b, w_out, ln1_g, ln1_b, w_up, conv_w, conv_b, w_down, ln2_g, ln2_b):
    wi = w_in[l]
    o_fb = 3 * NA_WIDTH
    o_cq = o_fb + FN_WIDTH
    o_ckv = o_cq + MLA_Q_LORA
    o_kr = o_ckv + MLA_KV_LORA
    o_sg = o_kr + MLA_ROPE
    half = MLA_ROPE // 2
    swap = np.concatenate([np.arange(half, MLA_ROPE), np.arange(half)])
    w_kr = wi[:, o_kr:o_sg]
    w_ckv = jnp.concatenate([wi[:, o_ckv:o_kr], w_kr, w_kr[:, swap],
                             jnp.zeros((D_MODEL, LANES - 2 * MLA_ROPE), F32)], axis=1)

    uq = w_uq[l]
    pad = jnp.zeros((MLA_Q_LORA, MLA_HEADS, MLA_PAD - MLA_NOPE - MLA_ROPE), F32)
    w_a = jnp.concatenate([uq, pad], axis=2).reshape(MLA_Q_LORA, MLA_HEADS * MLA_PAD)
    ukv = w_ukv[l]
    w_kn = jnp.concatenate([ukv[:, :, :MLA_NOPE],
                            jnp.zeros((MLA_KV_LORA, MLA_HEADS, MLA_PAD - MLA_NOPE), F32)],
                           axis=2).reshape(MLA_KV_LORA, MLA_HEADS * MLA_PAD)
    w_vt = ukv[:, :, MLA_NOPE:].reshape(MLA_KV_LORA, MLA_WIDTH).T
    w_s = jnp.concatenate([sgu_w[l][g] for g in range(SGU_HEADS)], axis=1)
    gmat =np.kron(np.eye(SGU_HEADS), np.full((SGU_HEAD_DIM, SGU_HEAD_DIM), 1.0 / SGU_HEAD_DIM)).astype(np.float32)
    s_b = jnp.repeat(sgu_b[l].T, SGU_HEAD_DIM, axis=1)
    r1 = lambda a: a.reshape(1, -1).astype(F32)
    half_val = jnp.concatenate([jnp.ones((1, D_FF), F32), jnp.full((1, D_FF), 0.5, F32)], axis=1)
    return dict(
        eg=r1(emb_ln_g), eb=r1(emb_ln_b),
        w_nat=wi[:, :o_fb].astype(BF16), w_fb=wi[:, o_fb:o_cq].astype(BF16),
        w_cq=wi[:, o_cq:o_ckv].astype(BF16), w_ckv=w_ckv.astype(BF16), w_sg=wi[:, o_sg:].astype(BF16),
        q_g=r1(mla_q_g[l]), w_a=w_a.T.astype(BF16),
        kv_g=r1(mla_kv_g[l]), w_kn=w_kn.astype(BF16), w_vt=w_vt.astype(BF16),
        gmat=jnp.asarray(gmat, BF16),
        sl_g=r1(sgu_ln_g[l]), sl_b=r1(sgu_ln_b[l]), w_s=w_s.astype(BF16), s_b=s_b.astype(F32),
        na_bias=_natten_bias(na_rpb[l]),
        w_out=w_out[l].astype(BF16), ln1_g=r1(ln1_g[l]), ln1_b=r1(ln1_b[l]),
        w_up=w_up[l].astype(BF16), conv_w=conv_w[l].astype(F32) * half_val, conv_b=r1(conv_b[l]) * half_val,
        w_down=w_down[l].astype(BF16), ln2_g=r1(ln2_g[l]), ln2_b=r1(ln2_b[l]),
    )


def _rope_tables(T):
    inv_freq = ROPE_THETA ** (-np.arange(0, MLA_ROPE, 2, dtype=np.float64) / MLA_ROPE)
    ang = np.arange(T, dtype=np.float64)[:, None] * inv_freq[None, :]
    cos, sin = np.cos(ang), np.sin(ang)
    cos2 = np.concatenate([cos, cos], axis=1)
    sin2 = np.concatenate([-sin, sin], axis=1)
    scale = (MLA_NOPE + MLA_ROPE) ** -0.5 * math.log2(math.e)
    zq = np.zeros((T, MLA_PAD - MLA_NOPE - MLA_ROPE))
    ca = np.concatenate([np.full((T, MLA_NOPE), scale), scale * cos2, zq], axis=1)
    cb = np.concatenate([np.zeros((T, MLA_NOPE)), scale * sin2, zq], axis=1)
    tk = np.concatenate([cos2, sin2, np.zeros((T, LANES - 2 * MLA_ROPE))], axis=1)
    f32 = lambda a: jnp.asarray(np.ascontiguousarray(a), F32)
    return dict(ca=f32(ca.T), cb=f32(cb.T), tk=f32(tk))


def _dft_tables(T):
    n1 = 1 << (int(math.log2(T)) // 2)
    n2 = T // n1

    def cs(num, den):
        ang = (num % den).astype(np.float64) * (2.0 * np.pi / den)
        return np.cos(ang), np.sin(ang)

    def table(x):
        return jnp.asarray(x.astype(np.float32)).astype(BF16)

    t2 = np.arange(n2, dtype=np.int64)[:, None, None]
    k1 = np.arange(n1, dtype=np.int64)[None, :, None]
    t1 = np.arange(n1, dtype=np.int64)[None, None, :]
    c, s = cs(k1 * (t1 * n2 + t2), T)
    m1 = table(np.concatenate([c, -s], axis=1))
    k2 = np.arange(n2, dtype=np.int64)
    c2, s2 = cs(k2[:, None] * k2[None, :], n2)
    cs2 = table(np.concatenate([c2, s2], axis=0))
    cg = np.arange(FN_GROUP_DIM, dtype=np.int64)
    cc, sc = cs(cg[:, None] * cg[None, :], FN_GROUP_DIM)
    norm = 1.0 / math.sqrt(T * FN_GROUP_DIM)
    eye = np.eye(FN_GROUPS)
    w = table(np.concatenate([np.kron(eye, cc), np.kron(eye, sc)], axis=0) * norm)
    return dict(n1=n1, n2=n2, m1=m1, cs2=cs2, w=w)


def _trunk(x, layers):
    B, T, _ = x.shape
    tabs = _rope_tables(T)
    ft = _dft_tables(T)
    h = x.reshape(B * T, D_MODEL)
    for l, lp in enumerate(layers):
        outs = _inproj(h, lp, tabs, l == 0, T)
        if l == 0:
            h = outs[0]
            outs = outs[1:]
        nat, fb, q, k, vt, od = outs
        oa = _natten(nat, lp['na_bias'], B, T)
        ob = _fourier(fb, ft, B, T)
        oct = _mla(q, k, vt, B, T)
        h1 = _outproj(oa, ob, oct, od, h, lp)
        h = _ffn(h1, lp, T)
    return h.reshape(B, T, D_MODEL)


def kernel(x_prompt, x_sample, emb_ln_g, emb_ln_b, w_in, na_rpb, mla_q_g, w_uq, mla_kv_g, w_ukv, sgu_ln_g,
           sgu_ln_b, sgu_w, sgu_b, w_out, ln1_g, ln1_b, w_up, conv_w, conv_b, w_down, ln2_g, ln2_b):
    layers = [_prep_layer(l, emb_ln_g, emb_ln_b, w_in, na_rpb, mla_q_g, w_uq, mla_kv_g, w_ukv, sgu_ln_g,
                          sgu_ln_b, sgu_w, sgu_b, w_out, ln1_g, ln1_b, w_up, conv_w, conv_b, w_down,
                          ln2_g, ln2_b) for l in range(DEPTH)]
    return (_trunk(x_prompt, layers), _trunk(x_sample, layers))
```

```python
import functools
import math

import numpy as np
import jax
import jax.numpy as jnp
from jax import lax
from jax.experimental import pallas as pl
from jax.experimental.pallas import tpu as pltpu

F32 = jnp.float32
BF16 = jnp.bfloat16

D_MODEL = 1024
DEPTH = 2
GRID_W = 64
HEAD_DIM = 64
NA_HEADS = 4
NA_ROWS = 8
NA_COLS = 16
NA_WIDTH = NA_HEADS * HEAD_DIM
FN_GROUPS = 4
FN_GROUP_DIM = 64
FN_WIDTH = FN_GROUPS * FN_GROUP_DIM
MLA_HEADS = 4
MLA_Q_LORA = 256
MLA_KV_LORA = 128
MLA_NOPE = 64
MLA_ROPE = 32
MLA_V = 64
MLA_WIDTH = MLA_HEADS * MLA_V
ROPE_THETA = 10000.0
SGU_HEADS = 4
SGU_HEAD_DIM = 64
SGU_WIDTH = SGU_HEADS * SGU_HEAD_DIM
SGU_CHUNK = 128
D_FF = 2816
ALPHA = (2 * DEPTH) ** 0.25
LN_EPS = 1e-5
RMS_EPS = 1e-6

LANES = 128
SUBLANES = 8
BF16_ROWS = 16
VMEM_LIMIT = 56 * 1024 * 1024

TM_IN = 512
TM_OUT = 512
TM_FFN = 512
FF_CHUNK = 256
NA_QROWS = 4
NA_QBLK = NA_QROWS * GRID_W
NA_KBLKS = 3
MLA_PAD = 128
MLA_TQ = 512
MLA_TK = 512
MLA_VT_CHUNK = 512
MLA_ONES = 16
MLA_UNROLL = 16
MLA_SCORE_BUFS = 4
HALO = SUBLANES

LOG2_E = math.log2(math.e)
HI = lax.Precision.HIGHEST
NT_DIMS = (((1,), (1,)), ((), ()))


def _dot(a, b):
    return jnp.dot(a, b, preferred_element_type=F32)


def _dot_nt(a, b):
    return lax.dot_general(a, b, NT_DIMS, preferred_element_type=F32)


def _layer_norm_rows(x, g, b):
    mu = jnp.mean(x, axis=-1, keepdims=True)
    xc = x - mu
    var = jnp.mean(xc * xc, axis=-1, keepdims=True)
    return xc * lax.rsqrt(var + LN_EPS) * g + b


GELU_C = 0.7978845608028654
GELU_K = 0.044715


def _gelu_tanh_x2(x):
    t = jnp.tanh(x * (GELU_C + (GELU_C * GELU_K) * (x * x)))
    return x + x * t


def _gelu_tanh(x):
    return 0.5 * _gelu_tanh_x2(x)


def _split_dot(v, m):
    hi = v.astype(BF16)
    lo = (v - hi.astype(F32)).astype(BF16)
    return _dot(hi, m) + _dot(lo, m)


def _params(*sem):
    return pltpu.CompilerParams(dimension_semantics=sem, vmem_limit_bytes=VMEM_LIMIT)


def _full(shape):
    nd = len(shape)
    return pl.BlockSpec(shape, lambda *_: (0,) * nd)


def _inproj_kernel(apply_ln, x_ref, eg_ref, eb_ref, wnat_ref, wfb_ref, wcq_ref, wckv_ref,
                   wsg_ref, qg_ref, wa_ref, kvg_ref, wkn_ref, wvt_ref, ca_ref, cb_ref,
                   tk_ref, gmat_ref, slg_ref, slb_ref, ws_ref, sb_ref, *out_refs):
    if apply_ln:
        xn_ref, nat_ref, fb_ref, q_ref, k_ref, vt_ref, od_ref = out_refs
    else:
        nat_ref, fb_ref, q_ref, k_ref, vt_ref, od_ref = out_refs
    x = x_ref[...]
    if apply_ln:
        x = _layer_norm_rows(x, eg_ref[...], eb_ref[...])
        xn_ref[...] = x
    xb = x.astype(BF16)
    tm = x.shape[0]

    cq = _dot(xb, wcq_ref[...])
    ckv_kr = _dot(xb, wckv_ref[...])
    sg = _dot(xb, wsg_ref[...])

    nat = _dot(xb, wnat_ref[...])
    nat_ref[:, :NA_WIDTH] = (nat[:, :NA_WIDTH] * (HEAD_DIM ** -0.5 * LOG2_E)).astype(BF16)
    nat_ref[:, NA_WIDTH:] = nat[:, NA_WIDTH:].astype(BF16)

    fb_ref[...] = _dot(xb, wfb_ref[...]).astype(BF16)

    cqn = (cq * lax.rsqrt(jnp.mean(cq * cq, axis=-1, keepdims=True) + RMS_EPS) * qg_ref[...]).astype(BF16)
    qa = _dot_nt(wa_ref[...], cqn)
    ca = ca_ref[...]
    cb = cb_ref[MLA_NOPE:MLA_NOPE + MLA_ROPE]
    half = MLA_ROPE // 2
    for h in range(MLA_HEADS):
        base = h * MLA_PAD
        qh = qa[base:base + MLA_PAD] * ca
        lo = qa[base + MLA_NOPE:base + MLA_NOPE + half]
        hi = qa[base + MLA_NOPE + half:base + MLA_NOPE + MLA_ROPE]
        rope = qh[MLA_NOPE:MLA_NOPE + MLA_ROPE] + jnp.concatenate([hi, lo], axis=0) * cb
        qh = jnp.concatenate([qh[:MLA_NOPE], rope, qh[MLA_NOPE + MLA_ROPE:]], axis=0).astype(BF16)
        for c in range(tm // MLA_TQ):
            q_ref[c, base:base + MLA_PAD, :] = qh[:, c * MLA_TQ:(c + 1) * MLA_TQ]

    ckv = ckv_kr[:, :MLA_KV_LORA]
    ckvn = (ckv * lax.rsqrt(jnp.mean(ckv * ckv, axis=-1, keepdims=True) + RMS_EPS) * kvg_ref[...]).astype(BF16)
    kr = ckv_kr[:, MLA_KV_LORA:] * tk_ref[...]
    kr = pltpu.roll(kr, MLA_NOPE, axis=1) + pltpu.roll(kr, MLA_NOPE - MLA_ROPE, axis=1)
    lane = lax.broadcasted_iota(jnp.int32, (1, MLA_PAD), 1)
    kr = jnp.where((lane >= MLA_NOPE) & (lane < MLA_NOPE + MLA_ROPE), kr, 0.0)
    kn = _dot(ckvn, wkn_ref[...])
    for h in range(MLA_HEADS):
        sl = slice(h * MLA_PAD, (h + 1) * MLA_PAD)
        k_ref[:, sl] = (kn[:, sl] + kr).astype(BF16)
    vt = _dot_nt(wvt_ref[...], ckvn)
    tkc = vt_ref.shape[2]
    for c in range(tm // tkc):
        vt_ref[c] = vt[:, c * tkc:(c + 1) * tkc].astype(BF16)

    sg = _gelu_tanh(sg)
    u = sg[:, :SGU_WIDTH]
    v = sg[:, SGU_WIDTH:]
    gmat = gmat_ref[...]
    mu = _split_dot(v, gmat)
    vc = v - mu
    var = _split_dot(vc * vc, gmat)
    vn = (vc * lax.rsqrt(var + LN_EPS) * slg_ref[...] + slb_ref[...]).astype(BF16)
    head = lax.broadcasted_iota(jnp.int32, (1, SGU_WIDTH), 1) // SGU_HEAD_DIM
    zero = jnp.zeros((), BF16)
    for c in range(tm // SGU_CHUNK):
        rows = slice(c * SGU_CHUNK, (c + 1) * SGU_CHUNK)
        vch = vn[rows]
        stacked = jnp.concatenate([jnp.where(head == g, vch, zero) for g in range(SGU_HEADS)], axis=0)
        mixed = sb_ref[...] + _dot(ws_ref[...], stacked)
        od_ref[rows, :] = (u[rows] * mixed).astype(BF16)


def _inproj(x, lp, tabs, apply_ln, T):
    n = x.shape[0]
    tm = TM_IN
    assert T % tm == 0 and tm % MLA_TQ == 0 and tm % MLA_VT_CHUNK == 0 and tm % SGU_CHUNK == 0
    nt = T // tm
    row = lambda i: (i, 0)
    tab = lambda i: (i % nt, 0)
    weights = [lp['eg'], lp['eb'], lp['w_nat'], lp['w_fb'], lp['w_cq'], lp['w_ckv'], lp['w_sg'],
               lp['q_g'], lp['w_a'], lp['kv_g'], lp['w_kn'], lp['w_vt']]
    tables = [tabs['ca'], tabs['cb'], tabs['tk']]
    tail = [lp['gmat'], lp['sl_g'], lp['sl_b'], lp['w_s'], lp['s_b']]
    tab_t = lambda i: (0, i % nt)
    in_specs = ([pl.BlockSpec((tm, D_MODEL), row)] + [_full(w.shape) for w in weights]
                + [pl.BlockSpec((MLA_PAD, tm), tab_t), pl.BlockSpec((MLA_PAD, tm), tab_t),
                   pl.BlockSpec((tm, LANES), tab)] + [_full(w.shape) for w in tail])
    out_shape = [jax.ShapeDtypeStruct((n, 3 * NA_WIDTH), BF16),
                 jax.ShapeDtypeStruct((n, FN_WIDTH), BF16),
                 jax.ShapeDtypeStruct((n // MLA_TQ, MLA_HEADS * MLA_PAD, MLA_TQ), BF16),
                 jax.ShapeDtypeStruct((n, MLA_HEADS * MLA_PAD), BF16),
                 jax.ShapeDtypeStruct((n // MLA_VT_CHUNK, MLA_WIDTH, MLA_VT_CHUNK), BF16),
                 jax.ShapeDtypeStruct((n, SGU_WIDTH), BF16)]
    out_specs = [pl.BlockSpec((tm, 3 * NA_WIDTH), row),
                 pl.BlockSpec((tm, FN_WIDTH), row),
                 pl.BlockSpec((tm // MLA_TQ, MLA_HEADS * MLA_PAD, MLA_TQ), lambda i: (i, 0, 0)),
                 pl.BlockSpec((tm, MLA_HEADS * MLA_PAD), row),
                 pl.BlockSpec((tm // MLA_VT_CHUNK, MLA_WIDTH, MLA_VT_CHUNK), lambda i: (i, 0, 0)),
                 pl.BlockSpec((tm, SGU_WIDTH), row)]
    if apply_ln:
        out_shape = [jax.ShapeDtypeStruct((n, D_MODEL), F32)] + out_shape
        out_specs = [pl.BlockSpec((tm, D_MODEL), row)] + out_specs
    return pl.pallas_call(
        functools.partial(_inproj_kernel, apply_ln),
        grid=(n // tm,),
        in_specs=in_specs,
        out_specs=out_specs,
        out_shape=out_shape,
        compiler_params=_params("parallel"),
        name="inproj_ln" if apply_ln else "inproj",
    )(x, *weights, *tables, *tail)


def _natten_kernel(q_ref, k0_ref, k1_ref, k2_ref, v0_ref, v1_ref, v2_ref, bias_ref, o_ref, s_ref, p_ref):
    q = q_ref[...]
    k_refs = (k0_ref, k1_ref, k2_ref)
    v_refs = (v0_ref, v1_ref, v2_ref)
    head = lax.broadcasted_iota(jnp.int32, (1, NA_WIDTH), 1) // HEAD_DIM
    zero = jnp.zeros((), BF16)

    def scores(h, slot):
        qh = jnp.where(head == h, q, zero)
        for d in range(NA_KBLKS):
            cols = slice(d * NA_QBLK, (d + 1) * NA_QBLK)
            s_ref[slot, :, cols] = _dot_nt(qh, k_refs[d][...]) + bias_ref[h, :, cols]

    scores(0, 0)
    acc = jnp.zeros((NA_QBLK, NA_WIDTH), F32)
    for h in range(NA_HEADS):
        if h + 1 < NA_HEADS:
            scores(h + 1, (h + 1) % 2)
        s = s_ref[h % 2]
        p = jnp.exp2(s - jnp.max(s, axis=-1, keepdims=True))
        inv = 1.0 / jnp.sum(p, axis=-1, keepdims=True)
        p_ref[...] = p.astype(BF16)
        pv = _dot(p_ref[:, 0:NA_QBLK], v_refs[0][...])
        for d in range(1, NA_KBLKS):
            pv = pv + _dot(p_ref[:, d * NA_QBLK:(d + 1) * NA_QBLK], v_refs[d][...])
        acc = jnp.where(head == h, pv * inv, acc)
    o_ref[...] = acc.astype(BF16)


def _natten(nat, bias, B, T):
    n = nat.shape[0]
    nblk = T // NA_QBLK

    def kv_map(d, col):
        def f(b, j):
            base = jnp.clip(j - 1, 0, nblk - NA_KBLKS)
            return (b * nblk + base + d, col)
        return f

    def bias_map(b, j):
        typ = jnp.where(j == 0, 0, jnp.where(j == nblk - 1, 2, 1))
        return (typ, 0, 0, 0)

    blk = (NA_QBLK, NA_WIDTH)
    in_specs = ([pl.BlockSpec(blk, lambda b, j: (b * nblk + j, 0))]
                + [pl.BlockSpec(blk, kv_map(d, 1)) for d in range(NA_KBLKS)]
                + [pl.BlockSpec(blk, kv_map(d, 2)) for d in range(NA_KBLKS)]
                + [pl.BlockSpec((None, NA_HEADS, NA_QBLK, NA_KBLKS * NA_QBLK), bias_map)])
    return pl.pallas_call(
        _natten_kernel,
        grid=(B, nblk),
        in_specs=in_specs,
        out_specs=pl.BlockSpec(blk, lambda b, j: (b * nblk + j, 0)),
        out_shape=jax.ShapeDtypeStruct((n, NA_WIDTH), BF16),
        scratch_shapes=[pltpu.VMEM((2, NA_QBLK, NA_KBLKS * NA_QBLK), F32),
                        pltpu.VMEM((NA_QBLK, NA_KBLKS * NA_QBLK), BF16)],
        compiler_params=_params("parallel", "arbitrary"),
        name="natten",
    )(nat, nat, nat, nat, nat, nat, nat, bias)


def _fft1_kernel(x_ref, m_ref, a_ref):
    kk = m_ref.shape[0]
    n1 = x_ref.shape[0]
    xt = jnp.swapaxes(x_ref[...], 0, 1)
    for i in range(kk):
        res = _dot(m_ref[i], xt[i])
        res = jnp.concatenate([res[:n1], res[n1:]], axis=1).astype(BF16)
        a_ref[:, i] = res.reshape(n1 // kk, kk, 2 * FN_WIDTH)


def _fft2_kernel(a_ref, cs_ref, w_ref, y_ref):
    n2, kk, _ = a_ref.shape
    cs = cs_ref[...]
    at = jnp.swapaxes(a_ref[...], 0, 1)
    g = []
    for i in range(kk):
        r = _dot(cs, at[i])
        g.append(jnp.concatenate([r[:n2, :FN_WIDTH] + r[n2:, FN_WIDTH:],
                                  r[:n2, FN_WIDTH:] - r[n2:, :FN_WIDTH]], axis=1).astype(BF16))
    g = jnp.concatenate(g, axis=0)
    y = _dot(g, w_ref[...]).astype(BF16)
    y_ref[...] = jnp.swapaxes(y.reshape(kk, n2, FN_WIDTH), 0, 1)


def _fourier(fb, ft, B, T):
    n1, n2 = ft['n1'], ft['n2']
    kk = BF16_ROWS
    a = pl.pallas_call(
        _fft1_kernel,
        grid=(B, n2 // kk),
        in_specs=[pl.BlockSpec((None, n1, kk, FN_WIDTH), lambda b, j: (b, 0, j, 0)),
                  pl.BlockSpec((kk, 2 * n1, n1), lambda b, j: (j, 0, 0))],
        out_specs=pl.BlockSpec((None, n1 // kk, kk, kk, 2 * FN_WIDTH), lambda b, j: (b, 0, j, 0, 0)),
        out_shape=jax.ShapeDtypeStruct((B, n1 // kk, n2, kk, 2 * FN_WIDTH), BF16),
        compiler_params=_params("parallel", "arbitrary"),
        name="fft_stage1",
    )(fb.reshape(B, n1, n2, FN_WIDTH), ft['m1'])
    y = pl.pallas_call(
        _fft2_kernel,
        grid=(B, n1 // kk),
        in_specs=[pl.BlockSpec((None, None, n2, kk, 2 * FN_WIDTH), lambda b, j: (b, j, 0, 0, 0)),
                  _full(ft['cs2'].shape), _full(ft['w'].shape)],
        out_specs=pl.BlockSpec((None, n2, kk, FN_WIDTH), lambda b, j: (b, 0, j, 0)),
        out_shape=jax.ShapeDtypeStruct((B, n2, n1, FN_WIDTH), BF16),
        compiler_params=_params("parallel", "arbitrary"),
        name="fft_stage2",
    )(a, ft['cs2'], ft['w'])
    return y.reshape(B * T, FN_WIDTH)


def _mla_kernel(q_ref, k_ref, vt_ref, o_ref, s_ref):
    tk = s_ref.shape[1]
    per = tk // MLA_VT_CHUNK
    nk = vt_ref.shape[0] // per
    nq, _, tq = q_ref.shape
    unroll = min(MLA_UNROLL, nk)
    ones = jnp.ones((MLA_ONES, tk), BF16)

    def scores(i, j):
        start = j * tk if isinstance(j, int) else pl.multiple_of(j * tk, tk)
        return _dot(k_ref[pl.ds(start, tk), :], q_ref[i])

    def update(carry, j, slot):
        m, acc = carry
        st = s_ref[slot]
        m_new = jnp.maximum(m, jnp.max(st, axis=0, keepdims=True))
        alpha = jnp.exp2(m - m_new)
        p = jnp.exp2(st - m_new).astype(BF16)
        vt = jnp.concatenate([vt_ref[j * per + u] for u in range(per)], axis=1)
        vte = jnp.concatenate([vt, ones], axis=0)
        return m_new, alpha * acc + _dot(vte, p)

    nbuf = s_ref.shape[0]
    ahead = nbuf // 2
    last = nq * nk - 1
    for f in range(ahead):
        s_ref[f] = scores(f // nk, f % nk)
    m0 = jnp.full((1, tq), -1e30, F32)
    acc0 = jnp.zeros((MLA_V + MLA_ONES, tq), F32)

    def body(t, carry):
        i = (t * unroll) // nk
        j0 = (t * unroll) % nk
        fresh = j0 == 0
        carry = (jnp.where(fresh, m0, carry[0]), jnp.where(fresh, acc0, carry[1]))
        for c in range(unroll):
            if c + ahead < unroll:
                s_ref[(c + ahead) % nbuf] = scores(i, j0 + c + ahead)
            else:
                nxt = jnp.minimum(t * unroll + c + ahead, last)
                s_ref[(c + ahead) % nbuf] = scores(nxt // nk, nxt % nk)
            carry = update(carry, j0 + c, c % nbuf)

        @pl.when(j0 + unroll == nk)
        def _():
            acc = carry[1]
            o_ref[i] = acc[:MLA_V] * (1.0 / acc[MLA_V:MLA_V + 1])

        return carry

    lax.fori_loop(0, nq * nk // unroll, body, (m0, acc0))


def _mla(q, k, vt, B, T):
    nq = T // MLA_TQ
    nk = T // MLA_TK
    unroll = min(MLA_UNROLL, nk)
    nbuf = MLA_SCORE_BUFS if unroll % MLA_SCORE_BUFS == 0 else 2
    assert nk % unroll == 0 and unroll % nbuf == 0
    return pl.pallas_call(
        _mla_kernel,
        grid=(B, MLA_HEADS),
        in_specs=[pl.BlockSpec((nq, MLA_PAD, MLA_TQ), lambda b, h: (b, h, 0)),
                  pl.BlockSpec((T, MLA_PAD), lambda b, h: (b, h)),
                  pl.BlockSpec((T // MLA_VT_CHUNK, MLA_V, MLA_VT_CHUNK), lambda b, h: (b, h, 0))],
        out_specs=pl.BlockSpec((nq, MLA_V, MLA_TQ), lambda b, h: (b, h, 0)),
        out_shape=jax.ShapeDtypeStruct((B * nq, MLA_WIDTH, MLA_TQ), F32),
        scratch_shapes=[pltpu.VMEM((nbuf, MLA_TK, MLA_TQ), F32)],
        compiler_params=_params("parallel", "arbitrary"),
        name="mla",
    )(q, k, vt)


def _outproj_kernel(oa_ref, ob_ref, oct_ref, od_ref, x_ref, w_ref, g_ref, b_ref, o_ref):
    w = NA_WIDTH
    oc = oct_ref[...].T.astype(BF16)
    y = (_dot(oa_ref[...], w_ref[0:w]) + _dot(ob_ref[...], w_ref[w:2 * w])
         + _dot(oc, w_ref[2 * w:3 * w]) + _dot(od_ref[...], w_ref[3 * w:4 * w]))
    o_ref[...] = _layer_norm_rows(ALPHA * x_ref[...] + y, g_ref[...], b_ref[...])


def _outproj(oa, ob, oct, od, x, lp):
    n = x.shape[0]
    tm = TM_OUT
    assert tm == MLA_TQ
    row = lambda i: (i, 0)
    return pl.pallas_call(
        _outproj_kernel,
        grid=(n // tm,),
        in_specs=[pl.BlockSpec((tm, NA_WIDTH), row), pl.BlockSpec((tm, FN_WIDTH), row),
                  pl.BlockSpec((None, MLA_WIDTH, tm), lambda i: (i, 0, 0)), pl.BlockSpec((tm, SGU_WIDTH), row),
                  pl.BlockSpec((tm, D_MODEL), row), _full(lp['w_out'].shape),
                  _full(lp['ln1_g'].shape), _full(lp['ln1_b'].shape)],
        out_specs=pl.BlockSpec((tm, D_MODEL), row),
        out_shape=jax.ShapeDtypeStruct((n, D_MODEL), F32),
        compiler_params=_params("parallel"),
        name="outproj",
    )(oa, ob, oct, od, x, lp['w_out'], lp['ln1_g'], lp['ln1_b'])


def _ffn_kernel(nt, x_ref, xp_ref, xn_ref, wup_ref, cw_ref, cb_ref, wd_ref, g_ref, b_ref, o_ref,
                xe_ref, act_ref):
    i = pl.program_id(0)
    tm = x_ref.shape[0]
    cf = FF_CHUNK
    first = (i % nt) == 0
    last = (i % nt) == nt - 1
    xe_ref[0:HALO] = jnp.where(first, 0.0, xp_ref[...]).astype(BF16)
    xe_ref[HALO:HALO + tm] = x_ref[...].astype(BF16)
    xe_ref[HALO + tm:] = jnp.where(last, 0.0, xn_ref[...]).astype(BF16)
    xe = xe_ref[...]

    rows = slice(HALO, HALO + tm)

    def conv(h, cols):
        prev = pltpu.roll(h, 1, axis=0)
        nxt = pltpu.roll(h, h.shape[0] - 1, axis=0)
        return (prev[rows] * cw_ref[0:1, cols] + h[rows] * cw_ref[1:2, cols]
                + nxt[rows] * cw_ref[2:3, cols] + cb_ref[:, cols])

    for c in range(D_FF // cf):
        gate = slice(c * cf, (c + 1) * cf)
        val = slice(D_FF + c * cf, D_FF + (c + 1) * cf)
        act = _gelu_tanh_x2(conv(_dot(xe, wup_ref[:, gate]), gate)) * conv(_dot(xe, wup_ref[:, val]), val)
        act_ref[:, gate] = act.astype(BF16)

    y = _dot(act_ref[...], wd_ref[...])
    o_ref[...] = _layer_norm_rows(ALPHA * x_ref[...] + y, g_ref[...], b_ref[...])


def _resident(shape):
    nd = len(shape)
    return pl.BlockSpec(shape, lambda *_: (0,) * nd, pipeline_mode=pl.Buffered(1))


def _ffn(x, lp, T):
    n = x.shape[0]
    tm = TM_FFN
    nt = T // tm
    assert T % tm == 0 and D_FF % FF_CHUNK == 0
    hb = tm // HALO
    nhb = n // HALO
    row = lambda i: (i, 0)
    return pl.pallas_call(
        functools.partial(_ffn_kernel, nt),
        grid=(n // tm,),
        in_specs=[pl.BlockSpec((tm, D_MODEL), row),
                  pl.BlockSpec((HALO, D_MODEL), lambda i: (jnp.maximum(i * hb - 1, 0), 0)),
                  pl.BlockSpec((HALO, D_MODEL), lambda i: (jnp.minimum((i + 1) * hb, nhb - 1), 0)),
                  _resident(lp['w_up'].shape), _resident(lp['conv_w'].shape), _resident(lp['conv_b'].shape),
                  _resident(lp['w_down'].shape), _resident(lp['ln2_g'].shape), _resident(lp['ln2_b'].shape)],
        out_specs=pl.BlockSpec((tm, D_MODEL), row),
        out_shape=jax.ShapeDtypeStruct((n, D_MODEL), F32),
        scratch_shapes=[pltpu.VMEM((tm + 2 * HALO, D_MODEL), BF16),
                        pltpu.VMEM((tm, D_FF), BF16)],
        compiler_params=_params("parallel"),
        name="conv_ffn",
    )(x, x, x, lp['w_up'], lp['conv_w'], lp['conv_b'], lp['w_down'], lp['ln2_g'], lp['ln2_b'])


def _natten_bias(rpb):
    nr, nc = 2 * NA_ROWS - 1, 2 * NA_COLS - 1
    krows = NA_KBLKS * NA_QROWS
    col = np.arange(GRID_W)
    ci = np.clip(col[None, :] - col[:, None] + NA_COLS - 1, 0, nc - 1)
    onehot_c = (ci.reshape(-1)[None, :] == np.arange(nc)[:, None]).astype(np.float32)
    cs = np.clip(col - NA_COLS // 2, 0, GRID_W - NA_COLS)
    dc = col[None, :] - cs[:, None]
    valid_c = (dc >= 0) & (dc < NA_COLS)
    qa = np.arange(NA_QROWS)
    kr = np.arange(krows)
    onehot_r, valid_r = [], []
    for kind in range(3):
        q_row = (0, NA_QROWS, 2 * NA_QROWS)[kind] + qa
        r_start = (np.zeros_like(qa), qa, np.full_like(qa, NA_QROWS))[kind]
        dr = kr[None, :] - r_start[:, None]
        valid_r.append((dr >= 0) & (dr < NA_ROWS))
        ri = np.clip(kr[None, :] - q_row[:, None] + NA_ROWS - 1, 0, nr - 1)
        onehot_r.append((ri.reshape(-1)[:, None] == np.arange(nr)[None, :]).astype(np.float32))
    onehot_r = np.stack(onehot_r)
    valid = np.stack(valid_r)[:, :, None, :, None] & valid_c[None, None, :, None, :]
    by_col = jnp.einsum('hrc,cx->hrx', rpb, onehot_c, precision=HI)
    full = jnp.einsum('kpr,hrx->khpx', onehot_r, by_col, precision=HI)
    full = full.reshape(3, NA_HEADS, NA_QROWS, krows, GRID_W, GRID_W).transpose(0, 1, 2, 4, 3, 5)
    full = jnp.where(valid[:, None], full * LOG2_E, -1e30)
    return full.reshape(3, NA_HEADS, NA_QBLK, krows * GRID_W).astype(F32)


def _prep_layer(l, emb_ln_g, emb_ln_b, w_in, na_rpb, mla_q_g, w_uq, mla_kv_g, w_ukv, sgu_ln_g, sgu_ln_b,
                sgu_w, sgu_b, w_out, ln1_g, ln1_b, w_up, conv_w, conv_b, w_down, ln2_g, ln2_b):
    wi = w_in[l]
    o_fb = 3 * NA_WIDTH
    o_cq = o_fb + FN_WIDTH
    o_ckv = o_cq + MLA_Q_LORA
    o_kr = o_ckv + MLA_KV_LORA
    o_sg = o_kr + MLA_ROPE
    half = MLA_ROPE // 2
    swap = np.concatenate([np.arange(half, MLA_ROPE), np.arange(half)])
    w_kr = wi[:, o_kr:o_sg]
    w_ckv = jnp.concatenate([wi[:, o_ckv:o_kr], w_kr, w_kr[:, swap],
                             jnp.zeros((D_MODEL, LANES - 2 * MLA_ROPE), F32)], axis=1)

    uq = w_uq[l]
    pad = jnp.zeros((MLA_Q_LORA, MLA_HEADS, MLA_PAD - MLA_NOPE - MLA_ROPE), F32)
    w_a = jnp.concatenate([uq, pad], axis=2).reshape(MLA_Q_LORA, MLA_HEADS * MLA_PAD)
    ukv = w_ukv[l]
    w_kn = jnp.concatenate([ukv[:, :, :MLA_NOPE],
                            jnp.zeros((MLA_KV_LORA, MLA_HEADS, MLA_PAD - MLA_NOPE), F32)],
                           axis=2).reshape(MLA_KV_LORA, MLA_HEADS * MLA_PAD)
    w_vt = ukv[:, :, MLA_NOPE:].reshape(MLA_KV_LORA, MLA_WIDTH).T
    w_s = jnp.concatenate([sgu_w[l][g] for g in range(SGU_HEADS)], axis=1)
    gmat =np.kron(np.eye(SGU_HEADS), np.full((SGU_HEAD_DIM, SGU_HEAD_DIM), 1.0 / SGU_HEAD_DIM)).astype(np.float32)
    s_b = jnp.repeat(sgu_b[l].T, SGU_HEAD_DIM, axis=1)
    r1 = lambda a: a.reshape(1, -1).astype(F32)
    half_val = jnp.concatenate([jnp.ones((1, D_FF), F32), jnp.full((1, D_FF), 0.5, F32)], axis=1)
    return dict(
        eg=r1(emb_ln_g), eb=r1(emb_ln_b),
        w_nat=wi[:, :o_fb].astype(BF16), w_fb=wi[:, o_fb:o_cq].astype(BF16),
        w_cq=wi[:, o_cq:o_ckv].astype(BF16), w_ckv=w_ckv.astype(BF16), w_sg=wi[:, o_sg:].astype(BF16),
        q_g=r1(mla_q_g[l]), w_a=w_a.T.astype(BF16),
        kv_g=r1(mla_kv_g[l]), w_kn=w_kn.astype(BF16), w_vt=w_vt.astype(BF16),
        gmat=jnp.asarray(gmat, BF16),
        sl_g=r1(sgu_ln_g[l]), sl_b=r1(sgu_ln_b[l]), w_s=w_s.astype(BF16), s_b=s_b.astype(F32),
        na_bias=_natten_bias(na_rpb[l]),
        w_out=w_out[l].astype(BF16), ln1_g=r1(ln1_g[l]), ln1_b=r1(ln1_b[l]),
        w_up=w_up[l].astype(BF16), conv_w=conv_w[l].astype(F32) * half_val, conv_b=r1(conv_b[l]) * half_val,
        w_down=w_down[l].astype(BF16), ln2_g=r1(ln2_g[l]), ln2_b=r1(ln2_b[l]),
    )


def _rope_tables(T):
    inv_freq = ROPE_THETA ** (-np.arange(0, MLA_ROPE, 2, dtype=np.float64) / MLA_ROPE)
    ang = np.arange(T, dtype=np.float64)[:, None] * inv_freq[None, :]
    cos, sin = np.cos(ang), np.sin(ang)
    cos2 = np.concatenate([cos, cos], axis=1)
    sin2 = np.concatenate([-sin, sin], axis=1)
    scale = (MLA_NOPE + MLA_ROPE) ** -0.5 * math.log2(math.e)
    zq = np.zeros((T, MLA_PAD - MLA_NOPE - MLA_ROPE))
    ca = np.concatenate([np.full((T, MLA_NOPE), scale), scale * cos2, zq], axis=1)
    cb = np.concatenate([np.zeros((T, MLA_NOPE)), scale * sin2, zq], axis=1)
    tk = np.concatenate([cos2, sin2, np.zeros((T, LANES - 2 * MLA_ROPE))], axis=1)
    f32 = lambda a: jnp.asarray(np.ascontiguousarray(a), F32)
    return dict(ca=f32(ca.T), cb=f32(cb.T), tk=f32(tk))


def _dft_tables(T):
    n1 = 1 << (int(math.log2(T)) // 2)
    n2 = T // n1

    def cs(num, den):
        ang = (num % den).astype(np.float64) * (2.0 * np.pi / den)
        return np.cos(ang), np.sin(ang)

    def table(x):
        return jnp.asarray(x.astype(np.float32)).astype(BF16)

    t2 = np.arange(n2, dtype=np.int64)[:, None, None]
    k1 = np.arange(n1, dtype=np.int64)[None, :, None]
    t1 = np.arange(n1, dtype=np.int64)[None, None, :]
    c, s = cs(k1 * (t1 * n2 + t2), T)
    m1 = table(np.concatenate([c, -s], axis=1))
    k2 = np.arange(n2, dtype=np.int64)
    c2, s2 = cs(k2[:, None] * k2[None, :], n2)
    cs2 = table(np.concatenate([c2, s2], axis=0))
    cg = np.arange(FN_GROUP_DIM, dtype=np.int64)
    cc, sc = cs(cg[:, None] * cg[None, :], FN_GROUP_DIM)
    norm = 1.0 / math.sqrt(T * FN_GROUP_DIM)
    eye = np.eye(FN_GROUPS)
    w = table(np.concatenate([np.kron(eye, cc), np.kron(eye, sc)], axis=0) * norm)
    return dict(n1=n1, n2=n2, m1=m1, cs2=cs2, w=w)


def _trunk(x, layers):
    B, T, _ = x.shape
    tabs = _rope_tables(T)
    ft = _dft_tables(T)
    h = x.reshape(B * T, D_MODEL)
    for l, lp in enumerate(layers):
        outs = _inproj(h, lp, tabs, l == 0, T)
        if l == 0:
            h = outs[0]
            outs = outs[1:]
        nat, fb, q, k, vt, od = outs
        oa = _natten(nat, lp['na_bias'], B, T)
        ob = _fourier(fb, ft, B, T)
        oct = _mla(q, k, vt, B, T)
        h1 = _outproj(oa, ob, oct, od, h, lp)
        h = _ffn(h1, lp, T)
    return h.reshape(B, T, D_MODEL)


def kernel(x_prompt, x_sample, emb_ln_g, emb_ln_b, w_in, na_rpb, mla_q_g, w_uq, mla_kv_g, w_ukv, sgu_ln_g,
           sgu_ln_b, sgu_w, sgu_b, w_out, ln1_g, ln1_b, w_up, conv_w, conv_b, w_down, ln2_g, ln2_b):
    layers = [_prep_layer(l, emb_ln_g, emb_ln_b, w_in, na_rpb, mla_q_g, w_uq, mla_kv_g, w_ukv, sgu_ln_g,
                          sgu_ln_b, sgu_w, sgu_b, w_out, ln1_g, ln1_b, w_up, conv_w, conv_b, w_down,
                          ln2_g, ln2_b) for l in range(DEPTH)]
    return (_trunk(x_prompt, layers), _trunk(x_sample, layers))
```

```python
import functools
import math

import numpy as np
import jax
import jax.numpy as jnp
from jax import lax
from jax.experimental import pallas as pl
from jax.experimental.pallas import tpu as pltpu

F32 = jnp.float32
BF16 = jnp.bfloat16

D_MODEL = 1024
DEPTH = 2
GRID_W = 64
HEAD_DIM = 64
NA_HEADS = 4
NA_ROWS = 8
NA_COLS = 16
NA_WIDTH = NA_HEADS * HEAD_DIM
FN_GROUPS = 4
FN_GROUP_DIM = 64
FN_WIDTH = FN_GROUPS * FN_GROUP_DIM
MLA_HEADS = 4
MLA_Q_LORA = 256
MLA_KV_LORA = 128
MLA_NOPE = 64
MLA_ROPE = 32
MLA_V = 64
MLA_WIDTH = MLA_HEADS * MLA_V
ROPE_THETA = 10000.0
SGU_HEADS = 4
SGU_HEAD_DIM = 64
SGU_WIDTH = SGU_HEADS * SGU_HEAD_DIM
SGU_CHUNK = 128
D_FF = 2816
ALPHA = (2 * DEPTH) ** 0.25
LN_EPS = 1e-5
RMS_EPS = 1e-6

LANES = 128
SUBLANES = 8
BF16_ROWS = 16
VMEM_LIMIT = 56 * 1024 * 1024

TM_IN = 512
TM_OUT = 512
TM_FFN = 512
FF_CHUNK = 256
NA_QROWS = 4
NA_QBLK = NA_QROWS * GRID_W
NA_KBLKS = 3
MLA_PAD = 128
MLA_TQ = 512
MLA_TK = 512
MLA_VT_CHUNK = 512
MLA_ONES = 16
MLA_UNROLL = 16
MLA_SCORE_BUFS = 4
HALO = SUBLANES

LOG2_E = math.log2(math.e)
HI = lax.Precision.HIGHEST
NT_DIMS = (((1,), (1,)), ((), ()))


def _dot(a, b):
    return jnp.dot(a, b, preferred_element_type=F32)


def _dot_nt(a, b):
    return lax.dot_general(a, b, NT_DIMS, preferred_element_type=F32)


def _layer_norm_rows(x, g, b):
    mu = jnp.mean(x, axis=-1, keepdims=True)
    xc = x - mu
    var = jnp.mean(xc * xc, axis=-1, keepdims=True)
    return xc * lax.rsqrt(var + LN_EPS) * g + b


GELU_C = 0.7978845608028654
GELU_K = 0.044715


def _gelu_tanh_x2(x):
    t = jnp.tanh(x * (GELU_C + (GELU_C * GELU_K) * (x * x)))
    return x + x * t


def _gelu_tanh(x):
    return 0.5 * _gelu_tanh_x2(x)


def _split_dot(v, m):
    hi = v.astype(BF16)
    lo = (v - hi.astype(F32)).astype(BF16)
    return _dot(hi, m) + _dot(lo, m)


def _params(*sem):
    return pltpu.CompilerParams(dimension_semantics=sem, vmem_limit_bytes=VMEM_LIMIT)


def _full(shape):
    nd = len(shape)
    return pl.BlockSpec(shape, lambda *_: (0,) * nd)


def _inproj_kernel(apply_ln, x_ref, eg_ref, eb_ref, wnat_ref, wfb_ref, wcq_ref, wckv_ref,
                   wsg_ref, qg_ref, wa_ref, kvg_ref, wkn_ref, wvt_ref, ca_ref, cb_ref,
                   tk_ref, gmat_ref, slg_ref, slb_ref, ws_ref, sb_ref, *out_refs):
    if apply_ln:
        xn_ref, nat_ref, fb_ref, q_ref, k_ref, vt_ref, od_ref = out_refs
    else:
        nat_ref, fb_ref, q_ref, k_ref, vt_ref, od_ref = out_refs
    x = x_ref[...]
    if apply_ln:
        x = _layer_norm_rows(x, eg_ref[...], eb_ref[...])
        xn_ref[...] = x
    xb = x.astype(BF16)
    tm = x.shape[0]

    cq = _dot(xb, wcq_ref[...])
    ckv_kr = _dot(xb, wckv_ref[...])
    sg = _dot(xb, wsg_ref[...])

    nat = _dot(xb, wnat_ref[...])
    nat_ref[:, :NA_WIDTH] = (nat[:, :NA_WIDTH] * (HEAD_DIM ** -0.5 * LOG2_E)).astype(BF16)
    nat_ref[:, NA_WIDTH:] = nat[:, NA_WIDTH:].astype(BF16)

    fb_ref[...] = _dot(xb, wfb_ref[...]).astype(BF16)

    cqn = (cq * lax.rsqrt(jnp.mean(cq * cq, axis=-1, keepdims=True) + RMS_EPS) * qg_ref[...]).astype(BF16)
    qa = _dot_nt(wa_ref[...], cqn)
    ca = ca_ref[...]
    cb = cb_ref[MLA_NOPE:MLA_NOPE + MLA_ROPE]
    half = MLA_ROPE // 2
    for h in range(MLA_HEADS):
        base = h * MLA_PAD
        qh = qa[base:base + MLA_PAD] * ca
        lo = qa[base + MLA_NOPE:base + MLA_NOPE + half]
        hi = qa[base + MLA_NOPE + half:base + MLA_NOPE + MLA_ROPE]
        rope = qh[MLA_NOPE:MLA_NOPE + MLA_ROPE] + jnp.concatenate([hi, lo], axis=0) * cb
        qh = jnp.concatenate([qh[:MLA_NOPE], rope, qh[MLA_NOPE + MLA_ROPE:]], axis=0).astype(BF16)
        for c in range(tm // MLA_TQ):
            q_ref[c, base:base + MLA_PAD, :] = qh[:, c * MLA_TQ:(c + 1) * MLA_TQ]

    ckv = ckv_kr[:, :MLA_KV_LORA]
    ckvn = (ckv * lax.rsqrt(jnp.mean(ckv * ckv, axis=-1, keepdims=True) + RMS_EPS) * kvg_ref[...]).astype(BF16)
    kr = ckv_kr[:, MLA_KV_LORA:] * tk_ref[...]
    kr = pltpu.roll(kr, MLA_NOPE, axis=1) + pltpu.roll(kr, MLA_NOPE - MLA_ROPE, axis=1)
    lane = lax.broadcasted_iota(jnp.int32, (1, MLA_PAD), 1)
    kr = jnp.where((lane >= MLA_NOPE) & (lane < MLA_NOPE + MLA_ROPE), kr, 0.0)
    kn = _dot(ckvn, wkn_ref[...])
    for h in range(MLA_HEADS):
        sl = slice(h * MLA_PAD, (h + 1) * MLA_PAD)
        k_ref[:, sl] = (kn[:, sl] + kr).astype(BF16)
    vt = _dot_nt(wvt_ref[...], ckvn)
    tkc = vt_ref.shape[2]
    for c in range(tm // tkc):
        vt_ref[c] = vt[:, c * tkc:(c + 1) * tkc].astype(BF16)

    sg = _gelu_tanh(sg)
    u = sg[:, :SGU_WIDTH]
    v = sg[:, SGU_WIDTH:]
    gmat = gmat_ref[...]
    mu = _split_dot(v, gmat)
    vc = v - mu
    var = _split_dot(vc * vc, gmat)
    vn = (vc * lax.rsqrt(var + LN_EPS) * slg_ref[...] + slb_ref[...]).astype(BF16)
    head = lax.broadcasted_iota(jnp.int32, (1, SGU_WIDTH), 1) // SGU_HEAD_DIM
    zero = jnp.zeros((), BF16)
    for c in range(tm // SGU_CHUNK):
        rows = slice(c * SGU_CHUNK, (c + 1) * SGU_CHUNK)
        vch = vn[rows]
        stacked = jnp.concatenate([jnp.where(head == g, vch, zero) for g in range(SGU_HEADS)], axis=0)
        mixed = sb_ref[...] + _dot(ws_ref[...], stacked)
        od_ref[rows, :] = (u[rows] * mixed).astype(BF16)


def _inproj(x, lp, tabs, apply_ln, T):
    n = x.shape[0]
    tm = TM_IN
    assert T % tm == 0 and tm % MLA_TQ == 0 and tm % MLA_VT_CHUNK == 0 and tm % SGU_CHUNK == 0
    nt = T // tm
    row = lambda i: (i, 0)
    tab = lambda i: (i % nt, 0)
    weights = [lp['eg'], lp['eb'], lp['w_nat'], lp['w_fb'], lp['w_cq'], lp['w_ckv'], lp['w_sg'],
               lp['q_g'], lp['w_a'], lp['kv_g'], lp['w_kn'], lp['w_vt']]
    tables = [tabs['ca'], tabs['cb'], tabs['tk']]
    tail = [lp['gmat'], lp['sl_g'], lp['sl_b'], lp['w_s'], lp['s_b']]
    tab_t = lambda i: (0, i % nt)
    in_specs = ([pl.BlockSpec((tm, D_MODEL), row)] + [_full(w.shape) for w in weights]
                + [pl.BlockSpec((MLA_PAD, tm), tab_t), pl.BlockSpec((MLA_PAD, tm), tab_t),
                   pl.BlockSpec((tm, LANES), tab)] + [_full(w.shape) for w in tail])
    out_shape = [jax.ShapeDtypeStruct((n, 3 * NA_WIDTH), BF16),
                 jax.ShapeDtypeStruct((n, FN_WIDTH), BF16),
                 jax.ShapeDtypeStruct((n // MLA_TQ, MLA_HEADS * MLA_PAD, MLA_TQ), BF16),
                 jax.ShapeDtypeStruct((n, MLA_HEADS * MLA_PAD), BF16),
                 jax.ShapeDtypeStruct((n // MLA_VT_CHUNK, MLA_WIDTH, MLA_VT_CHUNK), BF16),
                 jax.ShapeDtypeStruct((n, SGU_WIDTH), BF16)]
    out_specs = [pl.BlockSpec((tm, 3 * NA_WIDTH), row),
                 pl.BlockSpec((tm, FN_WIDTH), row),
                 pl.BlockSpec((tm // MLA_TQ, MLA_HEADS * MLA_PAD, MLA_TQ), lambda i: (i, 0, 0)),
                 pl.BlockSpec((tm, MLA_HEADS * MLA_PAD), row),
                 pl.BlockSpec((tm // MLA_VT_CHUNK, MLA_WIDTH, MLA_VT_CHUNK), lambda i: (i, 0, 0)),
                 pl.BlockSpec((tm, SGU_WIDTH), row)]
    if apply_ln:
        out_shape = [jax.ShapeDtypeStruct((n, D_MODEL), F32)] + out_shape
        out_specs = [pl.BlockSpec((tm, D_MODEL), row)] + out_specs
    return pl.pallas_call(
        functools.partial(_inproj_kernel, apply_ln),
        grid=(n // tm,),
        in_specs=in_specs,
        out_specs=out_specs,
        out_shape=out_shape,
        compiler_params=_params("parallel"),
        name="inproj_ln" if apply_ln else "inproj",
    )(x, *weights, *tables, *tail)


def _natten_kernel(q_ref, k0_ref, k1_ref, k2_ref, v0_ref, v1_ref, v2_ref, bias_ref, o_ref, s_ref, p_ref):
    q = q_ref[...]
    k_refs = (k0_ref, k1_ref, k2_ref)
    v_refs = (v0_ref, v1_ref, v2_ref)
    head = lax.broadcasted_iota(jnp.int32, (1, NA_WIDTH), 1) // HEAD_DIM
    zero = jnp.zeros((), BF16)

    def scores(h, slot):
        qh = jnp.where(head == h, q, zero)
        for d in range(NA_KBLKS):
            cols = slice(d * NA_QBLK, (d + 1) * NA_QBLK)
            s_ref[slot, :, cols] = _dot_nt(qh, k_refs[d][...]) + bias_ref[h, :, cols]

    scores(0, 0)
    acc = jnp.zeros((NA_QBLK, NA_WIDTH), F32)
    for h in range(NA_HEADS):
        if h + 1 < NA_HEADS:
            scores(h + 1, (h + 1) % 2)
        s = s_ref[h % 2]
        p = jnp.exp2(s - jnp.max(s, axis=-1, keepdims=True))
        inv = 1.0 / jnp.sum(p, axis=-1, keepdims=True)
        p_ref[...] = p.astype(BF16)
        pv = _dot(p_ref[:, 0:NA_QBLK], v_refs[0][...])
        for d in range(1, NA_KBLKS):
            pv = pv + _dot(p_ref[:, d * NA_QBLK:(d + 1) * NA_QBLK], v_refs[d][...])
        acc = jnp.where(head == h, pv * inv, acc)
    o_ref[...] = acc.astype(BF16)


def _natten(nat, bias, B, T):
    n = nat.shape[0]
    nblk = T // NA_QBLK

    def kv_map(d, col):
        def f(b, j):
            base = jnp.clip(j - 1, 0, nblk - NA_KBLKS)
            return (b * nblk + base + d, col)
        return f

    def bias_map(b, j):
        typ = jnp.where(j == 0, 0, jnp.where(j == nblk - 1, 2, 1))
        return (typ, 0, 0, 0)

    blk = (NA_QBLK, NA_WIDTH)
    in_specs = ([pl.BlockSpec(blk, lambda b, j: (b * nblk + j, 0))]
                + [pl.BlockSpec(blk, kv_map(d, 1)) for d in range(NA_KBLKS)]
                + [pl.BlockSpec(blk, kv_map(d, 2)) for d in range(NA_KBLKS)]
                + [pl.BlockSpec((None, NA_HEADS, NA_QBLK, NA_KBLKS * NA_QBLK), bias_map)])
    return pl.pallas_call(
        _natten_kernel,
        grid=(B, nblk),
        in_specs=in_specs,
        out_specs=pl.BlockSpec(blk, lambda b, j: (b * nblk + j, 0)),
        out_shape=jax.ShapeDtypeStruct((n, NA_WIDTH), BF16),
        scratch_shapes=[pltpu.VMEM((2, NA_QBLK, NA_KBLKS * NA_QBLK), F32),
                        pltpu.VMEM((NA_QBLK, NA_KBLKS * NA_QBLK), BF16)],
        compiler_params=_params("parallel", "arbitrary"),
        name="natten",
    )(nat, nat, nat, nat, nat, nat, nat, bias)


def _fft1_kernel(x_ref, m_ref, a_ref):
    kk = m_ref.shape[0]
    n1 = x_ref.shape[0]
    xt = jnp.swapaxes(x_ref[...], 0, 1)
    for i in range(kk):
        res = _dot(m_ref[i], xt[i])
        res = jnp.concatenate([res[:n1], res[n1:]], axis=1).astype(BF16)
        a_ref[:, i] = res.reshape(n1 // kk, kk, 2 * FN_WIDTH)


def _fft2_kernel(a_ref, cs_ref, w_ref, y_ref):
    n2, kk, _ = a_ref.shape
    cs = cs_ref[...]
    at = jnp.swapaxes(a_ref[...], 0, 1)
    g = []
    for i in range(kk):
        r = _dot(cs, at[i])
        g.append(jnp.concatenate([r[:n2, :FN_WIDTH] + r[n2:, FN_WIDTH:],
                                  r[:n2, FN_WIDTH:] - r[n2:, :FN_WIDTH]], axis=1).astype(BF16))
    g = jnp.concatenate(g, axis=0)
    y = _dot(g, w_ref[...]).astype(BF16)
    y_ref[...] = jnp.swapaxes(y.reshape(kk, n2, FN_WIDTH), 0, 1)


def _fourier(fb, ft, B, T):
    n1, n2 = ft['n1'], ft['n2']
    kk = BF16_ROWS
    a = pl.pallas_call(
        _fft1_kernel,
        grid=(B, n2 // kk),
        in_specs=[pl.BlockSpec((None, n1, kk, FN_WIDTH), lambda b, j: (b, 0, j, 0)),
                  pl.BlockSpec((kk, 2 * n1, n1), lambda b, j: (j, 0, 0))],
        out_specs=pl.BlockSpec((None, n1 // kk, kk, kk, 2 * FN_WIDTH), lambda b, j: (b, 0, j, 0, 0)),
        out_shape=jax.ShapeDtypeStruct((B, n1 // kk, n2, kk, 2 * FN_WIDTH), BF16),
        compiler_params=_params("parallel", "arbitrary"),
        name="fft_stage1",
    )(fb.reshape(B, n1, n2, FN_WIDTH), ft['m1'])
    y = pl.pallas_call(
        _fft2_kernel,
        grid=(B, n1 // kk),
        in_specs=[pl.BlockSpec((None, None, n2, kk, 2 * FN_WIDTH), lambda b, j: (b, j, 0, 0, 0)),
                  _full(ft['cs2'].shape), _full(ft['w'].shape)],
        out_specs=pl.BlockSpec((None, n2, kk, FN_WIDTH), lambda b, j: (b, 0, j, 0)),
        out_shape=jax.ShapeDtypeStruct((B, n2, n1, FN_WIDTH), BF16),
        compiler_params=_params("parallel", "arbitrary"),
        name="fft_stage2",
    )(a, ft['cs2'], ft['w'])
    return y.reshape(B * T, FN_WIDTH)


def _mla_trip(nq, nk):
    u = MLA_UNROLL
    while u > 2 and ((nq * nk) % u or (nk % u and u % nk)):
        u //= 2
    assert u % 2 == 0 and (nq * nk) % u == 0 and (nk % u == 0 or u % nk == 0)
    return u


def _mla_kernel(q_ref, k_ref, vt_ref, o_ref, s_ref):
    tk = s_ref.shape[1]
    per = tk // MLA_VT_CHUNK
    nk = vt_ref.shape[0] // per
    nq, _, tq = q_ref.shape
    unroll = _mla_trip(nq, nk)
    ones = jnp.ones((MLA_ONES, tk), BF16)

    def scores(i, j):
        start = j * tk if isinstance(j, int) else pl.multiple_of(j * tk, tk)
        return _dot(k_ref[pl.ds(start, tk), :], q_ref[i])

    def update(carry, j, slot):
        m, acc = carry
        st = s_ref[slot]
        m_new = jnp.maximum(m, jnp.max(st, axis=0, keepdims=True))
        alpha = jnp.exp2(m - m_new)
        p = jnp.exp2(st - m_new).astype(BF16)
        vt = jnp.concatenate([vt_ref[j * per + u] for u in range(per)], axis=1)
        vte = jnp.concatenate([vt, ones], axis=0)
        return m_new, alpha * acc + _dot(vte, p)

    nbuf = s_ref.shape[0]
    ahead = nbuf // 2
    last = nq * nk - 1
    for f in range(ahead):
        s_ref[f] = scores(f // nk, f % nk)
    m0 = jnp.full((1, tq), -1e30, F32)
    acc0 = jnp.zeros((MLA_V + MLA_ONES, tq), F32)

    def finish(i, acc):
        o_ref[i] = acc[:MLA_V] * (1.0 / acc[MLA_V:MLA_V + 1])

    whole_blocks = unroll % nk == 0

    def body(t, carry):
        if whole_blocks:
            place = lambda c: (t * (unroll // nk) + c // nk, c % nk)
        else:
            i0, j0 = (t * unroll) // nk, (t * unroll) % nk
            place = lambda c: (i0, j0 + c)
            fresh = j0 == 0
            carry = (jnp.where(fresh, m0, carry[0]), jnp.where(fresh, acc0, carry[1]))
        for c in range(unroll):
            if c + ahead < unroll:
                s_ref[(c + ahead) % nbuf] = scores(*place(c + ahead))
            else:
                nxt = jnp.minimum(t * unroll + c + ahead, last)
                s_ref[(c + ahead) % nbuf] = scores(nxt // nk, nxt % nk)
            i, j = place(c)
            if whole_blocks and j == 0:
                carry = (m0, acc0)
            carry = update(carry, j, c % nbuf)
            if whole_blocks and j == nk - 1:
                finish(i, carry[1])

        if not whole_blocks:
            @pl.when(j0 + unroll == nk)
            def _():
                finish(i0, carry[1])

        return carry

    lax.fori_loop(0, nq * nk // unroll, body, (m0, acc0))


def _mla(q, k, vt, B, T):
    nq = T // MLA_TQ
    nk = T // MLA_TK
    unroll = _mla_trip(nq, nk)
    nbuf = MLA_SCORE_BUFS if unroll % MLA_SCORE_BUFS == 0 else 2
    return pl.pallas_call(
        _mla_kernel,
        grid=(B, MLA_HEADS),
        in_specs=[pl.BlockSpec((nq, MLA_PAD, MLA_TQ), lambda b, h: (b, h, 0)),
                  pl.BlockSpec((T, MLA_PAD), lambda b, h: (b, h)),
                  pl.BlockSpec((T // MLA_VT_CHUNK, MLA_V, MLA_VT_CHUNK), lambda b, h: (b, h, 0))],
        out_specs=pl.BlockSpec((nq, MLA_V, MLA_TQ), lambda b, h: (b, h, 0)),
        out_shape=jax.ShapeDtypeStruct((B * nq, MLA_WIDTH, MLA_TQ), F32),
        scratch_shapes=[pltpu.VMEM((nbuf, MLA_TK, MLA_TQ), F32)],
        compiler_params=_params("parallel", "arbitrary"),
        name="mla",
    )(q, k, vt)


def _outproj_kernel(oa_ref, ob_ref, oct_ref, od_ref, x_ref, w_ref, g_ref, b_ref, o_ref):
    w = NA_WIDTH
    oc = oct_ref[...].T.astype(BF16)
    y = (_dot(oa_ref[...], w_ref[0:w]) + _dot(ob_ref[...], w_ref[w:2 * w])
         + _dot(oc, w_ref[2 * w:3 * w]) + _dot(od_ref[...], w_ref[3 * w:4 * w]))
    o_ref[...] = _layer_norm_rows(ALPHA * x_ref[...] + y, g_ref[...], b_ref[...])


def _outproj(oa, ob, oct, od, x, lp):
    n = x.shape[0]
    tm = TM_OUT
    assert tm == MLA_TQ
    row = lambda i: (i, 0)
    return pl.pallas_call(
        _outproj_kernel,
        grid=(n // tm,),
        in_specs=[pl.BlockSpec((tm, NA_WIDTH), row), pl.BlockSpec((tm, FN_WIDTH), row),
                  pl.BlockSpec((None, MLA_WIDTH, tm), lambda i: (i, 0, 0)), pl.BlockSpec((tm, SGU_WIDTH), row),
                  pl.BlockSpec((tm, D_MODEL), row), _full(lp['w_out'].shape),
                  _full(lp['ln1_g'].shape), _full(lp['ln1_b'].shape)],
        out_specs=pl.BlockSpec((tm, D_MODEL), row),
        out_shape=jax.ShapeDtypeStruct((n, D_MODEL), F32),
        compiler_params=_params("parallel"),
        name="outproj",
    )(oa, ob, oct, od, x, lp['w_out'], lp['ln1_g'], lp['ln1_b'])


def _ffn_kernel(nt, x_ref, xp_ref, xn_ref, wup_ref, cw_ref, cb_ref, wd_ref, g_ref, b_ref, o_ref,
                xe_ref, act_ref):
    i = pl.program_id(0)
    tm = x_ref.shape[0]
    cf = FF_CHUNK
    first = (i % nt) == 0
    last = (i % nt) == nt - 1
    xe_ref[0:HALO] = jnp.where(first, 0.0, xp_ref[...]).astype(BF16)
    xe_ref[HALO:HALO + tm] = x_ref[...].astype(BF16)
    xe_ref[HALO + tm:] = jnp.where(last, 0.0, xn_ref[...]).astype(BF16)
    xe = xe_ref[...]

    rows = slice(HALO, HALO + tm)

    def conv(h, cols):
        prev = pltpu.roll(h, 1, axis=0)
        nxt = pltpu.roll(h, h.shape[0] - 1, axis=0)
        return (prev[rows] * cw_ref[0:1, cols] + h[rows] * cw_ref[1:2, cols]
                + nxt[rows] * cw_ref[2:3, cols] + cb_ref[:, cols])

    for c in range(D_FF // cf):
        gate = slice(c * cf, (c + 1) * cf)
        val = slice(D_FF + c * cf, D_FF + (c + 1) * cf)
        act = _gelu_tanh_x2(conv(_dot(xe, wup_ref[:, gate]), gate)) * conv(_dot(xe, wup_ref[:, val]), val)
        act_ref[:, gate] = act.astype(BF16)

    y = _dot(act_ref[...], wd_ref[...])
    o_ref[...] = _layer_norm_rows(ALPHA * x_ref[...] + y, g_ref[...], b_ref[...])


def _resident(shape):
    nd = len(shape)
    return pl.BlockSpec(shape, lambda *_: (0,) * nd, pipeline_mode=pl.Buffered(1))


def _ffn(x, lp, T):
    n = x.shape[0]
    tm = TM_FFN
    nt = T // tm
    assert T % tm == 0 and D_FF % FF_CHUNK == 0
    hb = tm // HALO
    nhb = n // HALO
    row = lambda i: (i, 0)
    return pl.pallas_call(
        functools.partial(_ffn_kernel, nt),
        grid=(n // tm,),
        in_specs=[pl.BlockSpec((tm, D_MODEL), row),
                  pl.BlockSpec((HALO, D_MODEL), lambda i: (jnp.maximum(i * hb - 1, 0), 0)),
                  pl.BlockSpec((HALO, D_MODEL), lambda i: (jnp.minimum((i + 1) * hb, nhb - 1), 0)),
                  _resident(lp['w_up'].shape), _resident(lp['conv_w'].shape), _resident(lp['conv_b'].shape),
                  _resident(lp['w_down'].shape), _resident(lp['ln2_g'].shape), _resident(lp['ln2_b'].shape)],
        out_specs=pl.BlockSpec((tm, D_MODEL), row),
        out_shape=jax.ShapeDtypeStruct((n, D_MODEL), F32),
        scratch_shapes=[pltpu.VMEM((tm + 2 * HALO, D_MODEL), BF16),
                        pltpu.VMEM((tm, D_FF), BF16)],
        compiler_params=_params("parallel"),
        name="conv_ffn",
    )(x, x, x, lp['w_up'], lp['conv_w'], lp['conv_b'], lp['w_down'], lp['ln2_g'], lp['ln2_b'])


def _natten_bias(rpb):
    nr, nc = 2 * NA_ROWS - 1, 2 * NA_COLS - 1
    krows = NA_KBLKS * NA_QROWS
    col = np.arange(GRID_W)
    ci = np.clip(col[None, :] - col[:, None] + NA_COLS - 1, 0, nc - 1)
    onehot_c = (ci.reshape(-1)[None, :] == np.arange(nc)[:, None]).astype(np.float32)
    cs = np.clip(col - NA_COLS // 2, 0, GRID_W - NA_COLS)
    dc = col[None, :] - cs[:, None]
    valid_c = (dc >= 0) & (dc < NA_COLS)
    qa = np.arange(NA_QROWS)
    kr = np.arange(krows)
    onehot_r, valid_r = [], []
    for kind in range(3):
        q_row = (0, NA_QROWS, 2 * NA_QROWS)[kind] + qa
        r_start = (np.zeros_like(qa), qa, np.full_like(qa, NA_QROWS))[kind]
        dr = kr[None, :] - r_start[:, None]
        valid_r.append((dr >= 0) & (dr < NA_ROWS))
        ri = np.clip(kr[None, :] - q_row[:, None] + NA_ROWS - 1, 0, nr - 1)
        onehot_r.append((ri.reshape(-1)[:, None] == np.arange(nr)[None, :]).astype(np.float32))
    onehot_r = np.stack(onehot_r)
    valid = np.stack(valid_r)[:, :, None, :, None] & valid_c[None, None, :, None, :]
    by_col = jnp.einsum('hrc,cx->hrx', rpb, onehot_c, precision=HI)
    full = jnp.einsum('kpr,hrx->khpx', onehot_r, by_col, precision=HI)
    full = full.reshape(3, NA_HEADS, NA_QROWS, krows, GRID_W, GRID_W).transpose(0, 1, 2, 4, 3, 5)
    full = jnp.where(valid[:, None], full * LOG2_E, -1e30)
    return full.reshape(3, NA_HEADS, NA_QBLK, krows * GRID_W).astype(F32)


def _prep_layer(l, emb_ln_g, emb_ln_b, w_in, na_rpb, mla_q_g, w_uq, mla_kv_g, w_ukv, sgu_ln_g, sgu_ln_b,
                sgu_w, sgu_b, w_out, ln1_g, ln1_b, w_up, conv_w, conv_b, w_down, ln2_g, ln2_b):
    wi = w_in[l]
    o_fb = 3 * NA_WIDTH
    o_cq = o_fb + FN_WIDTH
    o_ckv = o_cq + MLA_Q_LORA
    o_kr = o_ckv + MLA_KV_LORA
    o_sg = o_kr + MLA_ROPE
    half = MLA_ROPE // 2
    swap = np.concatenate([np.arange(half, MLA_ROPE), np.arange(half)])
    w_kr = wi[:, o_kr:o_sg]
    w_ckv = jnp.concatenate([wi[:, o_ckv:o_kr], w_kr, w_kr[:, swap],
                             jnp.zeros((D_MODEL, LANES - 2 * MLA_ROPE), F32)], axis=1)

    uq = w_uq[l]
    pad = jnp.zeros((MLA_Q_LORA, MLA_HEADS, MLA_PAD - MLA_NOPE - MLA_ROPE), F32)
    w_a = jnp.concatenate([uq, pad], axis=2).reshape(MLA_Q_LORA, MLA_HEADS * MLA_PAD)
    ukv = w_ukv[l]
    w_kn = jnp.concatenate([ukv[:, :, :MLA_NOPE],
                            jnp.zeros((MLA_KV_LORA, MLA_HEADS, MLA_PAD - MLA_NOPE), F32)],
                           axis=2).reshape(MLA_KV_LORA, MLA_HEADS * MLA_PAD)
    w_vt = ukv[:, :, MLA_NOPE:].reshape(MLA_KV_LORA, MLA_WIDTH).T
    w_s = jnp.concatenate([sgu_w[l][g] for g in range(SGU_HEADS)], axis=1)
    gmat =np.kron(np.eye(SGU_HEADS), np.full((SGU_HEAD_DIM, SGU_HEAD_DIM), 1.0 / SGU_HEAD_DIM)).astype(np.float32)
    s_b = jnp.repeat(sgu_b[l].T, SGU_HEAD_DIM, axis=1)
    r1 = lambda a: a.reshape(1, -1).astype(F32)
    half_val = jnp.concatenate([jnp.ones((1, D_FF), F32), jnp.full((1, D_FF), 0.5, F32)], axis=1)
    return dict(
        eg=r1(emb_ln_g), eb=r1(emb_ln_b),
        w_nat=wi[:, :o_fb].astype(BF16), w_fb=wi[:, o_fb:o_cq].astype(BF16),
        w_cq=wi[:, o_cq:o_ckv].astype(BF16), w_ckv=w_ckv.astype(BF16), w_sg=wi[:, o_sg:].astype(BF16),
        q_g=r1(mla_q_g[l]), w_a=w_a.T.astype(BF16),
        kv_g=r1(mla_kv_g[l]), w_kn=w_kn.astype(BF16), w_vt=w_vt.astype(BF16),
        gmat=jnp.asarray(gmat, BF16),
        sl_g=r1(sgu_ln_g[l]), sl_b=r1(sgu_ln_b[l]), w_s=w_s.astype(BF16), s_b=s_b.astype(F32),
        na_bias=_natten_bias(na_rpb[l]),
        w_out=w_out[l].astype(BF16), ln1_g=r1(ln1_g[l]), ln1_b=r1(ln1_b[l]),
        w_up=w_up[l].astype(BF16), conv_w=conv_w[l].astype(F32) * half_val, conv_b=r1(conv_b[l]) * half_val,
        w_down=w_down[l].astype(BF16), ln2_g=r1(ln2_g[l]), ln2_b=r1(ln2_b[l]),
    )


def _rope_tables(T):
    inv_freq = ROPE_THETA ** (-np.arange(0, MLA_ROPE, 2, dtype=np.float64) / MLA_ROPE)
    ang = np.arange(T, dtype=np.float64)[:, None] * inv_freq[None, :]
    cos, sin = np.cos(ang), np.sin(ang)
    cos2 = np.concatenate([cos, cos], axis=1)
    sin2 = np.concatenate([-sin, sin], axis=1)
    scale = (MLA_NOPE + MLA_ROPE) ** -0.5 * math.log2(math.e)
    zq = np.zeros((T, MLA_PAD - MLA_NOPE - MLA_ROPE))
    ca = np.concatenate([np.full((T, MLA_NOPE), scale), scale * cos2, zq], axis=1)
    cb = np.concatenate([np.zeros((T, MLA_NOPE)), scale * sin2, zq], axis=1)
    tk = np.concatenate([cos2, sin2, np.zeros((T, LANES - 2 * MLA_ROPE))], axis=1)
    f32 = lambda a: jnp.asarray(np.ascontiguousarray(a), F32)
    return dict(ca=f32(ca.T), cb=f32(cb.T), tk=f32(tk))


def _dft_tables(T):
    n1 = 1 << (int(math.log2(T)) // 2)
    n2 = T // n1

    def cs(num, den):
        ang = (num % den).astype(np.float64) * (2.0 * np.pi / den)
        return np.cos(ang), np.sin(ang)

    def table(x):
        return jnp.asarray(x.astype(np.float32)).astype(BF16)

    t2 = np.arange(n2, dtype=np.int64)[:, None, None]
    k1 = np.arange(n1, dtype=np.int64)[None, :, None]
    t1 = np.arange(n1, dtype=np.int64)[None, None, :]
    c, s = cs(k1 * (t1 * n2 + t2), T)
    m1 = table(np.concatenate([c, -s], axis=1))
    k2 = np.arange(n2, dtype=np.int64)
    c2, s2 = cs(k2[:, None] * k2[None, :], n2)
    cs2 = table(np.concatenate([c2, s2], axis=0))
    cg = np.arange(FN_GROUP_DIM, dtype=np.int64)
    cc, sc = cs(cg[:, None] * cg[None, :], FN_GROUP_DIM)
    norm = 1.0 / math.sqrt(T * FN_GROUP_DIM)
    eye = np.eye(FN_GROUPS)
    w = table(np.concatenate([np.kron(eye, cc), np.kron(eye, sc)], axis=0) * norm)
    return dict(n1=n1, n2=n2, m1=m1, cs2=cs2, w=w)


def _trunk(x, layers):
    B, T, _ = x.shape
    tabs = _rope_tables(T)
    ft = _dft_tables(T)
    h = x.reshape(B * T, D_MODEL)
    for l, lp in enumerate(layers):
        outs = _inproj(h, lp, tabs, l == 0, T)
        if l == 0:
            h = outs[0]
            outs = outs[1:]
        nat, fb, q, k, vt, od = outs
        oa = _natten(nat, lp['na_bias'], B, T)
        ob = _fourier(fb, ft, B, T)
        oct = _mla(q, k, vt, B, T)
        h1 = _outproj(oa, ob, oct, od, h, lp)
        h = _ffn(h1, lp, T)
    return h.reshape(B, T, D_MODEL)


def kernel(x_prompt, x_sample, emb_ln_g, emb_ln_b, w_in, na_rpb, mla_q_g, w_uq, mla_kv_g, w_ukv, sgu_ln_g,
           sgu_ln_b, sgu_w, sgu_b, w_out, ln1_g, ln1_b, w_up, conv_w, conv_b, w_down, ln2_g, ln2_b):
    layers = [_prep_layer(l, emb_ln_g, emb_ln_b, w_in, na_rpb, mla_q_g, w_uq, mla_kv_g, w_ukv, sgu_ln_g,
                          sgu_ln_b, sgu_w, sgu_b, w_out, ln1_g, ln1_b, w_up, conv_w, conv_b, w_down,
                          ln2_g, ln2_b) for l in range(DEPTH)]
    return (_trunk(x_prompt, layers), _trunk(x_sample, layers))
```

```python
import functools
import math

import numpy as np
import jax
import jax.numpy as jnp
from jax import lax
from jax.experimental import pallas as pl
from jax.experimental.pallas import tpu as pltpu

F32 = jnp.float32
BF16 = jnp.bfloat16

D_MODEL = 1024
DEPTH = 2
GRID_W = 64
HEAD_DIM = 64
NA_HEADS = 4
NA_ROWS = 8
NA_COLS = 16
NA_WIDTH = NA_HEADS * HEAD_DIM
FN_GROUPS = 4
FN_GROUP_DIM = 64
FN_WIDTH = FN_GROUPS * FN_GROUP_DIM
MLA_HEADS = 4
MLA_Q_LORA = 256
MLA_KV_LORA = 128
MLA_NOPE = 64
MLA_ROPE = 32
MLA_V = 64
MLA_WIDTH = MLA_HEADS * MLA_V
ROPE_THETA = 10000.0
SGU_HEADS = 4
SGU_HEAD_DIM = 64
SGU_WIDTH = SGU_HEADS * SGU_HEAD_DIM
SGU_CHUNK = 128
D_FF = 2816
ALPHA = (2 * DEPTH) ** 0.25
LN_EPS = 1e-5
RMS_EPS = 1e-6

LANES = 128
SUBLANES = 8
BF16_ROWS = 16
VMEM_LIMIT = 56 * 1024 * 1024

TM_IN = 512
TM_OUT = 512
TM_FFN = 512
FF_CHUNK = 256
NA_QROWS = 4
NA_QBLK = NA_QROWS * GRID_W
NA_KBLKS = 3
MLA_PAD = 128
MLA_TQ = 512
MLA_TK = 512
MLA_VT_CHUNK = 512
MLA_ONES = 16
MLA_UNROLL = 16
MLA_SCORE_BUFS = 4
HALO = SUBLANES

LOG2_E = math.log2(math.e)
HI = lax.Precision.HIGHEST
NT_DIMS = (((1,), (1,)), ((), ()))


def _dot(a, b):
    return jnp.dot(a, b, preferred_element_type=F32)


def _dot_nt(a, b):
    return lax.dot_general(a, b, NT_DIMS, preferred_element_type=F32)


def _layer_norm_rows(x, g, b):
    mu = jnp.mean(x, axis=-1, keepdims=True)
    xc = x - mu
    var = jnp.mean(xc * xc, axis=-1, keepdims=True)
    return xc * lax.rsqrt(var + LN_EPS) * g + b


GELU_C = 0.7978845608028654
GELU_K = 0.044715


def _gelu_tanh_x2(x):
    t = jnp.tanh(x * (GELU_C + (GELU_C * GELU_K) * (x * x)))
    return x + x * t


def _gelu_tanh(x):
    return 0.5 * _gelu_tanh_x2(x)


def _split_dot(v, m):
    hi = v.astype(BF16)
    lo = (v - hi.astype(F32)).astype(BF16)
    return _dot(hi, m) + _dot(lo, m)


def _params(*sem):
    return pltpu.CompilerParams(dimension_semantics=sem, vmem_limit_bytes=VMEM_LIMIT)


def _full(shape):
    nd = len(shape)
    return pl.BlockSpec(shape, lambda *_: (0,) * nd)


def _inproj_kernel(apply_ln, x_ref, eg_ref, eb_ref, wnat_ref, wfb_ref, wcq_ref, wckv_ref,
                   wsg_ref, qg_ref, wa_ref, kvg_ref, wkn_ref, wvt_ref, ca_ref, cb_ref,
                   tk_ref, gmat_ref, slg_ref, slb_ref, ws_ref, sb_ref, *out_refs):
    if apply_ln:
        xn_ref, nat_ref, fb_ref, q_ref, k_ref, vt_ref, od_ref = out_refs
    else:
        nat_ref, fb_ref, q_ref, k_ref, vt_ref, od_ref = out_refs
    x = x_ref[...]
    if apply_ln:
        x = _layer_norm_rows(x, eg_ref[...], eb_ref[...])
        xn_ref[...] = x
    xb = x.astype(BF16)
    tm = x.shape[0]

    cq = _dot(xb, wcq_ref[...])
    ckv_kr = _dot(xb, wckv_ref[...])
    sg = _dot(xb, wsg_ref[...])

    nat = _dot(xb, wnat_ref[...])
    nat_ref[:, :NA_WIDTH] = (nat[:, :NA_WIDTH] * (HEAD_DIM ** -0.5 * LOG2_E)).astype(BF16)
    nat_ref[:, NA_WIDTH:] = nat[:, NA_WIDTH:].astype(BF16)

    fb_ref[...] = _dot(xb, wfb_ref[...]).astype(BF16)

    cqn = (cq * lax.rsqrt(jnp.mean(cq * cq, axis=-1, keepdims=True) + RMS_EPS) * qg_ref[...]).astype(BF16)
    qa = _dot_nt(wa_ref[...], cqn)
    ca = ca_ref[...]
    cb = cb_ref[MLA_NOPE:MLA_NOPE + MLA_ROPE]
    half = MLA_ROPE // 2
    for h in range(MLA_HEADS):
        base = h * MLA_PAD
        qh = qa[base:base + MLA_PAD] * ca
        lo = qa[base + MLA_NOPE:base + MLA_NOPE + half]
        hi = qa[base + MLA_NOPE + half:base + MLA_NOPE + MLA_ROPE]
        rope = qh[MLA_NOPE:MLA_NOPE + MLA_ROPE] + jnp.concatenate([hi, lo], axis=0) * cb
        qh = jnp.concatenate([qh[:MLA_NOPE], rope, qh[MLA_NOPE + MLA_ROPE:]], axis=0).astype(BF16)
        for c in range(tm // MLA_TQ):
            q_ref[c, base:base + MLA_PAD, :] = qh[:, c * MLA_TQ:(c + 1) * MLA_TQ]

    ckv = ckv_kr[:, :MLA_KV_LORA]
    ckvn = (ckv * lax.rsqrt(jnp.mean(ckv * ckv, axis=-1, keepdims=True) + RMS_EPS) * kvg_ref[...]).astype(BF16)
    kr = ckv_kr[:, MLA_KV_LORA:] * tk_ref[...]
    kr = pltpu.roll(kr, MLA_NOPE, axis=1) + pltpu.roll(kr, MLA_NOPE - MLA_ROPE, axis=1)
    lane = lax.broadcasted_iota(jnp.int32, (1, MLA_PAD), 1)
    kr = jnp.where((lane >= MLA_NOPE) & (lane < MLA_NOPE + MLA_ROPE), kr, 0.0)
    kn = _dot(ckvn, wkn_ref[...])
    for h in range(MLA_HEADS):
        sl = slice(h * MLA_PAD, (h + 1) * MLA_PAD)
        k_ref[:, sl] = (kn[:, sl] + kr).astype(BF16)
    vt = _dot_nt(wvt_ref[...], ckvn)
    tkc = vt_ref.shape[2]
    for c in range(tm // tkc):
        vt_ref[c] = vt[:, c * tkc:(c + 1) * tkc].astype(BF16)

    sg = _gelu_tanh(sg)
    u = sg[:, :SGU_WIDTH]
    v = sg[:, SGU_WIDTH:]
    gmat = gmat_ref[...]
    mu = _split_dot(v, gmat)
    vc = v - mu
    var = _split_dot(vc * vc, gmat)
    vn = (vc * lax.rsqrt(var + LN_EPS) * slg_ref[...] + slb_ref[...]).astype(BF16)
    head = lax.broadcasted_iota(jnp.int32, (1, SGU_WIDTH), 1) // SGU_HEAD_DIM
    zero = jnp.zeros((), BF16)
    for c in range(tm // SGU_CHUNK):
        rows = slice(c * SGU_CHUNK, (c + 1) * SGU_CHUNK)
        vch = vn[rows]
        stacked = jnp.concatenate([jnp.where(head == g, vch, zero) for g in range(SGU_HEADS)], axis=0)
        mixed = sb_ref[...] + _dot(ws_ref[...], stacked)
        od_ref[rows, :] = (u[rows] * mixed).astype(BF16)


def _inproj(x, lp, tabs, apply_ln, T):
    n = x.shape[0]
    tm = TM_IN
    assert T % tm == 0 and tm % MLA_TQ == 0 and tm % MLA_VT_CHUNK == 0 and tm % SGU_CHUNK == 0
    nt = T // tm
    row = lambda i: (i, 0)
    tab = lambda i: (i % nt, 0)
    weights = [lp['eg'], lp['eb'], lp['w_nat'], lp['w_fb'], lp['w_cq'], lp['w_ckv'], lp['w_sg'],
               lp['q_g'], lp['w_a'], lp['kv_g'], lp['w_kn'], lp['w_vt']]
    tables = [tabs['ca'], tabs['cb'], tabs['tk']]
    tail = [lp['gmat'], lp['sl_g'], lp['sl_b'], lp['w_s'], lp['s_b']]
    tab_t = lambda i: (0, i % nt)
    in_specs = ([pl.BlockSpec((tm, D_MODEL), row)] + [_full(w.shape) for w in weights]
                + [pl.BlockSpec((MLA_PAD, tm), tab_t), pl.BlockSpec((MLA_PAD, tm), tab_t),
                   pl.BlockSpec((tm, LANES), tab)] + [_full(w.shape) for w in tail])
    out_shape = [jax.ShapeDtypeStruct((n, 3 * NA_WIDTH), BF16),
                 jax.ShapeDtypeStruct((n, FN_WIDTH), BF16),
                 jax.ShapeDtypeStruct((n // MLA_TQ, MLA_HEADS * MLA_PAD, MLA_TQ), BF16),
                 jax.ShapeDtypeStruct((n, MLA_HEADS * MLA_PAD), BF16),
                 jax.ShapeDtypeStruct((n // MLA_VT_CHUNK, MLA_WIDTH, MLA_VT_CHUNK), BF16),
                 jax.ShapeDtypeStruct((n, SGU_WIDTH), BF16)]
    out_specs = [pl.BlockSpec((tm, 3 * NA_WIDTH), row),
                 pl.BlockSpec((tm, FN_WIDTH), row),
                 pl.BlockSpec((tm // MLA_TQ, MLA_HEADS * MLA_PAD, MLA_TQ), lambda i: (i, 0, 0)),
                 pl.BlockSpec((tm, MLA_HEADS * MLA_PAD), row),
                 pl.BlockSpec((tm // MLA_VT_CHUNK, MLA_WIDTH, MLA_VT_CHUNK), lambda i: (i, 0, 0)),
                 pl.BlockSpec((tm, SGU_WIDTH), row)]
    if apply_ln:
        out_shape = [jax.ShapeDtypeStruct((n, D_MODEL), F32)] + out_shape
        out_specs = [pl.BlockSpec((tm, D_MODEL), row)] + out_specs
    return pl.pallas_call(
        functools.partial(_inproj_kernel, apply_ln),
        grid=(n // tm,),
        in_specs=in_specs,
        out_specs=out_specs,
        out_shape=out_shape,
        compiler_params=_params("parallel"),
        name="inproj_ln" if apply_ln else "inproj",
    )(x, *weights, *tables, *tail)


def _natten_kernel(q_ref, k0_ref, k1_ref, k2_ref, v0_ref, v1_ref, v2_ref, bias_ref, o_ref, s_ref, p_ref):
    q = q_ref[...]
    k_refs = (k0_ref, k1_ref, k2_ref)
    v_refs = (v0_ref, v1_ref, v2_ref)
    head = lax.broadcasted_iota(jnp.int32, (1, NA_WIDTH), 1) // HEAD_DIM
    zero = jnp.zeros((), BF16)

    def scores(h, slot):
        qh = jnp.where(head == h, q, zero)
        for d in range(NA_KBLKS):
            cols = slice(d * NA_QBLK, (d + 1) * NA_QBLK)
            s_ref[slot, :, cols] = _dot_nt(qh, k_refs[d][...]) + bias_ref[h, :, cols]

    scores(0, 0)
    acc = jnp.zeros((NA_QBLK, NA_WIDTH), F32)
    for h in range(NA_HEADS):
        if h + 1 < NA_HEADS:
            scores(h + 1, (h + 1) % 2)
        s = s_ref[h % 2]
        p = jnp.exp2(s - jnp.max(s, axis=-1, keepdims=True))
        inv = 1.0 / jnp.sum(p, axis=-1, keepdims=True)
        p_ref[...] = p.astype(BF16)
        pv = _dot(p_ref[:, 0:NA_QBLK], v_refs[0][...])
        for d in range(1, NA_KBLKS):
            pv = pv + _dot(p_ref[:, d * NA_QBLK:(d + 1) * NA_QBLK], v_refs[d][...])
        acc = jnp.where(head == h, pv * inv, acc)
    o_ref[...] = acc.astype(BF16)


def _natten(nat, bias, B, T):
    n = nat.shape[0]
    nblk = T // NA_QBLK

    def kv_map(d, col):
        def f(b, j):
            base = jnp.clip(j - 1, 0, nblk - NA_KBLKS)
            return (b * nblk + base + d, col)
        return f

    def bias_map(b, j):
        typ = jnp.where(j == 0, 0, jnp.where(j == nblk - 1, 2, 1))
        return (typ, 0, 0, 0)

    blk = (NA_QBLK, NA_WIDTH)
    in_specs = ([pl.BlockSpec(blk, lambda b, j: (b * nblk + j, 0))]
                + [pl.BlockSpec(blk, kv_map(d, 1)) for d in range(NA_KBLKS)]
                + [pl.BlockSpec(blk, kv_map(d, 2)) for d in range(NA_KBLKS)]
                + [pl.BlockSpec((None, NA_HEADS, NA_QBLK, NA_KBLKS * NA_QBLK), bias_map)])
    return pl.pallas_call(
        _natten_kernel,
        grid=(B, nblk),
        in_specs=in_specs,
        out_specs=pl.BlockSpec(blk, lambda b, j: (b * nblk + j, 0)),
        out_shape=jax.ShapeDtypeStruct((n, NA_WIDTH), BF16),
        scratch_shapes=[pltpu.VMEM((2, NA_QBLK, NA_KBLKS * NA_QBLK), F32),
                        pltpu.VMEM((NA_QBLK, NA_KBLKS * NA_QBLK), BF16)],
        compiler_params=_params("parallel", "arbitrary"),
        name="natten",
    )(nat, nat, nat, nat, nat, nat, nat, bias)


def _fft1_kernel(x_ref, m_ref, a_ref):
    kk = m_ref.shape[0]
    n1 = x_ref.shape[0]
    xt = jnp.swapaxes(x_ref[...], 0, 1)
    for i in range(kk):
        res = _dot(m_ref[i], xt[i])
        res = jnp.concatenate([res[:n1], res[n1:]], axis=1).astype(BF16)
        a_ref[:, i] = res.reshape(n1 // kk, kk, 2 * FN_WIDTH)


def _fft2_kernel(a_ref, cs_ref, w_ref, y_ref):
    n2, kk, _ = a_ref.shape
    cs = cs_ref[...]
    at = jnp.swapaxes(a_ref[...], 0, 1)
    g = []
    for i in range(kk):
        r = _dot(cs, at[i])
        g.append(jnp.concatenate([r[:n2, :FN_WIDTH] + r[n2:, FN_WIDTH:],
                                  r[:n2, FN_WIDTH:] - r[n2:, :FN_WIDTH]], axis=1).astype(BF16))
    g = jnp.concatenate(g, axis=0)
    y = _dot(g, w_ref[...]).astype(BF16)
    y_ref[...] = jnp.swapaxes(y.reshape(kk, n2, FN_WIDTH), 0, 1)


def _fourier(fb, ft, B, T):
    n1, n2 = ft['n1'], ft['n2']
    kk = BF16_ROWS
    a = pl.pallas_call(
        _fft1_kernel,
        grid=(B, n2 // kk),
        in_specs=[pl.BlockSpec((None, n1, kk, FN_WIDTH), lambda b, j: (b, 0, j, 0)),
                  pl.BlockSpec((kk, 2 * n1, n1), lambda b, j: (j, 0, 0))],
        out_specs=pl.BlockSpec((None, n1 // kk, kk, kk, 2 * FN_WIDTH), lambda b, j: (b, 0, j, 0, 0)),
        out_shape=jax.ShapeDtypeStruct((B, n1 // kk, n2, kk, 2 * FN_WIDTH), BF16),
        compiler_params=_params("parallel", "arbitrary"),
        name="fft_stage1",
    )(fb.reshape(B, n1, n2, FN_WIDTH), ft['m1'])
    y = pl.pallas_call(
        _fft2_kernel,
        grid=(B, n1 // kk),
        in_specs=[pl.BlockSpec((None, None, n2, kk, 2 * FN_WIDTH), lambda b, j: (b, j, 0, 0, 0)),
                  _full(ft['cs2'].shape), _full(ft['w'].shape)],
        out_specs=pl.BlockSpec((None, n2, kk, FN_WIDTH), lambda b, j: (b, 0, j, 0)),
        out_shape=jax.ShapeDtypeStruct((B, n2, n1, FN_WIDTH), BF16),
        compiler_params=_params("parallel", "arbitrary"),
        name="fft_stage2",
    )(a, ft['cs2'], ft['w'])
    return y.reshape(B * T, FN_WIDTH)


def _mla_kernel(q_ref, k_ref, vt_ref, o_ref, s_ref):
    tk = s_ref.shape[1]
    per = tk // MLA_VT_CHUNK
    nk = vt_ref.shape[0] // per
    nq, _, tq = q_ref.shape
    unroll = min(MLA_UNROLL, nk)
    ones = jnp.ones((MLA_ONES, tk), BF16)

    def scores(i, j):
        start = j * tk if isinstance(j, int) else pl.multiple_of(j * tk, tk)
        return _dot(k_ref[pl.ds(start, tk), :], q_ref[i])

    def update(carry, j, slot):
        m, acc = carry
        st = s_ref[slot]
        m_new = jnp.maximum(m, jnp.max(st, axis=0, keepdims=True))
        alpha = jnp.exp2(m - m_new)
        p = jnp.exp2(st - m_new).astype(BF16)
        vt = jnp.concatenate([vt_ref[j * per + u] for u in range(per)], axis=1)
        vte = jnp.concatenate([vt, ones], axis=0)
        return m_new, alpha * acc + _dot(vte, p)

    nbuf = s_ref.shape[0]
    ahead = nbuf // 2
    last = nq * nk - 1
    for f in range(ahead):
        s_ref[f] = scores(f // nk, f % nk)
    m0 = jnp.full((1, tq), -1e30, F32)
    acc0 = jnp.zeros((MLA_V + MLA_ONES, tq), F32)

    def body(t, carry):
        i = (t * unroll) // nk
        j0 = (t * unroll) % nk
        fresh = j0 == 0
        carry = (jnp.where(fresh, m0, carry[0]), jnp.where(fresh, acc0, carry[1]))
        for c in range(unroll):
            if c + ahead < unroll:
                s_ref[(c + ahead) % nbuf] = scores(i, j0 + c + ahead)
            else:
                nxt = jnp.minimum(t * unroll + c + ahead, last)
                s_ref[(c + ahead) % nbuf] = scores(nxt // nk, nxt % nk)
            carry = update(carry, j0 + c, c % nbuf)

        @pl.when(j0 + unroll == nk)
        def _():
            acc = carry[1]
            o_ref[i] = (acc[:MLA_V] * (1.0 / acc[MLA_V:MLA_V + 1])).astype(o_ref.dtype)

        return carry

    lax.fori_loop(0, nq * nk // unroll, body, (m0, acc0))


def _mla(q, k, vt, B, T):
    nq = T // MLA_TQ
    nk = T // MLA_TK
    unroll = min(MLA_UNROLL, nk)
    nbuf = MLA_SCORE_BUFS if unroll % MLA_SCORE_BUFS == 0 else 2
    assert nk % unroll == 0 and unroll % nbuf == 0
    return pl.pallas_call(
        _mla_kernel,
        grid=(B, MLA_HEADS),
        in_specs=[pl.BlockSpec((nq, MLA_PAD, MLA_TQ), lambda b, h: (b, h, 0)),
                  pl.BlockSpec((T, MLA_PAD), lambda b, h: (b, h)),
                  pl.BlockSpec((T // MLA_VT_CHUNK, MLA_V, MLA_VT_CHUNK), lambda b, h: (b, h, 0))],
        out_specs=pl.BlockSpec((nq, MLA_V, MLA_TQ), lambda b, h: (b, h, 0)),
        out_shape=jax.ShapeDtypeStruct((B * nq, MLA_WIDTH, MLA_TQ), BF16),
        scratch_shapes=[pltpu.VMEM((nbuf, MLA_TK, MLA_TQ), F32)],
        compiler_params=_params("parallel", "arbitrary"),
        name="mla",
    )(q, k, vt)


def _outproj_kernel(oa_ref, ob_ref, oct_ref, od_ref, x_ref, w_ref, g_ref, b_ref, o_ref):
    w = NA_WIDTH
    tm = x_ref.shape[0]
    for r in range(0, tm, tm // 2):
        rows = slice(r, r + tm // 2)
        oc = oct_ref[:, rows].T
        y = (_dot(oa_ref[rows, :], w_ref[0:w]) + _dot(ob_ref[rows, :], w_ref[w:2 * w])
             + _dot(oc, w_ref[2 * w:3 * w]) + _dot(od_ref[rows, :], w_ref[3 * w:4 * w]))
        o_ref[rows, :] = _layer_norm_rows(ALPHA * x_ref[rows, :] + y, g_ref[...], b_ref[...])


def _outproj(oa, ob, oct, od, x, lp):
    n = x.shape[0]
    tm = TM_OUT
    assert tm == MLA_TQ
    row = lambda i: (i, 0)
    return pl.pallas_call(
        _outproj_kernel,
        grid=(n // tm,),
        in_specs=[pl.BlockSpec((tm, NA_WIDTH), row), pl.BlockSpec((tm, FN_WIDTH), row),
                  pl.BlockSpec((None, MLA_WIDTH, tm), lambda i: (i, 0, 0)), pl.BlockSpec((tm, SGU_WIDTH), row),
                  pl.BlockSpec((tm, D_MODEL), row), _full(lp['w_out'].shape),
                  _full(lp['ln1_g'].shape), _full(lp['ln1_b'].shape)],
        out_specs=pl.BlockSpec((tm, D_MODEL), row),
        out_shape=jax.ShapeDtypeStruct((n, D_MODEL), F32),
        compiler_params=_params("parallel"),
        name="outproj",
    )(oa, ob, oct, od, x, lp['w_out'], lp['ln1_g'], lp['ln1_b'])


def _ffn_kernel(nt, x_ref, xp_ref, xn_ref, wup_ref, cw_ref, cb_ref, wd_ref, g_ref, b_ref, o_ref,
                xe_ref, act_ref):
    i = pl.program_id(0)
    tm = x_ref.shape[0]
    cf = FF_CHUNK
    first = (i % nt) == 0
    last = (i % nt) == nt - 1
    xe_ref[0:HALO] = jnp.where(first, 0.0, xp_ref[...]).astype(BF16)
    xe_ref[HALO:HALO + tm] = x_ref[...].astype(BF16)
    xe_ref[HALO + tm:] = jnp.where(last, 0.0, xn_ref[...]).astype(BF16)
    xe = xe_ref[...]

    rows = slice(HALO, HALO + tm)

    def conv(h, cols):
        prev = pltpu.roll(h, 1, axis=0)
        nxt = pltpu.roll(h, h.shape[0] - 1, axis=0)
        return (prev[rows] * cw_ref[0:1, cols] + h[rows] * cw_ref[1:2, cols]
                + nxt[rows] * cw_ref[2:3, cols] + cb_ref[:, cols])

    for c in range(D_FF // cf):
        gate = slice(c * cf, (c + 1) * cf)
        val = slice(D_FF + c * cf, D_FF + (c + 1) * cf)
        act = _gelu_tanh_x2(conv(_dot(xe, wup_ref[:, gate]), gate)) * conv(_dot(xe, wup_ref[:, val]), val)
        act_ref[:, gate] = act.astype(BF16)

    y = _dot(act_ref[...], wd_ref[...])
    o_ref[...] = _layer_norm_rows(ALPHA * x_ref[...] + y, g_ref[...], b_ref[...])


def _resident(shape):
    nd = len(shape)
    return pl.BlockSpec(shape, lambda *_: (0,) * nd, pipeline_mode=pl.Buffered(1))


def _ffn(x, lp, T):
    n = x.shape[0]
    tm = TM_FFN
    nt = T // tm
    assert T % tm == 0 and D_FF % FF_CHUNK == 0
    hb = tm // HALO
    nhb = n // HALO
    row = lambda i: (i, 0)
    return pl.pallas_call(
        functools.partial(_ffn_kernel, nt),
        grid=(n // tm,),
        in_specs=[pl.BlockSpec((tm, D_MODEL), row),
                  pl.BlockSpec((HALO, D_MODEL), lambda i: (jnp.maximum(i * hb - 1, 0), 0)),
                  pl.BlockSpec((HALO, D_MODEL), lambda i: (jnp.minimum((i + 1) * hb, nhb - 1), 0)),
                  _resident(lp['w_up'].shape), _resident(lp['conv_w'].shape), _resident(lp['conv_b'].shape),
                  _resident(lp['w_down'].shape), _resident(lp['ln2_g'].shape), _resident(lp['ln2_b'].shape)],
        out_specs=pl.BlockSpec((tm, D_MODEL), row),
        out_shape=jax.ShapeDtypeStruct((n, D_MODEL), F32),
        scratch_shapes=[pltpu.VMEM((tm + 2 * HALO, D_MODEL), BF16),
                        pltpu.VMEM((tm, D_FF), BF16)],
        compiler_params=_params("parallel"),
        name="conv_ffn",
    )(x, x, x, lp['w_up'], lp['conv_w'], lp['conv_b'], lp['w_down'], lp['ln2_g'], lp['ln2_b'])


def _natten_bias(rpb):
    nr, nc = 2 * NA_ROWS - 1, 2 * NA_COLS - 1
    krows = NA_KBLKS * NA_QROWS
    col = np.arange(GRID_W)
    ci = np.clip(col[None, :] - col[:, None] + NA_COLS - 1, 0, nc - 1)
    onehot_c = (ci.reshape(-1)[None, :] == np.arange(nc)[:, None]).astype(np.float32)
    cs = np.clip(col - NA_COLS // 2, 0, GRID_W - NA_COLS)
    dc = col[None, :] - cs[:, None]
    valid_c = (dc >= 0) & (dc < NA_COLS)
    qa = np.arange(NA_QROWS)
    kr = np.arange(krows)
    onehot_r, valid_r = [], []
    for kind in range(3):
        q_row = (0, NA_QROWS, 2 * NA_QROWS)[kind] + qa
        r_start = (np.zeros_like(qa), qa, np.full_like(qa, NA_QROWS))[kind]
        dr = kr[None, :] - r_start[:, None]
        valid_r.append((dr >= 0) & (dr < NA_ROWS))
        ri = np.clip(kr[None, :] - q_row[:, None] + NA_ROWS - 1, 0, nr - 1)
        onehot_r.append((ri.reshape(-1)[:, None] == np.arange(nr)[None, :]).astype(np.float32))
    onehot_r = np.stack(onehot_r)
    valid = np.stack(valid_r)[:, :, None, :, None] & valid_c[None, None, :, None, :]
    by_col = jnp.einsum('hrc,cx->hrx', rpb, onehot_c, precision=HI)
    full = jnp.einsum('kpr,hrx->khpx', onehot_r, by_col, precision=HI)
    full = full.reshape(3, NA_HEADS, NA_QROWS, krows, GRID_W, GRID_W).transpose(0, 1, 2, 4, 3, 5)
    full = jnp.where(valid[:, None], full * LOG2_E, -1e30)
    return full.reshape(3, NA_HEADS, NA_QBLK, krows * GRID_W).astype(F32)


def _prep_layer(l, emb_ln_g, emb_ln_b, w_in, na_rpb, mla_q_g, w_uq, mla_kv_g, w_ukv, sgu_ln_g, sgu_ln_b,
                sgu_w, sgu_b, w_out, ln1_g, ln1_b, w_up, conv_w, conv_b, w_down, ln2_g, ln2_b):
    wi = w_in[l]
    o_fb = 3 * NA_WIDTH
    o_cq = o_fb + FN_WIDTH
    o_ckv = o_cq + MLA_Q_LORA
    o_kr = o_ckv + MLA_KV_LORA
    o_sg = o_kr + MLA_ROPE
    half = MLA_ROPE // 2
    swap = np.concatenate([np.arange(half, MLA_ROPE), np.arange(half)])
    w_kr = wi[:, o_kr:o_sg]
    w_ckv = jnp.concatenate([wi[:, o_ckv:o_kr], w_kr, w_kr[:, swap],
                             jnp.zeros((D_MODEL, LANES - 2 * MLA_ROPE), F32)], axis=1)

    uq = w_uq[l]
    pad = jnp.zeros((MLA_Q_LORA, MLA_HEADS, MLA_PAD - MLA_NOPE - MLA_ROPE), F32)
    w_a = jnp.concatenate([uq, pad], axis=2).reshape(MLA_Q_LORA, MLA_HEADS * MLA_PAD)
    ukv = w_ukv[l]
    w_kn = jnp.concatenate([ukv[:, :, :MLA_NOPE],
                            jnp.zeros((MLA_KV_LORA, MLA_HEADS, MLA_PAD - MLA_NOPE), F32)],
                           axis=2).reshape(MLA_KV_LORA, MLA_HEADS * MLA_PAD)
    w_vt = ukv[:, :, MLA_NOPE:].reshape(MLA_KV_LORA, MLA_WIDTH).T
    w_s = jnp.concatenate([sgu_w[l][g] for g in range(SGU_HEADS)], axis=1)
    gmat =np.kron(np.eye(SGU_HEADS), np.full((SGU_HEAD_DIM, SGU_HEAD_DIM), 1.0 / SGU_HEAD_DIM)).astype(np.float32)
    s_b = jnp.repeat(sgu_b[l].T, SGU_HEAD_DIM, axis=1)
    r1 = lambda a: a.reshape(1, -1).astype(F32)
    half_val = jnp.concatenate([jnp.ones((1, D_FF), F32), jnp.full((1, D_FF), 0.5, F32)], axis=1)
    return dict(
        eg=r1(emb_ln_g), eb=r1(emb_ln_b),
        w_nat=wi[:, :o_fb].astype(BF16), w_fb=wi[:, o_fb:o_cq].astype(BF16),
        w_cq=wi[:, o_cq:o_ckv].astype(BF16), w_ckv=w_ckv.astype(BF16), w_sg=wi[:, o_sg:].astype(BF16),
        q_g=r1(mla_q_g[l]), w_a=w_a.T.astype(BF16),
        kv_g=r1(mla_kv_g[l]), w_kn=w_kn.astype(BF16), w_vt=w_vt.astype(BF16),
        gmat=jnp.asarray(gmat, BF16),
        sl_g=r1(sgu_ln_g[l]), sl_b=r1(sgu_ln_b[l]), w_s=w_s.astype(BF16), s_b=s_b.astype(F32),
        na_bias=_natten_bias(na_rpb[l]),
        w_out=w_out[l].astype(BF16), ln1_g=r1(ln1_g[l]), ln1_b=r1(ln1_b[l]),
        w_up=w_up[l].astype(BF16), conv_w=conv_w[l].astype(F32) * half_val, conv_b=r1(conv_b[l]) * half_val,
        w_down=w_down[l].astype(BF16), ln2_g=r1(ln2_g[l]), ln2_b=r1(ln2_b[l]),
    )


def _rope_tables(T):
    inv_freq = ROPE_THETA ** (-np.arange(0, MLA_ROPE, 2, dtype=np.float64) / MLA_ROPE)
    ang = np.arange(T, dtype=np.float64)[:, None] * inv_freq[None, :]
    cos, sin = np.cos(ang), np.sin(ang)
    cos2 = np.concatenate([cos, cos], axis=1)
    sin2 = np.concatenate([-sin, sin], axis=1)
    scale = (MLA_NOPE + MLA_ROPE) ** -0.5 * math.log2(math.e)
    zq = np.zeros((T, MLA_PAD - MLA_NOPE - MLA_ROPE))
    ca = np.concatenate([np.full((T, MLA_NOPE), scale), scale * cos2, zq], axis=1)
    cb = np.concatenate([np.zeros((T, MLA_NOPE)), scale * sin2, zq], axis=1)
    tk = np.concatenate([cos2, sin2, np.zeros((T, LANES - 2 * MLA_ROPE))], axis=1)
    f32 = lambda a: jnp.asarray(np.ascontiguousarray(a), F32)
    return dict(ca=f32(ca.T), cb=f32(cb.T), tk=f32(tk))


def _dft_tables(T):
    n1 = 1 << (int(math.log2(T)) // 2)
    n2 = T // n1

    def cs(num, den):
        ang = (num % den).astype(np.float64) * (2.0 * np.pi / den)
        return np.cos(ang), np.sin(ang)

    def table(x):
        return jnp.asarray(x.astype(np.float32)).astype(BF16)

    t2 = np.arange(n2, dtype=np.int64)[:, None, None]
    k1 = np.arange(n1, dtype=np.int64)[None, :, None]
    t1 = np.arange(n1, dtype=np.int64)[None, None, :]
    c, s = cs(k1 * (t1 * n2 + t2), T)
    m1 = table(np.concatenate([c, -s], axis=1))
    k2 = np.arange(n2, dtype=np.int64)
    c2, s2 = cs(k2[:, None] * k2[None, :], n2)
    cs2 = table(np.concatenate([c2, s2], axis=0))
    cg = np.arange(FN_GROUP_DIM, dtype=np.int64)
    cc, sc = cs(cg[:, None] * cg[None, :], FN_GROUP_DIM)
    norm = 1.0 / math.sqrt(T * FN_GROUP_DIM)
    eye = np.eye(FN_GROUPS)
    w = table(np.concatenate([np.kron(eye, cc), np.kron(eye, sc)], axis=0) * norm)
    return dict(n1=n1, n2=n2, m1=m1, cs2=cs2, w=w)


def _trunk(x, layers):
    B, T, _ = x.shape
    tabs = _rope_tables(T)
    ft = _dft_tables(T)
    h = x.reshape(B * T, D_MODEL)
    for l, lp in enumerate(layers):
        outs = _inproj(h, lp, tabs, l == 0, T)
        if l == 0:
            h = outs[0]
            outs = outs[1:]
        nat, fb, q, k, vt, od = outs
        oa = _natten(nat, lp['na_bias'], B, T)
        ob = _fourier(fb, ft, B, T)
        oct = _mla(q, k, vt, B, T)
        h1 = _outproj(oa, ob, oct, od, h, lp)
        h = _ffn(h1, lp, T)
    return h.reshape(B, T, D_MODEL)


def kernel(x_prompt, x_sample, emb_ln_g, emb_ln_b, w_in, na_rpb, mla_q_g, w_uq, mla_kv_g, w_ukv, sgu_ln_g,
           sgu_ln_b, sgu_w, sgu_b, w_out, ln1_g, ln1_b, w_up, conv_w, conv_b, w_down, ln2_g, ln2_b):
    layers = [_prep_layer(l, emb_ln_g, emb_ln_b, w_in, na_rpb, mla_q_g, w_uq, mla_kv_g, w_ukv, sgu_ln_g,
                          sgu_ln_b, sgu_w, sgu_b, w_out, ln1_g, ln1_b, w_up, conv_w, conv_b, w_down,
                          ln2_g, ln2_b) for l in range(DEPTH)]
    return (_trunk(x_prompt, layers), _trunk(x_sample, layers))
```

```python
import functools
import math

import numpy as np
import jax
import jax.numpy as jnp
from jax import lax
from jax.experimental import pallas as pl
from jax.experimental.pallas import tpu as pltpu

F32 = jnp.float32
BF16 = jnp.bfloat16

D_MODEL = 1024
DEPTH = 2
GRID_W = 64
HEAD_DIM = 64
NA_HEADS = 4
NA_ROWS = 8
NA_COLS = 16
NA_WIDTH = NA_HEADS * HEAD_DIM
FN_GROUPS = 4
FN_GROUP_DIM = 64
FN_WIDTH = FN_GROUPS * FN_GROUP_DIM
MLA_HEADS = 4
MLA_Q_LORA = 256
MLA_KV_LORA = 128
MLA_NOPE = 64
MLA_ROPE = 32
MLA_V = 64
MLA_WIDTH = MLA_HEADS * MLA_V
ROPE_THETA = 10000.0
SGU_HEADS = 4
SGU_HEAD_DIM = 64
SGU_WIDTH = SGU_HEADS * SGU_HEAD_DIM
SGU_CHUNK = 128
D_FF = 2816
ALPHA = (2 * DEPTH) ** 0.25
LN_EPS = 1e-5
RMS_EPS = 1e-6

LANES = 128
SUBLANES = 8
BF16_ROWS = 16
VMEM_LIMIT = 56 * 1024 * 1024

TM_IN = 512
TM_OUT = 512
TM_FFN = 512
FF_CHUNK = 256
NA_QROWS = 4
NA_QBLK = NA_QROWS * GRID_W
NA_KBLKS = 3
MLA_PAD = 128
MLA_TQ = 512
MLA_TK = 512
MLA_VT_CHUNK = 512
MLA_ONES = 16
MLA_UNROLL = 16
MLA_SCORE_BUFS = 4
HALO = SUBLANES

LOG2_E = math.log2(math.e)
HI = lax.Precision.HIGHEST
NT_DIMS = (((1,), (1,)), ((), ()))


def _dot(a, b):
    return jnp.dot(a, b, preferred_element_type=F32)


def _dot_nt(a, b):
    return lax.dot_general(a, b, NT_DIMS, preferred_element_type=F32)


def _layer_norm_rows(x, g, b):
    mu = jnp.mean(x, axis=-1, keepdims=True)
    xc = x - mu
    var = jnp.mean(xc * xc, axis=-1, keepdims=True)
    return xc * lax.rsqrt(var + LN_EPS) * g + b


GELU_C = 0.7978845608028654
GELU_K = 0.044715


def _gelu_tanh_x2(x):
    t = jnp.tanh(x * (GELU_C + (GELU_C * GELU_K) * (x * x)))
    return x + x * t


def _gelu_tanh(x):
    return 0.5 * _gelu_tanh_x2(x)


def _split_dot(v, m):
    hi = v.astype(BF16)
    lo = (v - hi.astype(F32)).astype(BF16)
    return _dot(hi, m) + _dot(lo, m)


def _params(*sem):
    return pltpu.CompilerParams(dimension_semantics=sem, vmem_limit_bytes=VMEM_LIMIT)


def _full(shape):
    nd = len(shape)
    return pl.BlockSpec(shape, lambda *_: (0,) * nd)


def _inproj_kernel(apply_ln, x_ref, eg_ref, eb_ref, wnq_ref, wnk_ref, wnv_ref, wfb_ref, wcq_ref, wckv_ref,
                   wsg_ref, qg_ref, wa_ref, kvg_ref, wkn_ref, wvt_ref, ca_ref, cb_ref,
                   tk_ref, gmat_ref, slg_ref, slb_ref, ws_ref, sb_ref, *out_refs):
    if apply_ln:
        xn_ref, natq_ref, natk_ref, natv_ref, fb_ref, q_ref, k_ref, vt_ref, od_ref = out_refs
    else:
        natq_ref, natk_ref, natv_ref, fb_ref, q_ref, k_ref, vt_ref, od_ref = out_refs
    x = x_ref[...]
    if apply_ln:
        x = _layer_norm_rows(x, eg_ref[...], eb_ref[...])
        xn_ref[...] = x
    xb = x.astype(BF16)
    tm = x.shape[0]

    cq = _dot(xb, wcq_ref[...])
    ckv_kr = _dot(xb, wckv_ref[...])
    sg = _dot(xb, wsg_ref[...])

    natq = (_dot_nt(wnq_ref[...], xb) * (HEAD_DIM ** -0.5 * LOG2_E)).astype(BF16)
    natv = _dot_nt(wnv_ref[...], xb).astype(BF16)
    for c in range(tm // NA_QBLK):
        natq_ref[c] = natq[:, c * NA_QBLK:(c + 1) * NA_QBLK]
        natv_ref[c] = natv[:, c * NA_QBLK:(c + 1) * NA_QBLK]
    natk_ref[...] = _dot(xb, wnk_ref[...]).astype(BF16)

    fb_ref[...] = _dot(xb, wfb_ref[...]).astype(BF16)

    cqn = (cq * lax.rsqrt(jnp.mean(cq * cq, axis=-1, keepdims=True) + RMS_EPS) * qg_ref[...]).astype(BF16)
    qa = _dot_nt(wa_ref[...], cqn)
    ca = ca_ref[...]
    cb = cb_ref[MLA_NOPE:MLA_NOPE + MLA_ROPE]
    half = MLA_ROPE // 2
    for h in range(MLA_HEADS):
        base = h * MLA_PAD
        qh = qa[base:base + MLA_PAD] * ca
        lo = qa[base + MLA_NOPE:base + MLA_NOPE + half]
        hi = qa[base + MLA_NOPE + half:base + MLA_NOPE + MLA_ROPE]
        rope = qh[MLA_NOPE:MLA_NOPE + MLA_ROPE] + jnp.concatenate([hi, lo], axis=0) * cb
        qh = jnp.concatenate([qh[:MLA_NOPE], rope, qh[MLA_NOPE + MLA_ROPE:]], axis=0).astype(BF16)
        for c in range(tm // MLA_TQ):
            q_ref[c, base:base + MLA_PAD, :] = qh[:, c * MLA_TQ:(c + 1) * MLA_TQ]

    ckv = ckv_kr[:, :MLA_KV_LORA]
    ckvn = (ckv * lax.rsqrt(jnp.mean(ckv * ckv, axis=-1, keepdims=True) + RMS_EPS) * kvg_ref[...]).astype(BF16)
    kr = ckv_kr[:, MLA_KV_LORA:] * tk_ref[...]
    kr = pltpu.roll(kr, MLA_NOPE, axis=1) + pltpu.roll(kr, MLA_NOPE - MLA_ROPE, axis=1)
    lane = lax.broadcasted_iota(jnp.int32, (1, MLA_PAD), 1)
    kr = jnp.where((lane >= MLA_NOPE) & (lane < MLA_NOPE + MLA_ROPE), kr, 0.0)
    kn = _dot(ckvn, wkn_ref[...])
    for h in range(MLA_HEADS):
        sl = slice(h * MLA_PAD, (h + 1) * MLA_PAD)
        k_ref[:, sl] = (kn[:, sl] + kr).astype(BF16)
    vt = _dot_nt(wvt_ref[...], ckvn)
    tkc = vt_ref.shape[2]
    for c in range(tm // tkc):
        vt_ref[c] = vt[:, c * tkc:(c + 1) * tkc].astype(BF16)

    sg = _gelu_tanh(sg)
    u = sg[:, :SGU_WIDTH]
    v = sg[:, SGU_WIDTH:]
    gmat = gmat_ref[...]
    mu = _split_dot(v, gmat)
    vc = v - mu
    var = _split_dot(vc * vc, gmat)
    vn = (vc * lax.rsqrt(var + LN_EPS) * slg_ref[...] + slb_ref[...]).astype(BF16)
    head = lax.broadcasted_iota(jnp.int32, (1, SGU_WIDTH), 1) // SGU_HEAD_DIM
    zero = jnp.zeros((), BF16)
    for c in range(tm // SGU_CHUNK):
        rows = slice(c * SGU_CHUNK, (c + 1) * SGU_CHUNK)
        vch = vn[rows]
        stacked = jnp.concatenate([jnp.where(head == g, vch, zero) for g in range(SGU_HEADS)], axis=0)
        mixed = sb_ref[...] + _dot(ws_ref[...], stacked)
        od_ref[rows, :] = (u[rows] * mixed).astype(BF16)


def _inproj(x, lp, tabs, apply_ln, T):
    n = x.shape[0]
    tm = TM_IN
    assert T % tm == 0 and tm % MLA_TQ == 0 and tm % MLA_VT_CHUNK == 0 and tm % SGU_CHUNK == 0 and tm % NA_QBLK == 0
    nt = T // tm
    row = lambda i: (i, 0)
    tab = lambda i: (i % nt, 0)
    weights = [lp['eg'], lp['eb'], lp['w_nq'], lp['w_nk'], lp['w_nv'], lp['w_fb'], lp['w_cq'], lp['w_ckv'], lp['w_sg'],
               lp['q_g'], lp['w_a'], lp['kv_g'], lp['w_kn'], lp['w_vt']]
    tables = [tabs['ca'], tabs['cb'], tabs['tk']]
    tail = [lp['gmat'], lp['sl_g'], lp['sl_b'], lp['w_s'], lp['s_b']]
    tab_t = lambda i: (0, i % nt)
    in_specs = ([pl.BlockSpec((tm, D_MODEL), row)] + [_full(w.shape) for w in weights]
                + [pl.BlockSpec((MLA_PAD, tm), tab_t), pl.BlockSpec((MLA_PAD, tm), tab_t),
                   pl.BlockSpec((tm, LANES), tab)] + [_full(w.shape) for w in tail])
    nat_t = jax.ShapeDtypeStruct((n // NA_QBLK, NA_WIDTH, NA_QBLK), BF16)
    nat_t_spec = pl.BlockSpec((tm // NA_QBLK, NA_WIDTH, NA_QBLK), lambda i: (i, 0, 0))
    out_shape = [nat_t, jax.ShapeDtypeStruct((n, NA_WIDTH), BF16), nat_t,
                 jax.ShapeDtypeStruct((n, FN_WIDTH), BF16),
                 jax.ShapeDtypeStruct((n // MLA_TQ, MLA_HEADS * MLA_PAD, MLA_TQ), BF16),
                 jax.ShapeDtypeStruct((n, MLA_HEADS * MLA_PAD), BF16),
                 jax.ShapeDtypeStruct((n // MLA_VT_CHUNK, MLA_WIDTH, MLA_VT_CHUNK), BF16),
                 jax.ShapeDtypeStruct((n, SGU_WIDTH), BF16)]
    out_specs = [nat_t_spec, pl.BlockSpec((tm, NA_WIDTH), row), nat_t_spec,
                 pl.BlockSpec((tm, FN_WIDTH), row),
                 pl.BlockSpec((tm // MLA_TQ, MLA_HEADS * MLA_PAD, MLA_TQ), lambda i: (i, 0, 0)),
                 pl.BlockSpec((tm, MLA_HEADS * MLA_PAD), row),
                 pl.BlockSpec((tm // MLA_VT_CHUNK, MLA_WIDTH, MLA_VT_CHUNK), lambda i: (i, 0, 0)),
                 pl.BlockSpec((tm, SGU_WIDTH), row)]
    if apply_ln:
        out_shape = [jax.ShapeDtypeStruct((n, D_MODEL), F32)] + out_shape
        out_specs = [pl.BlockSpec((tm, D_MODEL), row)] + out_specs
    return pl.pallas_call(
        functools.partial(_inproj_kernel, apply_ln),
        grid=(n // tm,),
        in_specs=in_specs,
        out_specs=out_specs,
        out_shape=out_shape,
        compiler_params=_params("parallel"),
        name="inproj_ln" if apply_ln else "inproj",
    )(x, *weights, *tables, *tail)


def _natten_kernel(qt_ref, k0_ref, k1_ref, k2_ref, vt0_ref, vt1_ref, vt2_ref, bias_ref, o_ref, s_ref):
    qt = qt_ref[...]
    k_refs = (k0_ref, k1_ref, k2_ref)
    vt_refs = (vt0_ref, vt1_ref, vt2_ref)
    row_head = lax.broadcasted_iota(jnp.int32, (NA_WIDTH, 1), 0) // HEAD_DIM
    zero = jnp.zeros((), BF16)
    ones = jnp.ones((BF16_ROWS, NA_QBLK), BF16)

    def scores(h, slot):
        qh = jnp.where(row_head == h, qt, zero)
        for d in range(NA_KBLKS):
            rows = slice(d * NA_QBLK, (d + 1) * NA_QBLK)
            s_ref[slot, rows, :] = _dot(k_refs[d][...], qh) + bias_ref[h, rows, :]

    scores(0, 0)
    for h in range(NA_HEADS):
        if h + 1 < NA_HEADS:
            scores(h + 1, (h + 1) % 2)
        s = s_ref[h % 2]
        p = jnp.exp2(s - jnp.max(s, axis=0, keepdims=True)).astype(BF16)
        ch = slice(h * HEAD_DIM, (h + 1) * HEAD_DIM)
        acc = None
        for d in range(NA_KBLKS):
            vte = jnp.concatenate([vt_refs[d][ch, :], ones], axis=0)
            part = _dot(vte, p[d * NA_QBLK:(d + 1) * NA_QBLK])
            acc = part if acc is None else acc + part
        o_ref[ch, :] = (acc[:HEAD_DIM] * (1.0 / acc[HEAD_DIM:HEAD_DIM + 1])).astype(BF16)


def _natten(qt, k, vt, bias, B, T):
    nblk = T // NA_QBLK

    def kv_map(d, rank3):
        def f(b, j):
            base = jnp.clip(j - 1, 0, nblk - NA_KBLKS)
            return (b * nblk + base + d, 0, 0) if rank3 else (b * nblk + base + d, 0)
        return f

    def bias_map(b, j):
        typ = jnp.where(j == 0, 0, jnp.where(j == nblk - 1, 2, 1))
        return (typ, 0, 0, 0)

    tblk = (None, NA_WIDTH, NA_QBLK)
    in_specs = ([pl.BlockSpec(tblk, lambda b, j: (b * nblk + j, 0, 0))]
                + [pl.BlockSpec((NA_QBLK, NA_WIDTH), kv_map(d, False)) for d in range(NA_KBLKS)]
                + [pl.BlockSpec(tblk, kv_map(d, True)) for d in range(NA_KBLKS)]
                + [pl.BlockSpec((None, NA_HEADS, NA_KBLKS * NA_QBLK, NA_QBLK), bias_map)])
    return pl.pallas_call(
        _natten_kernel,
        grid=(B, nblk),
        in_specs=in_specs,
        out_specs=pl.BlockSpec(tblk, lambda b, j: (b * nblk + j, 0, 0)),
        out_shape=jax.ShapeDtypeStruct((B * nblk, NA_WIDTH, NA_QBLK), BF16),
        scratch_shapes=[pltpu.VMEM((2, NA_KBLKS * NA_QBLK, NA_QBLK), F32)],
        compiler_params=_params("parallel", "arbitrary"),
        name="natten",
    )(qt, k, k, k, vt, vt, vt, bias)


def _fft1_kernel(x_ref, m_ref, a_ref):
    kk = m_ref.shape[0]
    n1 = x_ref.shape[0]
    xt = jnp.swapaxes(x_ref[...], 0, 1)
    for i in range(kk):
        res = _dot(m_ref[i], xt[i])
        res = jnp.concatenate([res[:n1], res[n1:]], axis=1).astype(BF16)
        a_ref[:, i] = res.reshape(n1 // kk, kk, 2 * FN_WIDTH)


def _fft2_kernel(a_ref, cs_ref, w_ref, y_ref):
    n2, kk, _ = a_ref.shape
    cs = cs_ref[...]
    at = jnp.swapaxes(a_ref[...], 0, 1)
    g = []
    for i in range(kk):
        r = _dot(cs, at[i])
        g.append(jnp.concatenate([r[:n2, :FN_WIDTH] + r[n2:, FN_WIDTH:],
                                  r[:n2, FN_WIDTH:] - r[n2:, :FN_WIDTH]], axis=1).astype(BF16))
    g = jnp.concatenate(g, axis=0)
    y = _dot(g, w_ref[...]).astype(BF16)
    y_ref[...] = jnp.swapaxes(y.reshape(kk, n2, FN_WIDTH), 0, 1)


def _fourier(fb, ft, B, T):
    n1, n2 = ft['n1'], ft['n2']
    kk = BF16_ROWS
    a = pl.pallas_call(
        _fft1_kernel,
        grid=(B, n2 // kk),
        in_specs=[pl.BlockSpec((None, n1, kk, FN_WIDTH), lambda b, j: (b, 0, j, 0)),
                  pl.BlockSpec((kk, 2 * n1, n1), lambda b, j: (j, 0, 0))],
        out_specs=pl.BlockSpec((None, n1 // kk, kk, kk, 2 * FN_WIDTH), lambda b, j: (b, 0, j, 0, 0)),
        out_shape=jax.ShapeDtypeStruct((B, n1 // kk, n2, kk, 2 * FN_WIDTH), BF16),
        compiler_params=_params("parallel", "arbitrary"),
        name="fft_stage1",
    )(fb.reshape(B, n1, n2, FN_WIDTH), ft['m1'])
    y = pl.pallas_call(
        _fft2_kernel,
        grid=(B, n1 // kk),
        in_specs=[pl.BlockSpec((None, None, n2, kk, 2 * FN_WIDTH), lambda b, j: (b, j, 0, 0, 0)),
                  _full(ft['cs2'].shape), _full(ft['w'].shape)],
        out_specs=pl.BlockSpec((None, n2, kk, FN_WIDTH), lambda b, j: (b, 0, j, 0)),
        out_shape=jax.ShapeDtypeStruct((B, n2, n1, FN_WIDTH), BF16),
        compiler_params=_params("parallel", "arbitrary"),
        name="fft_stage2",
    )(a, ft['cs2'], ft['w'])
    return y.reshape(B * T, FN_WIDTH)


def _mla_kernel(q_ref, k_ref, vt_ref, o_ref, s_ref):
    tk = s_ref.shape[1]
    per = tk // MLA_VT_CHUNK
    nk = vt_ref.shape[0] // per
    nq, _, tq = q_ref.shape
    unroll = min(MLA_UNROLL, nk)
    ones = jnp.ones((MLA_ONES, tk), BF16)

    def scores(i, j):
        start = j * tk if isinstance(j, int) else pl.multiple_of(j * tk, tk)
        return _dot(k_ref[pl.ds(start, tk), :], q_ref[i])

    def update(carry, j, slot):
        m, acc = carry
        st = s_ref[slot]
        m_new = jnp.maximum(m, jnp.max(st, axis=0, keepdims=True))
        alpha = jnp.exp2(m - m_new)
        p = jnp.exp2(st - m_new).astype(BF16)
        vt = jnp.concatenate([vt_ref[j * per + u] for u in range(per)], axis=1)
        vte = jnp.concatenate([vt, ones], axis=0)
        return m_new, alpha * acc + _dot(vte, p)

    nbuf = s_ref.shape[0]
    ahead = nbuf // 2
    last = nq * nk - 1
    for f in range(ahead):
        s_ref[f] = scores(f // nk, f % nk)
    m0 = jnp.full((1, tq), -1e30, F32)
    acc0 = jnp.zeros((MLA_V + MLA_ONES, tq), F32)

    def body(t, carry):
        i = (t * unroll) // nk
        j0 = (t * unroll) % nk
        fresh = j0 == 0
        carry = (jnp.where(fresh, m0, carry[0]), jnp.where(fresh, acc0, carry[1]))
        for c in range(unroll):
            if c + ahead < unroll:
                s_ref[(c + ahead) % nbuf] = scores(i, j0 + c + ahead)
            else:
                nxt = jnp.minimum(t * unroll + c + ahead, last)
                s_ref[(c + ahead) % nbuf] = scores(nxt // nk, nxt % nk)
            carry = update(carry, j0 + c, c % nbuf)

        @pl.when(j0 + unroll == nk)
        def _():
            acc = carry[1]
            o_ref[i] = (acc[:MLA_V] * (1.0 / acc[MLA_V:MLA_V + 1])).astype(o_ref.dtype)

        return carry

    lax.fori_loop(0, nq * nk // unroll, body, (m0, acc0))


def _mla(q, k, vt, B, T):
    nq = T // MLA_TQ
    nk = T // MLA_TK
    unroll = min(MLA_UNROLL, nk)
    nbuf = MLA_SCORE_BUFS if unroll % MLA_SCORE_BUFS == 0 else 2
    assert nk % unroll == 0 and unroll % nbuf == 0
    return pl.pallas_call(
        _mla_kernel,
        grid=(B, MLA_HEADS),
        in_specs=[pl.BlockSpec((nq, MLA_PAD, MLA_TQ), lambda b, h: (b, h, 0)),
                  pl.BlockSpec((T, MLA_PAD), lambda b, h: (b, h)),
                  pl.BlockSpec((T // MLA_VT_CHUNK, MLA_V, MLA_VT_CHUNK), lambda b, h: (b, h, 0))],
        out_specs=pl.BlockSpec((nq, MLA_V, MLA_TQ), lambda b, h: (b, h, 0)),
        out_shape=jax.ShapeDtypeStruct((B * nq, MLA_WIDTH, MLA_TQ), BF16),
        scratch_shapes=[pltpu.VMEM((nbuf, MLA_TK, MLA_TQ), F32)],
        compiler_params=_params("parallel", "arbitrary"),
        name="mla",
    )(q, k, vt)


def _outproj_kernel(oat_ref, ob_ref, oct_ref, od_ref, x_ref, w_ref, g_ref, b_ref, o_ref):
    w = NA_WIDTH
    tm = x_ref.shape[0]
    for c in range(tm // NA_QBLK):
        rows = slice(c * NA_QBLK, (c + 1) * NA_QBLK)
        oc = oct_ref[:, rows].T
        y = (_dot(oat_ref[c].T, w_ref[0:w]) + _dot(ob_ref[rows, :], w_ref[w:2 * w])
             + _dot(oc, w_ref[2 * w:3 * w]) + _dot(od_ref[rows, :], w_ref[3 * w:4 * w]))
        o_ref[rows, :] = _layer_norm_rows(ALPHA * x_ref[rows, :] + y, g_ref[...], b_ref[...])


def _outproj(oa, ob, oct, od, x, lp):
    n = x.shape[0]
    tm = TM_OUT
    assert tm == MLA_TQ
    row = lambda i: (i, 0)
    return pl.pallas_call(
        _outproj_kernel,
        grid=(n // tm,),
        in_specs=[pl.BlockSpec((tm // NA_QBLK, NA_WIDTH, NA_QBLK), lambda i: (i, 0, 0)),
                  pl.BlockSpec((tm, FN_WIDTH), row),
                  pl.BlockSpec((None, MLA_WIDTH, tm), lambda i: (i, 0, 0)), pl.BlockSpec((tm, SGU_WIDTH), row),
                  pl.BlockSpec((tm, D_MODEL), row), _full(lp['w_out'].shape),
                  _full(lp['ln1_g'].shape), _full(lp['ln1_b'].shape)],
        out_specs=pl.BlockSpec((tm, D_MODEL), row),
        out_shape=jax.ShapeDtypeStruct((n, D_MODEL), F32),
        compiler_params=_params("parallel"),
        name="outproj",
    )(oa, ob, oct, od, x, lp['w_out'], lp['ln1_g'], lp['ln1_b'])


def _ffn_kernel(nt, x_ref, xp_ref, xn_ref, wup_ref, cw_ref, cb_ref, wd_ref, g_ref, b_ref, o_ref,
                xe_ref, act_ref):
    i = pl.program_id(0)
    tm = x_ref.shape[0]
    cf = FF_CHUNK
    first = (i % nt) == 0
    last = (i % nt) == nt - 1
    xe_ref[0:HALO] = jnp.where(first, 0.0, xp_ref[...]).astype(BF16)
    xe_ref[HALO:HALO + tm] = x_ref[...].astype(BF16)
    xe_ref[HALO + tm:] = jnp.where(last, 0.0, xn_ref[...]).astype(BF16)
    xe = xe_ref[...]

    rows = slice(HALO, HALO + tm)

    def conv(h, cols):
        prev = pltpu.roll(h, 1, axis=0)
        nxt = pltpu.roll(h, h.shape[0] - 1, axis=0)
        return (prev[rows] * cw_ref[0:1, cols] + h[rows] * cw_ref[1:2, cols]
                + nxt[rows] * cw_ref[2:3, cols] + cb_ref[:, cols])

    for c in range(D_FF // cf):
        gate = slice(c * cf, (c + 1) * cf)
        val = slice(D_FF + c * cf, D_FF + (c + 1) * cf)
        act = _gelu_tanh_x2(conv(_dot(xe, wup_ref[:, gate]), gate)) * conv(_dot(xe, wup_ref[:, val]), val)
        act_ref[:, gate] = act.astype(BF16)

    y = _dot(act_ref[...], wd_ref[...])
    o_ref[...] = _layer_norm_rows(ALPHA * x_ref[...] + y, g_ref[...], b_ref[...])


def _resident(shape):
    nd = len(shape)
    return pl.BlockSpec(shape, lambda *_: (0,) * nd, pipeline_mode=pl.Buffered(1))


def _ffn(x, lp, T):
    n = x.shape[0]
    tm = TM_FFN
    nt = T // tm
    assert T % tm == 0 and D_FF % FF_CHUNK == 0
    hb = tm // HALO
    nhb = n // HALO
    row = lambda i: (i, 0)
    return pl.pallas_call(
        functools.partial(_ffn_kernel, nt),
        grid=(n // tm,),
        in_specs=[pl.BlockSpec((tm, D_MODEL), row),
                  pl.BlockSpec((HALO, D_MODEL), lambda i: (jnp.maximum(i * hb - 1, 0), 0)),
                  pl.BlockSpec((HALO, D_MODEL), lambda i: (jnp.minimum((i + 1) * hb, nhb - 1), 0)),
                  _resident(lp['w_up'].shape), _resident(lp['conv_w'].shape), _resident(lp['conv_b'].shape),
                  _resident(lp['w_down'].shape), _resident(lp['ln2_g'].shape), _resident(lp['ln2_b'].shape)],
        out_specs=pl.BlockSpec((tm, D_MODEL), row),
        out_shape=jax.ShapeDtypeStruct((n, D_MODEL), F32),
        scratch_shapes=[pltpu.VMEM((tm + 2 * HALO, D_MODEL), BF16),
                        pltpu.VMEM((tm, D_FF), BF16)],
        compiler_params=_params("parallel"),
        name="conv_ffn",
    )(x, x, x, lp['w_up'], lp['conv_w'], lp['conv_b'], lp['w_down'], lp['ln2_g'], lp['ln2_b'])


def _natten_bias(rpb):
    nr, nc = 2 * NA_ROWS - 1, 2 * NA_COLS - 1
    krows = NA_KBLKS * NA_QROWS
    col = np.arange(GRID_W)
    ci = np.clip(col[None, :] - col[:, None] + NA_COLS - 1, 0, nc - 1)
    onehot_c = (ci.reshape(-1)[None, :] == np.arange(nc)[:, None]).astype(np.float32)
    cs = np.clip(col - NA_COLS // 2, 0, GRID_W - NA_COLS)
    dc = col[None, :] - cs[:, None]
    valid_c = (dc >= 0) & (dc < NA_COLS)
    qa = np.arange(NA_QROWS)
    kr = np.arange(krows)
    onehot_r, valid_r = [], []
    for kind in range(3):
        q_row = (0, NA_QROWS, 2 * NA_QROWS)[kind] + qa
        r_start = (np.zeros_like(qa), qa, np.full_like(qa, NA_QROWS))[kind]
        dr = kr[None, :] - r_start[:, None]
        valid_r.append((dr >= 0) & (dr < NA_ROWS))
        ri = np.clip(kr[None, :] - q_row[:, None] + NA_ROWS - 1, 0, nr - 1)
        onehot_r.append((ri.reshape(-1)[:, None] == np.arange(nr)[None, :]).astype(np.float32))
    onehot_r = np.stack(onehot_r)
    valid = np.stack(valid_r)[:, :, None, :, None] & valid_c[None, None, :, None, :]
    by_col = jnp.einsum('hrc,cx->hrx', rpb, onehot_c, precision=HI)
    full = jnp.einsum('kpr,hrx->khpx', onehot_r, by_col, precision=HI)
    full = full.reshape(3, NA_HEADS, NA_QROWS, krows, GRID_W, GRID_W).transpose(0, 1, 3, 5, 2, 4)
    valid = valid.transpose(0, 3, 4, 1, 2)
    full = jnp.where(valid[:, None], full * LOG2_E, -1e30)
    return full.reshape(3, NA_HEADS, krows * GRID_W, NA_QBLK).astype(F32)


def _prep_layer(l, emb_ln_g, emb_ln_b, w_in, na_rpb, mla_q_g, w_uq, mla_kv_g, w_ukv, sgu_ln_g, sgu_ln_b,
                sgu_w, sgu_b, w_out, ln1_g, ln1_b, w_up, conv_w, conv_b, w_down, ln2_g, ln2_b):
    wi = w_in[l]
    o_fb = 3 * NA_WIDTH
    o_cq = o_fb + FN_WIDTH
    o_ckv = o_cq + MLA_Q_LORA
    o_kr = o_ckv + MLA_KV_LORA
    o_sg = o_kr + MLA_ROPE
    half = MLA_ROPE // 2
    swap = np.concatenate([np.arange(half, MLA_ROPE), np.arange(half)])
    w_kr = wi[:, o_kr:o_sg]
    w_ckv = jnp.concatenate([wi[:, o_ckv:o_kr], w_kr, w_kr[:, swap],
                             jnp.zeros((D_MODEL, LANES - 2 * MLA_ROPE), F32)], axis=1)

    uq = w_uq[l]
    pad = jnp.zeros((MLA_Q_LORA, MLA_HEADS, MLA_PAD - MLA_NOPE - MLA_ROPE), F32)
    w_a = jnp.concatenate([uq, pad], axis=2).reshape(MLA_Q_LORA, MLA_HEADS * MLA_PAD)
    ukv = w_ukv[l]
    w_kn = jnp.concatenate([ukv[:, :, :MLA_NOPE],
                            jnp.zeros((MLA_KV_LORA, MLA_HEADS, MLA_PAD - MLA_NOPE), F32)],
                           axis=2).reshape(MLA_KV_LORA, MLA_HEADS * MLA_PAD)
    w_vt = ukv[:, :, MLA_NOPE:].reshape(MLA_KV_LORA, MLA_WIDTH).T
    w_s = jnp.concatenate([sgu_w[l][g] for g in range(SGU_HEADS)], axis=1)
    gmat =np.kron(np.eye(SGU_HEADS), np.full((SGU_HEAD_DIM, SGU_HEAD_DIM), 1.0 / SGU_HEAD_DIM)).astype(np.float32)
    s_b = jnp.repeat(sgu_b[l].T, SGU_HEAD_DIM, axis=1)
    r1 = lambda a: a.reshape(1, -1).astype(F32)
    half_val = jnp.concatenate([jnp.ones((1, D_FF), F32), jnp.full((1, D_FF), 0.5, F32)], axis=1)
    return dict(
        eg=r1(emb_ln_g), eb=r1(emb_ln_b),
        w_nq=wi[:, :NA_WIDTH].T.astype(BF16), w_nk=wi[:, NA_WIDTH:2 * NA_WIDTH].astype(BF16),
        w_nv=wi[:, 2 * NA_WIDTH:o_fb].T.astype(BF16), w_fb=wi[:, o_fb:o_cq].astype(BF16),
        w_cq=wi[:, o_cq:o_ckv].astype(BF16), w_ckv=w_ckv.astype(BF16), w_sg=wi[:, o_sg:].astype(BF16),
        q_g=r1(mla_q_g[l]), w_a=w_a.T.astype(BF16),
        kv_g=r1(mla_kv_g[l]), w_kn=w_kn.astype(BF16), w_vt=w_vt.astype(BF16),
        gmat=jnp.asarray(gmat, BF16),
        sl_g=r1(sgu_ln_g[l]), sl_b=r1(sgu_ln_b[l]), w_s=w_s.astype(BF16), s_b=s_b.astype(F32),
        na_bias=_natten_bias(na_rpb[l]),
        w_out=w_out[l].astype(BF16), ln1_g=r1(ln1_g[l]), ln1_b=r1(ln1_b[l]),
        w_up=w_up[l].astype(BF16), conv_w=conv_w[l].astype(F32) * half_val, conv_b=r1(conv_b[l]) * half_val,
        w_down=w_down[l].astype(BF16), ln2_g=r1(ln2_g[l]), ln2_b=r1(ln2_b[l]),
    )


def _rope_tables(T):
    inv_freq = ROPE_THETA ** (-np.arange(0, MLA_ROPE, 2, dtype=np.float64) / MLA_ROPE)
    ang = np.arange(T, dtype=np.float64)[:, None] * inv_freq[None, :]
    cos, sin = np.cos(ang), np.sin(ang)
    cos2 = np.concatenate([cos, cos], axis=1)
    sin2 = np.concatenate([-sin, sin], axis=1)
    scale = (MLA_NOPE + MLA_ROPE) ** -0.5 * math.log2(math.e)
    zq = np.zeros((T, MLA_PAD - MLA_NOPE - MLA_ROPE))
    ca = np.concatenate([np.full((T, MLA_NOPE), scale), scale * cos2, zq], axis=1)
    cb = np.concatenate([np.zeros((T, MLA_NOPE)), scale * sin2, zq], axis=1)
    tk = np.concatenate([cos2, sin2, np.zeros((T, LANES - 2 * MLA_ROPE))], axis=1)
    f32 = lambda a: jnp.asarray(np.ascontiguousarray(a), F32)
    return dict(ca=f32(ca.T), cb=f32(cb.T), tk=f32(tk))


def _dft_tables(T):
    n1 = 1 << (int(math.log2(T)) // 2)
    n2 = T // n1

    def cs(num, den):
        ang = (num % den).astype(np.float64) * (2.0 * np.pi / den)
        return np.cos(ang), np.sin(ang)

    def table(x):
        return jnp.asarray(x.astype(np.float32)).astype(BF16)

    t2 = np.arange(n2, dtype=np.int64)[:, None, None]
    k1 = np.arange(n1, dtype=np.int64)[None, :, None]
    t1 = np.arange(n1, dtype=np.int64)[None, None, :]
    c, s = cs(k1 * (t1 * n2 + t2), T)
    m1 = table(np.concatenate([c, -s], axis=1))
    k2 = np.arange(n2, dtype=np.int64)
    c2, s2 = cs(k2[:, None] * k2[None, :], n2)
    cs2 = table(np.concatenate([c2, s2], axis=0))
    cg = np.arange(FN_GROUP_DIM, dtype=np.int64)
    cc, sc = cs(cg[:, None] * cg[None, :], FN_GROUP_DIM)
    norm = 1.0 / math.sqrt(T * FN_GROUP_DIM)
    eye = np.eye(FN_GROUPS)
    w = table(np.concatenate([np.kron(eye, cc), np.kron(eye, sc)], axis=0) * norm)
    return dict(n1=n1, n2=n2, m1=m1, cs2=cs2, w=w)


def _trunk(x, layers):
    B, T, _ = x.shape
    tabs = _rope_tables(T)
    ft = _dft_tables(T)
    h = x.reshape(B * T, D_MODEL)
    for l, lp in enumerate(layers):
        outs = _inproj(h, lp, tabs, l == 0, T)
        if l == 0:
            h = outs[0]
            outs = outs[1:]
        natq, natk, natv, fb, q, k, vt, od = outs
        oa = _natten(natq, natk, natv, lp['na_bias'], B, T)
        ob = _fourier(fb, ft, B, T)
        oct = _mla(q, k, vt, B, T)
        h1 = _outproj(oa, ob, oct, od, h, lp)
        h = _ffn(h1, lp, T)
    return h.reshape(B, T, D_MODEL)


def kernel(x_prompt, x_sample, emb_ln_g, emb_ln_b, w_in, na_rpb, mla_q_g, w_uq, mla_kv_g, w_ukv, sgu_ln_g,
           sgu_ln_b, sgu_w, sgu_b, w_out, ln1_g, ln1_b, w_up, conv_w, conv_b, w_down, ln2_g, ln2_b):
    layers = [_prep_layer(l, emb_ln_g, emb_ln_b, w_in, na_rpb, mla_q_g, w_uq, mla_kv_g, w_ukv, sgu_ln_g,
                          sgu_ln_b, sgu_w, sgu_b, w_out, ln1_g, ln1_b, w_up, conv_w, conv_b, w_down,
                          ln2_g, ln2_b) for l in range(DEPTH)]
    return (_trunk(x_prompt, layers), _trunk(x_sample, layers))
```

```python
import functools
import math

import numpy as np
import jax
import jax.numpy as jnp
from jax import lax
from jax.experimental import pallas as pl
from jax.experimental.pallas import tpu as pltpu

F32 = jnp.float32
BF16 = jnp.bfloat16

D_MODEL = 1024
DEPTH = 2
GRID_W = 64
HEAD_DIM = 64
NA_HEADS = 4
NA_ROWS = 8
NA_COLS = 16
NA_WIDTH = NA_HEADS * HEAD_DIM
FN_GROUPS = 4
FN_GROUP_DIM = 64
FN_WIDTH = FN_GROUPS * FN_GROUP_DIM
MLA_HEADS = 4
MLA_Q_LORA = 256
MLA_KV_LORA = 128
MLA_NOPE = 64
MLA_ROPE = 32
MLA_V = 64
MLA_WIDTH = MLA_HEADS * MLA_V
ROPE_THETA = 10000.0
SGU_HEADS = 4
SGU_HEAD_DIM = 64
SGU_WIDTH = SGU_HEADS * SGU_HEAD_DIM
SGU_CHUNK = 128
D_FF = 2816
ALPHA = (2 * DEPTH) ** 0.25
LN_EPS = 1e-5
RMS_EPS = 1e-6

LANES = 128
SUBLANES = 8
BF16_ROWS = 16
VMEM_LIMIT = 56 * 1024 * 1024

TM_IN = 1024
TM_OUT = 1024
TM_FFN = 1024
FF_CHUNK = 256
NA_QROWS = 4
NA_QBLK = NA_QROWS * GRID_W
NA_KBLKS = 3
MLA_PAD = 128
MLA_TQ = 512
MLA_TK = 512
MLA_VT_CHUNK = 512
MLA_ONES = 16
MLA_UNROLL = 16
MLA_SCORE_BUFS = 4
HALO = SUBLANES

LOG2_E = math.log2(math.e)
HI = lax.Precision.HIGHEST
NT_DIMS = (((1,), (1,)), ((), ()))


def _dot(a, b):
    return jnp.dot(a, b, preferred_element_type=F32)


def _dot_nt(a, b):
    return lax.dot_general(a, b, NT_DIMS, preferred_element_type=F32)


def _layer_norm_rows(x, g, b):
    mu = jnp.mean(x, axis=-1, keepdims=True)
    xc = x - mu
    var = jnp.mean(xc * xc, axis=-1, keepdims=True)
    return xc * lax.rsqrt(var + LN_EPS) * g + b


GELU_C = 0.7978845608028654
GELU_K = 0.044715


def _gelu_tanh_x2(x):
    t = jnp.tanh(x * (GELU_C + (GELU_C * GELU_K) * (x * x)))
    return x + x * t


def _gelu_tanh(x):
    return 0.5 * _gelu_tanh_x2(x)


def _split_dot(v, m):
    hi = v.astype(BF16)
    lo = (v - hi.astype(F32)).astype(BF16)
    return _dot(hi, m) + _dot(lo, m)


def _params(*sem):
    return pltpu.CompilerParams(dimension_semantics=sem, vmem_limit_bytes=VMEM_LIMIT)


def _full(shape):
    nd = len(shape)
    return pl.BlockSpec(shape, lambda *_: (0,) * nd)


def _inproj_kernel(apply_ln, x_ref, eg_ref, eb_ref, wnq_ref, wnk_ref, wnv_ref, wfb_ref, wcq_ref, wckv_ref,
                   wsg_ref, qg_ref, wa_ref, kvg_ref, wkn_ref, wvt_ref, ca_ref, cb_ref,
                   tk_ref, gmat_ref, slg_ref, slb_ref, ws_ref, sb_ref, *out_refs):
    if apply_ln:
        xn_ref, natq_ref, natk_ref, natv_ref, fb_ref, q_ref, k_ref, vt_ref, od_ref = out_refs
    else:
        natq_ref, natk_ref, natv_ref, fb_ref, q_ref, k_ref, vt_ref, od_ref = out_refs
    x = x_ref[...]
    if apply_ln:
        x = _layer_norm_rows(x, eg_ref[...], eb_ref[...])
        xn_ref[...] = x
    xb = x.astype(BF16)
    tm = x.shape[0]

    cq = _dot(xb, wcq_ref[...])
    ckv_kr = _dot(xb, wckv_ref[...])
    sg = _dot(xb, wsg_ref[...])

    natq = (_dot_nt(wnq_ref[...], xb) * (HEAD_DIM ** -0.5 * LOG2_E)).astype(BF16)
    natv = _dot_nt(wnv_ref[...], xb).astype(BF16)
    for c in range(tm // NA_QBLK):
        natq_ref[c] = natq[:, c * NA_QBLK:(c + 1) * NA_QBLK]
        natv_ref[c] = natv[:, c * NA_QBLK:(c + 1) * NA_QBLK]
    natk_ref[...] = _dot(xb, wnk_ref[...]).astype(BF16)

    fb_ref[...] = _dot(xb, wfb_ref[...]).astype(BF16)

    cqn = (cq * lax.rsqrt(jnp.mean(cq * cq, axis=-1, keepdims=True) + RMS_EPS) * qg_ref[...]).astype(BF16)
    qa = _dot_nt(wa_ref[...], cqn)
    ca = ca_ref[...]
    cb = cb_ref[MLA_NOPE:MLA_NOPE + MLA_ROPE]
    half = MLA_ROPE // 2
    for h in range(MLA_HEADS):
        base = h * MLA_PAD
        qh = qa[base:base + MLA_PAD] * ca
        lo = qa[base + MLA_NOPE:base + MLA_NOPE + half]
        hi = qa[base + MLA_NOPE + half:base + MLA_NOPE + MLA_ROPE]
        rope = qh[MLA_NOPE:MLA_NOPE + MLA_ROPE] + jnp.concatenate([hi, lo], axis=0) * cb
        qh = jnp.concatenate([qh[:MLA_NOPE], rope, qh[MLA_NOPE + MLA_ROPE:]], axis=0).astype(BF16)
        for c in range(tm // MLA_TQ):
            q_ref[c, base:base + MLA_PAD, :] = qh[:, c * MLA_TQ:(c + 1) * MLA_TQ]

    ckv = ckv_kr[:, :MLA_KV_LORA]
    ckvn = (ckv * lax.rsqrt(jnp.mean(ckv * ckv, axis=-1, keepdims=True) + RMS_EPS) * kvg_ref[...]).astype(BF16)
    kr = ckv_kr[:, MLA_KV_LORA:] * tk_ref[...]
    kr = pltpu.roll(kr, MLA_NOPE, axis=1) + pltpu.roll(kr, MLA_NOPE - MLA_ROPE, axis=1)
    lane = lax.broadcasted_iota(jnp.int32, (1, MLA_PAD), 1)
    kr = jnp.where((lane >= MLA_NOPE) & (lane < MLA_NOPE + MLA_ROPE), kr, 0.0)
    kn = _dot(ckvn, wkn_ref[...])
    for h in range(MLA_HEADS):
        sl = slice(h * MLA_PAD, (h + 1) * MLA_PAD)
        k_ref[:, sl] = (kn[:, sl] + kr).astype(BF16)
    vt = _dot_nt(wvt_ref[...], ckvn)
    tkc = vt_ref.shape[2]
    for c in range(tm // tkc):
        vt_ref[c] = vt[:, c * tkc:(c + 1) * tkc].astype(BF16)

    sg = _gelu_tanh(sg)
    u = sg[:, :SGU_WIDTH]
    v = sg[:, SGU_WIDTH:]
    gmat = gmat_ref[...]
    mu = _split_dot(v, gmat)
    vc = v - mu
    var = _split_dot(vc * vc, gmat)
    vn = (vc * lax.rsqrt(var + LN_EPS) * slg_ref[...] + slb_ref[...]).astype(BF16)
    head = lax.broadcasted_iota(jnp.int32, (1, SGU_WIDTH), 1) // SGU_HEAD_DIM
    zero = jnp.zeros((), BF16)
    for c in range(tm // SGU_CHUNK):
        rows = slice(c * SGU_CHUNK, (c + 1) * SGU_CHUNK)
        vch = vn[rows]
        stacked = jnp.concatenate([jnp.where(head == g, vch, zero) for g in range(SGU_HEADS)], axis=0)
        mixed = sb_ref[...] + _dot(ws_ref[...], stacked)
        od_ref[rows, :] = (u[rows] * mixed).astype(BF16)


def _inproj(x, lp, tabs, apply_ln, T):
    n = x.shape[0]
    tm = TM_IN
    assert T % tm == 0 and tm % MLA_TQ == 0 and tm % MLA_VT_CHUNK == 0 and tm % SGU_CHUNK == 0 and tm % NA_QBLK == 0
    nt = T // tm
    row = lambda i: (i, 0)
    tab = lambda i: (i % nt, 0)
    weights = [lp['eg'], lp['eb'], lp['w_nq'], lp['w_nk'], lp['w_nv'], lp['w_fb'], lp['w_cq'], lp['w_ckv'], lp['w_sg'],
               lp['q_g'], lp['w_a'], lp['kv_g'], lp['w_kn'], lp['w_vt']]
    tables = [tabs['ca'], tabs['cb'], tabs['tk']]
    tail = [lp['gmat'], lp['sl_g'], lp['sl_b'], lp['w_s'], lp['s_b']]
    tab_t = lambda i: (0, i % nt)
    in_specs = ([pl.BlockSpec((tm, D_MODEL), row)] + [_full(w.shape) for w in weights]
                + [pl.BlockSpec((MLA_PAD, tm), tab_t), pl.BlockSpec((MLA_PAD, tm), tab_t),
                   pl.BlockSpec((tm, LANES), tab)] + [_full(w.shape) for w in tail])
    nat_t = jax.ShapeDtypeStruct((n // NA_QBLK, NA_WIDTH, NA_QBLK), BF16)
    nat_t_spec = pl.BlockSpec((tm // NA_QBLK, NA_WIDTH, NA_QBLK), lambda i: (i, 0, 0))
    out_shape = [nat_t, jax.ShapeDtypeStruct((n, NA_WIDTH), BF16), nat_t,
                 jax.ShapeDtypeStruct((n, FN_WIDTH), BF16),
                 jax.ShapeDtypeStruct((n // MLA_TQ, MLA_HEADS * MLA_PAD, MLA_TQ), BF16),
                 jax.ShapeDtypeStruct((n, MLA_HEADS * MLA_PAD), BF16),
                 jax.ShapeDtypeStruct((n // MLA_VT_CHUNK, MLA_WIDTH, MLA_VT_CHUNK), BF16),
                 jax.ShapeDtypeStruct((n, SGU_WIDTH), BF16)]
    out_specs = [nat_t_spec, pl.BlockSpec((tm, NA_WIDTH), row), nat_t_spec,
                 pl.BlockSpec((tm, FN_WIDTH), row),
                 pl.BlockSpec((tm // MLA_TQ, MLA_HEADS * MLA_PAD, MLA_TQ), lambda i: (i, 0, 0)),
                 pl.BlockSpec((tm, MLA_HEADS * MLA_PAD), row),
                 pl.BlockSpec((tm // MLA_VT_CHUNK, MLA_WIDTH, MLA_VT_CHUNK), lambda i: (i, 0, 0)),
                 pl.BlockSpec((tm, SGU_WIDTH), row)]
    if apply_ln:
        out_shape = [jax.ShapeDtypeStruct((n, D_MODEL), F32)] + out_shape
        out_specs = [pl.BlockSpec((tm, D_MODEL), row)] + out_specs
    return pl.pallas_call(
        functools.partial(_inproj_kernel, apply_ln),
        grid=(n // tm,),
        in_specs=in_specs,
        out_specs=out_specs,
        out_shape=out_shape,
        compiler_params=_params("parallel"),
        name="inproj_ln" if apply_ln else "inproj",
    )(x, *weights, *tables, *tail)


def _natten_kernel(qt_ref, k0_ref, k1_ref, k2_ref, vt0_ref, vt1_ref, vt2_ref, bias_ref, o_ref, s_ref):
    qt = qt_ref[...]
    k_refs = (k0_ref, k1_ref, k2_ref)
    vt_refs = (vt0_ref, vt1_ref, vt2_ref)
    row_head = lax.broadcasted_iota(jnp.int32, (NA_WIDTH, 1), 0) // HEAD_DIM
    zero = jnp.zeros((), BF16)
    ones = jnp.ones((BF16_ROWS, NA_QBLK), BF16)

    def scores(h, slot):
        qh = jnp.where(row_head == h, qt, zero)
        for d in range(NA_KBLKS):
            rows = slice(d * NA_QBLK, (d + 1) * NA_QBLK)
            s_ref[slot, rows, :] = _dot(k_refs[d][...], qh) + bias_ref[h, rows, :]

    scores(0, 0)
    for h in range(NA_HEADS):
        if h + 1 < NA_HEADS:
            scores(h + 1, (h + 1) % 2)
        s = s_ref[h % 2]
        p = jnp.exp2(s - jnp.max(s, axis=0, keepdims=True)).astype(BF16)
        ch = slice(h * HEAD_DIM, (h + 1) * HEAD_DIM)
        acc = None
        for d in range(NA_KBLKS):
            vte = jnp.concatenate([vt_refs[d][ch, :], ones], axis=0)
            part = _dot(vte, p[d * NA_QBLK:(d + 1) * NA_QBLK])
            acc = part if acc is None else acc + part
        o_ref[ch, :] = (acc[:HEAD_DIM] * (1.0 / acc[HEAD_DIM:HEAD_DIM + 1])).astype(BF16)


def _natten(qt, k, vt, bias, B, T):
    nblk = T // NA_QBLK

    def kv_map(d, rank3):
        def f(b, j):
            base = jnp.clip(j - 1, 0, nblk - NA_KBLKS)
            return (b * nblk + base + d, 0, 0) if rank3 else (b * nblk + base + d, 0)
        return f

    def bias_map(b, j):
        typ = jnp.where(j == 0, 0, jnp.where(j == nblk - 1, 2, 1))
        return (typ, 0, 0, 0)

    tblk = (None, NA_WIDTH, NA_QBLK)
    in_specs = ([pl.BlockSpec(tblk, lambda b, j: (b * nblk + j, 0, 0))]
                + [pl.BlockSpec((NA_QBLK, NA_WIDTH), kv_map(d, False)) for d in range(NA_KBLKS)]
                + [pl.BlockSpec(tblk, kv_map(d, True)) for d in range(NA_KBLKS)]
                + [pl.BlockSpec((None, NA_HEADS, NA_KBLKS * NA_QBLK, NA_QBLK), bias_map)])
    return pl.pallas_call(
        _natten_kernel,
        grid=(B, nblk),
        in_specs=in_specs,
        out_specs=pl.BlockSpec(tblk, lambda b, j: (b * nblk + j, 0, 0)),
        out_shape=jax.ShapeDtypeStruct((B * nblk, NA_WIDTH, NA_QBLK), BF16),
        scratch_shapes=[pltpu.VMEM((2, NA_KBLKS * NA_QBLK, NA_QBLK), F32)],
        compiler_params=_params("parallel", "arbitrary"),
        name="natten",
    )(qt, k, k, k, vt, vt, vt, bias)


def _fft1_kernel(x_ref, m_ref, a_ref):
    kk = m_ref.shape[0]
    n1 = x_ref.shape[0]
    xt = jnp.swapaxes(x_ref[...], 0, 1)
    for i in range(kk):
        res = _dot(m_ref[i], xt[i])
        res = jnp.concatenate([res[:n1], res[n1:]], axis=1).astype(BF16)
        a_ref[:, i] = res.reshape(n1 // kk, kk, 2 * FN_WIDTH)


def _fft2_kernel(a_ref, cs_ref, w_ref, y_ref):
    n2, kk, _ = a_ref.shape
    cs = cs_ref[...]
    at = jnp.swapaxes(a_ref[...], 0, 1)
    g = []
    for i in range(kk):
        r = _dot(cs, at[i])
        g.append(jnp.concatenate([r[:n2, :FN_WIDTH] + r[n2:, FN_WIDTH:],
                                  r[:n2, FN_WIDTH:] - r[n2:, :FN_WIDTH]], axis=1).astype(BF16))
    g = jnp.concatenate(g, axis=0)
    y = _dot(g, w_ref[...]).astype(BF16)
    y_ref[...] = jnp.swapaxes(y.reshape(kk, n2, FN_WIDTH), 0, 1)


def _fourier(fb, ft, B, T):
    n1, n2 = ft['n1'], ft['n2']
    kk = BF16_ROWS
    a = pl.pallas_call(
        _fft1_kernel,
        grid=(B, n2 // kk),
        in_specs=[pl.BlockSpec((None, n1, kk, FN_WIDTH), lambda b, j: (b, 0, j, 0)),
                  pl.BlockSpec((kk, 2 * n1, n1), lambda b, j: (j, 0, 0))],
        out_specs=pl.BlockSpec((None, n1 // kk, kk, kk, 2 * FN_WIDTH), lambda b, j: (b, 0, j, 0, 0)),
        out_shape=jax.ShapeDtypeStruct((B, n1 // kk, n2, kk, 2 * FN_WIDTH), BF16),
        compiler_params=_params("parallel", "arbitrary"),
        name="fft_stage1",
    )(fb.reshape(B, n1, n2, FN_WIDTH), ft['m1'])
    y = pl.pallas_call(
        _fft2_kernel,
        grid=(B, n1 // kk),
        in_specs=[pl.BlockSpec((None, None, n2, kk, 2 * FN_WIDTH), lambda b, j: (b, j, 0, 0, 0)),
                  _full(ft['cs2'].shape), _full(ft['w'].shape)],
        out_specs=pl.BlockSpec((None, n2, kk, FN_WIDTH), lambda b, j: (b, 0, j, 0)),
        out_shape=jax.ShapeDtypeStruct((B, n2, n1, FN_WIDTH), BF16),
        compiler_params=_params("parallel", "arbitrary"),
        name="fft_stage2",
    )(a, ft['cs2'], ft['w'])
    return y.reshape(B * T, FN_WIDTH)


def _mla_kernel(q_ref, k_ref, vt_ref, o_ref, s_ref):
    tk = s_ref.shape[1]
    per = tk // MLA_VT_CHUNK
    nk = vt_ref.shape[0] // per
    nq, _, tq = q_ref.shape
    unroll = min(MLA_UNROLL, nk)
    ones = jnp.ones((MLA_ONES, tk), BF16)

    def scores(i, j):
        start = j * tk if isinstance(j, int) else pl.multiple_of(j * tk, tk)
        return _dot(k_ref[pl.ds(start, tk), :], q_ref[i])

    def update(carry, j, slot):
        m, acc = carry
        st = s_ref[slot]
        m_new = jnp.maximum(m, jnp.max(st, axis=0, keepdims=True))
        alpha = jnp.exp2(m - m_new)
        p = jnp.exp2(st - m_new).astype(BF16)
        vt = jnp.concatenate([vt_ref[j * per + u] for u in range(per)], axis=1)
        vte = jnp.concatenate([vt, ones], axis=0)
        return m_new, alpha * acc + _dot(vte, p)

    nbuf = s_ref.shape[0]
    ahead = nbuf // 2
    last = nq * nk - 1
    for f in range(ahead):
        s_ref[f] = scores(f // nk, f % nk)
    m0 = jnp.full((1, tq), -1e30, F32)
    acc0 = jnp.zeros((MLA_V + MLA_ONES, tq), F32)

    def body(t, carry):
        i = (t * unroll) // nk
        j0 = (t * unroll) % nk
        fresh = j0 == 0
        carry = (jnp.where(fresh, m0, carry[0]), jnp.where(fresh, acc0, carry[1]))
        for c in range(unroll):
            if c + ahead < unroll:
                s_ref[(c + ahead) % nbuf] = scores(i, j0 + c + ahead)
            else:
                nxt = jnp.minimum(t * unroll + c + ahead, last)
                s_ref[(c + ahead) % nbuf] = scores(nxt // nk, nxt % nk)
            carry = update(carry, j0 + c, c % nbuf)

        @pl.when(j0 + unroll == nk)
        def _():
            acc = carry[1]
            o_ref[i] = (acc[:MLA_V] * (1.0 / acc[MLA_V:MLA_V + 1])).astype(o_ref.dtype)

        return carry

    lax.fori_loop(0, nq * nk // unroll, body, (m0, acc0))


def _mla(q, k, vt, B, T):
    nq = T // MLA_TQ
    nk = T // MLA_TK
    unroll = min(MLA_UNROLL, nk)
    nbuf = MLA_SCORE_BUFS if unroll % MLA_SCORE_BUFS == 0 else 2
    assert nk % unroll == 0 and unroll % nbuf == 0
    return pl.pallas_call(
        _mla_kernel,
        grid=(B, MLA_HEADS),
        in_specs=[pl.BlockSpec((nq, MLA_PAD, MLA_TQ), lambda b, h: (b, h, 0)),
                  pl.BlockSpec((T, MLA_PAD), lambda b, h: (b, h)),
                  pl.BlockSpec((T // MLA_VT_CHUNK, MLA_V, MLA_VT_CHUNK), lambda b, h: (b, h, 0))],
        out_specs=pl.BlockSpec((nq, MLA_V, MLA_TQ), lambda b, h: (b, h, 0)),
        out_shape=jax.ShapeDtypeStruct((B * nq, MLA_WIDTH, MLA_TQ), BF16),
        scratch_shapes=[pltpu.VMEM((nbuf, MLA_TK, MLA_TQ), F32)],
        compiler_params=_params("parallel", "arbitrary"),
        name="mla",
    )(q, k, vt)


def _outproj_kernel(oat_ref, ob_ref, oct_ref, od_ref, x_ref, w_ref, g_ref, b_ref, o_ref):
    w = NA_WIDTH
    tm = x_ref.shape[0]
    for c in range(tm // NA_QBLK):
        rows = slice(c * NA_QBLK, (c + 1) * NA_QBLK)
        oc = oct_ref[(c * NA_QBLK) // MLA_TQ, :, pl.ds((c * NA_QBLK) % MLA_TQ, NA_QBLK)].T
        y = (_dot(oat_ref[c].T, w_ref[0:w]) + _dot(ob_ref[rows, :], w_ref[w:2 * w])
             + _dot(oc, w_ref[2 * w:3 * w]) + _dot(od_ref[rows, :], w_ref[3 * w:4 * w]))
        o_ref[rows, :] = _layer_norm_rows(ALPHA * x_ref[rows, :] + y, g_ref[...], b_ref[...])


def _outproj(oa, ob, oct, od, x, lp):
    n = x.shape[0]
    tm = TM_OUT
    assert tm % MLA_TQ == 0 and MLA_TQ % NA_QBLK == 0
    row = lambda i: (i, 0)
    return pl.pallas_call(
        _outproj_kernel,
        grid=(n // tm,),
        in_specs=[pl.BlockSpec((tm // NA_QBLK, NA_WIDTH, NA_QBLK), lambda i: (i, 0, 0)),
                  pl.BlockSpec((tm, FN_WIDTH), row),
                  pl.BlockSpec((tm // MLA_TQ, MLA_WIDTH, MLA_TQ), lambda i: (i, 0, 0)),
                  pl.BlockSpec((tm, SGU_WIDTH), row),
                  pl.BlockSpec((tm, D_MODEL), row), _full(lp['w_out'].shape),
                  _full(lp['ln1_g'].shape), _full(lp['ln1_b'].shape)],
        out_specs=pl.BlockSpec((tm, D_MODEL), row),
        out_shape=jax.ShapeDtypeStruct((n, D_MODEL), F32),
        compiler_params=_params("parallel"),
        name="outproj",
    )(oa, ob, oct, od, x, lp['w_out'], lp['ln1_g'], lp['ln1_b'])


def _ffn_kernel(nt, x_ref, xp_ref, xn_ref, wup_ref, cw_ref, cb_ref, wd_ref, g_ref, b_ref, o_ref,
                xe_ref, act_ref):
    i = pl.program_id(0)
    tm = x_ref.shape[0]
    cf = FF_CHUNK
    first = (i % nt) == 0
    last = (i % nt) == nt - 1
    xe_ref[0:HALO] = jnp.where(first, 0.0, xp_ref[...]).astype(BF16)
    xe_ref[HALO:HALO + tm] = x_ref[...].astype(BF16)
    xe_ref[HALO + tm:] = jnp.where(last, 0.0, xn_ref[...]).astype(BF16)
    xe = xe_ref[...]

    rows = slice(HALO, HALO + tm)

    def conv(h, cols):
        prev = pltpu.roll(h, 1, axis=0)
        nxt = pltpu.roll(h, h.shape[0] - 1, axis=0)
        return (prev[rows] * cw_ref[0:1, cols] + h[rows] * cw_ref[1:2, cols]
                + nxt[rows] * cw_ref[2:3, cols] + cb_ref[:, cols])

    for c in range(D_FF // cf):
        gate = slice(c * cf, (c + 1) * cf)
        val = slice(D_FF + c * cf, D_FF + (c + 1) * cf)
        act = _gelu_tanh_x2(conv(_dot(xe, wup_ref[:, gate]), gate)) * conv(_dot(xe, wup_ref[:, val]), val)
        act_ref[:, gate] = act.astype(BF16)

    y = _dot(act_ref[...], wd_ref[...])
    o_ref[...] = _layer_norm_rows(ALPHA * x_ref[...] + y, g_ref[...], b_ref[...])


def _resident(shape):
    nd = len(shape)
    return pl.BlockSpec(shape, lambda *_: (0,) * nd, pipeline_mode=pl.Buffered(1))


def _ffn(x, lp, T):
    n = x.shape[0]
    tm = TM_FFN
    nt = T // tm
    assert T % tm == 0 and D_FF % FF_CHUNK == 0
    hb = tm // HALO
    nhb = n // HALO
    row = lambda i: (i, 0)
    return pl.pallas_call(
        functools.partial(_ffn_kernel, nt),
        grid=(n // tm,),
        in_specs=[pl.BlockSpec((tm, D_MODEL), row),
                  pl.BlockSpec((HALO, D_MODEL), lambda i: (jnp.maximum(i * hb - 1, 0), 0)),
                  pl.BlockSpec((HALO, D_MODEL), lambda i: (jnp.minimum((i + 1) * hb, nhb - 1), 0)),
                  _resident(lp['w_up'].shape), _resident(lp['conv_w'].shape), _resident(lp['conv_b'].shape),
                  _resident(lp['w_down'].shape), _resident(lp['ln2_g'].shape), _resident(lp['ln2_b'].shape)],
        out_specs=pl.BlockSpec((tm, D_MODEL), row),
        out_shape=jax.ShapeDtypeStruct((n, D_MODEL), F32),
        scratch_shapes=[pltpu.VMEM((tm + 2 * HALO, D_MODEL), BF16),
                        pltpu.VMEM((tm, D_FF), BF16)],
        compiler_params=_params("parallel"),
        name="conv_ffn",
    )(x, x, x, lp['w_up'], lp['conv_w'], lp['conv_b'], lp['w_down'], lp['ln2_g'], lp['ln2_b'])


def _natten_bias(rpb):
    nr, nc = 2 * NA_ROWS - 1, 2 * NA_COLS - 1
    krows = NA_KBLKS * NA_QROWS
    col = np.arange(GRID_W)
    ci = np.clip(col[None, :] - col[:, None] + NA_COLS - 1, 0, nc - 1)
    onehot_c = (ci.reshape(-1)[None, :] == np.arange(nc)[:, None]).astype(np.float32)
    cs = np.clip(col - NA_COLS // 2, 0, GRID_W - NA_COLS)
    dc = col[None, :] - cs[:, None]
    valid_c = (dc >= 0) & (dc < NA_COLS)
    qa = np.arange(NA_QROWS)
    kr = np.arange(krows)
    onehot_r, valid_r = [], []
    for kind in range(3):
        q_row = (0, NA_QROWS, 2 * NA_QROWS)[kind] + qa
        r_start = (np.zeros_like(qa), qa, np.full_like(qa, NA_QROWS))[kind]
        dr = kr[None, :] - r_start[:, None]
        valid_r.append((dr >= 0) & (dr < NA_ROWS))
        ri = np.clip(kr[None, :] - q_row[:, None] + NA_ROWS - 1, 0, nr - 1)
        onehot_r.append((ri.reshape(-1)[:, None] == np.arange(nr)[None, :]).astype(np.float32))
    onehot_r = np.stack(onehot_r)
    valid = np.stack(valid_r)[:, :, None, :, None] & valid_c[None, None, :, None, :]
    by_col = jnp.einsum('hrc,cx->hrx', rpb, onehot_c, precision=HI)
    full = jnp.einsum('kpr,hrx->khpx', onehot_r, by_col, precision=HI)
    full = full.reshape(3, NA_HEADS, NA_QROWS, krows, GRID_W, GRID_W).transpose(0, 1, 3, 5, 2, 4)
    valid = valid.transpose(0, 3, 4, 1, 2)
    full = jnp.where(valid[:, None], full * LOG2_E, -1e30)
    return full.reshape(3, NA_HEADS, krows * GRID_W, NA_QBLK).astype(F32)


def _prep_layer(l, emb_ln_g, emb_ln_b, w_in, na_rpb, mla_q_g, w_uq, mla_kv_g, w_ukv, sgu_ln_g, sgu_ln_b,
                sgu_w, sgu_b, w_out, ln1_g, ln1_b, w_up, conv_w, conv_b, w_down, ln2_g, ln2_b):
    wi = w_in[l]
    o_fb = 3 * NA_WIDTH
    o_cq = o_fb + FN_WIDTH
    o_ckv = o_cq + MLA_Q_LORA
    o_kr = o_ckv + MLA_KV_LORA
    o_sg = o_kr + MLA_ROPE
    half = MLA_ROPE // 2
    swap = np.concatenate([np.arange(half, MLA_ROPE), np.arange(half)])
    w_kr = wi[:, o_kr:o_sg]
    w_ckv = jnp.concatenate([wi[:, o_ckv:o_kr], w_kr, w_kr[:, swap],
                             jnp.zeros((D_MODEL, LANES - 2 * MLA_ROPE), F32)], axis=1)

    uq = w_uq[l]
    pad = jnp.zeros((MLA_Q_LORA, MLA_HEADS, MLA_PAD - MLA_NOPE - MLA_ROPE), F32)
    w_a = jnp.concatenate([uq, pad], axis=2).reshape(MLA_Q_LORA, MLA_HEADS * MLA_PAD)
    ukv = w_ukv[l]
    w_kn = jnp.concatenate([ukv[:, :, :MLA_NOPE],
                            jnp.zeros((MLA_KV_LORA, MLA_HEADS, MLA_PAD - MLA_NOPE), F32)],
                           axis=2).reshape(MLA_KV_LORA, MLA_HEADS * MLA_PAD)
    w_vt = ukv[:, :, MLA_NOPE:].reshape(MLA_KV_LORA, MLA_WIDTH).T
    w_s = jnp.concatenate([sgu_w[l][g] for g in range(SGU_HEADS)], axis=1)
    gmat =np.kron(np.eye(SGU_HEADS), np.full((SGU_HEAD_DIM, SGU_HEAD_DIM), 1.0 / SGU_HEAD_DIM)).astype(np.float32)
    s_b = jnp.repeat(sgu_b[l].T, SGU_HEAD_DIM, axis=1)
    r1 = lambda a: a.reshape(1, -1).astype(F32)
    half_val = jnp.concatenate([jnp.ones((1, D_FF), F32), jnp.full((1, D_FF), 0.5, F32)], axis=1)
    return dict(
        eg=r1(emb_ln_g), eb=r1(emb_ln_b),
        w_nq=wi[:, :NA_WIDTH].T.astype(BF16), w_nk=wi[:, NA_WIDTH:2 * NA_WIDTH].astype(BF16),
        w_nv=wi[:, 2 * NA_WIDTH:o_fb].T.astype(BF16), w_fb=wi[:, o_fb:o_cq].astype(BF16),
        w_cq=wi[:, o_cq:o_ckv].astype(BF16), w_ckv=w_ckv.astype(BF16), w_sg=wi[:, o_sg:].astype(BF16),
        q_g=r1(mla_q_g[l]), w_a=w_a.T.astype(BF16),
        kv_g=r1(mla_kv_g[l]), w_kn=w_kn.astype(BF16), w_vt=w_vt.astype(BF16),
        gmat=jnp.asarray(gmat, BF16),
        sl_g=r1(sgu_ln_g[l]), sl_b=r1(sgu_ln_b[l]), w_s=w_s.astype(BF16), s_b=s_b.astype(F32),
        na_bias=_natten_bias(na_rpb[l]),
        w_out=w_out[l].astype(BF16), ln1_g=r1(ln1_g[l]), ln1_b=r1(ln1_b[l]),
        w_up=w_up[l].astype(BF16), conv_w=conv_w[l].astype(F32) * half_val, conv_b=r1(conv_b[l]) * half_val,
        w_down=w_down[l].astype(BF16), ln2_g=r1(ln2_g[l]), ln2_b=r1(ln2_b[l]),
    )


def _rope_tables(T):
    inv_freq = ROPE_THETA ** (-np.arange(0, MLA_ROPE, 2, dtype=np.float64) / MLA_ROPE)
    ang = np.arange(T, dtype=np.float64)[:, None] * inv_freq[None, :]
    cos, sin = np.cos(ang), np.sin(ang)
    cos2 = np.concatenate([cos, cos], axis=1)
    sin2 = np.concatenate([-sin, sin], axis=1)
    scale = (MLA_NOPE + MLA_ROPE) ** -0.5 * math.log2(math.e)
    zq = np.zeros((T, MLA_PAD - MLA_NOPE - MLA_ROPE))
    ca = np.concatenate([np.full((T, MLA_NOPE), scale), scale * cos2, zq], axis=1)
    cb = np.concatenate([np.zeros((T, MLA_NOPE)), scale * sin2, zq], axis=1)
    tk = np.concatenate([cos2, sin2, np.zeros((T, LANES - 2 * MLA_ROPE))], axis=1)
    f32 = lambda a: jnp.asarray(np.ascontiguousarray(a), F32)
    return dict(ca=f32(ca.T), cb=f32(cb.T), tk=f32(tk))


def _dft_tables(T):
    n1 = 1 << (int(math.log2(T)) // 2)
    n2 = T // n1

    def cs(num, den):
        ang = (num % den).astype(np.float64) * (2.0 * np.pi / den)
        return np.cos(ang), np.sin(ang)

    def table(x):
        return jnp.asarray(x.astype(np.float32)).astype(BF16)

    t2 = np.arange(n2, dtype=np.int64)[:, None, None]
    k1 = np.arange(n1, dtype=np.int64)[None, :, None]
    t1 = np.arange(n1, dtype=np.int64)[None, None, :]
    c, s = cs(k1 * (t1 * n2 + t2), T)
    m1 = table(np.concatenate([c, -s], axis=1))
    k2 = np.arange(n2, dtype=np.int64)
    c2, s2 = cs(k2[:, None] * k2[None, :], n2)
    cs2 = table(np.concatenate([c2, s2], axis=0))
    cg = np.arange(FN_GROUP_DIM, dtype=np.int64)
    cc, sc = cs(cg[:, None] * cg[None, :], FN_GROUP_DIM)
    norm = 1.0 / math.sqrt(T * FN_GROUP_DIM)
    eye = np.eye(FN_GROUPS)
    w = table(np.concatenate([np.kron(eye, cc), np.kron(eye, sc)], axis=0) * norm)
    return dict(n1=n1, n2=n2, m1=m1, cs2=cs2, w=w)


def _trunk(x, layers):
    B, T, _ = x.shape
    tabs = _rope_tables(T)
    ft = _dft_tables(T)
    h = x.reshape(B * T, D_MODEL)
    for l, lp in enumerate(layers):
        outs = _inproj(h, lp, tabs, l == 0, T)
        if l == 0:
            h = outs[0]
            outs = outs[1:]
        natq, natk, natv, fb, q, k, vt, od = outs
        oa = _natten(natq, natk, natv, lp['na_bias'], B, T)
        ob = _fourier(fb, ft, B, T)
        oct = _mla(q, k, vt, B, T)
        h1 = _outproj(oa, ob, oct, od, h, lp)
        h = _ffn(h1, lp, T)
    return h.reshape(B, T, D_MODEL)


def kernel(x_prompt, x_sample, emb_ln_g, emb_ln_b, w_in, na_rpb, mla_q_g, w_uq, mla_kv_g, w_ukv, sgu_ln_g,
           sgu_ln_b, sgu_w, sgu_b, w_out, ln1_g, ln1_b, w_up, conv_w, conv_b, w_down, ln2_g, ln2_b):
    layers = [_prep_layer(l, emb_ln_g, emb_ln_b, w_in, na_rpb, mla_q_g, w_uq, mla_kv_g, w_ukv, sgu_ln_g,
                          sgu_ln_b, sgu_w, sgu_b, w_out, ln1_g, ln1_b, w_up, conv_w, conv_b, w_down,
                          ln2_g, ln2_b) for l in range(DEPTH)]
    return (_trunk(x_prompt, layers), _trunk(x_sample, layers))
```

```python
import functools
import math

import numpy as np
import jax
import jax.numpy as jnp
from jax import lax
from jax.experimental import pallas as pl
from jax.experimental.pallas import tpu as pltpu

F32 = jnp.float32
BF16 = jnp.bfloat16

D_MODEL = 1024
DEPTH = 2
GRID_W = 64
HEAD_DIM = 64
NA_HEADS = 4
NA_ROWS = 8
NA_COLS = 16
NA_WIDTH = NA_HEADS * HEAD_DIM
FN_GROUPS = 4
FN_GROUP_DIM = 64
FN_WIDTH = FN_GROUPS * FN_GROUP_DIM
MLA_HEADS = 4
MLA_Q_LORA = 256
MLA_KV_LORA = 128
MLA_NOPE = 64
MLA_ROPE = 32
MLA_V = 64
MLA_WIDTH = MLA_HEADS * MLA_V
ROPE_THETA = 10000.0
SGU_HEADS = 4
SGU_HEAD_DIM = 64
SGU_WIDTH = SGU_HEADS * SGU_HEAD_DIM
SGU_CHUNK = 128
D_FF = 2816
ALPHA = (2 * DEPTH) ** 0.25
LN_EPS = 1e-5
RMS_EPS = 1e-6

LANES = 128
SUBLANES = 8
BF16_ROWS = 16
VMEM_LIMIT = 56 * 1024 * 1024

TM_IN = 1024
TM_OUT = 1024
TM_FFN = 1024
FF_CHUNK = 256
NA_QROWS = 4
NA_QBLK = NA_QROWS * GRID_W
NA_KBLKS = 3
MLA_PAD = 128
MLA_TQ = 512
MLA_TK = 512
MLA_VT_CHUNK = 512
MLA_ONES = 16
MLA_UNROLL = 16
MLA_SCORE_BUFS = 4
HALO = SUBLANES

LOG2_E = math.log2(math.e)
HI = lax.Precision.HIGHEST
NT_DIMS = (((1,), (1,)), ((), ()))


def _dot(a, b):
    return jnp.dot(a, b, preferred_element_type=F32)


def _dot_nt(a, b):
    return lax.dot_general(a, b, NT_DIMS, preferred_element_type=F32)


def _layer_norm_rows(x, g, b):
    mu = jnp.mean(x, axis=-1, keepdims=True)
    xc = x - mu
    var = jnp.mean(xc * xc, axis=-1, keepdims=True)
    return xc * lax.rsqrt(var + LN_EPS) * g + b


GELU_C = 0.7978845608028654
GELU_K = 0.044715


def _gelu_tanh_x2(x):
    t = jnp.tanh(x * (GELU_C + (GELU_C * GELU_K) * (x * x)))
    return x + x * t


def _gelu_tanh(x):
    return 0.5 * _gelu_tanh_x2(x)


def _split_dot(v, m):
    hi = v.astype(BF16)
    lo = (v - hi.astype(F32)).astype(BF16)
    return _dot(hi, m) + _dot(lo, m)


def _params(*sem):
    return pltpu.CompilerParams(dimension_semantics=sem, vmem_limit_bytes=VMEM_LIMIT)


def _full(shape):
    nd = len(shape)
    return pl.BlockSpec(shape, lambda *_: (0,) * nd)


def _inproj_kernel(apply_ln, x_ref, eg_ref, eb_ref, wnq_ref, wnk_ref, wnv_ref, wfb_ref, wcq_ref, wckv_ref,
                   wsg_ref, qg_ref, wa_ref, kvg_ref, wkn_ref, wvt_ref, ca_ref, cb_ref,
                   tk_ref, gmat_ref, slg_ref, slb_ref, ws_ref, sb_ref, *out_refs):
    if apply_ln:
        xn_ref, natq_ref, natk_ref, natv_ref, fb_ref, q_ref, k_ref, vt_ref, od_ref = out_refs
    else:
        natq_ref, natk_ref, natv_ref, fb_ref, q_ref, k_ref, vt_ref, od_ref = out_refs
    x = x_ref[...]
    if apply_ln:
        x = _layer_norm_rows(x, eg_ref[...], eb_ref[...])
        xn_ref[...] = x
    xb = x.astype(BF16)
    tm = x.shape[0]

    cq = _dot(xb, wcq_ref[...])
    ckv_kr = _dot(xb, wckv_ref[...])
    sg = _dot(xb, wsg_ref[...])

    natq = (_dot_nt(wnq_ref[...], xb) * (HEAD_DIM ** -0.5 * LOG2_E)).astype(BF16)
    natv = _dot_nt(wnv_ref[...], xb).astype(BF16)
    for c in range(tm // NA_QBLK):
        natq_ref[c] = natq[:, c * NA_QBLK:(c + 1) * NA_QBLK]
        natv_ref[c] = natv[:, c * NA_QBLK:(c + 1) * NA_QBLK]
    natk_ref[...] = _dot(xb, wnk_ref[...]).astype(BF16)

    fb_ref[...] = _dot(xb, wfb_ref[...]).astype(BF16)

    cqn = (cq * lax.rsqrt(jnp.mean(cq * cq, axis=-1, keepdims=True) + RMS_EPS) * qg_ref[...]).astype(BF16)
    qa = _dot_nt(wa_ref[...], cqn)
    ca = ca_ref[...]
    cb = cb_ref[MLA_NOPE:MLA_NOPE + MLA_ROPE]
    half = MLA_ROPE // 2
    for h in range(MLA_HEADS):
        base = h * MLA_PAD
        qh = qa[base:base + MLA_PAD] * ca
        lo = qa[base + MLA_NOPE:base + MLA_NOPE + half]
        hi = qa[base + MLA_NOPE + half:base + MLA_NOPE + MLA_ROPE]
        rope = qh[MLA_NOPE:MLA_NOPE + MLA_ROPE] + jnp.concatenate([hi, lo], axis=0) * cb
        qh = jnp.concatenate([qh[:MLA_NOPE], rope, qh[MLA_NOPE + MLA_ROPE:]], axis=0).astype(BF16)
        for c in range(tm // MLA_TQ):
            q_ref[c, base:base + MLA_PAD, :] = qh[:, c * MLA_TQ:(c + 1) * MLA_TQ]

    ckv = ckv_kr[:, :MLA_KV_LORA]
    ckvn = (ckv * lax.rsqrt(jnp.mean(ckv * ckv, axis=-1, keepdims=True) + RMS_EPS) * kvg_ref[...]).astype(BF16)
    kr = ckv_kr[:, MLA_KV_LORA:] * tk_ref[...]
    kr = pltpu.roll(kr, MLA_NOPE, axis=1) + pltpu.roll(kr, MLA_NOPE - MLA_ROPE, axis=1)
    lane = lax.broadcasted_iota(jnp.int32, (1, MLA_PAD), 1)
    kr = jnp.where((lane >= MLA_NOPE) & (lane < MLA_NOPE + MLA_ROPE), kr, 0.0)
    kn = _dot(ckvn, wkn_ref[...])
    for h in range(MLA_HEADS):
        sl = slice(h * MLA_PAD, (h + 1) * MLA_PAD)
        k_ref[:, sl] = (kn[:, sl] + kr).astype(BF16)
    vt = _dot_nt(wvt_ref[...], ckvn)
    tkc = vt_ref.shape[2]
    for c in range(tm // tkc):
        vt_ref[c] = vt[:, c * tkc:(c + 1) * tkc].astype(BF16)

    sg = _gelu_tanh(sg)
    u = sg[:, :SGU_WIDTH]
    v = sg[:, SGU_WIDTH:]
    gmat = gmat_ref[...]
    mu = _split_dot(v, gmat)
    vc = v - mu
    var = _split_dot(vc * vc, gmat)
    vn = (vc * lax.rsqrt(var + LN_EPS) * slg_ref[...] + slb_ref[...]).astype(BF16)
    head = lax.broadcasted_iota(jnp.int32, (1, SGU_WIDTH), 1) // SGU_HEAD_DIM
    zero = jnp.zeros((), BF16)
    for c in range(tm // SGU_CHUNK):
        rows = slice(c * SGU_CHUNK, (c + 1) * SGU_CHUNK)
        vch = vn[rows]
        stacked = jnp.concatenate([jnp.where(head == g, vch, zero) for g in range(SGU_HEADS)], axis=0)
        mixed = sb_ref[...] + _dot(ws_ref[...], stacked)
        od_ref[rows, :] = (u[rows] * mixed).astype(BF16)


def _inproj(x, lp, tabs, apply_ln, T):
    n = x.shape[0]
    tm = TM_IN
    assert T % tm == 0 and tm % MLA_TQ == 0 and tm % MLA_VT_CHUNK == 0 and tm % SGU_CHUNK == 0 and tm % NA_QBLK == 0
    nt = T // tm
    row = lambda i: (i, 0)
    tab = lambda i: (i % nt, 0)
    weights = [lp['eg'], lp['eb'], lp['w_nq'], lp['w_nk'], lp['w_nv'], lp['w_fb'], lp['w_cq'], lp['w_ckv'], lp['w_sg'],
               lp['q_g'], lp['w_a'], lp['kv_g'], lp['w_kn'], lp['w_vt']]
    tables = [tabs['ca'], tabs['cb'], tabs['tk']]
    tail = [lp['gmat'], lp['sl_g'], lp['sl_b'], lp['w_s'], lp['s_b']]
    tab_t = lambda i: (0, i % nt)
    in_specs = ([pl.BlockSpec((tm, D_MODEL), row)] + [_full(w.shape) for w in weights]
                + [pl.BlockSpec((MLA_PAD, tm), tab_t), pl.BlockSpec((MLA_PAD, tm), tab_t),
                   pl.BlockSpec((tm, LANES), tab)] + [_full(w.shape) for w in tail])
    nat_t = jax.ShapeDtypeStruct((n // NA_QBLK, NA_WIDTH, NA_QBLK), BF16)
    nat_t_spec = pl.BlockSpec((tm // NA_QBLK, NA_WIDTH, NA_QBLK), lambda i: (i, 0, 0))
    out_shape = [nat_t, jax.ShapeDtypeStruct((n, NA_WIDTH), BF16), nat_t,
                 jax.ShapeDtypeStruct((n, FN_WIDTH), BF16),
                 jax.ShapeDtypeStruct((n // MLA_TQ, MLA_HEADS * MLA_PAD, MLA_TQ), BF16),
                 jax.ShapeDtypeStruct((n, MLA_HEADS * MLA_PAD), BF16),
                 jax.ShapeDtypeStruct((n // MLA_VT_CHUNK, MLA_WIDTH, MLA_VT_CHUNK), BF16),
                 jax.ShapeDtypeStruct((n, SGU_WIDTH), BF16)]
    out_specs = [nat_t_spec, pl.BlockSpec((tm, NA_WIDTH), row), nat_t_spec,
                 pl.BlockSpec((tm, FN_WIDTH), row),
                 pl.BlockSpec((tm // MLA_TQ, MLA_HEADS * MLA_PAD, MLA_TQ), lambda i: (i, 0, 0)),
                 pl.BlockSpec((tm, MLA_HEADS * MLA_PAD), row),
                 pl.BlockSpec((tm // MLA_VT_CHUNK, MLA_WIDTH, MLA_VT_CHUNK), lambda i: (i, 0, 0)),
                 pl.BlockSpec((tm, SGU_WIDTH), row)]
    if apply_ln:
        out_shape = [jax.ShapeDtypeStruct((n, D_MODEL), F32)] + out_shape
        out_specs = [pl.BlockSpec((tm, D_MODEL), row)] + out_specs
    return pl.pallas_call(
        functools.partial(_inproj_kernel, apply_ln),
        grid=(n // tm,),
        in_specs=in_specs,
        out_specs=out_specs,
        out_shape=out_shape,
        compiler_params=_params("parallel"),
        name="inproj_ln" if apply_ln else "inproj",
    )(x, *weights, *tables, *tail)


def _natten_kernel(qt_ref, k0_ref, k1_ref, k2_ref, k3_ref, vt0_ref, vt1_ref, vt2_ref, vt3_ref,
                   bias_a_ref, bias_b_ref, o_ref, s_ref):
    k_refs = (k0_ref, k1_ref, k2_ref, k3_ref)
    vt_refs = (vt0_ref, vt1_ref, vt2_ref, vt3_ref)
    row_head = lax.broadcasted_iota(jnp.int32, (NA_WIDTH, 1), 0) // HEAD_DIM
    zero = jnp.zeros((), BF16)
    ones = jnp.ones((BF16_ROWS, NA_QBLK), BF16)

    def block(qi, off, bias_ref):
        qt = qt_ref[qi]

        def scores(h, slot):
            qh = jnp.where(row_head == h, qt, zero)
            for d in range(NA_KBLKS):
                rows = slice(d * NA_QBLK, (d + 1) * NA_QBLK)
                s_ref[slot, rows, :] = _dot(k_refs[off + d][...], qh) + bias_ref[h, rows, :]

        scores(0, 0)
        for h in range(NA_HEADS):
            if h + 1 < NA_HEADS:
                scores(h + 1, (h + 1) % 2)
            s = s_ref[h % 2]
            p = jnp.exp2(s - jnp.max(s, axis=0, keepdims=True)).astype(BF16)
            ch = slice(h * HEAD_DIM, (h + 1) * HEAD_DIM)
            acc = None
            for d in range(NA_KBLKS):
                vte = jnp.concatenate([vt_refs[off + d][ch, :], ones], axis=0)
                part = _dot(vte, p[d * NA_QBLK:(d + 1) * NA_QBLK])
                acc = part if acc is None else acc + part
            o_ref[qi, ch, :] = (acc[:HEAD_DIM] * (1.0 / acc[HEAD_DIM:HEAD_DIM + 1])).astype(BF16)

    i = pl.program_id(1)
    last = pl.num_programs(1) - 1
    for cond, off_a, off_b in ((i == 0, 0, 0), ((i > 0) & (i < last), 0, 1), (i == last, 1, 1)):
        @pl.when(cond)
        def _(off_a=off_a, off_b=off_b):
            block(0, off_a, bias_a_ref)
            block(1, off_b, bias_b_ref)


def _natten(qt, k, vt, bias, B, T):
    nblk = T // NA_QBLK
    npair = nblk // 2
    nwin = NA_KBLKS + 1
    assert nblk % 2 == 0 and nblk >= nwin

    def kv_map(d, rank3):
        def f(b, i):
            blk = b * nblk + jnp.clip(2 * i - 1, 0, nblk - nwin) + d
            return (blk, 0, 0) if rank3 else (blk, 0)
        return f

    bias_blk = (None, NA_HEADS, NA_KBLKS * NA_QBLK, NA_QBLK)
    tblk = (None, NA_WIDTH, NA_QBLK)
    pair = (2, NA_WIDTH, NA_QBLK)
    in_specs = ([pl.BlockSpec(pair, lambda b, i: (b * npair + i, 0, 0))]
                + [pl.BlockSpec((NA_QBLK, NA_WIDTH), kv_map(d, False)) for d in range(nwin)]
                + [pl.BlockSpec(tblk, kv_map(d, True)) for d in range(nwin)]
                + [pl.BlockSpec(bias_blk, lambda b, i: (jnp.where(i == 0, 0, 1), 0, 0, 0)),
                   pl.BlockSpec(bias_blk, lambda b, i: (jnp.where(i == npair - 1, 2, 1), 0, 0, 0))])
    return pl.pallas_call(
        _natten_kernel,
        grid=(B, npair),
        in_specs=in_specs,
        out_specs=pl.BlockSpec(pair, lambda b, i: (b * npair + i, 0, 0)),
        out_shape=jax.ShapeDtypeStruct((B * nblk, NA_WIDTH, NA_QBLK), BF16),
        scratch_shapes=[pltpu.VMEM((2, NA_KBLKS * NA_QBLK, NA_QBLK), F32)],
        compiler_params=_params("parallel", "arbitrary"),
        name="natten",
    )(qt, k, k, k, k, vt, vt, vt, vt, bias, bias)


def _fft1_kernel(x_ref, m_ref, a_ref):
    kk = m_ref.shape[0]
    n1 = x_ref.shape[0]
    xt = jnp.swapaxes(x_ref[...], 0, 1)
    for i in range(kk):
        res = _dot(m_ref[i], xt[i])
        res = jnp.concatenate([res[:n1], res[n1:]], axis=1).astype(BF16)
        a_ref[:, i] = res.reshape(n1 // kk, kk, 2 * FN_WIDTH)


def _fft2_kernel(a_ref, cs_ref, w_ref, y_ref):
    n2, kk, _ = a_ref.shape
    cs = cs_ref[...]
    at = jnp.swapaxes(a_ref[...], 0, 1)
    g = []
    for i in range(kk):
        r = _dot(cs, at[i])
        g.append(jnp.concatenate([r[:n2, :FN_WIDTH] + r[n2:, FN_WIDTH:],
                                  r[:n2, FN_WIDTH:] - r[n2:, :FN_WIDTH]], axis=1).astype(BF16))
    g = jnp.concatenate(g, axis=0)
    y = _dot(g, w_ref[...]).astype(BF16)
    y_ref[...] = jnp.swapaxes(y.reshape(kk, n2, FN_WIDTH), 0, 1)


def _fourier(fb, ft, B, T):
    n1, n2 = ft['n1'], ft['n2']
    kk = BF16_ROWS
    a = pl.pallas_call(
        _fft1_kernel,
        grid=(B, n2 // kk),
        in_specs=[pl.BlockSpec((None, n1, kk, FN_WIDTH), lambda b, j: (b, 0, j, 0)),
                  pl.BlockSpec((kk, 2 * n1, n1), lambda b, j: (j, 0, 0))],
        out_specs=pl.BlockSpec((None, n1 // kk, kk, kk, 2 * FN_WIDTH), lambda b, j: (b, 0, j, 0, 0)),
        out_shape=jax.ShapeDtypeStruct((B, n1 // kk, n2, kk, 2 * FN_WIDTH), BF16),
        compiler_params=_params("parallel", "arbitrary"),
        name="fft_stage1",
    )(fb.reshape(B, n1, n2, FN_WIDTH), ft['m1'])
    y = pl.pallas_call(
        _fft2_kernel,
        grid=(B, n1 // kk),
        in_specs=[pl.BlockSpec((None, None, n2, kk, 2 * FN_WIDTH), lambda b, j: (b, j, 0, 0, 0)),
                  _full(ft['cs2'].shape), _full(ft['w'].shape)],
        out_specs=pl.BlockSpec((None, n2, kk, FN_WIDTH), lambda b, j: (b, 0, j, 0)),
        out_shape=jax.ShapeDtypeStruct((B, n2, n1, FN_WIDTH), BF16),
        compiler_params=_params("parallel", "arbitrary"),
        name="fft_stage2",
    )(a, ft['cs2'], ft['w'])
    return y.reshape(B * T, FN_WIDTH)


def _mla_kernel(q_ref, k_ref, vt_ref, o_ref, s_ref):
    tk = s_ref.shape[1]
    per = tk // MLA_VT_CHUNK
    nk = vt_ref.shape[0] // per
    nq, _, tq = q_ref.shape
    unroll = min(MLA_UNROLL, nk)
    ones = jnp.ones((MLA_ONES, tk), BF16)

    def scores(i, j):
        start = j * tk if isinstance(j, int) else pl.multiple_of(j * tk, tk)
        return _dot(k_ref[pl.ds(start, tk), :], q_ref[i])

    def update(carry, j, slot):
        m, acc = carry
        st = s_ref[slot]
        m_new = jnp.maximum(m, jnp.max(st, axis=0, keepdims=True))
        alpha = jnp.exp2(m - m_new)
        p = jnp.exp2(st - m_new).astype(BF16)
        vt = jnp.concatenate([vt_ref[j * per + u] for u in range(per)], axis=1)
        vte = jnp.concatenate([vt, ones], axis=0)
        return m_new, alpha * acc + _dot(vte, p)

    nbuf = s_ref.shape[0]
    ahead = nbuf // 2
    last = nq * nk - 1
    for f in range(ahead):
        s_ref[f] = scores(f // nk, f % nk)
    m0 = jnp.full((1, tq), -1e30, F32)
    acc0 = jnp.zeros((MLA_V + MLA_ONES, tq), F32)

    def body(t, carry):
        i = (t * unroll) // nk
        j0 = (t * unroll) % nk
        fresh = j0 == 0
        carry = (jnp.where(fresh, m0, carry[0]), jnp.where(fresh, acc0, carry[1]))
        for c in range(unroll):
            if c + ahead < unroll:
                s_ref[(c + ahead) % nbuf] = scores(i, j0 + c + ahead)
            else:
                nxt = jnp.minimum(t * unroll + c + ahead, last)
                s_ref[(c + ahead) % nbuf] = scores(nxt // nk, nxt % nk)
            carry = update(carry, j0 + c, c % nbuf)

        @pl.when(j0 + unroll == nk)
        def _():
            acc = carry[1]
            o_ref[i] = (acc[:MLA_V] * (1.0 / acc[MLA_V:MLA_V + 1])).astype(o_ref.dtype)

        return carry

    lax.fori_loop(0, nq * nk // unroll, body, (m0, acc0))


def _mla(q, k, vt, B, T):
    nq = T // MLA_TQ
    nk = T // MLA_TK
    unroll = min(MLA_UNROLL, nk)
    nbuf = MLA_SCORE_BUFS if unroll % MLA_SCORE_BUFS == 0 else 2
    assert nk % unroll == 0 and unroll % nbuf == 0
    return pl.pallas_call(
        _mla_kernel,
        grid=(B, MLA_HEADS),
        in_specs=[pl.BlockSpec((nq, MLA_PAD, MLA_TQ), lambda b, h: (b, h, 0)),
                  pl.BlockSpec((T, MLA_PAD), lambda b, h: (b, h)),
                  pl.BlockSpec((T // MLA_VT_CHUNK, MLA_V, MLA_VT_CHUNK), lambda b, h: (b, h, 0))],
        out_specs=pl.BlockSpec((nq, MLA_V, MLA_TQ), lambda b, h: (b, h, 0)),
        out_shape=jax.ShapeDtypeStruct((B * nq, MLA_WIDTH, MLA_TQ), BF16),
        scratch_shapes=[pltpu.VMEM((nbuf, MLA_TK, MLA_TQ), F32)],
        compiler_params=_params("parallel", "arbitrary"),
        name="mla",
    )(q, k, vt)


def _outproj_kernel(oat_ref, ob_ref, oct_ref, od_ref, x_ref, w_ref, g_ref, b_ref, o_ref):
    w = NA_WIDTH
    tm = x_ref.shape[0]
    for c in range(tm // NA_QBLK):
        rows = slice(c * NA_QBLK, (c + 1) * NA_QBLK)
        oc = oct_ref[(c * NA_QBLK) // MLA_TQ, :, pl.ds((c * NA_QBLK) % MLA_TQ, NA_QBLK)].T
        y = (_dot(oat_ref[c].T, w_ref[0:w]) + _dot(ob_ref[rows, :], w_ref[w:2 * w])
             + _dot(oc, w_ref[2 * w:3 * w]) + _dot(od_ref[rows, :], w_ref[3 * w:4 * w]))
        o_ref[rows, :] = _layer_norm_rows(ALPHA * x_ref[rows, :] + y, g_ref[...], b_ref[...])


def _outproj(oa, ob, oct, od, x, lp):
    n = x.shape[0]
    tm = TM_OUT
    assert tm % MLA_TQ == 0 and MLA_TQ % NA_QBLK == 0
    row = lambda i: (i, 0)
    return pl.pallas_call(
        _outproj_kernel,
        grid=(n // tm,),
        in_specs=[pl.BlockSpec((tm // NA_QBLK, NA_WIDTH, NA_QBLK), lambda i: (i, 0, 0)),
                  pl.BlockSpec((tm, FN_WIDTH), row),
                  pl.BlockSpec((tm // MLA_TQ, MLA_WIDTH, MLA_TQ), lambda i: (i, 0, 0)),
                  pl.BlockSpec((tm, SGU_WIDTH), row),
                  pl.BlockSpec((tm, D_MODEL), row), _full(lp['w_out'].shape),
                  _full(lp['ln1_g'].shape), _full(lp['ln1_b'].shape)],
        out_specs=pl.BlockSpec((tm, D_MODEL), row),
        out_shape=jax.ShapeDtypeStruct((n, D_MODEL), F32),
        compiler_params=_params("parallel"),
        name="outproj",
    )(oa, ob, oct, od, x, lp['w_out'], lp['ln1_g'], lp['ln1_b'])


def _ffn_kernel(nt, x_ref, xp_ref, xn_ref, wup_ref, cw_ref, cb_ref, wd_ref, g_ref, b_ref, o_ref,
                xe_ref, act_ref):
    i = pl.program_id(0)
    tm = x_ref.shape[0]
    cf = FF_CHUNK
    first = (i % nt) == 0
    last = (i % nt) == nt - 1
    xe_ref[0:HALO] = jnp.where(first, 0.0, xp_ref[...]).astype(BF16)
    xe_ref[HALO:HALO + tm] = x_ref[...].astype(BF16)
    xe_ref[HALO + tm:] = jnp.where(last, 0.0, xn_ref[...]).astype(BF16)
    xe = xe_ref[...]

    rows = slice(HALO, HALO + tm)

    def conv(h, cols):
        prev = pltpu.roll(h, 1, axis=0)
        nxt = pltpu.roll(h, h.shape[0] - 1, axis=0)
        return (prev[rows] * cw_ref[0:1, cols] + h[rows] * cw_ref[1:2, cols]
                + nxt[rows] * cw_ref[2:3, cols] + cb_ref[:, cols])

    for c in range(D_FF // cf):
        gate = slice(c * cf, (c + 1) * cf)
        val = slice(D_FF + c * cf, D_FF + (c + 1) * cf)
        act = _gelu_tanh_x2(conv(_dot(xe, wup_ref[:, gate]), gate)) * conv(_dot(xe, wup_ref[:, val]), val)
        act_ref[:, gate] = act.astype(BF16)

    y = _dot(act_ref[...], wd_ref[...])
    o_ref[...] = _layer_norm_rows(ALPHA * x_ref[...] + y, g_ref[...], b_ref[...])


def _resident(shape):
    nd = len(shape)
    return pl.BlockSpec(shape, lambda *_: (0,) * nd, pipeline_mode=pl.Buffered(1))


def _ffn(x, lp, T):
    n = x.shape[0]
    tm = TM_FFN
    nt = T // tm
    assert T % tm == 0 and D_FF % FF_CHUNK == 0
    hb = tm // HALO
    nhb = n // HALO
    row = lambda i: (i, 0)
    return pl.pallas_call(
        functools.partial(_ffn_kernel, nt),
        grid=(n // tm,),
        in_specs=[pl.BlockSpec((tm, D_MODEL), row),
                  pl.BlockSpec((HALO, D_MODEL), lambda i: (jnp.maximum(i * hb - 1, 0), 0)),
                  pl.BlockSpec((HALO, D_MODEL), lambda i: (jnp.minimum((i + 1) * hb, nhb - 1), 0)),
                  _resident(lp['w_up'].shape), _resident(lp['conv_w'].shape), _resident(lp['conv_b'].shape),
                  _resident(lp['w_down'].shape), _resident(lp['ln2_g'].shape), _resident(lp['ln2_b'].shape)],
        out_specs=pl.BlockSpec((tm, D_MODEL), row),
        out_shape=jax.ShapeDtypeStruct((n, D_MODEL), F32),
        scratch_shapes=[pltpu.VMEM((tm + 2 * HALO, D_MODEL), BF16),
                        pltpu.VMEM((tm, D_FF), BF16)],
        compiler_params=_params("parallel"),
        name="conv_ffn",
    )(x, x, x, lp['w_up'], lp['conv_w'], lp['conv_b'], lp['w_down'], lp['ln2_g'], lp['ln2_b'])


def _natten_bias(rpb):
    nr, nc = 2 * NA_ROWS - 1, 2 * NA_COLS - 1
    krows = NA_KBLKS * NA_QROWS
    col = np.arange(GRID_W)
    ci = np.clip(col[None, :] - col[:, None] + NA_COLS - 1, 0, nc - 1)
    onehot_c = (ci.reshape(-1)[None, :] == np.arange(nc)[:, None]).astype(np.float32)
    cs = np.clip(col - NA_COLS // 2, 0, GRID_W - NA_COLS)
    dc = col[None, :] - cs[:, None]
    valid_c = (dc >= 0) & (dc < NA_COLS)
    qa = np.arange(NA_QROWS)
    kr = np.arange(krows)
    onehot_r, valid_r = [], []
    for kind in range(3):
        q_row = (0, NA_QROWS, 2 * NA_QROWS)[kind] + qa
        r_start = (np.zeros_like(qa), qa, np.full_like(qa, NA_QROWS))[kind]
        dr = kr[None, :] - r_start[:, None]
        valid_r.append((dr >= 0) & (dr < NA_ROWS))
        ri = np.clip(kr[None, :] - q_row[:, None] + NA_ROWS - 1, 0, nr - 1)
        onehot_r.append((ri.reshape(-1)[:, None] == np.arange(nr)[None, :]).astype(np.float32))
    onehot_r = np.stack(onehot_r)
    valid = np.stack(valid_r)[:, :, None, :, None] & valid_c[None, None, :, None, :]
    by_col = jnp.einsum('hrc,cx->hrx', rpb, onehot_c, precision=HI)
    full = jnp.einsum('kpr,hrx->khpx', onehot_r, by_col, precision=HI)
    full = full.reshape(3, NA_HEADS, NA_QROWS, krows, GRID_W, GRID_W).transpose(0, 1, 3, 5, 2, 4)
    valid = valid.transpose(0, 3, 4, 1, 2)
    full = jnp.where(valid[:, None], full * LOG2_E, -1e30)
    return full.reshape(3, NA_HEADS, krows * GRID_W, NA_QBLK).astype(F32)


def _prep_layer(l, emb_ln_g, emb_ln_b, w_in, na_rpb, mla_q_g, w_uq, mla_kv_g, w_ukv, sgu_ln_g, sgu_ln_b,
                sgu_w, sgu_b, w_out, ln1_g, ln1_b, w_up, conv_w, conv_b, w_down, ln2_g, ln2_b):
    wi = w_in[l]
    o_fb = 3 * NA_WIDTH
    o_cq = o_fb + FN_WIDTH
    o_ckv = o_cq + MLA_Q_LORA
    o_kr = o_ckv + MLA_KV_LORA
    o_sg = o_kr + MLA_ROPE
    half = MLA_ROPE // 2
    swap = np.concatenate([np.arange(half, MLA_ROPE), np.arange(half)])
    w_kr = wi[:, o_kr:o_sg]
    w_ckv = jnp.concatenate([wi[:, o_ckv:o_kr], w_kr, w_kr[:, swap],
                             jnp.zeros((D_MODEL, LANES - 2 * MLA_ROPE), F32)], axis=1)

    uq = w_uq[l]
    pad = jnp.zeros((MLA_Q_LORA, MLA_HEADS, MLA_PAD - MLA_NOPE - MLA_ROPE), F32)
    w_a = jnp.concatenate([uq, pad], axis=2).reshape(MLA_Q_LORA, MLA_HEADS * MLA_PAD)
    ukv = w_ukv[l]
    w_kn = jnp.concatenate([ukv[:, :, :MLA_NOPE],
                            jnp.zeros((MLA_KV_LORA, MLA_HEADS, MLA_PAD - MLA_NOPE), F32)],
                           axis=2).reshape(MLA_KV_LORA, MLA_HEADS * MLA_PAD)
    w_vt = ukv[:, :, MLA_NOPE:].reshape(MLA_KV_LORA, MLA_WIDTH).T
    w_s = jnp.concatenate([sgu_w[l][g] for g in range(SGU_HEADS)], axis=1)
    gmat =np.kron(np.eye(SGU_HEADS), np.full((SGU_HEAD_DIM, SGU_HEAD_DIM), 1.0 / SGU_HEAD_DIM)).astype(np.float32)
    s_b = jnp.repeat(sgu_b[l].T, SGU_HEAD_DIM, axis=1)
    r1 = lambda a: a.reshape(1, -1).astype(F32)
    half_val = jnp.concatenate([jnp.ones((1, D_FF), F32), jnp.full((1, D_FF), 0.5, F32)], axis=1)
    return dict(
        eg=r1(emb_ln_g), eb=r1(emb_ln_b),
        w_nq=wi[:, :NA_WIDTH].T.astype(BF16), w_nk=wi[:, NA_WIDTH:2 * NA_WIDTH].astype(BF16),
        w_nv=wi[:, 2 * NA_WIDTH:o_fb].T.astype(BF16), w_fb=wi[:, o_fb:o_cq].astype(BF16),
        w_cq=wi[:, o_cq:o_ckv].astype(BF16), w_ckv=w_ckv.astype(BF16), w_sg=wi[:, o_sg:].astype(BF16),
        q_g=r1(mla_q_g[l]), w_a=w_a.T.astype(BF16),
        kv_g=r1(mla_kv_g[l]), w_kn=w_kn.astype(BF16), w_vt=w_vt.astype(BF16),
        gmat=jnp.asarray(gmat, BF16),
        sl_g=r1(sgu_ln_g[l]), sl_b=r1(sgu_ln_b[l]), w_s=w_s.astype(BF16), s_b=s_b.astype(F32),
        na_bias=_natten_bias(na_rpb[l]),
        w_out=w_out[l].astype(BF16), ln1_g=r1(ln1_g[l]), ln1_b=r1(ln1_b[l]),
        w_up=w_up[l].astype(BF16), conv_w=conv_w[l].astype(F32) * half_val, conv_b=r1(conv_b[l]) * half_val,
        w_down=w_down[l].astype(BF16), ln2_g=r1(ln2_g[l]), ln2_b=r1(ln2_b[l]),
    )


def _rope_tables(T):
    inv_freq = ROPE_THETA ** (-np.arange(0, MLA_ROPE, 2, dtype=np.float64) / MLA_ROPE)
    ang = np.arange(T, dtype=np.float64)[:, None] * inv_freq[None, :]
    cos, sin = np.cos(ang), np.sin(ang)
    cos2 = np.concatenate([cos, cos], axis=1)
    sin2 = np.concatenate([-sin, sin], axis=1)
    scale = (MLA_NOPE + MLA_ROPE) ** -0.5 * math.log2(math.e)
    zq = np.zeros((T, MLA_PAD - MLA_NOPE - MLA_ROPE))
    ca = np.concatenate([np.full((T, MLA_NOPE), scale), scale * cos2, zq], axis=1)
    cb = np.concatenate([np.zeros((T, MLA_NOPE)), scale * sin2, zq], axis=1)
    tk = np.concatenate([cos2, sin2, np.zeros((T, LANES - 2 * MLA_ROPE))], axis=1)
    f32 = lambda a: jnp.asarray(np.ascontiguousarray(a), F32)
    return dict(ca=f32(ca.T), cb=f32(cb.T), tk=f32(tk))


def _dft_tables(T):
    n1 = 1 << (int(math.log2(T)) // 2)
    n2 = T // n1

    def cs(num, den):
        ang = (num % den).astype(np.float64) * (2.0 * np.pi / den)
        return np.cos(ang), np.sin(ang)

    def table(x):
        return jnp.asarray(x.astype(np.float32)).astype(BF16)

    t2 = np.arange(n2, dtype=np.int64)[:, None, None]
    k1 = np.arange(n1, dtype=np.int64)[None, :, None]
    t1 = np.arange(n1, dtype=np.int64)[None, None, :]
    c, s = cs(k1 * (t1 * n2 + t2), T)
    m1 = table(np.concatenate([c, -s], axis=1))
    k2 = np.arange(n2, dtype=np.int64)
    c2, s2 = cs(k2[:, None] * k2[None, :], n2)
    cs2 = table(np.concatenate([c2, s2], axis=0))
    cg = np.arange(FN_GROUP_DIM, dtype=np.int64)
    cc, sc = cs(cg[:, None] * cg[None, :], FN_GROUP_DIM)
    norm = 1.0 / math.sqrt(T * FN_GROUP_DIM)
    eye = np.eye(FN_GROUPS)
    w = table(np.concatenate([np.kron(eye, cc), np.kron(eye, sc)], axis=0) * norm)
    return dict(n1=n1, n2=n2, m1=m1, cs2=cs2, w=w)


def _trunk(x, layers):
    B, T, _ = x.shape
    tabs = _rope_tables(T)
    ft = _dft_tables(T)
    h = x.reshape(B * T, D_MODEL)
    for l, lp in enumerate(layers):
        outs = _inproj(h, lp, tabs, l == 0, T)
        if l == 0:
            h = outs[0]
            outs = outs[1:]
        natq, natk, natv, fb, q, k, vt, od = outs
        oa = _natten(natq, natk, natv, lp['na_bias'], B, T)
        ob = _fourier(fb, ft, B, T)
        oct = _mla(q, k, vt, B, T)
        h1 = _outproj(oa, ob, oct, od, h, lp)
        h = _ffn(h1, lp, T)
    return h.reshape(B, T, D_MODEL)


def kernel(x_prompt, x_sample, emb_ln_g, emb_ln_b, w_in, na_rpb, mla_q_g, w_uq, mla_kv_g, w_ukv, sgu_ln_g,
           sgu_ln_b, sgu_w, sgu_b, w_out, ln1_g, ln1_b, w_up, conv_w, conv_b, w_down, ln2_g, ln2_b):
    layers = [_prep_layer(l, emb_ln_g, emb_ln_b, w_in, na_rpb, mla_q_g, w_uq, mla_kv_g, w_ukv, sgu_ln_g,
                          sgu_ln_b, sgu_w, sgu_b, w_out, ln1_g, ln1_b, w_up, conv_w, conv_b, w_down,
                          ln2_g, ln2_b) for l in range(DEPTH)]
    return (_trunk(x_prompt, layers), _trunk(x_sample, layers))
```

```python
import functools
import math

import numpy as np
import jax
import jax.numpy as jnp
from jax import lax
from jax.experimental import pallas as pl
from jax.experimental.pallas import tpu as pltpu

F32 = jnp.float32
BF16 = jnp.bfloat16

D_MODEL = 1024
DEPTH = 2
GRID_W = 64
HEAD_DIM = 64
NA_HEADS = 4
NA_ROWS = 8
NA_COLS = 16
NA_WIDTH = NA_HEADS * HEAD_DIM
FN_GROUPS = 4
FN_GROUP_DIM = 64
FN_WIDTH = FN_GROUPS * FN_GROUP_DIM
MLA_HEADS = 4
MLA_Q_LORA = 256
MLA_KV_LORA = 128
MLA_NOPE = 64
MLA_ROPE = 32
MLA_V = 64
MLA_WIDTH = MLA_HEADS * MLA_V
ROPE_THETA = 10000.0
SGU_HEADS = 4
SGU_HEAD_DIM = 64
SGU_WIDTH = SGU_HEADS * SGU_HEAD_DIM
SGU_CHUNK = 128
D_FF = 2816
ALPHA = (2 * DEPTH) ** 0.25
LN_EPS = 1e-5
RMS_EPS = 1e-6

LANES = 128
SUBLANES = 8
BF16_ROWS = 16
VMEM_LIMIT = 56 * 1024 * 1024

TM_IN = 1024
TM_OUT = 1024
TM_FFN = 1024
FF_CHUNK = 256
NA_QROWS = 4
NA_QBLK = NA_QROWS * GRID_W
NA_KBLKS = 3
MLA_PAD = 128
MLA_TQ = 512
MLA_TK = 512
MLA_VT_CHUNK = 512
MLA_ONES = 16
MLA_UNROLL = 32
MLA_SCORE_BUFS = 4
HALO = SUBLANES

LOG2_E = math.log2(math.e)
HI = lax.Precision.HIGHEST
NT_DIMS = (((1,), (1,)), ((), ()))


def _dot(a, b):
    return jnp.dot(a, b, preferred_element_type=F32)


def _dot_nt(a, b):
    return lax.dot_general(a, b, NT_DIMS, preferred_element_type=F32)


def _layer_norm_rows(x, g, b):
    mu = jnp.mean(x, axis=-1, keepdims=True)
    xc = x - mu
    var = jnp.mean(xc * xc, axis=-1, keepdims=True)
    return xc * lax.rsqrt(var + LN_EPS) * g + b


GELU_C = 0.7978845608028654
GELU_K = 0.044715


def _gelu_tanh_x2(x):
    t = jnp.tanh(x * (GELU_C + (GELU_C * GELU_K) * (x * x)))
    return x + x * t


def _gelu_tanh(x):
    return 0.5 * _gelu_tanh_x2(x)


def _split_dot(v, m):
    hi = v.astype(BF16)
    lo = (v - hi.astype(F32)).astype(BF16)
    return _dot(hi, m) + _dot(lo, m)


def _params(*sem):
    return pltpu.CompilerParams(dimension_semantics=sem, vmem_limit_bytes=VMEM_LIMIT)


def _full(shape):
    nd = len(shape)
    return pl.BlockSpec(shape, lambda *_: (0,) * nd)


def _inproj_kernel(apply_ln, x_ref, eg_ref, eb_ref, wnq_ref, wnk_ref, wnv_ref, wfb_ref, wcq_ref, wckv_ref,
                   wsg_ref, qg_ref, wa_ref, kvg_ref, wkn_ref, wvt_ref, ca_ref, cb_ref,
                   tk_ref, gmat_ref, slg_ref, slb_ref, ws_ref, sb_ref, *out_refs):
    if apply_ln:
        xn_ref, natq_ref, natk_ref, natv_ref, fb_ref, q_ref, k_ref, vt_ref, od_ref = out_refs
    else:
        natq_ref, natk_ref, natv_ref, fb_ref, q_ref, k_ref, vt_ref, od_ref = out_refs
    x = x_ref[...]
    if apply_ln:
        x = _layer_norm_rows(x, eg_ref[...], eb_ref[...])
        xn_ref[...] = x
    xb = x.astype(BF16)
    tm = x.shape[0]

    cq = _dot(xb, wcq_ref[...])
    ckv_kr = _dot(xb, wckv_ref[...])
    sg = _dot(xb, wsg_ref[...])

    natq = (_dot_nt(wnq_ref[...], xb) * (HEAD_DIM ** -0.5 * LOG2_E)).astype(BF16)
    natv = _dot_nt(wnv_ref[...], xb).astype(BF16)
    for c in range(tm // NA_QBLK):
        natq_ref[c] = natq[:, c * NA_QBLK:(c + 1) * NA_QBLK]
        natv_ref[c] = natv[:, c * NA_QBLK:(c + 1) * NA_QBLK]
    natk_ref[...] = _dot(xb, wnk_ref[...]).astype(BF16)

    fb_ref[...] = _dot(xb, wfb_ref[...]).astype(BF16)

    cqn = (cq * lax.rsqrt(jnp.mean(cq * cq, axis=-1, keepdims=True) + RMS_EPS) * qg_ref[...]).astype(BF16)
    qa = _dot_nt(wa_ref[...], cqn)
    ca = ca_ref[...]
    cb = cb_ref[MLA_NOPE:MLA_NOPE + MLA_ROPE]
    half = MLA_ROPE // 2
    for h in range(MLA_HEADS):
        base = h * MLA_PAD
        qh = qa[base:base + MLA_PAD] * ca
        lo = qa[base + MLA_NOPE:base + MLA_NOPE + half]
        hi = qa[base + MLA_NOPE + half:base + MLA_NOPE + MLA_ROPE]
        rope = qh[MLA_NOPE:MLA_NOPE + MLA_ROPE] + jnp.concatenate([hi, lo], axis=0) * cb
        qh = jnp.concatenate([qh[:MLA_NOPE], rope, qh[MLA_NOPE + MLA_ROPE:]], axis=0).astype(BF16)
        for c in range(tm // MLA_TQ):
            q_ref[c, base:base + MLA_PAD, :] = qh[:, c * MLA_TQ:(c + 1) * MLA_TQ]

    ckv = ckv_kr[:, :MLA_KV_LORA]
    ckvn = (ckv * lax.rsqrt(jnp.mean(ckv * ckv, axis=-1, keepdims=True) + RMS_EPS) * kvg_ref[...]).astype(BF16)
    kr = ckv_kr[:, MLA_KV_LORA:] * tk_ref[...]
    kr = pltpu.roll(kr, MLA_NOPE, axis=1) + pltpu.roll(kr, MLA_NOPE - MLA_ROPE, axis=1)
    lane = lax.broadcasted_iota(jnp.int32, (1, MLA_PAD), 1)
    kr = jnp.where((lane >= MLA_NOPE) & (lane < MLA_NOPE + MLA_ROPE), kr, 0.0)
    kn = _dot(ckvn, wkn_ref[...])
    for h in range(MLA_HEADS):
        sl = slice(h * MLA_PAD, (h + 1) * MLA_PAD)
        k_ref[:, sl] = (kn[:, sl] + kr).astype(BF16)
    vt = _dot_nt(wvt_ref[...], ckvn)
    tkc = vt_ref.shape[2]
    for c in range(tm // tkc):
        vt_ref[c] = vt[:, c * tkc:(c + 1) * tkc].astype(BF16)

    sg = _gelu_tanh(sg)
    u = sg[:, :SGU_WIDTH]
    v = sg[:, SGU_WIDTH:]
    gmat = gmat_ref[...]
    mu = _split_dot(v, gmat)
    vc = v - mu
    var = _split_dot(vc * vc, gmat)
    vn = (vc * lax.rsqrt(var + LN_EPS) * slg_ref[...] + slb_ref[...]).astype(BF16)
    head = lax.broadcasted_iota(jnp.int32, (1, SGU_WIDTH), 1) // SGU_HEAD_DIM
    zero = jnp.zeros((), BF16)
    for c in range(tm // SGU_CHUNK):
        rows = slice(c * SGU_CHUNK, (c + 1) * SGU_CHUNK)
        vch = vn[rows]
        stacked = jnp.concatenate([jnp.where(head == g, vch, zero) for g in range(SGU_HEADS)], axis=0)
        mixed = sb_ref[...] + _dot(ws_ref[...], stacked)
        od_ref[rows, :] = (u[rows] * mixed).astype(BF16)


def _inproj(x, lp, tabs, apply_ln, T):
    n = x.shape[0]
    tm = TM_IN
    assert T % tm == 0 and tm % MLA_TQ == 0 and tm % MLA_VT_CHUNK == 0 and tm % SGU_CHUNK == 0 and tm % NA_QBLK == 0
    nt = T // tm
    row = lambda i: (i, 0)
    tab = lambda i: (i % nt, 0)
    weights = [lp['eg'], lp['eb'], lp['w_nq'], lp['w_nk'], lp['w_nv'], lp['w_fb'], lp['w_cq'], lp['w_ckv'], lp['w_sg'],
               lp['q_g'], lp['w_a'], lp['kv_g'], lp['w_kn'], lp['w_vt']]
    tables = [tabs['ca'], tabs['cb'], tabs['tk']]
    tail = [lp['gmat'], lp['sl_g'], lp['sl_b'], lp['w_s'], lp['s_b']]
    tab_t = lambda i: (0, i % nt)
    in_specs = ([pl.BlockSpec((tm, D_MODEL), row)] + [_full(w.shape) for w in weights]
                + [pl.BlockSpec((MLA_PAD, tm), tab_t), pl.BlockSpec((MLA_PAD, tm), tab_t),
                   pl.BlockSpec((tm, LANES), tab)] + [_full(w.shape) for w in tail])
    nat_t = jax.ShapeDtypeStruct((n // NA_QBLK, NA_WIDTH, NA_QBLK), BF16)
    nat_t_spec = pl.BlockSpec((tm // NA_QBLK, NA_WIDTH, NA_QBLK), lambda i: (i, 0, 0))
    out_shape = [nat_t, jax.ShapeDtypeStruct((n, NA_WIDTH), BF16), nat_t,
                 jax.ShapeDtypeStruct((n, FN_WIDTH), BF16),
                 jax.ShapeDtypeStruct((n // MLA_TQ, MLA_HEADS * MLA_PAD, MLA_TQ), BF16),
                 jax.ShapeDtypeStruct((n, MLA_HEADS * MLA_PAD), BF16),
                 jax.ShapeDtypeStruct((n // MLA_VT_CHUNK, MLA_WIDTH, MLA_VT_CHUNK), BF16),
                 jax.ShapeDtypeStruct((n, SGU_WIDTH), BF16)]
    out_specs = [nat_t_spec, pl.BlockSpec((tm, NA_WIDTH), row), nat_t_spec,
                 pl.BlockSpec((tm, FN_WIDTH), row),
                 pl.BlockSpec((tm // MLA_TQ, MLA_HEADS * MLA_PAD, MLA_TQ), lambda i: (i, 0, 0)),
                 pl.BlockSpec((tm, MLA_HEADS * MLA_PAD), row),
                 pl.BlockSpec((tm // MLA_VT_CHUNK, MLA_WIDTH, MLA_VT_CHUNK), lambda i: (i, 0, 0)),
                 pl.BlockSpec((tm, SGU_WIDTH), row)]
    if apply_ln:
        out_shape = [jax.ShapeDtypeStruct((n, D_MODEL), F32)] + out_shape
        out_specs = [pl.BlockSpec((tm, D_MODEL), row)] + out_specs
    return pl.pallas_call(
        functools.partial(_inproj_kernel, apply_ln),
        grid=(n // tm,),
        in_specs=in_specs,
        out_specs=out_specs,
        out_shape=out_shape,
        compiler_params=_params("parallel"),
        name="inproj_ln" if apply_ln else "inproj",
    )(x, *weights, *tables, *tail)


def _natten_kernel(qt_ref, k0_ref, k1_ref, k2_ref, k3_ref, vt0_ref, vt1_ref, vt2_ref, vt3_ref,
                   bias_a_ref, bias_b_ref, o_ref, s_ref):
    k_refs = (k0_ref, k1_ref, k2_ref, k3_ref)
    vt_refs = (vt0_ref, vt1_ref, vt2_ref, vt3_ref)
    row_head = lax.broadcasted_iota(jnp.int32, (NA_WIDTH, 1), 0) // HEAD_DIM
    zero = jnp.zeros((), BF16)
    ones = jnp.ones((BF16_ROWS, NA_QBLK), BF16)

    def block(qi, off, bias_ref):
        qt = qt_ref[qi]

        def scores(h, slot):
            qh = jnp.where(row_head == h, qt, zero)
            for d in range(NA_KBLKS):
                rows = slice(d * NA_QBLK, (d + 1) * NA_QBLK)
                s_ref[slot, rows, :] = _dot(k_refs[off + d][...], qh) + bias_ref[h, rows, :]

        scores(0, 0)
        for h in range(NA_HEADS):
            if h + 1 < NA_HEADS:
                scores(h + 1, (h + 1) % 2)
            s = s_ref[h % 2]
            p = jnp.exp2(s - jnp.max(s, axis=0, keepdims=True)).astype(BF16)
            ch = slice(h * HEAD_DIM, (h + 1) * HEAD_DIM)
            acc = None
            for d in range(NA_KBLKS):
                vte = jnp.concatenate([vt_refs[off + d][ch, :], ones], axis=0)
                part = _dot(vte, p[d * NA_QBLK:(d + 1) * NA_QBLK])
                acc = part if acc is None else acc + part
            o_ref[qi, ch, :] = (acc[:HEAD_DIM] * (1.0 / acc[HEAD_DIM:HEAD_DIM + 1])).astype(BF16)

    i = pl.program_id(1)
    last = pl.num_programs(1) - 1
    for cond, off_a, off_b in ((i == 0, 0, 0), ((i > 0) & (i < last), 0, 1), (i == last, 1, 1)):
        @pl.when(cond)
        def _(off_a=off_a, off_b=off_b):
            block(0, off_a, bias_a_ref)
            block(1, off_b, bias_b_ref)


def _natten(qt, k, vt, bias, B, T):
    nblk = T // NA_QBLK
    npair = nblk // 2
    nwin = NA_KBLKS + 1
    assert nblk % 2 == 0 and nblk >= nwin

    def kv_map(d, rank3):
        def f(b, i):
            blk = b * nblk + jnp.clip(2 * i - 1, 0, nblk - nwin) + d
            return (blk, 0, 0) if rank3 else (blk, 0)
        return f

    bias_blk = (None, NA_HEADS, NA_KBLKS * NA_QBLK, NA_QBLK)
    tblk = (None, NA_WIDTH, NA_QBLK)
    pair = (2, NA_WIDTH, NA_QBLK)
    in_specs = ([pl.BlockSpec(pair, lambda b, i: (b * npair + i, 0, 0))]
                + [pl.BlockSpec((NA_QBLK, NA_WIDTH), kv_map(d, False)) for d in range(nwin)]
                + [pl.BlockSpec(tblk, kv_map(d, True)) for d in range(nwin)]
                + [pl.BlockSpec(bias_blk, lambda b, i: (jnp.where(i == 0, 0, 1), 0, 0, 0)),
                   pl.BlockSpec(bias_blk, lambda b, i: (jnp.where(i == npair - 1, 2, 1), 0, 0, 0))])
    return pl.pallas_call(
        _natten_kernel,
        grid=(B, npair),
        in_specs=in_specs,
        out_specs=pl.BlockSpec(pair, lambda b, i: (b * npair + i, 0, 0)),
        out_shape=jax.ShapeDtypeStruct((B * nblk, NA_WIDTH, NA_QBLK), BF16),
        scratch_shapes=[pltpu.VMEM((2, NA_KBLKS * NA_QBLK, NA_QBLK), F32)],
        compiler_params=_params("parallel", "arbitrary"),
        name="natten",
    )(qt, k, k, k, k, vt, vt, vt, vt, bias, bias)


def _fft1_kernel(x_ref, m_ref, a_ref):
    kk = m_ref.shape[0]
    n1 = x_ref.shape[0]
    xt = jnp.swapaxes(x_ref[...], 0, 1)
    for i in range(kk):
        res = _dot(m_ref[i], xt[i])
        res = jnp.concatenate([res[:n1], res[n1:]], axis=1).astype(BF16)
        a_ref[:, i] = res.reshape(n1 // kk, kk, 2 * FN_WIDTH)


def _fft2_kernel(a_ref, cs_ref, w_ref, y_ref):
    n2, kk, _ = a_ref.shape
    cs = cs_ref[...]
    at = jnp.swapaxes(a_ref[...], 0, 1)
    g = []
    for i in range(kk):
        r = _dot(cs, at[i])
        g.append(jnp.concatenate([r[:n2, :FN_WIDTH] + r[n2:, FN_WIDTH:],
                                  r[:n2, FN_WIDTH:] - r[n2:, :FN_WIDTH]], axis=1).astype(BF16))
    g = jnp.concatenate(g, axis=0)
    y = _dot(g, w_ref[...]).astype(BF16)
    y_ref[...] = jnp.swapaxes(y.reshape(kk, n2, FN_WIDTH), 0, 1)


def _fourier(fb, ft, B, T):
    n1, n2 = ft['n1'], ft['n2']
    kk = BF16_ROWS
    a = pl.pallas_call(
        _fft1_kernel,
        grid=(B, n2 // kk),
        in_specs=[pl.BlockSpec((None, n1, kk, FN_WIDTH), lambda b, j: (b, 0, j, 0)),
                  pl.BlockSpec((kk, 2 * n1, n1), lambda b, j: (j, 0, 0))],
        out_specs=pl.BlockSpec((None, n1 // kk, kk, kk, 2 * FN_WIDTH), lambda b, j: (b, 0, j, 0, 0)),
        out_shape=jax.ShapeDtypeStruct((B, n1 // kk, n2, kk, 2 * FN_WIDTH), BF16),
        compiler_params=_params("parallel", "arbitrary"),
        name="fft_stage1",
    )(fb.reshape(B, n1, n2, FN_WIDTH), ft['m1'])
    y = pl.pallas_call(
        _fft2_kernel,
        grid=(B, n1 // kk),
        in_specs=[pl.BlockSpec((None, None, n2, kk, 2 * FN_WIDTH), lambda b, j: (b, j, 0, 0, 0)),
                  _full(ft['cs2'].shape), _full(ft['w'].shape)],
        out_specs=pl.BlockSpec((None, n2, kk, FN_WIDTH), lambda b, j: (b, 0, j, 0)),
        out_shape=jax.ShapeDtypeStruct((B, n2, n1, FN_WIDTH), BF16),
        compiler_params=_params("parallel", "arbitrary"),
        name="fft_stage2",
    )(a, ft['cs2'], ft['w'])
    return y.reshape(B * T, FN_WIDTH)


def _mla_kernel(q_ref, k_ref, vt_ref, o_ref, s_ref):
    tk = s_ref.shape[1]
    per = tk // MLA_VT_CHUNK
    nk = vt_ref.shape[0] // per
    nq, _, tq = q_ref.shape
    unroll = min(MLA_UNROLL, nk)
    ones = jnp.ones((MLA_ONES, tk), BF16)

    def scores(i, j):
        start = j * tk if isinstance(j, int) else pl.multiple_of(j * tk, tk)
        return _dot(k_ref[pl.ds(start, tk), :], q_ref[i])

    def update(carry, j, slot):
        m, acc = carry
        st = s_ref[slot]
        m_new = jnp.maximum(m, jnp.max(st, axis=0, keepdims=True))
        alpha = jnp.exp2(m - m_new)
        p = jnp.exp2(st - m_new).astype(BF16)
        vt = jnp.concatenate([vt_ref[j * per + u] for u in range(per)], axis=1)
        vte = jnp.concatenate([vt, ones], axis=0)
        return m_new, alpha * acc + _dot(vte, p)

    nbuf = s_ref.shape[0]
    ahead = nbuf // 2
    last = nq * nk - 1
    for f in range(ahead):
        s_ref[f] = scores(f // nk, f % nk)
    m0 = jnp.full((1, tq), -1e30, F32)
    acc0 = jnp.zeros((MLA_V + MLA_ONES, tq), F32)

    def body(t, carry):
        i = (t * unroll) // nk
        j0 = (t * unroll) % nk
        fresh = j0 == 0
        carry = (jnp.where(fresh, m0, carry[0]), jnp.where(fresh, acc0, carry[1]))
        for c in range(unroll):
            if c + ahead < unroll:
                s_ref[(c + ahead) % nbuf] = scores(i, j0 + c + ahead)
            else:
                nxt = jnp.minimum(t * unroll + c + ahead, last)
                s_ref[(c + ahead) % nbuf] = scores(nxt // nk, nxt % nk)
            carry = update(carry, j0 + c, c % nbuf)

        @pl.when(j0 + unroll == nk)
        def _():
            acc = carry[1]
            o_ref[i] = (acc[:MLA_V] * (1.0 / acc[MLA_V:MLA_V + 1])).astype(o_ref.dtype)

        return carry

    lax.fori_loop(0, nq * nk // unroll, body, (m0, acc0))


def _mla(q, k, vt, B, T):
    nq = T // MLA_TQ
    nk = T // MLA_TK
    unroll = min(MLA_UNROLL, nk)
    nbuf = MLA_SCORE_BUFS if unroll % MLA_SCORE_BUFS == 0 else 2
    assert nk % unroll == 0 and unroll % nbuf == 0
    return pl.pallas_call(
        _mla_kernel,
        grid=(B, MLA_HEADS),
        in_specs=[pl.BlockSpec((nq, MLA_PAD, MLA_TQ), lambda b, h: (b, h, 0)),
                  pl.BlockSpec((T, MLA_PAD), lambda b, h: (b, h)),
                  pl.BlockSpec((T // MLA_VT_CHUNK, MLA_V, MLA_VT_CHUNK), lambda b, h: (b, h, 0))],
        out_specs=pl.BlockSpec((nq, MLA_V, MLA_TQ), lambda b, h: (b, h, 0)),
        out_shape=jax.ShapeDtypeStruct((B * nq, MLA_WIDTH, MLA_TQ), BF16),
        scratch_shapes=[pltpu.VMEM((nbuf, MLA_TK, MLA_TQ), F32)],
        compiler_params=_params("parallel", "arbitrary"),
        name="mla",
    )(q, k, vt)


def _outproj_kernel(oat_ref, ob_ref, oct_ref, od_ref, x_ref, w_ref, g_ref, b_ref, o_ref):
    w = NA_WIDTH
    tm = x_ref.shape[0]
    for c in range(tm // NA_QBLK):
        rows = slice(c * NA_QBLK, (c + 1) * NA_QBLK)
        oc = oct_ref[(c * NA_QBLK) // MLA_TQ, :, pl.ds((c * NA_QBLK) % MLA_TQ, NA_QBLK)].T
        y = (_dot(oat_ref[c].T, w_ref[0:w]) + _dot(ob_ref[rows, :], w_ref[w:2 * w])
             + _dot(oc, w_ref[2 * w:3 * w]) + _dot(od_ref[rows, :], w_ref[3 * w:4 * w]))
        o_ref[rows, :] = _layer_norm_rows(ALPHA * x_ref[rows, :] + y, g_ref[...], b_ref[...])


def _outproj(oa, ob, oct, od, x, lp):
    n = x.shape[0]
    tm = TM_OUT
    assert tm % MLA_TQ == 0 and MLA_TQ % NA_QBLK == 0
    row = lambda i: (i, 0)
    return pl.pallas_call(
        _outproj_kernel,
        grid=(n // tm,),
        in_specs=[pl.BlockSpec((tm // NA_QBLK, NA_WIDTH, NA_QBLK), lambda i: (i, 0, 0)),
                  pl.BlockSpec((tm, FN_WIDTH), row),
                  pl.BlockSpec((tm // MLA_TQ, MLA_WIDTH, MLA_TQ), lambda i: (i, 0, 0)),
                  pl.BlockSpec((tm, SGU_WIDTH), row),
                  pl.BlockSpec((tm, D_MODEL), row), _full(lp['w_out'].shape),
                  _full(lp['ln1_g'].shape), _full(lp['ln1_b'].shape)],
        out_specs=pl.BlockSpec((tm, D_MODEL), row),
        out_shape=jax.ShapeDtypeStruct((n, D_MODEL), F32),
        compiler_params=_params("parallel"),
        name="outproj",
    )(oa, ob, oct, od, x, lp['w_out'], lp['ln1_g'], lp['ln1_b'])


def _ffn_kernel(nt, x_ref, xp_ref, xn_ref, wup_ref, cw_ref, cb_ref, wd_ref, g_ref, b_ref, o_ref,
                xe_ref, act_ref):
    i = pl.program_id(0)
    tm = x_ref.shape[0]
    cf = FF_CHUNK
    first = (i % nt) == 0
    last = (i % nt) == nt - 1
    xe_ref[0:HALO] = jnp.where(first, 0.0, xp_ref[...]).astype(BF16)
    xe_ref[HALO:HALO + tm] = x_ref[...].astype(BF16)
    xe_ref[HALO + tm:] = jnp.where(last, 0.0, xn_ref[...]).astype(BF16)
    xe = xe_ref[...]

    rows = slice(HALO, HALO + tm)

    def conv(h, cols):
        prev = pltpu.roll(h, 1, axis=0)
        nxt = pltpu.roll(h, h.shape[0] - 1, axis=0)
        return (prev[rows] * cw_ref[0:1, cols] + h[rows] * cw_ref[1:2, cols]
                + nxt[rows] * cw_ref[2:3, cols] + cb_ref[:, cols])

    for c in range(D_FF // cf):
        gate = slice(c * cf, (c + 1) * cf)
        val = slice(D_FF + c * cf, D_FF + (c + 1) * cf)
        act = _gelu_tanh_x2(conv(_dot(xe, wup_ref[:, gate]), gate)) * conv(_dot(xe, wup_ref[:, val]), val)
        act_ref[:, gate] = act.astype(BF16)

    y = _dot(act_ref[...], wd_ref[...])
    o_ref[...] = _layer_norm_rows(ALPHA * x_ref[...] + y, g_ref[...], b_ref[...])


def _resident(shape):
    nd = len(shape)
    return pl.BlockSpec(shape, lambda *_: (0,) * nd, pipeline_mode=pl.Buffered(1))


def _ffn(x, lp, T):
    n = x.shape[0]
    tm = TM_FFN
    nt = T // tm
    assert T % tm == 0 and D_FF % FF_CHUNK == 0
    hb = tm // HALO
    nhb = n // HALO
    row = lambda i: (i, 0)
    return pl.pallas_call(
        functools.partial(_ffn_kernel, nt),
        grid=(n // tm,),
        in_specs=[pl.BlockSpec((tm, D_MODEL), row),
                  pl.BlockSpec((HALO, D_MODEL), lambda i: (jnp.maximum(i * hb - 1, 0), 0)),
                  pl.BlockSpec((HALO, D_MODEL), lambda i: (jnp.minimum((i + 1) * hb, nhb - 1), 0)),
                  _resident(lp['w_up'].shape), _resident(lp['conv_w'].shape), _resident(lp['conv_b'].shape),
                  _resident(lp['w_down'].shape), _resident(lp['ln2_g'].shape), _resident(lp['ln2_b'].shape)],
        out_specs=pl.BlockSpec((tm, D_MODEL), row),
        out_shape=jax.ShapeDtypeStruct((n, D_MODEL), F32),
        scratch_shapes=[pltpu.VMEM((tm + 2 * HALO, D_MODEL), BF16),
                        pltpu.VMEM((tm, D_FF), BF16)],
        compiler_params=_params("parallel"),
        name="conv_ffn",
    )(x, x, x, lp['w_up'], lp['conv_w'], lp['conv_b'], lp['w_down'], lp['ln2_g'], lp['ln2_b'])


def _natten_bias(rpb):
    nr, nc = 2 * NA_ROWS - 1, 2 * NA_COLS - 1
    krows = NA_KBLKS * NA_QROWS
    col = np.arange(GRID_W)
    ci = np.clip(col[None, :] - col[:, None] + NA_COLS - 1, 0, nc - 1)
    onehot_c = (ci.reshape(-1)[None, :] == np.arange(nc)[:, None]).astype(np.float32)
    cs = np.clip(col - NA_COLS // 2, 0, GRID_W - NA_COLS)
    dc = col[None, :] - cs[:, None]
    valid_c = (dc >= 0) & (dc < NA_COLS)
    qa = np.arange(NA_QROWS)
    kr = np.arange(krows)
    onehot_r, valid_r = [], []
    for kind in range(3):
        q_row = (0, NA_QROWS, 2 * NA_QROWS)[kind] + qa
        r_start = (np.zeros_like(qa), qa, np.full_like(qa, NA_QROWS))[kind]
        dr = kr[None, :] - r_start[:, None]
        valid_r.append((dr >= 0) & (dr < NA_ROWS))
        ri = np.clip(kr[None, :] - q_row[:, None] + NA_ROWS - 1, 0, nr - 1)
        onehot_r.append((ri.reshape(-1)[:, None] == np.arange(nr)[None, :]).astype(np.float32))
    onehot_r = np.stack(onehot_r)
    valid = np.stack(valid_r)[:, :, None, :, None] & valid_c[None, None, :, None, :]
    by_col = jnp.einsum('hrc,cx->hrx', rpb, onehot_c, precision=HI)
    full = jnp.einsum('kpr,hrx->khpx', onehot_r, by_col, precision=HI)
    full = full.reshape(3, NA_HEADS, NA_QROWS, krows, GRID_W, GRID_W).transpose(0, 1, 3, 5, 2, 4)
    valid = valid.transpose(0, 3, 4, 1, 2)
    full = jnp.where(valid[:, None], full * LOG2_E, -1e30)
    return full.reshape(3, NA_HEADS, krows * GRID_W, NA_QBLK).astype(F32)


def _prep_layer(l, emb_ln_g, emb_ln_b, w_in, na_rpb, mla_q_g, w_uq, mla_kv_g, w_ukv, sgu_ln_g, sgu_ln_b,
                sgu_w, sgu_b, w_out, ln1_g, ln1_b, w_up, conv_w, conv_b, w_down, ln2_g, ln2_b):
    wi = w_in[l]
    o_fb = 3 * NA_WIDTH
    o_cq = o_fb + FN_WIDTH
    o_ckv = o_cq + MLA_Q_LORA
    o_kr = o_ckv + MLA_KV_LORA
    o_sg = o_kr + MLA_ROPE
    half = MLA_ROPE // 2
    swap = np.concatenate([np.arange(half, MLA_ROPE), np.arange(half)])
    w_kr = wi[:, o_kr:o_sg]
    w_ckv = jnp.concatenate([wi[:, o_ckv:o_kr], w_kr, w_kr[:, swap],
                             jnp.zeros((D_MODEL, LANES - 2 * MLA_ROPE), F32)], axis=1)

    uq = w_uq[l]
    pad = jnp.zeros((MLA_Q_LORA, MLA_HEADS, MLA_PAD - MLA_NOPE - MLA_ROPE), F32)
    w_a = jnp.concatenate([uq, pad], axis=2).reshape(MLA_Q_LORA, MLA_HEADS * MLA_PAD)
    ukv = w_ukv[l]
    w_kn = jnp.concatenate([ukv[:, :, :MLA_NOPE],
                            jnp.zeros((MLA_KV_LORA, MLA_HEADS, MLA_PAD - MLA_NOPE), F32)],
                           axis=2).reshape(MLA_KV_LORA, MLA_HEADS * MLA_PAD)
    w_vt = ukv[:, :, MLA_NOPE:].reshape(MLA_KV_LORA, MLA_WIDTH).T
    w_s = jnp.concatenate([sgu_w[l][g] for g in range(SGU_HEADS)], axis=1)
    gmat =np.kron(np.eye(SGU_HEADS), np.full((SGU_HEAD_DIM, SGU_HEAD_DIM), 1.0 / SGU_HEAD_DIM)).astype(np.float32)
    s_b = jnp.repeat(sgu_b[l].T, SGU_HEAD_DIM, axis=1)
    r1 = lambda a: a.reshape(1, -1).astype(F32)
    half_val = jnp.concatenate([jnp.ones((1, D_FF), F32), jnp.full((1, D_FF), 0.5, F32)], axis=1)
    return dict(
        eg=r1(emb_ln_g), eb=r1(emb_ln_b),
        w_nq=wi[:, :NA_WIDTH].T.astype(BF16), w_nk=wi[:, NA_WIDTH:2 * NA_WIDTH].astype(BF16),
        w_nv=wi[:, 2 * NA_WIDTH:o_fb].T.astype(BF16), w_fb=wi[:, o_fb:o_cq].astype(BF16),
        w_cq=wi[:, o_cq:o_ckv].astype(BF16), w_ckv=w_ckv.astype(BF16), w_sg=wi[:, o_sg:].astype(BF16),
        q_g=r1(mla_q_g[l]), w_a=w_a.T.astype(BF16),
        kv_g=r1(mla_kv_g[l]), w_kn=w_kn.astype(BF16), w_vt=w_vt.astype(BF16),
        gmat=jnp.asarray(gmat, BF16),
        sl_g=r1(sgu_ln_g[l]), sl_b=r1(sgu_ln_b[l]), w_s=w_s.astype(BF16), s_b=s_b.astype(F32),
        na_bias=_natten_bias(na_rpb[l]),
        w_out=w_out[l].astype(BF16), ln1_g=r1(ln1_g[l]), ln1_b=r1(ln1_b[l]),
        w_up=w_up[l].astype(BF16), conv_w=conv_w[l].astype(F32) * half_val, conv_b=r1(conv_b[l]) * half_val,
        w_down=w_down[l].astype(BF16), ln2_g=r1(ln2_g[l]), ln2_b=r1(ln2_b[l]),
    )


def _rope_tables(T):
    inv_freq = ROPE_THETA ** (-np.arange(0, MLA_ROPE, 2, dtype=np.float64) / MLA_ROPE)
    ang = np.arange(T, dtype=np.float64)[:, None] * inv_freq[None, :]
    cos, sin = np.cos(ang), np.sin(ang)
    cos2 = np.concatenate([cos, cos], axis=1)
    sin2 = np.concatenate([-sin, sin], axis=1)
    scale = (MLA_NOPE + MLA_ROPE) ** -0.5 * math.log2(math.e)
    zq = np.zeros((T, MLA_PAD - MLA_NOPE - MLA_ROPE))
    ca = np.concatenate([np.full((T, MLA_NOPE), scale), scale * cos2, zq], axis=1)
    cb = np.concatenate([np.zeros((T, MLA_NOPE)), scale * sin2, zq], axis=1)
    tk = np.concatenate([cos2, sin2, np.zeros((T, LANES - 2 * MLA_ROPE))], axis=1)
    f32 = lambda a: jnp.asarray(np.ascontiguousarray(a), F32)
    return dict(ca=f32(ca.T), cb=f32(cb.T), tk=f32(tk))


def _dft_tables(T):
    n1 = 1 << (int(math.log2(T)) // 2)
    n2 = T // n1

    def cs(num, den):
        ang = (num % den).astype(np.float64) * (2.0 * np.pi / den)
        return np.cos(ang), np.sin(ang)

    def table(x):
        return jnp.asarray(x.astype(np.float32)).astype(BF16)

    t2 = np.arange(n2, dtype=np.int64)[:, None, None]
    k1 = np.arange(n1, dtype=np.int64)[None, :, None]
    t1 = np.arange(n1, dtype=np.int64)[None, None, :]
    c, s = cs(k1 * (t1 * n2 + t2), T)
    m1 = table(np.concatenate([c, -s], axis=1))
    k2 = np.arange(n2, dtype=np.int64)
    c2, s2 = cs(k2[:, None] * k2[None, :], n2)
    cs2 = table(np.concatenate([c2, s2], axis=0))
    cg = np.arange(FN_GROUP_DIM, dtype=np.int64)
    cc, sc = cs(cg[:, None] * cg[None, :], FN_GROUP_DIM)
    norm = 1.0 / math.sqrt(T * FN_GROUP_DIM)
    eye = np.eye(FN_GROUPS)
    w = table(np.concatenate([np.kron(eye, cc), np.kron(eye, sc)], axis=0) * norm)
    return dict(n1=n1, n2=n2, m1=m1, cs2=cs2, w=w)


def _trunk(x, layers):
    B, T, _ = x.shape
    tabs = _rope_tables(T)
    ft = _dft_tables(T)
    h = x.reshape(B * T, D_MODEL)
    for l, lp in enumerate(layers):
        outs = _inproj(h, lp, tabs, l == 0, T)
        if l == 0:
            h = outs[0]
            outs = outs[1:]
        natq, natk, natv, fb, q, k, vt, od = outs
        oa = _natten(natq, natk, natv, lp['na_bias'], B, T)
        ob = _fourier(fb, ft, B, T)
        oct = _mla(q, k, vt, B, T)
        h1 = _outproj(oa, ob, oct, od, h, lp)
        h = _ffn(h1, lp, T)
    return h.reshape(B, T, D_MODEL)


def kernel(x_prompt, x_sample, emb_ln_g, emb_ln_b, w_in, na_rpb, mla_q_g, w_uq, mla_kv_g, w_ukv, sgu_ln_g,
           sgu_ln_b, sgu_w, sgu_b, w_out, ln1_g, ln1_b, w_up, conv_w, conv_b, w_down, ln2_g, ln2_b):
    layers = [_prep_layer(l, emb_ln_g, emb_ln_b, w_in, na_rpb, mla_q_g, w_uq, mla_kv_g, w_ukv, sgu_ln_g,
                          sgu_ln_b, sgu_w, sgu_b, w_out, ln1_g, ln1_b, w_up, conv_w, conv_b, w_down,
                          ln2_g, ln2_b) for l in range(DEPTH)]
    return (_trunk(x_prompt, layers), _trunk(x_sample, layers))
```

```python
import functools
import math

import numpy as np
import jax
import jax.numpy as jnp
from jax import lax
from jax.experimental import pallas as pl
from jax.experimental.pallas import tpu as pltpu

F32 = jnp.float32
BF16 = jnp.bfloat16

D_MODEL = 1024
DEPTH = 2
GRID_W = 64
HEAD_DIM = 64
NA_HEADS = 4
NA_ROWS = 8
NA_COLS = 16
NA_WIDTH = NA_HEADS * HEAD_DIM
FN_GROUPS = 4
FN_GROUP_DIM = 64
FN_WIDTH = FN_GROUPS * FN_GROUP_DIM
MLA_HEADS = 4
MLA_Q_LORA = 256
MLA_KV_LORA = 128
MLA_NOPE = 64
MLA_ROPE = 32
MLA_V = 64
MLA_WIDTH = MLA_HEADS * MLA_V
ROPE_THETA = 10000.0
SGU_HEADS = 4
SGU_HEAD_DIM = 64
SGU_WIDTH = SGU_HEADS * SGU_HEAD_DIM
SGU_CHUNK = 128
D_FF = 2816
ALPHA = (2 * DEPTH) ** 0.25
LN_EPS = 1e-5
RMS_EPS = 1e-6

LANES = 128
SUBLANES = 8
BF16_ROWS = 16
VMEM_LIMIT = 56 * 1024 * 1024

TM_IN = 1024
TM_OUT = 1024
TM_FFN = 1024
FF_CHUNK = 256
NA_QROWS = 4
NA_QBLK = NA_QROWS * GRID_W
NA_KBLKS = 3
MLA_PAD = 128
MLA_TQ = 512
MLA_TK = 512
MLA_VT_CHUNK = 512
MLA_ONES = 16
MLA_UNROLL = 32
MLA_SCORE_BUFS = 4
HALO = SUBLANES

LOG2_E = math.log2(math.e)
HI = lax.Precision.HIGHEST
NT_DIMS = (((1,), (1,)), ((), ()))


def _dot(a, b):
    return jnp.dot(a, b, preferred_element_type=F32)


def _dot_nt(a, b):
    return lax.dot_general(a, b, NT_DIMS, preferred_element_type=F32)


def _layer_norm_rows(x, g, b):
    mu = jnp.mean(x, axis=-1, keepdims=True)
    xc = x - mu
    var = jnp.mean(xc * xc, axis=-1, keepdims=True)
    return xc * lax.rsqrt(var + LN_EPS) * g + b


GELU_C = 0.7978845608028654
GELU_K = 0.044715


def _gelu_tanh_x2(x):
    t = jnp.tanh(x * (GELU_C + (GELU_C * GELU_K) * (x * x)))
    return x + x * t


def _gelu_tanh(x):
    return 0.5 * _gelu_tanh_x2(x)


def _split_dot(v, m):
    hi = v.astype(BF16)
    lo = (v - hi.astype(F32)).astype(BF16)
    return _dot(hi, m) + _dot(lo, m)


def _params(*sem):
    return pltpu.CompilerParams(dimension_semantics=sem, vmem_limit_bytes=VMEM_LIMIT)


def _full(shape):
    nd = len(shape)
    return pl.BlockSpec(shape, lambda *_: (0,) * nd)


def _inproj_kernel(apply_ln, x_ref, eg_ref, eb_ref, wnq_ref, wnk_ref, wnv_ref, wfb_ref, wcq_ref, wckv_ref,
                   wsg_ref, qg_ref, wa_ref, kvg_ref, wkn_ref, wvt_ref, ca_ref, cb_ref,
                   tk_ref, gmat_ref, slg_ref, slb_ref, ws_ref, sb_ref, *out_refs):
    if apply_ln:
        xn_ref, natq_ref, natk_ref, natv_ref, fb_ref, q_ref, k_ref, vt_ref, od_ref = out_refs
    else:
        natq_ref, natk_ref, natv_ref, fb_ref, q_ref, k_ref, vt_ref, od_ref = out_refs
    x = x_ref[...]
    if apply_ln:
        x = _layer_norm_rows(x, eg_ref[...], eb_ref[...])
        xn_ref[...] = x
    xb = x.astype(BF16)
    tm = x.shape[0]

    cq = _dot(xb, wcq_ref[...])
    ckv_kr = _dot(xb, wckv_ref[...])
    sg = _dot(xb, wsg_ref[...])

    natq = (_dot_nt(wnq_ref[...], xb) * (HEAD_DIM ** -0.5 * LOG2_E)).astype(BF16)
    natv = _dot_nt(wnv_ref[...], xb).astype(BF16)
    for c in range(tm // NA_QBLK):
        natq_ref[c] = natq[:, c * NA_QBLK:(c + 1) * NA_QBLK]
        natv_ref[c] = natv[:, c * NA_QBLK:(c + 1) * NA_QBLK]
    natk_ref[...] = _dot(xb, wnk_ref[...]).astype(BF16)

    fb_ref[...] = _dot(xb, wfb_ref[...]).astype(BF16)

    cqn = (cq * lax.rsqrt(jnp.mean(cq * cq, axis=-1, keepdims=True) + RMS_EPS) * qg_ref[...]).astype(BF16)
    qa = _dot_nt(wa_ref[...], cqn)
    ca = ca_ref[...]
    cb = cb_ref[MLA_NOPE:MLA_NOPE + MLA_ROPE]
    half = MLA_ROPE // 2
    for h in range(MLA_HEADS):
        base = h * MLA_PAD
        qh = qa[base:base + MLA_PAD] * ca
        lo = qa[base + MLA_NOPE:base + MLA_NOPE + half]
        hi = qa[base + MLA_NOPE + half:base + MLA_NOPE + MLA_ROPE]
        rope = qh[MLA_NOPE:MLA_NOPE + MLA_ROPE] + jnp.concatenate([hi, lo], axis=0) * cb
        qh = jnp.concatenate([qh[:MLA_NOPE], rope, qh[MLA_NOPE + MLA_ROPE:]], axis=0).astype(BF16)
        for c in range(tm // MLA_TQ):
            q_ref[c, base:base + MLA_PAD, :] = qh[:, c * MLA_TQ:(c + 1) * MLA_TQ]

    ckv = ckv_kr[:, :MLA_KV_LORA]
    ckvn = (ckv * lax.rsqrt(jnp.mean(ckv * ckv, axis=-1, keepdims=True) + RMS_EPS) * kvg_ref[...]).astype(BF16)
    kr = ckv_kr[:, MLA_KV_LORA:] * tk_ref[...]
    kr = pltpu.roll(kr, MLA_NOPE, axis=1) + pltpu.roll(kr, MLA_NOPE - MLA_ROPE, axis=1)
    lane = lax.broadcasted_iota(jnp.int32, (1, MLA_PAD), 1)
    kr = jnp.where((lane >= MLA_NOPE) & (lane < MLA_NOPE + MLA_ROPE), kr, 0.0)
    kn = _dot(ckvn, wkn_ref[...])
    for h in range(MLA_HEADS):
        sl = slice(h * MLA_PAD, (h + 1) * MLA_PAD)
        k_ref[:, sl] = (kn[:, sl] + kr).astype(BF16)
    vt = _dot_nt(wvt_ref[...], ckvn)
    tkc = vt_ref.shape[2]
    for c in range(tm // tkc):
        vt_ref[c] = vt[:, c * tkc:(c + 1) * tkc].astype(BF16)

    sg = _gelu_tanh(sg)
    u = sg[:, :SGU_WIDTH]
    v = sg[:, SGU_WIDTH:]
    gmat = gmat_ref[...]
    mu = _split_dot(v, gmat)
    vc = v - mu
    var = _split_dot(vc * vc, gmat)
    vn = (vc * lax.rsqrt(var + LN_EPS) * slg_ref[...] + slb_ref[...]).astype(BF16)
    head = lax.broadcasted_iota(jnp.int32, (1, SGU_WIDTH), 1) // SGU_HEAD_DIM
    zero = jnp.zeros((), BF16)
    for c in range(tm // SGU_CHUNK):
        rows = slice(c * SGU_CHUNK, (c + 1) * SGU_CHUNK)
        vch = vn[rows]
        stacked = jnp.concatenate([jnp.where(head == g, vch, zero) for g in range(SGU_HEADS)], axis=0)
        mixed = sb_ref[...] + _dot(ws_ref[...], stacked)
        od_ref[rows, :] = (u[rows] * mixed).astype(BF16)


def _inproj(x, lp, tabs, apply_ln, T):
    n = x.shape[0]
    tm = TM_IN
    assert T % tm == 0 and tm % MLA_TQ == 0 and tm % MLA_VT_CHUNK == 0 and tm % SGU_CHUNK == 0 and tm % NA_QBLK == 0
    nt = T // tm
    row = lambda i: (i, 0)
    tab = lambda i: (i % nt, 0)
    weights = [lp['eg'], lp['eb'], lp['w_nq'], lp['w_nk'], lp['w_nv'], lp['w_fb'], lp['w_cq'], lp['w_ckv'], lp['w_sg'],
               lp['q_g'], lp['w_a'], lp['kv_g'], lp['w_kn'], lp['w_vt']]
    tables = [tabs['ca'], tabs['cb'], tabs['tk']]
    tail = [lp['gmat'], lp['sl_g'], lp['sl_b'], lp['w_s'], lp['s_b']]
    tab_t = lambda i: (0, i % nt)
    in_specs = ([pl.BlockSpec((tm, D_MODEL), row)] + [_full(w.shape) for w in weights]
                + [pl.BlockSpec((MLA_PAD, tm), tab_t), pl.BlockSpec((MLA_PAD, tm), tab_t),
                   pl.BlockSpec((tm, LANES), tab)] + [_full(w.shape) for w in tail])
    nat_t = jax.ShapeDtypeStruct((n // NA_QBLK, NA_WIDTH, NA_QBLK), BF16)
    nat_t_spec = pl.BlockSpec((tm // NA_QBLK, NA_WIDTH, NA_QBLK), lambda i: (i, 0, 0))
    out_shape = [nat_t, jax.ShapeDtypeStruct((n, NA_WIDTH), BF16), nat_t,
                 jax.ShapeDtypeStruct((n, FN_WIDTH), BF16),
                 jax.ShapeDtypeStruct((n // MLA_TQ, MLA_HEADS * MLA_PAD, MLA_TQ), BF16),
                 jax.ShapeDtypeStruct((n, MLA_HEADS * MLA_PAD), BF16),
                 jax.ShapeDtypeStruct((n // MLA_VT_CHUNK, MLA_WIDTH, MLA_VT_CHUNK), BF16),
                 jax.ShapeDtypeStruct((n, SGU_WIDTH), BF16)]
    out_specs = [nat_t_spec, pl.BlockSpec((tm, NA_WIDTH), row), nat_t_spec,
                 pl.BlockSpec((tm, FN_WIDTH), row),
                 pl.BlockSpec((tm // MLA_TQ, MLA_HEADS * MLA_PAD, MLA_TQ), lambda i: (i, 0, 0)),
                 pl.BlockSpec((tm, MLA_HEADS * MLA_PAD), row),
                 pl.BlockSpec((tm // MLA_VT_CHUNK, MLA_WIDTH, MLA_VT_CHUNK), lambda i: (i, 0, 0)),
                 pl.BlockSpec((tm, SGU_WIDTH), row)]
    if apply_ln:
        out_shape = [jax.ShapeDtypeStruct((n, D_MODEL), F32)] + out_shape
        out_specs = [pl.BlockSpec((tm, D_MODEL), row)] + out_specs
    return pl.pallas_call(
        functools.partial(_inproj_kernel, apply_ln),
        grid=(n // tm,),
        in_specs=in_specs,
        out_specs=out_specs,
        out_shape=out_shape,
        compiler_params=_params("parallel"),
        name="inproj_ln" if apply_ln else "inproj",
    )(x, *weights, *tables, *tail)


def _natten_kernel(qt_ref, k0_ref, k1_ref, k2_ref, k3_ref, vt0_ref, vt1_ref, vt2_ref, vt3_ref,
                   bias_a_ref, bias_b_ref, o_ref, s_ref):
    k_refs = (k0_ref, k1_ref, k2_ref, k3_ref)
    vt_refs = (vt0_ref, vt1_ref, vt2_ref, vt3_ref)
    row_head = lax.broadcasted_iota(jnp.int32, (NA_WIDTH, 1), 0) // HEAD_DIM
    zero = jnp.zeros((), BF16)
    ones = jnp.ones((BF16_ROWS, NA_QBLK), BF16)

    def block(qi, off, bias_ref):
        qt = qt_ref[qi]

        def scores(h, slot):
            qh = jnp.where(row_head == h, qt, zero)
            for d in range(NA_KBLKS):
                rows = slice(d * NA_QBLK, (d + 1) * NA_QBLK)
                s_ref[slot, rows, :] = _dot(k_refs[off + d][...], qh) + bias_ref[h, rows, :]

        scores(0, 0)
        for h in range(NA_HEADS):
            if h + 1 < NA_HEADS:
                scores(h + 1, (h + 1) % 2)
            s = s_ref[h % 2]
            p = jnp.exp2(s - jnp.max(s, axis=0, keepdims=True)).astype(BF16)
            ch = slice(h * HEAD_DIM, (h + 1) * HEAD_DIM)
            acc = None
            for d in range(NA_KBLKS):
                vte = jnp.concatenate([vt_refs[off + d][ch, :], ones], axis=0)
                part = _dot(vte, p[d * NA_QBLK:(d + 1) * NA_QBLK])
                acc = part if acc is None else acc + part
            o_ref[qi, ch, :] = (acc[:HEAD_DIM] * (1.0 / acc[HEAD_DIM:HEAD_DIM + 1])).astype(BF16)

    i = pl.program_id(1)
    last = pl.num_programs(1) - 1
    for cond, off_a, off_b in ((i == 0, 0, 0), ((i > 0) & (i < last), 0, 1), (i == last, 1, 1)):
        @pl.when(cond)
        def _(off_a=off_a, off_b=off_b):
            block(0, off_a, bias_a_ref)
            block(1, off_b, bias_b_ref)


def _natten(qt, k, vt, bias, B, T):
    nblk = T // NA_QBLK
    npair = nblk // 2
    nwin = NA_KBLKS + 1
    assert nblk % 2 == 0 and nblk >= nwin

    def kv_map(d, rank3):
        def f(b, i):
            blk = b * nblk + jnp.clip(2 * i - 1, 0, nblk - nwin) + d
            return (blk, 0, 0) if rank3 else (blk, 0)
        return f

    bias_blk = (None, NA_HEADS, NA_KBLKS * NA_QBLK, NA_QBLK)
    tblk = (None, NA_WIDTH, NA_QBLK)
    pair = (2, NA_WIDTH, NA_QBLK)
    in_specs = ([pl.BlockSpec(pair, lambda b, i: (b * npair + i, 0, 0))]
                + [pl.BlockSpec((NA_QBLK, NA_WIDTH), kv_map(d, False)) for d in range(nwin)]
                + [pl.BlockSpec(tblk, kv_map(d, True)) for d in range(nwin)]
                + [pl.BlockSpec(bias_blk, lambda b, i: (jnp.where(i == 0, 0, 1), 0, 0, 0)),
                   pl.BlockSpec(bias_blk, lambda b, i: (jnp.where(i == npair - 1, 2, 1), 0, 0, 0))])
    return pl.pallas_call(
        _natten_kernel,
        grid=(B, npair),
        in_specs=in_specs,
        out_specs=pl.BlockSpec(pair, lambda b, i: (b * npair + i, 0, 0)),
        out_shape=jax.ShapeDtypeStruct((B * nblk, NA_WIDTH, NA_QBLK), BF16),
        scratch_shapes=[pltpu.VMEM((2, NA_KBLKS * NA_QBLK, NA_QBLK), F32)],
        compiler_params=_params("parallel", "arbitrary"),
        name="natten",
    )(qt, k, k, k, k, vt, vt, vt, vt, bias, bias)


def _fft1_kernel(x_ref, m_ref, a_ref):
    kk = m_ref.shape[0]
    n1 = x_ref.shape[0]
    xt = jnp.swapaxes(x_ref[...], 0, 1)
    for i in range(kk):
        res = _dot(m_ref[i], xt[i])
        res = jnp.concatenate([res[:n1], res[n1:]], axis=1).astype(BF16)
        a_ref[:, i] = res.reshape(n1 // kk, kk, 2 * FN_WIDTH)


def _fft2_kernel(a_ref, cs_ref, w_ref, y_ref):
    n2, kk, _ = a_ref.shape
    cs = cs_ref[...]
    at = jnp.swapaxes(a_ref[...], 0, 1)
    g = []
    for i in range(kk):
        r = _dot(cs, at[i])
        g.append(jnp.concatenate([r[:n2, :FN_WIDTH] + r[n2:, FN_WIDTH:],
                                  r[:n2, FN_WIDTH:] - r[n2:, :FN_WIDTH]], axis=1).astype(BF16))
    g = jnp.concatenate(g, axis=0)
    y = _dot(g, w_ref[...]).astype(BF16)
    y_ref[...] = jnp.swapaxes(y.reshape(kk, n2, FN_WIDTH), 0, 1)


def _fourier(fb, ft, B, T):
    n1, n2 = ft['n1'], ft['n2']
    kk = BF16_ROWS
    a = pl.pallas_call(
        _fft1_kernel,
        grid=(B, n2 // kk),
        in_specs=[pl.BlockSpec((None, n1, kk, FN_WIDTH), lambda b, j: (b, 0, j, 0)),
                  pl.BlockSpec((kk, 2 * n1, n1), lambda b, j: (j, 0, 0))],
        out_specs=pl.BlockSpec((None, n1 // kk, kk, kk, 2 * FN_WIDTH), lambda b, j: (b, 0, j, 0, 0)),
        out_shape=jax.ShapeDtypeStruct((B, n1 // kk, n2, kk, 2 * FN_WIDTH), BF16),
        compiler_params=_params("parallel", "arbitrary"),
        name="fft_stage1",
    )(fb.reshape(B, n1, n2, FN_WIDTH), ft['m1'])
    y = pl.pallas_call(
        _fft2_kernel,
        grid=(B, n1 // kk),
        in_specs=[pl.BlockSpec((None, None, n2, kk, 2 * FN_WIDTH), lambda b, j: (b, j, 0, 0, 0)),
                  _full(ft['cs2'].shape), _full(ft['w'].shape)],
        out_specs=pl.BlockSpec((None, n2, kk, FN_WIDTH), lambda b, j: (b, 0, j, 0)),
        out_shape=jax.ShapeDtypeStruct((B, n2, n1, FN_WIDTH), BF16),
        compiler_params=_params("parallel", "arbitrary"),
        name="fft_stage2",
    )(a, ft['cs2'], ft['w'])
    return y.reshape(B * T, FN_WIDTH)


def _mla_trip(nq, nk):
    u = MLA_UNROLL
    while u > 2 and ((nq * nk) % u or (nk % u and u % nk)):
        u //= 2
    assert u % 2 == 0 and (nq * nk) % u == 0 and (nk % u == 0 or u % nk == 0)
    return u


def _mla_kernel(q_ref, k_ref, vt_ref, o_ref, s_ref):
    tk = s_ref.shape[1]
    per = tk // MLA_VT_CHUNK
    nk = vt_ref.shape[0] // per
    nq, _, tq = q_ref.shape
    unroll = _mla_trip(nq, nk)
    ones = jnp.ones((MLA_ONES, tk), BF16)

    def scores(i, j):
        start = j * tk if isinstance(j, int) else pl.multiple_of(j * tk, tk)
        return _dot(k_ref[pl.ds(start, tk), :], q_ref[i])

    def update(carry, j, slot):
        m, acc = carry
        st = s_ref[slot]
        m_new = jnp.maximum(m, jnp.max(st, axis=0, keepdims=True))
        alpha = jnp.exp2(m - m_new)
        p = jnp.exp2(st - m_new).astype(BF16)
        vt = jnp.concatenate([vt_ref[j * per + u] for u in range(per)], axis=1)
        vte = jnp.concatenate([vt, ones], axis=0)
        return m_new, alpha * acc + _dot(vte, p)

    nbuf = s_ref.shape[0]
    ahead = nbuf // 2
    last = nq * nk - 1
    for f in range(ahead):
        s_ref[f] = scores(f // nk, f % nk)
    m0 = jnp.full((1, tq), -1e30, F32)
    acc0 = jnp.zeros((MLA_V + MLA_ONES, tq), F32)

    def finish(i, acc):
        o_ref[i] = (acc[:MLA_V] * (1.0 / acc[MLA_V:MLA_V + 1])).astype(o_ref.dtype)

    whole_blocks = unroll % nk == 0

    def body(t, carry):
        if whole_blocks:
            place = lambda c: (t * (unroll // nk) + c // nk, c % nk)
        else:
            i0, j0 = (t * unroll) // nk, (t * unroll) % nk
            place = lambda c: (i0, j0 + c)
            fresh = j0 == 0
            carry = (jnp.where(fresh, m0, carry[0]), jnp.where(fresh, acc0, carry[1]))
        for c in range(unroll):
            if c + ahead < unroll:
                s_ref[(c + ahead) % nbuf] = scores(*place(c + ahead))
            else:
                nxt = jnp.minimum(t * unroll + c + ahead, last)
                s_ref[(c + ahead) % nbuf] = scores(nxt // nk, nxt % nk)
            i, j = place(c)
            if whole_blocks and j == 0:
                carry = (m0, acc0)
            carry = update(carry, j, c % nbuf)
            if whole_blocks and j == nk - 1:
                finish(i, carry[1])

        if not whole_blocks:
            @pl.when(j0 + unroll == nk)
            def _():
                finish(i0, carry[1])

        return carry

    lax.fori_loop(0, nq * nk // unroll, body, (m0, acc0))


def _mla(q, k, vt, B, T):
    nq = T // MLA_TQ
    nk = T // MLA_TK
    unroll = _mla_trip(nq, nk)
    nbuf = MLA_SCORE_BUFS if unroll % MLA_SCORE_BUFS == 0 else 2
    return pl.pallas_call(
        _mla_kernel,
        grid=(B, MLA_HEADS),
        in_specs=[pl.BlockSpec((nq, MLA_PAD, MLA_TQ), lambda b, h: (b, h, 0)),
                  pl.BlockSpec((T, MLA_PAD), lambda b, h: (b, h)),
                  pl.BlockSpec((T // MLA_VT_CHUNK, MLA_V, MLA_VT_CHUNK), lambda b, h: (b, h, 0))],
        out_specs=pl.BlockSpec((nq, MLA_V, MLA_TQ), lambda b, h: (b, h, 0)),
        out_shape=jax.ShapeDtypeStruct((B * nq, MLA_WIDTH, MLA_TQ), BF16),
        scratch_shapes=[pltpu.VMEM((nbuf, MLA_TK, MLA_TQ), F32)],
        compiler_params=_params("parallel", "arbitrary"),
        name="mla",
    )(q, k, vt)


def _outproj_kernel(oat_ref, ob_ref, oct_ref, od_ref, x_ref, w_ref, g_ref, b_ref, o_ref):
    w = NA_WIDTH
    tm = x_ref.shape[0]
    for c in range(tm // NA_QBLK):
        rows = slice(c * NA_QBLK, (c + 1) * NA_QBLK)
        oc = oct_ref[(c * NA_QBLK) // MLA_TQ, :, pl.ds((c * NA_QBLK) % MLA_TQ, NA_QBLK)].T
        y = (_dot(oat_ref[c].T, w_ref[0:w]) + _dot(ob_ref[rows, :], w_ref[w:2 * w])
             + _dot(oc, w_ref[2 * w:3 * w]) + _dot(od_ref[rows, :], w_ref[3 * w:4 * w]))
        o_ref[rows, :] = _layer_norm_rows(ALPHA * x_ref[rows, :] + y, g_ref[...], b_ref[...])


def _outproj(oa, ob, oct, od, x, lp):
    n = x.shape[0]
    tm = TM_OUT
    assert tm % MLA_TQ == 0 and MLA_TQ % NA_QBLK == 0
    row = lambda i: (i, 0)
    return pl.pallas_call(
        _outproj_kernel,
        grid=(n // tm,),
        in_specs=[pl.BlockSpec((tm // NA_QBLK, NA_WIDTH, NA_QBLK), lambda i: (i, 0, 0)),
                  pl.BlockSpec((tm, FN_WIDTH), row),
                  pl.BlockSpec((tm // MLA_TQ, MLA_WIDTH, MLA_TQ), lambda i: (i, 0, 0)),
                  pl.BlockSpec((tm, SGU_WIDTH), row),
                  pl.BlockSpec((tm, D_MODEL), row), _full(lp['w_out'].shape),
                  _full(lp['ln1_g'].shape), _full(lp['ln1_b'].shape)],
        out_specs=pl.BlockSpec((tm, D_MODEL), row),
        out_shape=jax.ShapeDtypeStruct((n, D_MODEL), F32),
        compiler_params=_params("parallel"),
        name="outproj",
    )(oa, ob, oct, od, x, lp['w_out'], lp['ln1_g'], lp['ln1_b'])


def _ffn_kernel(nt, x_ref, xp_ref, xn_ref, wup_ref, cw_ref, cb_ref, wd_ref, g_ref, b_ref, o_ref,
                xe_ref, act_ref):
    i = pl.program_id(0)
    tm = x_ref.shape[0]
    cf = FF_CHUNK
    first = (i % nt) == 0
    last = (i % nt) == nt - 1
    xe_ref[0:HALO] = jnp.where(first, 0.0, xp_ref[...]).astype(BF16)
    xe_ref[HALO:HALO + tm] = x_ref[...].astype(BF16)
    xe_ref[HALO + tm:] = jnp.where(last, 0.0, xn_ref[...]).astype(BF16)
    xe = xe_ref[...]

    rows = slice(HALO, HALO + tm)

    def conv(h, cols):
        prev = pltpu.roll(h, 1, axis=0)
        nxt = pltpu.roll(h, h.shape[0] - 1, axis=0)
        return (prev[rows] * cw_ref[0:1, cols] + h[rows] * cw_ref[1:2, cols]
                + nxt[rows] * cw_ref[2:3, cols] + cb_ref[:, cols])

    for c in range(D_FF // cf):
        gate = slice(c * cf, (c + 1) * cf)
        val = slice(D_FF + c * cf, D_FF + (c + 1) * cf)
        act = _gelu_tanh_x2(conv(_dot(xe, wup_ref[:, gate]), gate)) * conv(_dot(xe, wup_ref[:, val]), val)
        act_ref[:, gate] = act.astype(BF16)

    y = _dot(act_ref[...], wd_ref[...])
    o_ref[...] = _layer_norm_rows(ALPHA * x_ref[...] + y, g_ref[...], b_ref[...])


def _resident(shape):
    nd = len(shape)
    return pl.BlockSpec(shape, lambda *_: (0,) * nd, pipeline_mode=pl.Buffered(1))


def _ffn(x, lp, T):
    n = x.shape[0]
    tm = TM_FFN
    nt = T // tm
    assert T % tm == 0 and D_FF % FF_CHUNK == 0
    hb = tm // HALO
    nhb = n // HALO
    row = lambda i: (i, 0)
    return pl.pallas_call(
        functools.partial(_ffn_kernel, nt),
        grid=(n // tm,),
        in_specs=[pl.BlockSpec((tm, D_MODEL), row),
                  pl.BlockSpec((HALO, D_MODEL), lambda i: (jnp.maximum(i * hb - 1, 0), 0)),
                  pl.BlockSpec((HALO, D_MODEL), lambda i: (jnp.minimum((i + 1) * hb, nhb - 1), 0)),
                  _resident(lp['w_up'].shape), _resident(lp['conv_w'].shape), _resident(lp['conv_b'].shape),
                  _resident(lp['w_down'].shape), _resident(lp['ln2_g'].shape), _resident(lp['ln2_b'].shape)],
        out_specs=pl.BlockSpec((tm, D_MODEL), row),
        out_shape=jax.ShapeDtypeStruct((n, D_MODEL), F32),
        scratch_shapes=[pltpu.VMEM((tm + 2 * HALO, D_MODEL), BF16),
                        pltpu.VMEM((tm, D_FF), BF16)],
        compiler_params=_params("parallel"),
        name="conv_ffn",
    )(x, x, x, lp['w_up'], lp['conv_w'], lp['conv_b'], lp['w_down'], lp['ln2_g'], lp['ln2_b'])


def _natten_bias(rpb):
    nr, nc = 2 * NA_ROWS - 1, 2 * NA_COLS - 1
    krows = NA_KBLKS * NA_QROWS
    col = np.arange(GRID_W)
    ci = np.clip(col[None, :] - col[:, None] + NA_COLS - 1, 0, nc - 1)
    onehot_c = (ci.reshape(-1)[None, :] == np.arange(nc)[:, None]).astype(np.float32)
    cs = np.clip(col - NA_COLS // 2, 0, GRID_W - NA_COLS)
    dc = col[None, :] - cs[:, None]
    valid_c = (dc >= 0) & (dc < NA_COLS)
    qa = np.arange(NA_QROWS)
    kr = np.arange(krows)
    onehot_r, valid_r = [], []
    for kind in range(3):
        q_row = (0, NA_QROWS, 2 * NA_QROWS)[kind] + qa
        r_start = (np.zeros_like(qa), qa, np.full_like(qa, NA_QROWS))[kind]
        dr = kr[None, :] - r_start[:, None]
        valid_r.append((dr >= 0) & (dr < NA_ROWS))
        ri = np.clip(kr[None, :] - q_row[:, None] + NA_ROWS - 1, 0, nr - 1)
        onehot_r.append((ri.reshape(-1)[:, None] == np.arange(nr)[None, :]).astype(np.float32))
    onehot_r = np.stack(onehot_r)
    valid = np.stack(valid_r)[:, :, None, :, None] & valid_c[None, None, :, None, :]
    by_col = jnp.einsum('hrc,cx->hrx', rpb, onehot_c, precision=HI)
    full = jnp.einsum('kpr,hrx->khpx', onehot_r, by_col, precision=HI)
    full = full.reshape(3, NA_HEADS, NA_QROWS, krows, GRID_W, GRID_W).transpose(0, 1, 3, 5, 2, 4)
    valid = valid.transpose(0, 3, 4, 1, 2)
    full = jnp.where(valid[:, None], full * LOG2_E, -1e30)
    return full.reshape(3, NA_HEADS, krows * GRID_W, NA_QBLK).astype(F32)


def _prep_layer(l, emb_ln_g, emb_ln_b, w_in, na_rpb, mla_q_g, w_uq, mla_kv_g, w_ukv, sgu_ln_g, sgu_ln_b,
                sgu_w, sgu_b, w_out, ln1_g, ln1_b, w_up, conv_w, conv_b, w_down, ln2_g, ln2_b):
    wi = w_in[l]
    o_fb = 3 * NA_WIDTH
    o_cq = o_fb + FN_WIDTH
    o_ckv = o_cq + MLA_Q_LORA
    o_kr = o_ckv + MLA_KV_LORA
    o_sg = o_kr + MLA_ROPE
    half = MLA_ROPE // 2
    swap = np.concatenate([np.arange(half, MLA_ROPE), np.arange(half)])
    w_kr = wi[:, o_kr:o_sg]
    w_ckv = jnp.concatenate([wi[:, o_ckv:o_kr], w_kr, w_kr[:, swap],
                             jnp.zeros((D_MODEL, LANES - 2 * MLA_ROPE), F32)], axis=1)

    uq = w_uq[l]
    pad = jnp.zeros((MLA_Q_LORA, MLA_HEADS, MLA_PAD - MLA_NOPE - MLA_ROPE), F32)
    w_a = jnp.concatenate([uq, pad], axis=2).reshape(MLA_Q_LORA, MLA_HEADS * MLA_PAD)
    ukv = w_ukv[l]
    w_kn = jnp.concatenate([ukv[:, :, :MLA_NOPE],
                            jnp.zeros((MLA_KV_LORA, MLA_HEADS, MLA_PAD - MLA_NOPE), F32)],
                           axis=2).reshape(MLA_KV_LORA, MLA_HEADS * MLA_PAD)
    w_vt = ukv[:, :, MLA_NOPE:].reshape(MLA_KV_LORA, MLA_WIDTH).T
    w_s = jnp.concatenate([sgu_w[l][g] for g in range(SGU_HEADS)], axis=1)
    gmat =np.kron(np.eye(SGU_HEADS), np.full((SGU_HEAD_DIM, SGU_HEAD_DIM), 1.0 / SGU_HEAD_DIM)).astype(np.float32)
    s_b = jnp.repeat(sgu_b[l].T, SGU_HEAD_DIM, axis=1)
    r1 = lambda a: a.reshape(1, -1).astype(F32)
    half_val = jnp.concatenate([jnp.ones((1, D_FF), F32), jnp.full((1, D_FF), 0.5, F32)], axis=1)
    return dict(
        eg=r1(emb_ln_g), eb=r1(emb_ln_b),
        w_nq=wi[:, :NA_WIDTH].T.astype(BF16), w_nk=wi[:, NA_WIDTH:2 * NA_WIDTH].astype(BF16),
        w_nv=wi[:, 2 * NA_WIDTH:o_fb].T.astype(BF16), w_fb=wi[:, o_fb:o_cq].astype(BF16),
        w_cq=wi[:, o_cq:o_ckv].astype(BF16), w_ckv=w_ckv.astype(BF16), w_sg=wi[:, o_sg:].astype(BF16),
        q_g=r1(mla_q_g[l]), w_a=w_a.T.astype(BF16),
        kv_g=r1(mla_kv_g[l]), w_kn=w_kn.astype(BF16), w_vt=w_vt.astype(BF16),
        gmat=jnp.asarray(gmat, BF16),
        sl_g=r1(sgu_ln_g[l]), sl_b=r1(sgu_ln_b[l]), w_s=w_s.astype(BF16), s_b=s_b.astype(F32),
        na_bias=_natten_bias(na_rpb[l]),
        w_out=w_out[l].astype(BF16), ln1_g=r1(ln1_g[l]), ln1_b=r1(ln1_b[l]),
        w_up=w_up[l].astype(BF16), conv_w=conv_w[l].astype(F32) * half_val, conv_b=r1(conv_b[l]) * half_val,
        w_down=w_down[l].astype(BF16), ln2_g=r1(ln2_g[l]), ln2_b=r1(ln2_b[l]),
    )


def _rope_tables(T):
    inv_freq = ROPE_THETA ** (-np.arange(0, MLA_ROPE, 2, dtype=np.float64) / MLA_ROPE)
    ang = np.arange(T, dtype=np.float64)[:, None] * inv_freq[None, :]
    cos, sin = np.cos(ang), np.sin(ang)
    cos2 = np.concatenate([cos, cos], axis=1)
    sin2 = np.concatenate([-sin, sin], axis=1)
    scale = (MLA_NOPE + MLA_ROPE) ** -0.5 * math.log2(math.e)
    zq = np.zeros((T, MLA_PAD - MLA_NOPE - MLA_ROPE))
    ca = np.concatenate([np.full((T, MLA_NOPE), scale), scale * cos2, zq], axis=1)
    cb = np.concatenate([np.zeros((T, MLA_NOPE)), scale * sin2, zq], axis=1)
    tk = np.concatenate([cos2, sin2, np.zeros((T, LANES - 2 * MLA_ROPE))], axis=1)
    f32 = lambda a: jnp.asarray(np.ascontiguousarray(a), F32)
    return dict(ca=f32(ca.T), cb=f32(cb.T), tk=f32(tk))


def _dft_tables(T):
    n1 = 1 << (int(math.log2(T)) // 2)
    n2 = T // n1

    def cs(num, den):
        ang = (num % den).astype(np.float64) * (2.0 * np.pi / den)
        return np.cos(ang), np.sin(ang)

    def table(x):
        return jnp.asarray(x.astype(np.float32)).astype(BF16)

    t2 = np.arange(n2, dtype=np.int64)[:, None, None]
    k1 = np.arange(n1, dtype=np.int64)[None, :, None]
    t1 = np.arange(n1, dtype=np.int64)[None, None, :]
    c, s = cs(k1 * (t1 * n2 + t2), T)
    m1 = table(np.concatenate([c, -s], axis=1))
    k2 = np.arange(n2, dtype=np.int64)
    c2, s2 = cs(k2[:, None] * k2[None, :], n2)
    cs2 = table(np.concatenate([c2, s2], axis=0))
    cg = np.arange(FN_GROUP_DIM, dtype=np.int64)
    cc, sc = cs(cg[:, None] * cg[None, :], FN_GROUP_DIM)
    norm = 1.0 / math.sqrt(T * FN_GROUP_DIM)
    eye = np.eye(FN_GROUPS)
    w = table(np.concatenate([np.kron(eye, cc), np.kron(eye, sc)], axis=0) * norm)
    return dict(n1=n1, n2=n2, m1=m1, cs2=cs2, w=w)


def _trunk(x, layers):
    B, T, _ = x.shape
    tabs = _rope_tables(T)
    ft = _dft_tables(T)
    h = x.reshape(B * T, D_MODEL)
    for l, lp in enumerate(layers):
        outs = _inproj(h, lp, tabs, l == 0, T)
        if l == 0:
            h = outs[0]
            outs = outs[1:]
        natq, natk, natv, fb, q, k, vt, od = outs
        oa = _natten(natq, natk, natv, lp['na_bias'], B, T)
        ob = _fourier(fb, ft, B, T)
        oct = _mla(q, k, vt, B, T)
        h1 = _outproj(oa, ob, oct, od, h, lp)
        h = _ffn(h1, lp, T)
    return h.reshape(B, T, D_MODEL)


def kernel(x_prompt, x_sample, emb_ln_g, emb_ln_b, w_in, na_rpb, mla_q_g, w_uq, mla_kv_g, w_ukv, sgu_ln_g,
           sgu_ln_b, sgu_w, sgu_b, w_out, ln1_g, ln1_b, w_up, conv_w, conv_b, w_down, ln2_g, ln2_b):
    layers = [_prep_layer(l, emb_ln_g, emb_ln_b, w_in, na_rpb, mla_q_g, w_uq, mla_kv_g, w_ukv, sgu_ln_g,
                          sgu_ln_b, sgu_w, sgu_b, w_out, ln1_g, ln1_b, w_up, conv_w, conv_b, w_down,
                          ln2_g, ln2_b) for l in range(DEPTH)]
    return (_trunk(x_prompt, layers), _trunk(x_sample, layers))
```

```python
import functools
import math

import numpy as np
import jax
import jax.numpy as jnp
from jax import lax
from jax.experimental import pallas as pl
from jax.experimental.pallas import tpu as pltpu

F32 = jnp.float32
BF16 = jnp.bfloat16

D_MODEL = 1024
DEPTH = 2
GRID_W = 64
HEAD_DIM = 64
NA_HEADS = 4
NA_ROWS = 8
NA_COLS = 16
NA_WIDTH = NA_HEADS * HEAD_DIM
FN_GROUPS = 4
FN_GROUP_DIM = 64
FN_WIDTH = FN_GROUPS * FN_GROUP_DIM
MLA_HEADS = 4
MLA_Q_LORA = 256
MLA_KV_LORA = 128
MLA_NOPE = 64
MLA_ROPE = 32
MLA_V = 64
MLA_WIDTH = MLA_HEADS * MLA_V
ROPE_THETA = 10000.0
SGU_HEADS = 4
SGU_HEAD_DIM = 64
SGU_WIDTH = SGU_HEADS * SGU_HEAD_DIM
SGU_CHUNK = 128
D_FF = 2816
ALPHA = (2 * DEPTH) ** 0.25
LN_EPS = 1e-5
RMS_EPS = 1e-6

LANES = 128
SUBLANES = 8
BF16_ROWS = 16
VMEM_LIMIT = 56 * 1024 * 1024

TM_IN = 1024
TM_OUT = 1024
TM_FFN = 1024
FF_CHUNK = 256
NA_QROWS = 4
NA_QBLK = NA_QROWS * GRID_W
NA_KBLKS = 3
MLA_PAD = 128
MLA_TQ = 512
MLA_TK = 512
MLA_VT_CHUNK = 512
MLA_ONES = 16
MLA_UNROLL = 32
MLA_SCORE_BUFS = 4
HALO = SUBLANES

LOG2_E = math.log2(math.e)
HI = lax.Precision.HIGHEST
NT_DIMS = (((1,), (1,)), ((), ()))


def _dot(a, b):
    return jnp.dot(a, b, preferred_element_type=F32)


def _dot_nt(a, b):
    return lax.dot_general(a, b, NT_DIMS, preferred_element_type=F32)


def _layer_norm_rows(x, g, b):
    mu = jnp.mean(x, axis=-1, keepdims=True)
    xc = x - mu
    var = jnp.mean(xc * xc, axis=-1, keepdims=True)
    return xc * lax.rsqrt(var + LN_EPS) * g + b


GELU_C = 0.7978845608028654
GELU_K = 0.044715


def _gelu_tanh_x2(x):
    t = jnp.tanh(x * (GELU_C + (GELU_C * GELU_K) * (x * x)))
    return x + x * t


def _gelu_tanh(x):
    return 0.5 * _gelu_tanh_x2(x)


def _split_dot(v, m):
    hi = v.astype(BF16)
    lo = (v - hi.astype(F32)).astype(BF16)
    return _dot(hi, m) + _dot(lo, m)


def _params(*sem):
    return pltpu.CompilerParams(dimension_semantics=sem, vmem_limit_bytes=VMEM_LIMIT)


def _full(shape):
    nd = len(shape)
    return pl.BlockSpec(shape, lambda *_: (0,) * nd)


def _inproj_kernel(apply_ln, x_ref, eg_ref, eb_ref, wnq_ref, wnk_ref, wnv_ref, wfb_ref, wcq_ref, wckv_ref,
                   wsg_ref, qg_ref, wa_ref, kvg_ref, wkn_ref, wvt_ref, ca_ref, cb_ref,
                   tk_ref, gmat_ref, slg_ref, slb_ref, ws_ref, sb_ref, *out_refs):
    if apply_ln:
        xn_ref, natq_ref, natk_ref, natv_ref, fb_ref, q_ref, k_ref, vt_ref, od_ref = out_refs
    else:
        natq_ref, natk_ref, natv_ref, fb_ref, q_ref, k_ref, vt_ref, od_ref = out_refs
    x = x_ref[...]
    if apply_ln:
        x = _layer_norm_rows(x, eg_ref[...], eb_ref[...])
        xn_ref[...] = x
    xb = x.astype(BF16)
    tm = x.shape[0]

    cq = _dot(xb, wcq_ref[...])
    ckv_kr = _dot(xb, wckv_ref[...])
    sg = _dot(xb, wsg_ref[...])

    natq = (_dot_nt(wnq_ref[...], xb) * (HEAD_DIM ** -0.5 * LOG2_E)).astype(BF16)
    natv = _dot_nt(wnv_ref[...], xb).astype(BF16)
    for c in range(tm // NA_QBLK):
        natq_ref[c] = natq[:, c * NA_QBLK:(c + 1) * NA_QBLK]
        natv_ref[c] = natv[:, c * NA_QBLK:(c + 1) * NA_QBLK]
    natk_ref[...] = _dot(xb, wnk_ref[...]).astype(BF16)

    fb_ref[...] = _dot(xb, wfb_ref[...]).astype(BF16)

    cqn = (cq * lax.rsqrt(jnp.mean(cq * cq, axis=-1, keepdims=True) + RMS_EPS) * qg_ref[...]).astype(BF16)
    qa = _dot_nt(wa_ref[...], cqn)
    ca = ca_ref[...]
    cb = cb_ref[MLA_NOPE:MLA_NOPE + MLA_ROPE]
    half = MLA_ROPE // 2
    for h in range(MLA_HEADS):
        base = h * MLA_PAD
        qh = qa[base:base + MLA_PAD] * ca
        lo = qa[base + MLA_NOPE:base + MLA_NOPE + half]
        hi = qa[base + MLA_NOPE + half:base + MLA_NOPE + MLA_ROPE]
        rope = qh[MLA_NOPE:MLA_NOPE + MLA_ROPE] + jnp.concatenate([hi, lo], axis=0) * cb
        qh = jnp.concatenate([qh[:MLA_NOPE], rope, qh[MLA_NOPE + MLA_ROPE:]], axis=0).astype(BF16)
        for c in range(tm // MLA_TQ):
            q_ref[c, base:base + MLA_PAD, :] = qh[:, c * MLA_TQ:(c + 1) * MLA_TQ]

    ckv = ckv_kr[:, :MLA_KV_LORA]
    ckvn = (ckv * lax.rsqrt(jnp.mean(ckv * ckv, axis=-1, keepdims=True) + RMS_EPS) * kvg_ref[...]).astype(BF16)
    kr = ckv_kr[:, MLA_KV_LORA:] * tk_ref[...]
    kr = pltpu.roll(kr, MLA_NOPE, axis=1) + pltpu.roll(kr, MLA_NOPE - MLA_ROPE, axis=1)
    lane = lax.broadcasted_iota(jnp.int32, (1, MLA_PAD), 1)
    kr = jnp.where((lane >= MLA_NOPE) & (lane < MLA_NOPE + MLA_ROPE), kr, 0.0)
    kn = _dot(ckvn, wkn_ref[...])
    for h in range(MLA_HEADS):
        sl = slice(h * MLA_PAD, (h + 1) * MLA_PAD)
        k_ref[:, sl] = (kn[:, sl] + kr).astype(BF16)
    vt = _dot_nt(wvt_ref[...], ckvn)
    tkc = vt_ref.shape[2]
    for c in range(tm // tkc):
        vt_ref[c] = vt[:, c * tkc:(c + 1) * tkc].astype(BF16)

    sg = _gelu_tanh(sg)
    u = sg[:, :SGU_WIDTH]
    v = sg[:, SGU_WIDTH:]
    gmat = gmat_ref[...]
    mu = _split_dot(v, gmat)
    vc = v - mu
    var = _split_dot(vc * vc, gmat)
    vn = (vc * lax.rsqrt(var + LN_EPS) * slg_ref[...] + slb_ref[...]).astype(BF16)
    head = lax.broadcasted_iota(jnp.int32, (1, SGU_WIDTH), 1) // SGU_HEAD_DIM
    zero = jnp.zeros((), BF16)
    for c in range(tm // SGU_CHUNK):
        rows = slice(c * SGU_CHUNK, (c + 1) * SGU_CHUNK)
        vch = vn[rows]
        stacked = jnp.concatenate([jnp.where(head == g, vch, zero) for g in range(SGU_HEADS)], axis=0)
        mixed = sb_ref[...] + _dot(ws_ref[...], stacked)
        od_ref[rows, :] = (u[rows] * mixed).astype(BF16)


def _inproj(x, lp, tabs, apply_ln, T):
    n = x.shape[0]
    tm = TM_IN
    assert T % tm == 0 and tm % MLA_TQ == 0 and tm % MLA_VT_CHUNK == 0 and tm % SGU_CHUNK == 0 and tm % NA_QBLK == 0
    nt = T // tm
    row = lambda i: (i, 0)
    tab = lambda i: (i % nt, 0)
    weights = [lp['eg'], lp['eb'], lp['w_nq'], lp['w_nk'], lp['w_nv'], lp['w_fb'], lp['w_cq'], lp['w_ckv'], lp['w_sg'],
               lp['q_g'], lp['w_a'], lp['kv_g'], lp['w_kn'], lp['w_vt']]
    tables = [tabs['ca'], tabs['cb'], tabs['tk']]
    tail = [lp['gmat'], lp['sl_g'], lp['sl_b'], lp['w_s'], lp['s_b']]
    tab_t = lambda i: (0, i % nt)
    in_specs = ([pl.BlockSpec((tm, D_MODEL), row)] + [_full(w.shape) for w in weights]
                + [pl.BlockSpec((MLA_PAD, tm), tab_t), pl.BlockSpec((MLA_PAD, tm), tab_t),
                   pl.BlockSpec((tm, LANES), tab)] + [_full(w.shape) for w in tail])
    nat_t = jax.ShapeDtypeStruct((n // NA_QBLK, NA_WIDTH, NA_QBLK), BF16)
    nat_t_spec = pl.BlockSpec((tm // NA_QBLK, NA_WIDTH, NA_QBLK), lambda i: (i, 0, 0))
    out_shape = [nat_t, jax.ShapeDtypeStruct((n, NA_WIDTH), BF16), nat_t,
                 jax.ShapeDtypeStruct((n, FN_WIDTH), BF16),
                 jax.ShapeDtypeStruct((n // MLA_TQ, MLA_HEADS * MLA_PAD, MLA_TQ), BF16),
                 jax.ShapeDtypeStruct((n, MLA_HEADS * MLA_PAD), BF16),
                 jax.ShapeDtypeStruct((n // MLA_VT_CHUNK, MLA_WIDTH, MLA_VT_CHUNK), BF16),
                 jax.ShapeDtypeStruct((n, SGU_WIDTH), BF16)]
    out_specs = [nat_t_spec, pl.BlockSpec((tm, NA_WIDTH), row), nat_t_spec,
                 pl.BlockSpec((tm, FN_WIDTH), row),
                 pl.BlockSpec((tm // MLA_TQ, MLA_HEADS * MLA_PAD, MLA_TQ), lambda i: (i, 0, 0)),
                 pl.BlockSpec((tm, MLA_HEADS * MLA_PAD), row),
                 pl.BlockSpec((tm // MLA_VT_CHUNK, MLA_WIDTH, MLA_VT_CHUNK), lambda i: (i, 0, 0)),
                 pl.BlockSpec((tm, SGU_WIDTH), row)]
    if apply_ln:
        out_shape = [jax.ShapeDtypeStruct((n, D_MODEL), F32)] + out_shape
        out_specs = [pl.BlockSpec((tm, D_MODEL), row)] + out_specs
    return pl.pallas_call(
        functools.partial(_inproj_kernel, apply_ln),
        grid=(n // tm,),
        in_specs=in_specs,
        out_specs=out_specs,
        out_shape=out_shape,
        compiler_params=_params("parallel"),
        name="inproj_ln" if apply_ln else "inproj",
    )(x, *weights, *tables, *tail)


def _natten_kernel(qt_ref, k0_ref, k1_ref, k2_ref, k3_ref, vt0_ref, vt1_ref, vt2_ref, vt3_ref,
                   bias_a_ref, bias_b_ref, o_ref, s_ref):
    k_refs = (k0_ref, k1_ref, k2_ref, k3_ref)
    vt_refs = (vt0_ref, vt1_ref, vt2_ref, vt3_ref)
    row_head = lax.broadcasted_iota(jnp.int32, (NA_WIDTH, 1), 0) // HEAD_DIM
    zero = jnp.zeros((), BF16)
    ones = jnp.ones((BF16_ROWS, NA_QBLK), BF16)

    def block(qi, off, bias_ref):
        qt = qt_ref[qi]

        def scores(h, slot):
            qh = jnp.where(row_head == h, qt, zero)
            for d in range(NA_KBLKS):
                rows = slice(d * NA_QBLK, (d + 1) * NA_QBLK)
                s_ref[slot, rows, :] = _dot(k_refs[off + d][...], qh) + bias_ref[h, rows, :]

        scores(0, 0)
        for h in range(NA_HEADS):
            if h + 1 < NA_HEADS:
                scores(h + 1, (h + 1) % 2)
            s = s_ref[h % 2]
            p = jnp.exp2(s - jnp.max(s, axis=0, keepdims=True)).astype(BF16)
            ch = slice(h * HEAD_DIM, (h + 1) * HEAD_DIM)
            acc = None
            for d in range(NA_KBLKS):
                vte = jnp.concatenate([vt_refs[off + d][ch, :], ones], axis=0)
                part = _dot(vte, p[d * NA_QBLK:(d + 1) * NA_QBLK])
                acc = part if acc is None else acc + part
            o_ref[qi, ch, :] = (acc[:HEAD_DIM] * (1.0 / acc[HEAD_DIM:HEAD_DIM + 1])).astype(BF16)

    i = pl.program_id(1)
    last = pl.num_programs(1) - 1
    for cond, off_a, off_b in ((i == 0, 0, 0), ((i > 0) & (i < last), 0, 1), (i == last, 1, 1)):
        @pl.when(cond)
        def _(off_a=off_a, off_b=off_b):
            block(0, off_a, bias_a_ref)
            block(1, off_b, bias_b_ref)


def _natten(qt, k, vt, bias, B, T):
    nblk = T // NA_QBLK
    npair = nblk // 2
    nwin = NA_KBLKS + 1
    assert nblk % 2 == 0 and nblk >= nwin

    def kv_map(d, rank3):
        def f(b, i):
            blk = b * nblk + jnp.clip(2 * i - 1, 0, nblk - nwin) + d
            return (blk, 0, 0) if rank3 else (blk, 0)
        return f

    bias_blk = (None, NA_HEADS, NA_KBLKS * NA_QBLK, NA_QBLK)
    tblk = (None, NA_WIDTH, NA_QBLK)
    pair = (2, NA_WIDTH, NA_QBLK)
    in_specs = ([pl.BlockSpec(pair, lambda b, i: (b * npair + i, 0, 0))]
                + [pl.BlockSpec((NA_QBLK, NA_WIDTH), kv_map(d, False)) for d in range(nwin)]
                + [pl.BlockSpec(tblk, kv_map(d, True)) for d in range(nwin)]
                + [pl.BlockSpec(bias_blk, lambda b, i: (jnp.where(i == 0, 0, 1), 0, 0, 0)),
                   pl.BlockSpec(bias_blk, lambda b, i: (jnp.where(i == npair - 1, 2, 1), 0, 0, 0))])
    return pl.pallas_call(
        _natten_kernel,
        grid=(B, npair),
        in_specs=in_specs,
        out_specs=pl.BlockSpec(pair, lambda b, i: (b * npair + i, 0, 0)),
        out_shape=jax.ShapeDtypeStruct((B * nblk, NA_WIDTH, NA_QBLK), BF16),
        scratch_shapes=[pltpu.VMEM((2, NA_KBLKS * NA_QBLK, NA_QBLK), F32)],
        compiler_params=_params("parallel", "arbitrary"),
        name="natten",
    )(qt, k, k, k, k, vt, vt, vt, vt, bias, bias)


def _fft1_kernel(x_ref, m_ref, a_ref):
    kk = m_ref.shape[0]
    n1 = x_ref.shape[0]
    xt = jnp.swapaxes(x_ref[...], 0, 1)
    for i in range(kk):
        res = _dot(m_ref[i], xt[i])
        res = jnp.concatenate([res[:n1], res[n1:]], axis=1).astype(BF16)
        a_ref[:, i] = res.reshape(n1 // kk, kk, 2 * FN_WIDTH)


def _fft2_kernel(a_ref, cs_ref, w_ref, y_ref):
    n2, kk, _ = a_ref.shape
    cs = cs_ref[...]
    at = jnp.swapaxes(a_ref[...], 0, 1)
    g = []
    for i in range(kk):
        r = _dot(cs, at[i])
        g.append(jnp.concatenate([r[:n2, :FN_WIDTH] + r[n2:, FN_WIDTH:],
                                  r[:n2, FN_WIDTH:] - r[n2:, :FN_WIDTH]], axis=1).astype(BF16))
    g = jnp.concatenate(g, axis=0)
    y = _dot(g, w_ref[...]).astype(BF16)
    y_ref[...] = jnp.swapaxes(y.reshape(kk, n2, FN_WIDTH), 0, 1)


def _fourier(fb, ft, B, T):
    n1, n2 = ft['n1'], ft['n2']
    kk = BF16_ROWS
    a = pl.pallas_call(
        _fft1_kernel,
        grid=(B, n2 // kk),
        in_specs=[pl.BlockSpec((None, n1, kk, FN_WIDTH), lambda b, j: (b, 0, j, 0)),
                  pl.BlockSpec((kk, 2 * n1, n1), lambda b, j: (j, 0, 0))],
        out_specs=pl.BlockSpec((None, n1 // kk, kk, kk, 2 * FN_WIDTH), lambda b, j: (b, 0, j, 0, 0)),
        out_shape=jax.ShapeDtypeStruct((B, n1 // kk, n2, kk, 2 * FN_WIDTH), BF16),
        compiler_params=_params("parallel", "arbitrary"),
        name="fft_stage1",
    )(fb.reshape(B, n1, n2, FN_WIDTH), ft['m1'])
    y = pl.pallas_call(
        _fft2_kernel,
        grid=(B, n1 // kk),
        in_specs=[pl.BlockSpec((None, None, n2, kk, 2 * FN_WIDTH), lambda b, j: (b, j, 0, 0, 0)),
                  _full(ft['cs2'].shape), _full(ft['w'].shape)],
        out_specs=pl.BlockSpec((None, n2, kk, FN_WIDTH), lambda b, j: (b, 0, j, 0)),
        out_shape=jax.ShapeDtypeStruct((B, n2, n1, FN_WIDTH), BF16),
        compiler_params=_params("parallel", "arbitrary"),
        name="fft_stage2",
    )(a, ft['cs2'], ft['w'])
    return y.reshape(B * T, FN_WIDTH)


def _mla_trip(nq, nk):
    u = MLA_UNROLL
    while u > 2 and ((nq * nk) % u or (nk % u and u % nk)):
        u //= 2
    assert u % 2 == 0 and (nq * nk) % u == 0 and (nk % u == 0 or u % nk == 0)
    return u


def _mla_kernel(q_ref, k_ref, vt_ref, o_ref, s_ref):
    tk = s_ref.shape[1]
    per = tk // MLA_VT_CHUNK
    nk = vt_ref.shape[0] // per
    nq, _, tq = q_ref.shape
    unroll = _mla_trip(nq, nk)
    ones = jnp.ones((MLA_ONES, tk), BF16)

    def scores(i, j):
        start = j * tk if isinstance(j, int) else pl.multiple_of(j * tk, tk)
        return _dot(k_ref[pl.ds(start, tk), :], q_ref[i])

    def update(carry, j, slot):
        m, acc = carry
        st = s_ref[slot]
        m_new = jnp.maximum(m, jnp.max(st, axis=0, keepdims=True))
        alpha = jnp.exp2(m - m_new)
        p = jnp.exp2(st - m_new).astype(BF16)
        vt = jnp.concatenate([vt_ref[j * per + u] for u in range(per)], axis=1)
        vte = jnp.concatenate([vt, ones], axis=0)
        return m_new, alpha * acc + _dot(vte, p)

    nbuf = s_ref.shape[0]
    ahead = nbuf // 2
    last = nq * nk - 1
    for f in range(ahead):
        s_ref[f] = scores(f // nk, f % nk)
    m0 = jnp.full((1, tq), -1e30, F32)
    acc0 = jnp.zeros((MLA_V + MLA_ONES, tq), F32)

    def finish(i, acc):
        o_ref[i] = (acc[:MLA_V] * (1.0 / acc[MLA_V:MLA_V + 1])).astype(o_ref.dtype)

    whole_blocks = unroll > nk

    def body(t, carry):
        if whole_blocks:
            place = lambda c: (t * (unroll // nk) + c // nk, c % nk)
        else:
            i0, j0 = (t * unroll) // nk, (t * unroll) % nk
            place = lambda c: (i0, j0 + c)
            fresh = j0 == 0
            carry = (jnp.where(fresh, m0, carry[0]), jnp.where(fresh, acc0, carry[1]))
        for c in range(unroll):
            if c + ahead < unroll:
                s_ref[(c + ahead) % nbuf] = scores(*place(c + ahead))
            else:
                nxt = jnp.minimum(t * unroll + c + ahead, last)
                s_ref[(c + ahead) % nbuf] = scores(nxt // nk, nxt % nk)
            i, j = place(c)
            if whole_blocks and j == 0:
                carry = (m0, acc0)
            carry = update(carry, j, c % nbuf)
            if whole_blocks and j == nk - 1:
                finish(i, carry[1])

        if not whole_blocks:
            @pl.when(j0 + unroll == nk)
            def _():
                finish(i0, carry[1])

        return carry

    lax.fori_loop(0, nq * nk // unroll, body, (m0, acc0))


def _mla(q, k, vt, B, T):
    nq = T // MLA_TQ
    nk = T // MLA_TK
    unroll = _mla_trip(nq, nk)
    nbuf = MLA_SCORE_BUFS if unroll % MLA_SCORE_BUFS == 0 else 2
    return pl.pallas_call(
        _mla_kernel,
        grid=(B, MLA_HEADS),
        in_specs=[pl.BlockSpec((nq, MLA_PAD, MLA_TQ), lambda b, h: (b, h, 0)),
                  pl.BlockSpec((T, MLA_PAD), lambda b, h: (b, h)),
                  pl.BlockSpec((T // MLA_VT_CHUNK, MLA_V, MLA_VT_CHUNK), lambda b, h: (b, h, 0))],
        out_specs=pl.BlockSpec((nq, MLA_V, MLA_TQ), lambda b, h: (b, h, 0)),
        out_shape=jax.ShapeDtypeStruct((B * nq, MLA_WIDTH, MLA_TQ), BF16),
        scratch_shapes=[pltpu.VMEM((nbuf, MLA_TK, MLA_TQ), F32)],
        compiler_params=_params("parallel", "arbitrary"),
        name="mla",
    )(q, k, vt)


def _outproj_kernel(oat_ref, ob_ref, oct_ref, od_ref, x_ref, w_ref, g_ref, b_ref, o_ref):
    w = NA_WIDTH
    tm = x_ref.shape[0]
    for c in range(tm // NA_QBLK):
        rows = slice(c * NA_QBLK, (c + 1) * NA_QBLK)
        oc = oct_ref[(c * NA_QBLK) // MLA_TQ, :, pl.ds((c * NA_QBLK) % MLA_TQ, NA_QBLK)].T
        y = (_dot(oat_ref[c].T, w_ref[0:w]) + _dot(ob_ref[rows, :], w_ref[w:2 * w])
             + _dot(oc, w_ref[2 * w:3 * w]) + _dot(od_ref[rows, :], w_ref[3 * w:4 * w]))
        o_ref[rows, :] = _layer_norm_rows(ALPHA * x_ref[rows, :] + y, g_ref[...], b_ref[...])


def _outproj(oa, ob, oct, od, x, lp):
    n = x.shape[0]
    tm = TM_OUT
    assert tm % MLA_TQ == 0 and MLA_TQ % NA_QBLK == 0
    row = lambda i: (i, 0)
    return pl.pallas_call(
        _outproj_kernel,
        grid=(n // tm,),
        in_specs=[pl.BlockSpec((tm // NA_QBLK, NA_WIDTH, NA_QBLK), lambda i: (i, 0, 0)),
                  pl.BlockSpec((tm, FN_WIDTH), row),
                  pl.BlockSpec((tm // MLA_TQ, MLA_WIDTH, MLA_TQ), lambda i: (i, 0, 0)),
                  pl.BlockSpec((tm, SGU_WIDTH), row),
                  pl.BlockSpec((tm, D_MODEL), row), _full(lp['w_out'].shape),
                  _full(lp['ln1_g'].shape), _full(lp['ln1_b'].shape)],
        out_specs=pl.BlockSpec((tm, D_MODEL), row),
        out_shape=jax.ShapeDtypeStruct((n, D_MODEL), F32),
        compiler_params=_params("parallel"),
        name="outproj",
    )(oa, ob, oct, od, x, lp['w_out'], lp['ln1_g'], lp['ln1_b'])


def _ffn_kernel(nt, x_ref, xp_ref, xn_ref, wup_ref, cw_ref, cb_ref, wd_ref, g_ref, b_ref, o_ref,
                xe_ref, act_ref):
    i = pl.program_id(0)
    tm = x_ref.shape[0]
    cf = FF_CHUNK
    first = (i % nt) == 0
    last = (i % nt) == nt - 1
    xe_ref[0:HALO] = jnp.where(first, 0.0, xp_ref[...]).astype(BF16)
    xe_ref[HALO:HALO + tm] = x_ref[...].astype(BF16)
    xe_ref[HALO + tm:] = jnp.where(last, 0.0, xn_ref[...]).astype(BF16)
    xe = xe_ref[...]

    rows = slice(HALO, HALO + tm)

    def conv(h, cols):
        prev = pltpu.roll(h, 1, axis=0)
        nxt = pltpu.roll(h, h.shape[0] - 1, axis=0)
        return (prev[rows] * cw_ref[0:1, cols] + h[rows] * cw_ref[1:2, cols]
                + nxt[rows] * cw_ref[2:3, cols] + cb_ref[:, cols])

    for c in range(D_FF // cf):
        gate = slice(c * cf, (c + 1) * cf)
        val = slice(D_FF + c * cf, D_FF + (c + 1) * cf)
        act = _gelu_tanh_x2(conv(_dot(xe, wup_ref[:, gate]), gate)) * conv(_dot(xe, wup_ref[:, val]), val)
        act_ref[:, gate] = act.astype(BF16)

    y = _dot(act_ref[...], wd_ref[...])
    o_ref[...] = _layer_norm_rows(ALPHA * x_ref[...] + y, g_ref[...], b_ref[...])


def _resident(shape):
    nd = len(shape)
    return pl.BlockSpec(shape, lambda *_: (0,) * nd, pipeline_mode=pl.Buffered(1))


def _ffn(x, lp, T):
    n = x.shape[0]
    tm = TM_FFN
    nt = T // tm
    assert T % tm == 0 and D_FF % FF_CHUNK == 0
    hb = tm // HALO
    nhb = n // HALO
    row = lambda i: (i, 0)
    return pl.pallas_call(
        functools.partial(_ffn_kernel, nt),
        grid=(n // tm,),
        in_specs=[pl.BlockSpec((tm, D_MODEL), row),
                  pl.BlockSpec((HALO, D_MODEL), lambda i: (jnp.maximum(i * hb - 1, 0), 0)),
                  pl.BlockSpec((HALO, D_MODEL), lambda i: (jnp.minimum((i + 1) * hb, nhb - 1), 0)),
                  _resident(lp['w_up'].shape), _resident(lp['conv_w'].shape), _resident(lp['conv_b'].shape),
                  _resident(lp['w_down'].shape), _resident(lp['ln2_g'].shape), _resident(lp['ln2_b'].shape)],
        out_specs=pl.BlockSpec((tm, D_MODEL), row),
        out_shape=jax.ShapeDtypeStruct((n, D_MODEL), F32),
        scratch_shapes=[pltpu.VMEM((tm + 2 * HALO, D_MODEL), BF16),
                        pltpu.VMEM((tm, D_FF), BF16)],
        compiler_params=_params("parallel"),
        name="conv_ffn",
    )(x, x, x, lp['w_up'], lp['conv_w'], lp['conv_b'], lp['w_down'], lp['ln2_g'], lp['ln2_b'])


def _natten_bias(rpb):
    nr, nc = 2 * NA_ROWS - 1, 2 * NA_COLS - 1
    krows = NA_KBLKS * NA_QROWS
    col = np.arange(GRID_W)
    ci = np.clip(col[None, :] - col[:, None] + NA_COLS - 1, 0, nc - 1)
    onehot_c = (ci.reshape(-1)[None, :] == np.arange(nc)[:, None]).astype(np.float32)
    cs = np.clip(col - NA_COLS // 2, 0, GRID_W - NA_COLS)
    dc = col[None, :] - cs[:, None]
    valid_c = (dc >= 0) & (dc < NA_COLS)
    qa = np.arange(NA_QROWS)
    kr = np.arange(krows)
    onehot_r, valid_r = [], []
    for kind in range(3):
        q_row = (0, NA_QROWS, 2 * NA_QROWS)[kind] + qa
        r_start = (np.zeros_like(qa), qa, np.full_like(qa, NA_QROWS))[kind]
        dr = kr[None, :] - r_start[:, None]
        valid_r.append((dr >= 0) & (dr < NA_ROWS))
        ri = np.clip(kr[None, :] - q_row[:, None] + NA_ROWS - 1, 0, nr - 1)
        onehot_r.append((ri.reshape(-1)[:, None] == np.arange(nr)[None, :]).astype(np.float32))
    onehot_r = np.stack(onehot_r)
    valid = np.stack(valid_r)[:, :, None, :, None] & valid_c[None, None, :, None, :]
    by_col = jnp.einsum('hrc,cx->hrx', rpb, onehot_c, precision=HI)
    full = jnp.einsum('kpr,hrx->khpx', onehot_r, by_col, precision=HI)
    full = full.reshape(3, NA_HEADS, NA_QROWS, krows, GRID_W, GRID_W).transpose(0, 1, 3, 5, 2, 4)
    valid = valid.transpose(0, 3, 4, 1, 2)
    full = jnp.where(valid[:, None], full * LOG2_E, -1e30)
    return full.reshape(3, NA_HEADS, krows * GRID_W, NA_QBLK).astype(F32)


def _prep_layer(l, emb_ln_g, emb_ln_b, w_in, na_rpb, mla_q_g, w_uq, mla_kv_g, w_ukv, sgu_ln_g, sgu_ln_b,
                sgu_w, sgu_b, w_out, ln1_g, ln1_b, w_up, conv_w, conv_b, w_down, ln2_g, ln2_b):
    wi = w_in[l]
    o_fb = 3 * NA_WIDTH
    o_cq = o_fb + FN_WIDTH
    o_ckv = o_cq + MLA_Q_LORA
    o_kr = o_ckv + MLA_KV_LORA
    o_sg = o_kr + MLA_ROPE
    half = MLA_ROPE // 2
    swap = np.concatenate([np.arange(half, MLA_ROPE), np.arange(half)])
    w_kr = wi[:, o_kr:o_sg]
    w_ckv = jnp.concatenate([wi[:, o_ckv:o_kr], w_kr, w_kr[:, swap],
                             jnp.zeros((D_MODEL, LANES - 2 * MLA_ROPE), F32)], axis=1)

    uq = w_uq[l]
    pad = jnp.zeros((MLA_Q_LORA, MLA_HEADS, MLA_PAD - MLA_NOPE - MLA_ROPE), F32)
    w_a = jnp.concatenate([uq, pad], axis=2).reshape(MLA_Q_LORA, MLA_HEADS * MLA_PAD)
    ukv = w_ukv[l]
    w_kn = jnp.concatenate([ukv[:, :, :MLA_NOPE],
                            jnp.zeros((MLA_KV_LORA, MLA_HEADS, MLA_PAD - MLA_NOPE), F32)],
                           axis=2).reshape(MLA_KV_LORA, MLA_HEADS * MLA_PAD)
    w_vt = ukv[:, :, MLA_NOPE:].reshape(MLA_KV_LORA, MLA_WIDTH).T
    w_s = jnp.concatenate([sgu_w[l][g] for g in range(SGU_HEADS)], axis=1)
    gmat =np.kron(np.eye(SGU_HEADS), np.full((SGU_HEAD_DIM, SGU_HEAD_DIM), 1.0 / SGU_HEAD_DIM)).astype(np.float32)
    s_b = jnp.repeat(sgu_b[l].T, SGU_HEAD_DIM, axis=1)
    r1 = lambda a: a.reshape(1, -1).astype(F32)
    half_val = jnp.concatenate([jnp.ones((1, D_FF), F32), jnp.full((1, D_FF), 0.5, F32)], axis=1)
    return dict(
        eg=r1(emb_ln_g), eb=r1(emb_ln_b),
        w_nq=wi[:, :NA_WIDTH].T.astype(BF16), w_nk=wi[:, NA_WIDTH:2 * NA_WIDTH].astype(BF16),
        w_nv=wi[:, 2 * NA_WIDTH:o_fb].T.astype(BF16), w_fb=wi[:, o_fb:o_cq].astype(BF16),
        w_cq=wi[:, o_cq:o_ckv].astype(BF16), w_ckv=w_ckv.astype(BF16), w_sg=wi[:, o_sg:].astype(BF16),
        q_g=r1(mla_q_g[l]), w_a=w_a.T.astype(BF16),
        kv_g=r1(mla_kv_g[l]), w_kn=w_kn.astype(BF16), w_vt=w_vt.astype(BF16),
        gmat=jnp.asarray(gmat, BF16),
        sl_g=r1(sgu_ln_g[l]), sl_b=r1(sgu_ln_b[l]), w_s=w_s.astype(BF16), s_b=s_b.astype(F32),
        na_bias=_natten_bias(na_rpb[l]),
        w_out=w_out[l].astype(BF16), ln1_g=r1(ln1_g[l]), ln1_b=r1(ln1_b[l]),
        w_up=w_up[l].astype(BF16), conv_w=conv_w[l].astype(F32) * half_val, conv_b=r1(conv_b[l]) * half_val,
        w_down=w_down[l].astype(BF16), ln2_g=r1(ln2_g[l]), ln2_b=r1(ln2_b[l]),
    )


def _rope_tables(T):
    inv_freq = ROPE_THETA ** (-np.arange(0, MLA_ROPE, 2, dtype=np.float64) / MLA_ROPE)
    ang = np.arange(T, dtype=np.float64)[:, None] * inv_freq[None, :]
    cos, sin = np.cos(ang), np.sin(ang)
    cos2 = np.concatenate([cos, cos], axis=1)
    sin2 = np.concatenate([-sin, sin], axis=1)
    scale = (MLA_NOPE + MLA_ROPE) ** -0.5 * math.log2(math.e)
    zq = np.zeros((T, MLA_PAD - MLA_NOPE - MLA_ROPE))
    ca = np.concatenate([np.full((T, MLA_NOPE), scale), scale * cos2, zq], axis=1)
    cb = np.concatenate([np.zeros((T, MLA_NOPE)), scale * sin2, zq], axis=1)
    tk = np.concatenate([cos2, sin2, np.zeros((T, LANES - 2 * MLA_ROPE))], axis=1)
    f32 = lambda a: jnp.asarray(np.ascontiguousarray(a), F32)
    return dict(ca=f32(ca.T), cb=f32(cb.T), tk=f32(tk))


def _dft_tables(T):
    n1 = 1 << (int(math.log2(T)) // 2)
    n2 = T // n1

    def cs(num, den):
        ang = (num % den).astype(np.float64) * (2.0 * np.pi / den)
        return np.cos(ang), np.sin(ang)

    def table(x):
        return jnp.asarray(x.astype(np.float32)).astype(BF16)

    t2 = np.arange(n2, dtype=np.int64)[:, None, None]
    k1 = np.arange(n1, dtype=np.int64)[None, :, None]
    t1 = np.arange(n1, dtype=np.int64)[None, None, :]
    c, s = cs(k1 * (t1 * n2 + t2), T)
    m1 = table(np.concatenate([c, -s], axis=1))
    k2 = np.arange(n2, dtype=np.int64)
    c2, s2 = cs(k2[:, None] * k2[None, :], n2)
    cs2 = table(np.concatenate([c2, s2], axis=0))
    cg = np.arange(FN_GROUP_DIM, dtype=np.int64)
    cc, sc = cs(cg[:, None] * cg[None, :], FN_GROUP_DIM)
    norm = 1.0 / math.sqrt(T * FN_GROUP_DIM)
    eye = np.eye(FN_GROUPS)
    w = table(np.concatenate([np.kron(eye, cc), np.kron(eye, sc)], axis=0) * norm)
    return dict(n1=n1, n2=n2, m1=m1, cs2=cs2, w=w)


def _trunk(x, layers):
    B, T, _ = x.shape
    tabs = _rope_tables(T)
    ft = _dft_tables(T)
    h = x.reshape(B * T, D_MODEL)
    for l, lp in enumerate(layers):
        outs = _inproj(h, lp, tabs, l == 0, T)
        if l == 0:
            h = outs[0]
            outs = outs[1:]
        natq, natk, natv, fb, q, k, vt, od = outs
        oa = _natten(natq, natk, natv, lp['na_bias'], B, T)
        ob = _fourier(fb, ft, B, T)
        oct = _mla(q, k, vt, B, T)
        h1 = _outproj(oa, ob, oct, od, h, lp)
        h = _ffn(h1, lp, T)
    return h.reshape(B, T, D_MODEL)


def kernel(x_prompt, x_sample, emb_ln_g, emb_ln_b, w_in, na_rpb, mla_q_g, w_uq, mla_kv_g, w_ukv, sgu_ln_g,
           sgu_ln_b, sgu_w, sgu_b, w_out, ln1_g, ln1_b, w_up, conv_w, conv_b, w_down, ln2_g, ln2_b):
    layers = [_prep_layer(l, emb_ln_g, emb_ln_b, w_in, na_rpb, mla_q_g, w_uq, mla_kv_g, w_ukv, sgu_ln_g,
                          sgu_ln_b, sgu_w, sgu_b, w_out, ln1_g, ln1_b, w_up, conv_w, conv_b, w_down,
                          ln2_g, ln2_b) for l in range(DEPTH)]
    return (_trunk(x_prompt, layers), _trunk(x_sample, layers))
```

```python
import functools
import math

import numpy as np
import jax
import jax.numpy as jnp
from jax import lax
from jax.experimental import pallas as pl
from jax.experimental.pallas import tpu as pltpu

F32 = jnp.float32
BF16 = jnp.bfloat16

D_MODEL = 1024
DEPTH = 2
GRID_W = 64
HEAD_DIM = 64
NA_HEADS = 4
NA_ROWS = 8
NA_COLS = 16
NA_WIDTH = NA_HEADS * HEAD_DIM
FN_GROUPS = 4
FN_GROUP_DIM = 64
FN_WIDTH = FN_GROUPS * FN_GROUP_DIM
MLA_HEADS = 4
MLA_Q_LORA = 256
MLA_KV_LORA = 128
MLA_NOPE = 64
MLA_ROPE = 32
MLA_V = 64
MLA_WIDTH = MLA_HEADS * MLA_V
ROPE_THETA = 10000.0
SGU_HEADS = 4
SGU_HEAD_DIM = 64
SGU_WIDTH = SGU_HEADS * SGU_HEAD_DIM
SGU_CHUNK = 128
D_FF = 2816
ALPHA = (2 * DEPTH) ** 0.25
LN_EPS = 1e-5
RMS_EPS = 1e-6

LANES = 128
SUBLANES = 8
BF16_ROWS = 16
VMEM_LIMIT = 56 * 1024 * 1024

TM_IN = 1024
TM_OUT = 1024
TM_FFN = 1024
FF_CHUNK = 256
NA_QROWS = 4
NA_QBLK = NA_QROWS * GRID_W
NA_KBLKS = 3
MLA_PAD = 128
MLA_TQ = 512
MLA_TK = 512
MLA_VT_CHUNK = 512
MLA_ONES = 16
MLA_UNROLL = 32
MLA_SCORE_BUFS = 4
HALO = SUBLANES

LOG2_E = math.log2(math.e)
HI = lax.Precision.HIGHEST
NT_DIMS = (((1,), (1,)), ((), ()))


def _dot(a, b):
    return jnp.dot(a, b, preferred_element_type=F32)


def _dot_nt(a, b):
    return lax.dot_general(a, b, NT_DIMS, preferred_element_type=F32)


def _layer_norm_rows(x, g, b):
    mu = jnp.mean(x, axis=-1, keepdims=True)
    xc = x - mu
    var = jnp.mean(xc * xc, axis=-1, keepdims=True)
    return xc * lax.rsqrt(var + LN_EPS) * g + b


GELU_C = 0.7978845608028654
GELU_K = 0.044715


def _gelu_tanh_x2(x):
    t = jnp.tanh(x * (GELU_C + (GELU_C * GELU_K) * (x * x)))
    return x + x * t


def _gelu_tanh(x):
    return 0.5 * _gelu_tanh_x2(x)


def _split_dot(v, m):
    hi = v.astype(BF16)
    lo = (v - hi.astype(F32)).astype(BF16)
    return _dot(hi, m) + _dot(lo, m)


def _params(*sem):
    return pltpu.CompilerParams(dimension_semantics=sem, vmem_limit_bytes=VMEM_LIMIT)


def _full(shape):
    nd = len(shape)
    return pl.BlockSpec(shape, lambda *_: (0,) * nd)


def _inproj_kernel(apply_ln, x_ref, eg_ref, eb_ref, wnq_ref, wnk_ref, wnv_ref, wfb_ref, wcq_ref, wckv_ref,
                   wsg_ref, qg_ref, wa_ref, kvg_ref, wkn_ref, wvt_ref, ca_ref, cb_ref,
                   tk_ref, gmat_ref, slg_ref, slb_ref, ws_ref, sb_ref, *out_refs):
    if apply_ln:
        xn_ref, natq_ref, natk_ref, natv_ref, fb_ref, q_ref, k_ref, vt_ref, od_ref = out_refs
    else:
        natq_ref, natk_ref, natv_ref, fb_ref, q_ref, k_ref, vt_ref, od_ref = out_refs
    x = x_ref[...]
    if apply_ln:
        x = _layer_norm_rows(x, eg_ref[...], eb_ref[...])
        xn_ref[...] = x
    xb = x.astype(BF16)
    tm = x.shape[0]

    cq = _dot(xb, wcq_ref[...])
    ckv_kr = _dot(xb, wckv_ref[...])
    sg = _dot(xb, wsg_ref[...])

    natq = (_dot_nt(wnq_ref[...], xb) * (HEAD_DIM ** -0.5 * LOG2_E)).astype(BF16)
    natv = _dot_nt(wnv_ref[...], xb).astype(BF16)
    for c in range(tm // NA_QBLK):
        natq_ref[c] = natq[:, c * NA_QBLK:(c + 1) * NA_QBLK]
        natv_ref[c] = natv[:, c * NA_QBLK:(c + 1) * NA_QBLK]
    natk_ref[...] = _dot(xb, wnk_ref[...]).astype(BF16)

    fb_ref[...] = _dot(xb, wfb_ref[...]).astype(BF16)

    cqn = (cq * lax.rsqrt(jnp.mean(cq * cq, axis=-1, keepdims=True) + RMS_EPS) * qg_ref[...]).astype(BF16)
    qa = _dot_nt(wa_ref[...], cqn)
    ca = ca_ref[...]
    cb = cb_ref[MLA_NOPE:MLA_NOPE + MLA_ROPE]
    half = MLA_ROPE // 2
    for h in range(MLA_HEADS):
        base = h * MLA_PAD
        qh = qa[base:base + MLA_PAD] * ca
        lo = qa[base + MLA_NOPE:base + MLA_NOPE + half]
        hi = qa[base + MLA_NOPE + half:base + MLA_NOPE + MLA_ROPE]
        rope = qh[MLA_NOPE:MLA_NOPE + MLA_ROPE] + jnp.concatenate([hi, lo], axis=0) * cb
        qh = jnp.concatenate([qh[:MLA_NOPE], rope, qh[MLA_NOPE + MLA_ROPE:]], axis=0).astype(BF16)
        for c in range(tm // MLA_TQ):
            q_ref[c, base:base + MLA_PAD, :] = qh[:, c * MLA_TQ:(c + 1) * MLA_TQ]

    ckv = ckv_kr[:, :MLA_KV_LORA]
    ckvn = (ckv * lax.rsqrt(jnp.mean(ckv * ckv, axis=-1, keepdims=True) + RMS_EPS) * kvg_ref[...]).astype(BF16)
    kr = ckv_kr[:, MLA_KV_LORA:] * tk_ref[...]
    kr = pltpu.roll(kr, MLA_NOPE, axis=1) + pltpu.roll(kr, MLA_NOPE - MLA_ROPE, axis=1)
    lane = lax.broadcasted_iota(jnp.int32, (1, MLA_PAD), 1)
    kr = jnp.where((lane >= MLA_NOPE) & (lane < MLA_NOPE + MLA_ROPE), kr, 0.0)
    kn = _dot(ckvn, wkn_ref[...])
    for h in range(MLA_HEADS):
        sl = slice(h * MLA_PAD, (h + 1) * MLA_PAD)
        k_ref[:, sl] = (kn[:, sl] + kr).astype(BF16)
    vt = _dot_nt(wvt_ref[...], ckvn)
    tkc = vt_ref.shape[2]
    for c in range(tm // tkc):
        vt_ref[c] = vt[:, c * tkc:(c + 1) * tkc].astype(BF16)

    sg = _gelu_tanh(sg)
    u = sg[:, :SGU_WIDTH]
    v = sg[:, SGU_WIDTH:]
    gmat = gmat_ref[...]
    mu = _split_dot(v, gmat)
    vc = v - mu
    var = _split_dot(vc * vc, gmat)
    vn = (vc * lax.rsqrt(var + LN_EPS) * slg_ref[...] + slb_ref[...]).astype(BF16)
    head = lax.broadcasted_iota(jnp.int32, (1, SGU_WIDTH), 1) // SGU_HEAD_DIM
    zero = jnp.zeros((), BF16)
    for c in range(tm // SGU_CHUNK):
        rows = slice(c * SGU_CHUNK, (c + 1) * SGU_CHUNK)
        vch = vn[rows]
        stacked = jnp.concatenate([jnp.where(head == g, vch, zero) for g in range(SGU_HEADS)], axis=0)
        mixed = sb_ref[...] + _dot(ws_ref[...], stacked)
        od_ref[rows, :] = (u[rows] * mixed).astype(BF16)


def _inproj(x, lp, tabs, apply_ln, T):
    n = x.shape[0]
    tm = TM_IN
    assert T % tm == 0 and tm % MLA_TQ == 0 and tm % MLA_VT_CHUNK == 0 and tm % SGU_CHUNK == 0 and tm % NA_QBLK == 0
    nt = T // tm
    row = lambda i: (i, 0)
    tab = lambda i: (i % nt, 0)
    weights = [lp['eg'], lp['eb'], lp['w_nq'], lp['w_nk'], lp['w_nv'], lp['w_fb'], lp['w_cq'], lp['w_ckv'], lp['w_sg'],
               lp['q_g'], lp['w_a'], lp['kv_g'], lp['w_kn'], lp['w_vt']]
    tables = [tabs['ca'], tabs['cb'], tabs['tk']]
    tail = [lp['gmat'], lp['sl_g'], lp['sl_b'], lp['w_s'], lp['s_b']]
    tab_t = lambda i: (0, i % nt)
    in_specs = ([pl.BlockSpec((tm, D_MODEL), row)] + [_full(w.shape) for w in weights]
                + [pl.BlockSpec((MLA_PAD, tm), tab_t), pl.BlockSpec((MLA_PAD, tm), tab_t),
                   pl.BlockSpec((tm, LANES), tab)] + [_full(w.shape) for w in tail])
    nat_t = jax.ShapeDtypeStruct((n // NA_QBLK, NA_WIDTH, NA_QBLK), BF16)
    nat_t_spec = pl.BlockSpec((tm // NA_QBLK, NA_WIDTH, NA_QBLK), lambda i: (i, 0, 0))
    out_shape = [nat_t, jax.ShapeDtypeStruct((n, NA_WIDTH), BF16), nat_t,
                 jax.ShapeDtypeStruct((n, FN_WIDTH), BF16),
                 jax.ShapeDtypeStruct((n // MLA_TQ, MLA_HEADS * MLA_PAD, MLA_TQ), BF16),
                 jax.ShapeDtypeStruct((n, MLA_HEADS * MLA_PAD), BF16),
                 jax.ShapeDtypeStruct((n // MLA_VT_CHUNK, MLA_WIDTH, MLA_VT_CHUNK), BF16),
                 jax.ShapeDtypeStruct((n, SGU_WIDTH), BF16)]
    out_specs = [nat_t_spec, pl.BlockSpec((tm, NA_WIDTH), row), nat_t_spec,
                 pl.BlockSpec((tm, FN_WIDTH), row),
                 pl.BlockSpec((tm // MLA_TQ, MLA_HEADS * MLA_PAD, MLA_TQ), lambda i: (i, 0, 0)),
                 pl.BlockSpec((tm, MLA_HEADS * MLA_PAD), row),
                 pl.BlockSpec((tm // MLA_VT_CHUNK, MLA_WIDTH, MLA_VT_CHUNK), lambda i: (i, 0, 0)),
                 pl.BlockSpec((tm, SGU_WIDTH), row)]
    if apply_ln:
        out_shape = [jax.ShapeDtypeStruct((n, D_MODEL), F32)] + out_shape
        out_specs = [pl.BlockSpec((tm, D_MODEL), row)] + out_specs
    return pl.pallas_call(
        functools.partial(_inproj_kernel, apply_ln),
        grid=(n // tm,),
        in_specs=in_specs,
        out_specs=out_specs,
        out_shape=out_shape,
        compiler_params=_params("parallel"),
        name="inproj_ln" if apply_ln else "inproj",
    )(x, *weights, *tables, *tail)


def _natten_kernel(npair, qt_ref, k0_ref, k1_ref, k2_ref, k3_ref, vt0_ref, vt1_ref, vt2_ref, vt3_ref,
                   bias_a_ref, bias_b_ref, o_ref, s_ref):
    k_refs = (k0_ref, k1_ref, k2_ref, k3_ref)
    vt_refs = (vt0_ref, vt1_ref, vt2_ref, vt3_ref)
    row_head = lax.broadcasted_iota(jnp.int32, (NA_WIDTH, 1), 0) // HEAD_DIM
    zero = jnp.zeros((), BF16)
    ones = jnp.ones((BF16_ROWS, NA_QBLK), BF16)

    def block(qi, off, bias_ref):
        qt = qt_ref[qi]

        def scores(h, slot):
            qh = jnp.where(row_head == h, qt, zero)
            for d in range(NA_KBLKS):
                rows = slice(d * NA_QBLK, (d + 1) * NA_QBLK)
                s_ref[slot, rows, :] = _dot(k_refs[off + d][...], qh) + bias_ref[h, rows, :]

        scores(0, 0)
        for h in range(NA_HEADS):
            if h + 1 < NA_HEADS:
                scores(h + 1, (h + 1) % 2)
            s = s_ref[h % 2]
            p = jnp.exp2(s - jnp.max(s, axis=0, keepdims=True)).astype(BF16)
            ch = slice(h * HEAD_DIM, (h + 1) * HEAD_DIM)
            acc = None
            for d in range(NA_KBLKS):
                vte = jnp.concatenate([vt_refs[off + d][ch, :], ones], axis=0)
                part = _dot(vte, p[d * NA_QBLK:(d + 1) * NA_QBLK])
                acc = part if acc is None else acc + part
            o_ref[qi, ch, :] = (acc[:HEAD_DIM] * (1.0 / acc[HEAD_DIM:HEAD_DIM + 1])).astype(BF16)

    i = pl.program_id(1)
    last = npair - 1
    for cond, off_a, off_b in ((i == 0, 0, 0), ((i > 0) & (i < last), 0, 1), (i == last, 1, 1)):
        @pl.when(cond)
        def _(off_a=off_a, off_b=off_b):
            block(0, off_a, bias_a_ref)
            block(1, off_b, bias_b_ref)


def _natten(qt, k, vt, bias, B, T):
    nblk = T // NA_QBLK
    npair = nblk // 2
    nwin = NA_KBLKS + 1
    assert nblk % 2 == 0 and nblk >= nwin

    def kv_map(d, rank3):
        def f(b, i):
            blk = b * nblk + jnp.clip(2 * i - 1, 0, nblk - nwin) + d
            return (blk, 0, 0) if rank3 else (blk, 0)
        return f

    bias_blk = (None, NA_HEADS, NA_KBLKS * NA_QBLK, NA_QBLK)
    tblk = (None, NA_WIDTH, NA_QBLK)
    pair = (2, NA_WIDTH, NA_QBLK)
    in_specs = ([pl.BlockSpec(pair, lambda b, i: (b * npair + i, 0, 0))]
                + [pl.BlockSpec((NA_QBLK, NA_WIDTH), kv_map(d, False)) for d in range(nwin)]
                + [pl.BlockSpec(tblk, kv_map(d, True)) for d in range(nwin)]
                + [pl.BlockSpec(bias_blk, lambda b, i: (jnp.where(i == 0, 0, 1), 0, 0, 0)),
                   pl.BlockSpec(bias_blk, lambda b, i: (jnp.where(i == npair - 1, 2, 1), 0, 0, 0))])
    return pl.pallas_call(
        functools.partial(_natten_kernel, npair),
        grid=(B, npair),
        in_specs=in_specs,
        out_specs=pl.BlockSpec(pair, lambda b, i: (b * npair + i, 0, 0)),
        out_shape=jax.ShapeDtypeStruct((B * nblk, NA_WIDTH, NA_QBLK), BF16),
        scratch_shapes=[pltpu.VMEM((2, NA_KBLKS * NA_QBLK, NA_QBLK), F32)],
        compiler_params=_params("parallel", "arbitrary"),
        name="natten",
    )(qt, k, k, k, k, vt, vt, vt, vt, bias, bias)


def _fft1_kernel(x_ref, m_ref, a_ref):
    kk = m_ref.shape[0]
    n1 = x_ref.shape[0]
    xt = jnp.swapaxes(x_ref[...], 0, 1)
    for i in range(kk):
        res = _dot(m_ref[i], xt[i])
        res = jnp.concatenate([res[:n1], res[n1:]], axis=1).astype(BF16)
        a_ref[:, i] = res.reshape(n1 // kk, kk, 2 * FN_WIDTH)


def _fft2_kernel(a_ref, cs_ref, w_ref, y_ref):
    n2, kk, _ = a_ref.shape
    cs = cs_ref[...]
    at = jnp.swapaxes(a_ref[...], 0, 1)
    g = []
    for i in range(kk):
        r = _dot(cs, at[i])
        g.append(jnp.concatenate([r[:n2, :FN_WIDTH] + r[n2:, FN_WIDTH:],
                                  r[:n2, FN_WIDTH:] - r[n2:, :FN_WIDTH]], axis=1).astype(BF16))
    g = jnp.concatenate(g, axis=0)
    y = _dot(g, w_ref[...]).astype(BF16)
    y_ref[...] = jnp.swapaxes(y.reshape(kk, n2, FN_WIDTH), 0, 1)


def _fourier(fb, ft, B, T):
    n1, n2 = ft['n1'], ft['n2']
    kk = BF16_ROWS
    a = pl.pallas_call(
        _fft1_kernel,
        grid=(B, n2 // kk),
        in_specs=[pl.BlockSpec((None, n1, kk, FN_WIDTH), lambda b, j: (b, 0, j, 0)),
                  pl.BlockSpec((kk, 2 * n1, n1), lambda b, j: (j, 0, 0))],
        out_specs=pl.BlockSpec((None, n1 // kk, kk, kk, 2 * FN_WIDTH), lambda b, j: (b, 0, j, 0, 0)),
        out_shape=jax.ShapeDtypeStruct((B, n1 // kk, n2, kk, 2 * FN_WIDTH), BF16),
        compiler_params=_params("parallel", "arbitrary"),
        name="fft_stage1",
    )(fb.reshape(B, n1, n2, FN_WIDTH), ft['m1'])
    y = pl.pallas_call(
        _fft2_kernel,
        grid=(B, n1 // kk),
        in_specs=[pl.BlockSpec((None, None, n2, kk, 2 * FN_WIDTH), lambda b, j: (b, j, 0, 0, 0)),
                  _full(ft['cs2'].shape), _full(ft['w'].shape)],
        out_specs=pl.BlockSpec((None, n2, kk, FN_WIDTH), lambda b, j: (b, 0, j, 0)),
        out_shape=jax.ShapeDtypeStruct((B, n2, n1, FN_WIDTH), BF16),
        compiler_params=_params("parallel", "arbitrary"),
        name="fft_stage2",
    )(a, ft['cs2'], ft['w'])
    return y.reshape(B * T, FN_WIDTH)


def _mla_trip(nq, nk):
    u = MLA_UNROLL
    while u > 2 and ((nq * nk) % u or (nk % u and u % nk)):
        u //= 2
    assert u % 2 == 0 and (nq * nk) % u == 0 and (nk % u == 0 or u % nk == 0)
    return u


def _mla_kernel(q_ref, k_ref, vt_ref, o_ref, s_ref):
    tk = s_ref.shape[1]
    per = tk // MLA_VT_CHUNK
    nk = vt_ref.shape[0] // per
    nq, _, tq = q_ref.shape
    unroll = _mla_trip(nq, nk)
    ones = jnp.ones((MLA_ONES, tk), BF16)

    def scores(i, j):
        start = j * tk if isinstance(j, int) else pl.multiple_of(j * tk, tk)
        return _dot(k_ref[pl.ds(start, tk), :], q_ref[i])

    def update(carry, j, slot):
        m, acc = carry
        st = s_ref[slot]
        m_new = jnp.maximum(m, jnp.max(st, axis=0, keepdims=True))
        alpha = jnp.exp2(m - m_new)
        p = jnp.exp2(st - m_new).astype(BF16)
        vt = jnp.concatenate([vt_ref[j * per + u] for u in range(per)], axis=1)
        vte = jnp.concatenate([vt, ones], axis=0)
        return m_new, alpha * acc + _dot(vte, p)

    nbuf = s_ref.shape[0]
    ahead = nbuf // 2
    last = nq * nk - 1
    for f in range(ahead):
        s_ref[f] = scores(f // nk, f % nk)
    m0 = jnp.full((1, tq), -1e30, F32)
    acc0 = jnp.zeros((MLA_V + MLA_ONES, tq), F32)

    def finish(i, acc):
        o_ref[i] = (acc[:MLA_V] * (1.0 / acc[MLA_V:MLA_V + 1])).astype(o_ref.dtype)

    whole_blocks = unroll > nk

    def body(t, carry):
        if whole_blocks:
            place = lambda c: (t * (unroll // nk) + c // nk, c % nk)
        else:
            i0, j0 = (t * unroll) // nk, (t * unroll) % nk
            place = lambda c: (i0, j0 + c)
            fresh = j0 == 0
            carry = (jnp.where(fresh, m0, carry[0]), jnp.where(fresh, acc0, carry[1]))
        for c in range(unroll):
            if c + ahead < unroll:
                s_ref[(c + ahead) % nbuf] = scores(*place(c + ahead))
            else:
                nxt = jnp.minimum(t * unroll + c + ahead, last)
                s_ref[(c + ahead) % nbuf] = scores(nxt // nk, nxt % nk)
            i, j = place(c)
            if whole_blocks and j == 0:
                carry = (m0, acc0)
            carry = update(carry, j, c % nbuf)
            if whole_blocks and j == nk - 1:
                finish(i, carry[1])

        if not whole_blocks:
            @pl.when(j0 + unroll == nk)
            def _():
                finish(i0, carry[1])

        return carry

    lax.fori_loop(0, nq * nk // unroll, body, (m0, acc0))


def _mla(q, k, vt, B, T):
    nq = T // MLA_TQ
    nk = T // MLA_TK
    unroll = _mla_trip(nq, nk)
    nbuf = MLA_SCORE_BUFS if unroll % MLA_SCORE_BUFS == 0 else 2
    return pl.pallas_call(
        _mla_kernel,
        grid=(B, MLA_HEADS),
        in_specs=[pl.BlockSpec((nq, MLA_PAD, MLA_TQ), lambda b, h: (b, h, 0)),
                  pl.BlockSpec((T, MLA_PAD), lambda b, h: (b, h)),
                  pl.BlockSpec((T // MLA_VT_CHUNK, MLA_V, MLA_VT_CHUNK), lambda b, h: (b, h, 0))],
        out_specs=pl.BlockSpec((nq, MLA_V, MLA_TQ), lambda b, h: (b, h, 0)),
        out_shape=jax.ShapeDtypeStruct((B * nq, MLA_WIDTH, MLA_TQ), BF16),
        scratch_shapes=[pltpu.VMEM((nbuf, MLA_TK, MLA_TQ), F32)],
        compiler_params=_params("parallel", "arbitrary"),
        name="mla",
    )(q, k, vt)


def _outproj_kernel(oat_ref, ob_ref, oct_ref, od_ref, x_ref, w_ref, g_ref, b_ref, o_ref):
    w = NA_WIDTH
    tm = x_ref.shape[0]
    for c in range(tm // NA_QBLK):
        rows = slice(c * NA_QBLK, (c + 1) * NA_QBLK)
        oc = oct_ref[(c * NA_QBLK) // MLA_TQ, :, pl.ds((c * NA_QBLK) % MLA_TQ, NA_QBLK)].T
        y = (_dot(oat_ref[c].T, w_ref[0:w]) + _dot(ob_ref[rows, :], w_ref[w:2 * w])
             + _dot(oc, w_ref[2 * w:3 * w]) + _dot(od_ref[rows, :], w_ref[3 * w:4 * w]))
        o_ref[rows, :] = _layer_norm_rows(ALPHA * x_ref[rows, :] + y, g_ref[...], b_ref[...])


def _outproj(oa, ob, oct, od, x, lp):
    n = x.shape[0]
    tm = TM_OUT
    assert tm % MLA_TQ == 0 and MLA_TQ % NA_QBLK == 0
    row = lambda i: (i, 0)
    return pl.pallas_call(
        _outproj_kernel,
        grid=(n // tm,),
        in_specs=[pl.BlockSpec((tm // NA_QBLK, NA_WIDTH, NA_QBLK), lambda i: (i, 0, 0)),
                  pl.BlockSpec((tm, FN_WIDTH), row),
                  pl.BlockSpec((tm // MLA_TQ, MLA_WIDTH, MLA_TQ), lambda i: (i, 0, 0)),
                  pl.BlockSpec((tm, SGU_WIDTH), row),
                  pl.BlockSpec((tm, D_MODEL), row), _full(lp['w_out'].shape),
                  _full(lp['ln1_g'].shape), _full(lp['ln1_b'].shape)],
        out_specs=pl.BlockSpec((tm, D_MODEL), row),
        out_shape=jax.ShapeDtypeStruct((n, D_MODEL), F32),
        compiler_params=_params("parallel"),
        name="outproj",
    )(oa, ob, oct, od, x, lp['w_out'], lp['ln1_g'], lp['ln1_b'])


def _ffn_kernel(nt, x_ref, xp_ref, xn_ref, wup_ref, cw_ref, cb_ref, wd_ref, g_ref, b_ref, o_ref,
                xe_ref, act_ref):
    i = pl.program_id(0)
    tm = x_ref.shape[0]
    cf = FF_CHUNK
    first = (i % nt) == 0
    last = (i % nt) == nt - 1
    xe_ref[0:HALO] = jnp.where(first, 0.0, xp_ref[...]).astype(BF16)
    xe_ref[HALO:HALO + tm] = x_ref[...].astype(BF16)
    xe_ref[HALO + tm:] = jnp.where(last, 0.0, xn_ref[...]).astype(BF16)
    xe = xe_ref[...]

    rows = slice(HALO, HALO + tm)

    def conv(h, cols):
        prev = pltpu.roll(h, 1, axis=0)
        nxt = pltpu.roll(h, h.shape[0] - 1, axis=0)
        return (prev[rows] * cw_ref[0:1, cols] + h[rows] * cw_ref[1:2, cols]
                + nxt[rows] * cw_ref[2:3, cols] + cb_ref[:, cols])

    for c in range(D_FF // cf):
        gate = slice(c * cf, (c + 1) * cf)
        val = slice(D_FF + c * cf, D_FF + (c + 1) * cf)
        act = _gelu_tanh_x2(conv(_dot(xe, wup_ref[:, gate]), gate)) * conv(_dot(xe, wup_ref[:, val]), val)
        act_ref[:, gate] = act.astype(BF16)

    y = _dot(act_ref[...], wd_ref[...])
    o_ref[...] = _layer_norm_rows(ALPHA * x_ref[...] + y, g_ref[...], b_ref[...])


def _resident(shape):
    nd = len(shape)
    return pl.BlockSpec(shape, lambda *_: (0,) * nd, pipeline_mode=pl.Buffered(1))


def _ffn(x, lp, T):
    n = x.shape[0]
    tm = TM_FFN
    nt = T // tm
    assert T % tm == 0 and D_FF % FF_CHUNK == 0
    hb = tm // HALO
    nhb = n // HALO
    row = lambda i: (i, 0)
    return pl.pallas_call(
        functools.partial(_ffn_kernel, nt),
        grid=(n // tm,),
        in_specs=[pl.BlockSpec((tm, D_MODEL), row),
                  pl.BlockSpec((HALO, D_MODEL), lambda i: (jnp.maximum(i * hb - 1, 0), 0)),
                  pl.BlockSpec((HALO, D_MODEL), lambda i: (jnp.minimum((i + 1) * hb, nhb - 1), 0)),
                  _resident(lp['w_up'].shape), _resident(lp['conv_w'].shape), _resident(lp['conv_b'].shape),
                  _resident(lp['w_down'].shape), _resident(lp['ln2_g'].shape), _resident(lp['ln2_b'].shape)],
        out_specs=pl.BlockSpec((tm, D_MODEL), row),
        out_shape=jax.ShapeDtypeStruct((n, D_MODEL), F32),
        scratch_shapes=[pltpu.VMEM((tm + 2 * HALO, D_MODEL), BF16),
                        pltpu.VMEM((tm, D_FF), BF16)],
        compiler_params=_params("parallel"),
        name="conv_ffn",
    )(x, x, x, lp['w_up'], lp['conv_w'], lp['conv_b'], lp['w_down'], lp['ln2_g'], lp['ln2_b'])


def _natten_bias(rpb):
    nr, nc = 2 * NA_ROWS - 1, 2 * NA_COLS - 1
    krows = NA_KBLKS * NA_QROWS
    col = np.arange(GRID_W)
    ci = np.clip(col[None, :] - col[:, None] + NA_COLS - 1, 0, nc - 1)
    onehot_c = (ci.reshape(-1)[None, :] == np.arange(nc)[:, None]).astype(np.float32)
    cs = np.clip(col - NA_COLS // 2, 0, GRID_W - NA_COLS)
    dc = col[None, :] - cs[:, None]
    valid_c = (dc >= 0) & (dc < NA_COLS)
    qa = np.arange(NA_QROWS)
    kr = np.arange(krows)
    onehot_r, valid_r = [], []
    for kind in range(3):
        q_row = (0, NA_QROWS, 2 * NA_QROWS)[kind] + qa
        r_start = (np.zeros_like(qa), qa, np.full_like(qa, NA_QROWS))[kind]
        dr = kr[None, :] - r_start[:, None]
        valid_r.append((dr >= 0) & (dr < NA_ROWS))
        ri = np.clip(kr[None, :] - q_row[:, None] + NA_ROWS - 1, 0, nr - 1)
        onehot_r.append((ri.reshape(-1)[:, None] == np.arange(nr)[None, :]).astype(np.float32))
    onehot_r = np.stack(onehot_r)
    valid = np.stack(valid_r)[:, :, None, :, None] & valid_c[None, None, :, None, :]
    by_col = jnp.einsum('hrc,cx->hrx', rpb, onehot_c, precision=HI)
    full = jnp.einsum('kpr,hrx->khpx', onehot_r, by_col, precision=HI)
    full = full.reshape(3, NA_HEADS, NA_QROWS, krows, GRID_W, GRID_W).transpose(0, 1, 3, 5, 2, 4)
    valid = valid.transpose(0, 3, 4, 1, 2)
    full = jnp.where(valid[:, None], full * LOG2_E, -1e30)
    return full.reshape(3, NA_HEADS, krows * GRID_W, NA_QBLK).astype(F32)


def _prep_layer(l, emb_ln_g, emb_ln_b, w_in, na_rpb, mla_q_g, w_uq, mla_kv_g, w_ukv, sgu_ln_g, sgu_ln_b,
                sgu_w, sgu_b, w_out, ln1_g, ln1_b, w_up, conv_w, conv_b, w_down, ln2_g, ln2_b):
    wi = w_in[l]
    o_fb = 3 * NA_WIDTH
    o_cq = o_fb + FN_WIDTH
    o_ckv = o_cq + MLA_Q_LORA
    o_kr = o_ckv + MLA_KV_LORA
    o_sg = o_kr + MLA_ROPE
    half = MLA_ROPE // 2
    swap = np.concatenate([np.arange(half, MLA_ROPE), np.arange(half)])
    w_kr = wi[:, o_kr:o_sg]
    w_ckv = jnp.concatenate([wi[:, o_ckv:o_kr], w_kr, w_kr[:, swap],
                             jnp.zeros((D_MODEL, LANES - 2 * MLA_ROPE), F32)], axis=1)

    uq = w_uq[l]
    pad = jnp.zeros((MLA_Q_LORA, MLA_HEADS, MLA_PAD - MLA_NOPE - MLA_ROPE), F32)
    w_a = jnp.concatenate([uq, pad], axis=2).reshape(MLA_Q_LORA, MLA_HEADS * MLA_PAD)
    ukv = w_ukv[l]
    w_kn = jnp.concatenate([ukv[:, :, :MLA_NOPE],
                            jnp.zeros((MLA_KV_LORA, MLA_HEADS, MLA_PAD - MLA_NOPE), F32)],
                           axis=2).reshape(MLA_KV_LORA, MLA_HEADS * MLA_PAD)
    w_vt = ukv[:, :, MLA_NOPE:].reshape(MLA_KV_LORA, MLA_WIDTH).T
    w_s = jnp.concatenate([sgu_w[l][g] for g in range(SGU_HEADS)], axis=1)
    gmat =np.kron(np.eye(SGU_HEADS), np.full((SGU_HEAD_DIM, SGU_HEAD_DIM), 1.0 / SGU_HEAD_DIM)).astype(np.float32)
    s_b = jnp.repeat(sgu_b[l].T, SGU_HEAD_DIM, axis=1)
    r1 = lambda a: a.reshape(1, -1).astype(F32)
    half_val = jnp.concatenate([jnp.ones((1, D_FF), F32), jnp.full((1, D_FF), 0.5, F32)], axis=1)
    return dict(
        eg=r1(emb_ln_g), eb=r1(emb_ln_b),
        w_nq=wi[:, :NA_WIDTH].T.astype(BF16), w_nk=wi[:, NA_WIDTH:2 * NA_WIDTH].astype(BF16),
        w_nv=wi[:, 2 * NA_WIDTH:o_fb].T.astype(BF16), w_fb=wi[:, o_fb:o_cq].astype(BF16),
        w_cq=wi[:, o_cq:o_ckv].astype(BF16), w_ckv=w_ckv.astype(BF16), w_sg=wi[:, o_sg:].astype(BF16),
        q_g=r1(mla_q_g[l]), w_a=w_a.T.astype(BF16),
        kv_g=r1(mla_kv_g[l]), w_kn=w_kn.astype(BF16), w_vt=w_vt.astype(BF16),
        gmat=jnp.asarray(gmat, BF16),
        sl_g=r1(sgu_ln_g[l]), sl_b=r1(sgu_ln_b[l]), w_s=w_s.astype(BF16), s_b=s_b.astype(F32),
        na_bias=_natten_bias(na_rpb[l]),
        w_out=w_out[l].astype(BF16), ln1_g=r1(ln1_g[l]), ln1_b=r1(ln1_b[l]),
        w_up=w_up[l].astype(BF16), conv_w=conv_w[l].astype(F32) * half_val, conv_b=r1(conv_b[l]) * half_val,
        w_down=w_down[l].astype(BF16), ln2_g=r1(ln2_g[l]), ln2_b=r1(ln2_b[l]),
    )


def _rope_tables(T):
    inv_freq = ROPE_THETA ** (-np.arange(0, MLA_ROPE, 2, dtype=np.float64) / MLA_ROPE)
    ang = np.arange(T, dtype=np.float64)[:, None] * inv_freq[None, :]
    cos, sin = np.cos(ang), np.sin(ang)
    cos2 = np.concatenate([cos, cos], axis=1)
    sin2 = np.concatenate([-sin, sin], axis=1)
    scale = (MLA_NOPE + MLA_ROPE) ** -0.5 * math.log2(math.e)
    zq = np.zeros((T, MLA_PAD - MLA_NOPE - MLA_ROPE))
    ca = np.concatenate([np.full((T, MLA_NOPE), scale), scale * cos2, zq], axis=1)
    cb = np.concatenate([np.zeros((T, MLA_NOPE)), scale * sin2, zq], axis=1)
    tk = np.concatenate([cos2, sin2, np.zeros((T, LANES - 2 * MLA_ROPE))], axis=1)
    f32 = lambda a: jnp.asarray(np.ascontiguousarray(a), F32)
    return dict(ca=f32(ca.T), cb=f32(cb.T), tk=f32(tk))


def _dft_tables(T):
    n1 = 1 << (int(math.log2(T)) // 2)
    n2 = T // n1

    def cs(num, den):
        ang = (num % den).astype(np.float64) * (2.0 * np.pi / den)
        return np.cos(ang), np.sin(ang)

    def table(x):
        return jnp.asarray(x.astype(np.float32)).astype(BF16)

    t2 = np.arange(n2, dtype=np.int64)[:, None, None]
    k1 = np.arange(n1, dtype=np.int64)[None, :, None]
    t1 = np.arange(n1, dtype=np.int64)[None, None, :]
    c, s = cs(k1 * (t1 * n2 + t2), T)
    m1 = table(np.concatenate([c, -s], axis=1))
    k2 = np.arange(n2, dtype=np.int64)
    c2, s2 = cs(k2[:, None] * k2[None, :], n2)
    cs2 = table(np.concatenate([c2, s2], axis=0))
    cg = np.arange(FN_GROUP_DIM, dtype=np.int64)
    cc, sc = cs(cg[:, None] * cg[None, :], FN_GROUP_DIM)
    norm = 1.0 / math.sqrt(T * FN_GROUP_DIM)
    eye = np.eye(FN_GROUPS)
    w = table(np.concatenate([np.kron(eye, cc), np.kron(eye, sc)], axis=0) * norm)
    return dict(n1=n1, n2=n2, m1=m1, cs2=cs2, w=w)


def _trunk(x, layers):
    B, T, _ = x.shape
    tabs = _rope_tables(T)
    ft = _dft_tables(T)
    h = x.reshape(B * T, D_MODEL)
    for l, lp in enumerate(layers):
        outs = _inproj(h, lp, tabs, l == 0, T)
        if l == 0:
            h = outs[0]
            outs = outs[1:]
        natq, natk, natv, fb, q, k, vt, od = outs
        oa = _natten(natq, natk, natv, lp['na_bias'], B, T)
        ob = _fourier(fb, ft, B, T)
        oct = _mla(q, k, vt, B, T)
        h1 = _outproj(oa, ob, oct, od, h, lp)
        h = _ffn(h1, lp, T)
    return h.reshape(B, T, D_MODEL)


def kernel(x_prompt, x_sample, emb_ln_g, emb_ln_b, w_in, na_rpb, mla_q_g, w_uq, mla_kv_g, w_ukv, sgu_ln_g,
           sgu_ln_b, sgu_w, sgu_b, w_out, ln1_g, ln1_b, w_up, conv_w, conv_b, w_down, ln2_g, ln2_b):
    layers = [_prep_layer(l, emb_ln_g, emb_ln_b, w_in, na_rpb, mla_q_g, w_uq, mla_kv_g, w_ukv, sgu_ln_g,
                          sgu_ln_b, sgu_w, sgu_b, w_out, ln1_g, ln1_b, w_up, conv_w, conv_b, w_down,
                          ln2_g, ln2_b) for l in range(DEPTH)]
    return (_trunk(x_prompt, layers), _trunk(x_sample, layers))
```
